```python
import math
import jax, jax.numpy as jnp
from jax import lax
import numpy as np

D_MODEL = 1024
BATCH = 16
SEQ = 2048
DEPTH = 1

HEAD_DIM = 64
N_HEADS_A = 8
N_HEADS_B = 8
WIDTH_A = N_HEADS_A * HEAD_DIM
WIDTH_B = N_HEADS_B * HEAD_DIM
MIX_WIDTH = WIDTH_A + WIDTH_B
DILATED_PATTERNS = ((128, 1), (512, 4), (2048, 16))
BLOCK = 128
N_EXPERTS = 32
TOP_K = 4
D_EXPERT = D_MODEL
SWIGLU_LIMIT = 7.0
SWIGLU_ALPHA = 1.702
PLE_DIM = 256
FORGET_BIAS_INIT = 2.0
NORM_EPS = 1e-6

IN_COLS = 3 * WIDTH_A + 3 * WIDTH_B + N_HEADS_B

kernel_name = "hybrid_dilated_fox_moe_ple_block"


def rms_norm(x, g):
    xf = x.astype(jnp.float32)
    y = xf * lax.rsqrt(jnp.mean(xf * xf, axis=-1, keepdims=True) + NORM_EPS)
    return (y * g.astype(jnp.float32)).astype(x.dtype)


def alibi_slopes(n_heads):
    return jnp.asarray(2.0 ** (-8.0 * np.arange(1, n_heads + 1) / n_heads), dtype=jnp.float32)


def dilated_window_attention(q, k, v, slopes, window, dilation):
    B, S, H, Dh = q.shape
    steps = window // dilation
    n = -(-S // (dilation * BLOCK)) * BLOCK
    sp = n * dilation
    nb = n // BLOCK
    pad = ((0, 0), (0, sp - S), (0, 0), (0, 0))

    def to_blocks(t):
        t = jnp.pad(t, pad).reshape(B, n, dilation, H, Dh).transpose(0, 2, 1, 3, 4)
        return t.reshape(B, dilation, nb, BLOCK, H, Dh)

    def with_prev(t):
        prev = jnp.pad(t[:, :, :-1], ((0, 0), (0, 0), (1, 0), (0, 0), (0, 0), (0, 0)))
        return jnp.concatenate([prev, t], axis=3)

    qb = to_blocks(q)
    kk = with_prev(to_blocks(k))
    vv = with_prev(to_blocks(v))

    s = jnp.einsum('brnqhd,brnkhd->brnhqk', qb, kk).astype(jnp.float32) * (Dh ** -0.5)
    qi = jnp.arange(BLOCK)[:, None]
    kj = jnp.arange(2 * BLOCK)[None, :]
    rel = qi + BLOCK - kj
    band = (rel >= 0) & (rel <= steps)
    has_key = (jnp.arange(nb)[:, None, None] > 0) | (kj[None] >= BLOCK)
    mask = band[None] & has_key
    alibi = -slopes[:, None, None] * (rel * dilation).astype(jnp.float32)[None]
    s = s + alibi[None, None, None]
    s = jnp.where(mask[None, None, :, None], s, -jnp.inf)
    lse = jax.nn.logsumexp(s, axis=-1)
    pr = jnp.exp(s - lse[..., None]).astype(v.dtype)
    o = jnp.einsum('brnhqk,brnkhd->brnqhd', pr, vv)

    o = o.reshape(B, dilation, n, H, Dh).transpose(0, 2, 1, 3, 4).reshape(B, sp, H, Dh)[:, :S]
    lse = lse.transpose(0, 1, 2, 4, 3).reshape(B, dilation, n, H).transpose(0, 2, 1, 3)
    lse = lse.reshape(B, sp, H)[:, :S]
    return o, lse


def longnet_mixture(q, k, v):
    slopes = alibi_slopes(q.shape[2])
    outs, lses = [], []
    for window, dilation in DILATED_PATTERNS:
        o, l = dilated_window_attention(q, k, v, slopes, window, dilation)
        outs.append(o)
        lses.append(l)
    w = jax.nn.softmax(jnp.stack(lses, axis=0), axis=0)
    o = jnp.sum(w[..., None] * jnp.stack(outs, axis=0).astype(jnp.float32), axis=0)
    return o.astype(q.dtype)


def forgetting_attention(q, k, v, log_f):
    B, S, H, Dh = q.shape
    c = jnp.cumsum(log_f.astype(jnp.float32), axis=1).transpose(0, 2, 1)
    kpos = jnp.arange(S)
    scale = Dh ** -0.5

    def one_block(i):
        start = i * BLOCK
        qb = lax.dynamic_slice_in_dim(q, start, BLOCK, axis=1)
        cq = lax.dynamic_slice_in_dim(c, start, BLOCK, axis=2)
        s = jnp.einsum('bqhd,bkhd->bhqk', qb, k).astype(jnp.float32) * scale
        s = s + cq[..., :, None] - c[..., None, :]
        qpos = start + jnp.arange(BLOCK)
        s = jnp.where(kpos[None, :] <= qpos[:, None], s, -jnp.inf)
        pr = jax.nn.softmax(s, axis=-1).astype(v.dtype)
        return jnp.einsum('bhqk,bkhd->bqhd', pr, v)

    o = lax.map(one_block, jnp.arange(S // BLOCK))
    return o.transpose(1, 0, 2, 3, 4).reshape(B, S, H, Dh)


def clamped_swiglu(h):
    x_glu = jnp.minimum(h[..., ::2], SWIGLU_LIMIT)
    x_lin = jnp.clip(h[..., 1::2], -SWIGLU_LIMIT, SWIGLU_LIMIT)
    return x_glu * jax.nn.sigmoid(SWIGLU_ALPHA * x_glu) * (x_lin + 1.0)


def moe_ffn(u, w_router, b_router, w_gate_up, b_gate_up, w_down, b_down):
    B, S, D = u.shape
    t = u.reshape(B * S, D)
    logits = (t @ w_router).astype(jnp.float32) + b_router.astype(jnp.float32)
    top_val, top_idx = lax.top_k(logits, TOP_K)
    gates = jax.nn.softmax(top_val, axis=-1)
    comb = jnp.sum(jax.nn.one_hot(top_idx, N_EXPERTS, dtype=jnp.float32) * gates[..., None], axis=1)
    y = jnp.zeros((B * S, D), jnp.float32)
    for e in range(N_EXPERTS):
        h = t @ w_gate_up[e] + b_gate_up[e]
        out = clamped_swiglu(h) @ w_down[e] + b_down[e]
        y = y + comb[:, e:e + 1] * out.astype(jnp.float32)
    return y.astype(u.dtype).reshape(B, S, D)


def setup_inputs(seed: int = 0) -> dict:
    key = jax.random.key(seed)
    ks = jax.random.split(key, 20)
    f32 = jnp.float32
    nrm = lambda k, shape, s: (jax.random.normal(k, shape, f32) * s).astype(f32)
    return {
        "x": nrm(ks[0], (BATCH, SEQ, D_MODEL), 1.0),
        "p": nrm(ks[1], (DEPTH, BATCH, SEQ, PLE_DIM), 1.0),
        "g_mix": 1.0 + nrm(ks[2], (DEPTH, D_MODEL), 0.02),
        "w_in": nrm(ks[3], (DEPTH, D_MODEL, IN_COLS), D_MODEL ** -0.5),
        "b_f": FORGET_BIAS_INIT + nrm(ks[4], (DEPTH, N_HEADS_B), 0.5),
        "g_qa": 1.0 + nrm(ks[5], (DEPTH, HEAD_DIM), 0.02),
        "g_ka": 1.0 + nrm(ks[6], (DEPTH, HEAD_DIM), 0.02),
        "g_qb": 1.0 + nrm(ks[7], (DEPTH, HEAD_DIM), 0.02),
        "g_kb": 1.0 + nrm(ks[8], (DEPTH, HEAD_DIM), 0.02),
        "w_o": nrm(ks[9], (DEPTH, MIX_WIDTH, D_MODEL), MIX_WIDTH ** -0.5),
        "g_ffn": 1.0 + nrm(ks[10], (DEPTH, D_MODEL), 0.02),
        "w_router": nrm(ks[11], (DEPTH, D_MODEL, N_EXPERTS), D_MODEL ** -0.5),
        "b_router": nrm(ks[12], (DEPTH, N_EXPERTS), 0.01),
        "w_gate_up": nrm(ks[13], (DEPTH, N_EXPERTS, D_MODEL, 2 * D_EXPERT), D_MODEL ** -0.5),
        "b_gate_up": nrm(ks[14], (DEPTH, N_EXPERTS, 2 * D_EXPERT), 0.01),
        "w_down": nrm(ks[15], (DEPTH, N_EXPERTS, D_EXPERT, D_MODEL), D_EXPERT ** -0.5),
        "b_down": nrm(ks[16], (DEPTH, N_EXPERTS, D_MODEL), 0.01),
        "g_ple": 1.0 + nrm(ks[17], (DEPTH, D_MODEL), 0.02),
        "w_ple_gate": nrm(ks[18], (DEPTH, D_MODEL, D_MODEL), D_MODEL ** -0.5),
        "w_ple_proj": nrm(ks[19], (DEPTH, PLE_DIM, D_MODEL), PLE_DIM ** -0.5),
    }


def reference(x, p, g_mix, w_in, b_f, g_qa, g_ka, g_qb, g_kb, w_o, g_ffn,
              w_router, b_router, w_gate_up, b_gate_up, w_down, b_down,
              g_ple, w_ple_gate, w_ple_proj):
    B, S, D = x.shape
    h = x
    for i in range(DEPTH):
        u = rms_norm(h, g_mix[i])
        z = u @ w_in[i]
        o0 = 0
        qa = z[..., o0:o0 + WIDTH_A].reshape(B, S, N_HEADS_A, HEAD_DIM); o0 += WIDTH_A
        ka = z[..., o0:o0 + WIDTH_A].reshape(B, S, N_HEADS_A, HEAD_DIM); o0 += WIDTH_A
        va = z[..., o0:o0 + WIDTH_A].reshape(B, S, N_HEADS_A, HEAD_DIM); o0 += WIDTH_A
        qb = z[..., o0:o0 + WIDTH_B].reshape(B, S, N_HEADS_B, HEAD_DIM); o0 += WIDTH_B
        kb = z[..., o0:o0 + WIDTH_B].reshape(B, S, N_HEADS_B, HEAD_DIM); o0 += WIDTH_B
        vb = z[..., o0:o0 + WIDTH_B].reshape(B, S, N_HEADS_B, HEAD_DIM); o0 += WIDTH_B
        f_logit = z[..., o0:o0 + N_HEADS_B].astype(jnp.float32) + b_f[i].astype(jnp.float32)

        out_a = longnet_mixture(rms_norm(qa, g_qa[i]), rms_norm(ka, g_ka[i]), va)
        out_b = forgetting_attention(rms_norm(qb, g_qb[i]), rms_norm(kb, g_kb[i]), vb,
                                     jax.nn.log_sigmoid(f_logit))
        mix = jnp.concatenate([out_a.reshape(B, S, WIDTH_A).astype(h.dtype),
                               out_b.reshape(B, S, WIDTH_B).astype(h.dtype)], axis=-1)
        h = h + mix @ w_o[i]

        h = h + moe_ffn(rms_norm(h, g_ffn[i]), w_router[i], b_router[i],
                        w_gate_up[i], b_gate_up[i], w_down[i], b_down[i])

        gate = jax.nn.sigmoid((rms_norm(h, g_ple[i]) @ w_ple_gate[i]).astype(jnp.float32))
        h = h + (gate * (p[i] @ w_ple_proj[i]).astype(jnp.float32)).astype(h.dtype)
    return h
```

```python
import functools

import jax
import jax.numpy as jnp
from jax import lax
from jax.experimental import pallas as pl
from jax.experimental.pallas import tpu as pltpu

HEAD_DIM = 64
N_HEADS_A = 8
N_HEADS_B = 8
WIDTH_A = N_HEADS_A * HEAD_DIM
WIDTH_B = N_HEADS_B * HEAD_DIM
DILATED_PATTERNS = ((128, 1), (512, 4), (2048, 16))
BLOCK = 128
N_EXPERTS = 32
TOP_K = 4
SWIGLU_LIMIT = 7.0
SWIGLU_ALPHA = 1.702
NORM_EPS = 1e-6

LANES = 128
PAIR = LANES // HEAD_DIM
ROW_TILE = 512
GMM_TILE = 512
VMEM_LIMIT = 56 * 1024 * 1024

F32 = jnp.float32
BF16 = jnp.bfloat16
NEG_INF = float("-inf")


def _cparams(*sem):
    return pltpu.CompilerParams(dimension_semantics=sem, vmem_limit_bytes=VMEM_LIMIT)


def _rms(x, g):
    return x * lax.rsqrt(jnp.mean(x * x, axis=-1, keepdims=True) + NORM_EPS) * g


def _in_proj_kernel(x_ref, g_ref, w_ref, wf_ref, bf_ref, gain_ref, bd_ref, z_ref, lf_ref):
    u = _rms(x_ref[...], g_ref[...]).astype(BF16)
    chunk = WIDTH_A
    normed = {0: 0, 1: 1, 3: 2, 4: 3}
    for c in range(6):
        acc = jnp.dot(u, w_ref[:, c * chunk:(c + 1) * chunk], preferred_element_type=F32)
        if c in normed:
            sq = (acc * acc).astype(BF16)
            half = chunk // 2
            ss = jnp.concatenate(
                [jnp.dot(sq[:, j * half:(j + 1) * half], bd_ref[...], preferred_element_type=F32)
                 for j in range(2)], axis=1)
            r = normed[c]
            acc = acc * lax.rsqrt(ss * (1.0 / HEAD_DIM) + NORM_EPS) * gain_ref[r:r + 1, :]
        z_ref[:, c * chunk:(c + 1) * chunk] = acc.astype(BF16)
    zf = jnp.dot(u, wf_ref[...], preferred_element_type=F32) + bf_ref[...]
    lf_ref[...] = jax.nn.log_sigmoid(zf)


def _in_proj(x2, g_mix, w_qkv, w_f, b_f, gains, bd):
    n, d = x2.shape
    cols = w_qkv.shape[1]
    tm = ROW_TILE
    const = lambda i: (0, 0)
    return pl.pallas_call(
        _in_proj_kernel,
        grid=(n // tm,),
        in_specs=[
            pl.BlockSpec((tm, d), lambda i: (i, 0)),
            pl.BlockSpec((1, d), const),
            pl.BlockSpec((d, cols), const),
            pl.BlockSpec((d, LANES), const),
            pl.BlockSpec((1, LANES), const),
            pl.BlockSpec(gains.shape, const),
            pl.BlockSpec(bd.shape, const),
        ],
        out_specs=[
            pl.BlockSpec((tm, cols), lambda i: (i, 0)),
            pl.BlockSpec((tm, LANES), lambda i: (i, 0)),
        ],
        out_shape=[
            jax.ShapeDtypeStruct((n, cols), BF16),
            jax.ShapeDtypeStruct((n, LANES), F32),
        ],
        compiler_params=_cparams("parallel"),
        name="in_proj",
    )(x2, g_mix, w_qkv, w_f, b_f, gains, bd)


def _cumsum_kernel(lf_ref, tri_ref, ccol_ref, crow_ref):
    s = lf_ref.shape[0]
    carry = jnp.zeros((1, LANES), F32)
    for blk in range(s // BLOCK):
        rows = slice(blk * BLOCK, (blk + 1) * BLOCK)
        part = jnp.dot(tri_ref[...], lf_ref[rows, :], precision=lax.Precision.HIGHEST,
                       preferred_element_type=F32) + carry
        ccol_ref[rows, :] = part
        crow_ref[0, :, rows] = part.T[:N_HEADS_B, :]
        carry = part[BLOCK - 1:BLOCK, :]


def _cumsum(logf, batch, seq):
    tri = (lax.broadcasted_iota(jnp.int32, (BLOCK, BLOCK), 0)
           >= lax.broadcasted_iota(jnp.int32, (BLOCK, BLOCK), 1)).astype(F32)
    return pl.pallas_call(
        _cumsum_kernel,
        grid=(batch,),
        in_specs=[
            pl.BlockSpec((seq, LANES), lambda b: (b, 0)),
            pl.BlockSpec((BLOCK, BLOCK), lambda b: (0, 0)),
        ],
        out_specs=[
            pl.BlockSpec((seq, LANES), lambda b: (b, 0)),
            pl.BlockSpec((1, N_HEADS_B, seq), lambda b: (b, 0, 0)),
        ],
        out_shape=[
            jax.ShapeDtypeStruct((batch * seq, LANES), F32),
            jax.ShapeDtypeStruct((batch, N_HEADS_B, seq), F32),
        ],
        compiler_params=_cparams("parallel"),
        name="cumsum",
    )(logf, tri)


def _head_lane_mask(h):
    lane = lax.broadcasted_iota(jnp.int32, (1, LANES), 1)
    return (lane >= h * HEAD_DIM) & (lane < (h + 1) * HEAD_DIM)


def _merge_heads(acc0, acc1):
    first = _head_lane_mask(0)
    num = jnp.where(first, acc0, acc1)
    den = pltpu.roll(jnp.where(first, acc1, acc0), HEAD_DIM, axis=1)
    return num, den


def _fox_kernel(q_ref, k_ref, v_ref, ccol_ref, crow_ref, o_ref, *, tile):
    pair = pl.program_id(1)
    i = pl.program_id(2)
    lane = lax.broadcasted_iota(jnp.int32, (1, LANES), 1)
    q = q_ref[...]
    ccol = ccol_ref[...]
    row = lax.broadcasted_iota(jnp.int32, (tile, tile), 0)
    col = lax.broadcasted_iota(jnp.int32, (tile, tile), 1)
    causal = col <= row
    accs = []
    for h in range(PAIR):
        in_h = _head_lane_mask(h)
        qh = jnp.where(in_h, q, jnp.zeros_like(q))
        cq = jnp.sum(jnp.where(lane == PAIR * pair + h, ccol, 0.0), axis=-1, keepdims=True)

        def step(j, carry, diag, h=h, in_h=in_h, qh=qh, cq=cq):
            m, acc = carry
            off = pl.multiple_of(j * tile, tile)
            k = k_ref[pl.ds(off, tile), :]
            v = v_ref[pl.ds(off, tile), :]
            vh = jnp.where(in_h, v, jnp.ones_like(v))
            ck = crow_ref[0, h:h + 1, pl.ds(off, tile)]
            s = lax.dot_general(qh, k, (((1,), (1,)), ((), ())), preferred_element_type=F32)
            s = s + (cq - ck)
            if diag:
                s = jnp.where(causal, s, NEG_INF)
            m_new = jnp.maximum(m, jnp.max(s, axis=-1, keepdims=True))
            alpha = jnp.exp(m - m_new)
            p = jnp.exp(s - m_new).astype(BF16)
            acc = alpha * acc + jnp.dot(p, vh, preferred_element_type=F32)
            return m_new, acc

        carry = step(i, (jnp.full((tile, 1), NEG_INF, F32), jnp.zeros((tile, LANES), F32)), True)
        _, acc = lax.fori_loop(0, i, lambda j, c: step(j, c, False), carry)
        accs.append(acc)
    num, den = _merge_heads(*accs)
    o_ref[...] = (num / den).astype(o_ref.dtype)


def _fox(z, ccol, crow2, batch, seq, tile=256):
    n = z.shape[0]
    nq = seq // tile
    npair = N_HEADS_B // PAIR
    base = 3 * WIDTH_A // LANES
    qcol, kcol, vcol = base, base + WIDTH_B // LANES, base + 2 * WIDTH_B // LANES
    return pl.pallas_call(
        functools.partial(_fox_kernel, tile=tile),
        grid=(batch, npair, nq),
        in_specs=[
            pl.BlockSpec((tile, LANES), lambda b, p, i: (b * nq + i, qcol + p)),
            pl.BlockSpec((seq, LANES), lambda b, p, i: (b, kcol + p)),
            pl.BlockSpec((seq, LANES), lambda b, p, i: (b, vcol + p)),
            pl.BlockSpec((tile, LANES), lambda b, p, i: (b * nq + i, 0)),
            pl.BlockSpec((1, PAIR, seq), lambda b, p, i: (b * npair + p, 0, 0)),
        ],
        out_specs=pl.BlockSpec((tile, LANES), lambda b, p, i: (b * nq + i, p)),
        out_shape=jax.ShapeDtypeStruct((n, WIDTH_B), BF16),
        compiler_params=_cparams("parallel", "parallel", "arbitrary"),
        name="fox",
    )(z, z, z, ccol, crow2)


def _dilated_kernel(slope_ref, q_ref, k_ref, v_ref, o_ref, nat, perm4, perm16, vals, dens, maxs, *, seq):
    pair = pl.program_id(1)
    first = _head_lane_mask(0)
    qi = lax.broadcasted_iota(jnp.int32, (BLOCK, 2 * BLOCK), 0)
    kj = lax.broadcasted_iota(jnp.int32, (BLOCK, 2 * BLOCK), 1)
    rel = qi + BLOCK - kj
    band = (rel >= 0) & (rel <= BLOCK)
    relf = rel.astype(F32)
    quarter = seq // 4

    def deinterleave(src, dst, t):
        def body(span_start, span):
            for r in range(4):
                dst[t, pl.ds(span_start + r * (span // 4), span // 4), :] = \
                    src[t, pl.ds(span_start + r, span // 4, stride=4), :]
        return body

    def interleave(src, dst, t):
        def body(span_start, span):
            for r in range(4):
                dst[t, pl.ds(span_start + r, span // 4, stride=4), :] = \
                    src[t, pl.ds(span_start + r * (span // 4), span // 4), :]
        return body

    for t, ref in enumerate((q_ref, k_ref, v_ref)):
        nat[t] = ref[...].astype(F32)
        deinterleave(nat, perm4, t)(0, seq)
        for r in range(4):
            deinterleave(perm4, perm16, t)(r * quarter, quarter)

    def run_pattern(p, src, dil, n_first):
        units = seq // BLOCK
        per_class = units // dil

        def unit(u, with_prev):
            start = pl.multiple_of(u * BLOCK, BLOCK)
            qb = src[0, pl.ds(start, BLOCK), :].astype(BF16)
            if with_prev:
                prev = pl.multiple_of(start - BLOCK, BLOCK)
                kk = src[1, pl.ds(prev, 2 * BLOCK), :].astype(BF16)
                vv = src[2, pl.ds(prev, 2 * BLOCK), :].astype(BF16)
            else:
                kk = src[1, pl.ds(start, BLOCK), :].astype(BF16)
                vv = src[2, pl.ds(start, BLOCK), :].astype(BF16)
            accs, ms = [], []
            for h in range(PAIR):
                in_h = _head_lane_mask(h)
                slope = slope_ref[PAIR * pair + h] * float(dil)
                qh = jnp.where(in_h, qb, jnp.zeros_like(qb))
                vh = jnp.where(in_h, vv, jnp.ones_like(vv))
                s = lax.dot_general(qh, kk, (((1,), (1,)), ((), ())), preferred_element_type=F32)
                if with_prev:
                    s = jnp.where(band, s - slope * relf, NEG_INF)
                else:
                    s = jnp.where(band[:, BLOCK:], s - slope * relf[:, BLOCK:], NEG_INF)
                m = jnp.max(s, axis=-1, keepdims=True)
                pr = jnp.exp(s - m).astype(BF16)
                accs.append(jnp.dot(pr, vh, preferred_element_type=F32))
                ms.append(m)
            num, den = _merge_heads(*accs)
            vals[p, pl.ds(start, BLOCK), :] = num
            dens[p, pl.ds(start, BLOCK), :] = den
            maxs[p, pl.ds(start, BLOCK), :] = jnp.where(first, ms[0], ms[1])

        def class_body(c, _):
            unit(c * per_class, False)
            if per_class > 1:
                lax.fori_loop(1, per_class, lambda b, __: (unit(c * per_class + b, True), 0)[1], 0)
            return 0
        lax.fori_loop(0, dil, class_body, 0)

    run_pattern(0, nat, 1, None)
    run_pattern(1, perm4, 4, None)
    run_pattern(2, perm16, 16, None)

    for arr in (vals, dens, maxs):
        for r in range(4):
            interleave(arr, perm16, 2)(r * quarter, quarter)
        interleave(perm16, arr, 2)(0, seq)
        perm4[1] = arr[1]
        interleave(perm4, arr, 1)(0, seq)

    m_all = jnp.maximum(jnp.maximum(maxs[0], maxs[1]), maxs[2])
    num = jnp.zeros((seq, LANES), F32)
    den = jnp.zeros((seq, LANES), F32)
    for p in range(3):
        e = jnp.exp(maxs[p] - m_all)
        num = num + e * vals[p]
        den = den + e * dens[p]
    o_ref[...] = (num / den).astype(o_ref.dtype)


def _dilated(z, slopes, batch, seq):
    n = z.shape[0]
    npair = N_HEADS_A // PAIR
    qcol, kcol, vcol = 0, WIDTH_A // LANES, 2 * WIDTH_A // LANES
    blk = lambda c0: pl.BlockSpec((seq, LANES), lambda b, p: (b, c0 + p))
    return pl.pallas_call(
        functools.partial(_dilated_kernel, seq=seq),
        grid=(batch, npair),
        in_specs=[pl.BlockSpec(memory_space=pltpu.SMEM), blk(qcol), blk(kcol), blk(vcol)],
        out_specs=pl.BlockSpec((seq, LANES), lambda b, p: (b, p)),
        out_shape=jax.ShapeDtypeStruct((n, WIDTH_A), BF16),
        scratch_shapes=[pltpu.VMEM((3, seq, LANES), F32) for _ in range(6)],
        compiler_params=_cparams("parallel", "parallel"),
        name="dilated",
    )(slopes, z, z, z)


def _pack_bf16_pairs(a, b):
    hi = pltpu.bitcast(a.astype(BF16).astype(F32), jnp.int32)
    lo = pltpu.bitcast(b.astype(BF16).astype(F32), jnp.int32)
    return (hi & jnp.int32(-65536)) | lax.shift_right_logical(lo, jnp.int32(16))


def _unpack_bf16_pairs(w):
    a = pltpu.bitcast(w & jnp.int32(-65536), F32)
    b = pltpu.bitcast(lax.shift_left(w, jnp.int32(16)), F32)
    return a, b


def _post_attn_kernel(ma_ref, mb_ref, x_ref, wo_ref, g_ref, wr_ref, br_ref, tri_ref,
                      h_ref, up_ref, idx_ref, gate_ref, rank_ref, cnt_ref, carry):
    @pl.when(pl.program_id(0) == 0)
    def _():
        carry[...] = jnp.zeros_like(carry)

    y = jnp.dot(ma_ref[...], wo_ref[:WIDTH_A, :], preferred_element_type=F32)
    y = y + jnp.dot(mb_ref[...], wo_ref[WIDTH_A:, :], preferred_element_type=F32)
    h = x_ref[...] + y
    h_ref[...] = h
    u = _rms(h, g_ref[...])
    half = u.shape[1] // 2
    up_ref[...] = _pack_bf16_pairs(u[:, :half], u[:, half:])

    logits = jnp.dot(u, wr_ref[...], precision=lax.Precision.HIGHEST,
                     preferred_element_type=F32) + br_ref[...]
    lane = lax.broadcasted_iota(jnp.int32, logits.shape, 1).astype(F32)
    work = logits
    idxs, tops = [], []
    for _ in range(TOP_K):
        top = jnp.max(work, axis=-1, keepdims=True)
        idx = jnp.min(jnp.where(work == top, lane, float(LANES)), axis=-1, keepdims=True)
        work = jnp.where(lane == idx, NEG_INF, work)
        idxs.append(idx)
        tops.append(top)
    exps = [jnp.exp(t - tops[0]) for t in tops]
    total = exps[0] + exps[1] + exps[2] + exps[3]

    onehot = jnp.zeros(logits.shape, F32)
    for idx in idxs:
        onehot = onehot + (lane == idx).astype(F32)
    before = jnp.dot(tri_ref[...], onehot.astype(BF16), preferred_element_type=F32) + carry[...]
    carry[...] = carry[...] + jnp.sum(onehot, axis=0, keepdims=True)
    cnt_ref[...] = carry[...]

    idx_out = jnp.zeros(logits.shape, F32)
    gate_out = jnp.zeros(logits.shape, F32)
    rank_out = jnp.zeros(logits.shape, F32)
    for k in range(TOP_K):
        rank_k = jnp.sum(jnp.where(lane == idxs[k], before, 0.0), axis=-1, keepdims=True)
        idx_out = jnp.where(lane == float(k), idxs[k], idx_out)
        gate_out = jnp.where(lane == float(k), exps[k] / total, gate_out)
        rank_out = jnp.where(lane == float(k), rank_k, rank_out)
    idx_ref[...] = idx_out.astype(jnp.int32)
    gate_ref[...] = gate_out
    rank_ref[...] = rank_out.astype(jnp.int32)


def _post_attn(mix_a, mix_b, x2, w_o, g_ffn, w_r, b_r):
    n, d = x2.shape
    tm = ROW_TILE
    tri = (lax.broadcasted_iota(jnp.int32, (tm, tm), 0)
           > lax.broadcasted_iota(jnp.int32, (tm, tm), 1)).astype(BF16)
    const = lambda i: (0, 0)
    row = lambda w: pl.BlockSpec((tm, w), lambda i: (i, 0))
    return pl.pallas_call(
        _post_attn_kernel,
        grid=(n // tm,),
        in_specs=[
            row(WIDTH_A), row(WIDTH_B), row(d),
            pl.BlockSpec(w_o.shape, const),
            pl.BlockSpec((1, d), const),
            pl.BlockSpec((d, LANES), const),
            pl.BlockSpec((1, LANES), const),
            pl.BlockSpec((tm, tm), const),
        ],
        out_specs=[row(d), row(d // 2), row(LANES), row(LANES), row(LANES),
                   pl.BlockSpec((1, LANES), const)],
        out_shape=[
            jax.ShapeDtypeStruct((n, d), F32),
            jax.ShapeDtypeStruct((n, d // 2), jnp.int32),
            jax.ShapeDtypeStruct((n, LANES), jnp.int32),
            jax.ShapeDtypeStruct((n, LANES), F32),
            jax.ShapeDtypeStruct((n, LANES), jnp.int32),
            jax.ShapeDtypeStruct((1, LANES), F32),
        ],
        scratch_shapes=[pltpu.VMEM((1, LANES), F32)],
        compiler_params=_cparams("arbitrary"),
        name="post_attn",
    )(mix_a, mix_b, x2, w_o, g_ffn, w_r, b_r, tri)


def _gmm_kernel(te_ref, used_ref, xs_ref, wg_ref, wl_ref, bg_ref, bl_ref, wd_ref, bd_ref, ys_ref):
    i = pl.program_id(0)

    @pl.when(used_ref[i] > 0)
    def _():
        a, b = _unpack_bf16_pairs(xs_ref[...])
        x = jnp.concatenate([a, b], axis=1).astype(BF16)
        hg = jnp.dot(x, wg_ref[0], preferred_element_type=F32) + bg_ref[0]
        hl = jnp.dot(x, wl_ref[0], preferred_element_type=F32) + bl_ref[0]
        xg = jnp.minimum(hg, SWIGLU_LIMIT)
        xl = jnp.clip(hl, -SWIGLU_LIMIT, SWIGLU_LIMIT)
        act = xg * jax.nn.sigmoid(SWIGLU_ALPHA * xg) * (xl + 1.0)
        ys_ref[...] = jnp.dot(act.astype(BF16), wd_ref[0], preferred_element_type=F32) + bd_ref[0]

    @pl.when(used_ref[i] == 0)
    def _():
        ys_ref[...] = jnp.zeros_like(ys_ref)


def _gmm(tile_expert, tile_used, xs, w_glu, w_lin, b_glu, b_lin, w_down, b_down):
    rows, half = xs.shape
    d = 2 * half
    de = w_glu.shape[2]
    tm = GMM_TILE
    wspec = lambda shape: pl.BlockSpec((1,) + shape, lambda i, te, used: (te[i], 0, 0))
    grid_spec = pltpu.PrefetchScalarGridSpec(
        num_scalar_prefetch=2,
        grid=(rows // tm,),
        in_specs=[
            pl.BlockSpec((tm, half), lambda i, te, used: (i, 0)),
            wspec((d, de)), wspec((d, de)), wspec((1, de)), wspec((1, de)),
            wspec((de, d)), wspec((1, d)),
        ],
        out_specs=pl.BlockSpec((tm, d), lambda i, te, used: (i, 0)),
    )
    return pl.pallas_call(
        _gmm_kernel,
        grid_spec=grid_spec,
        out_shape=jax.ShapeDtypeStruct((rows, d), F32),
        compiler_params=_cparams("arbitrary"),
        name="gmm",
    )(tile_expert, tile_used, xs, w_glu, w_lin, b_glu, b_lin, w_down, b_down)


def _final_kernel(h_ref, yk_ref, gate_ref, p_ref, g_ref, wg_ref, wp_ref, o_ref):
    gates = gate_ref[...]
    h = h_ref[...]
    for k in range(TOP_K):
        h = h + gates[:, k:k + 1] * yk_ref[k]
    u = _rms(h, g_ref[...]).astype(BF16)
    gate = jax.nn.sigmoid(jnp.dot(u, wg_ref[...], preferred_element_type=F32))
    proj = jnp.dot(p_ref[...].astype(BF16), wp_ref[...], preferred_element_type=F32)
    o_ref[...] = h + gate * proj


def _final(h1, yk, gates, p2, g_ple, w_gate, w_proj):
    n, d = h1.shape
    tm = ROW_TILE
    const = lambda i: (0, 0)
    return pl.pallas_call(
        _final_kernel,
        grid=(n // tm,),
        in_specs=[
            pl.BlockSpec((tm, d), lambda i: (i, 0)),
            pl.BlockSpec((TOP_K, tm, d), lambda i: (0, i, 0)),
            pl.BlockSpec((tm, LANES), lambda i: (i, 0)),
            pl.BlockSpec((tm, p2.shape[1]), lambda i: (i, 0)),
            pl.BlockSpec((1, d), const),
            pl.BlockSpec(w_gate.shape, const),
            pl.BlockSpec(w_proj.shape, const),
        ],
        out_specs=pl.BlockSpec((tm, d), lambda i: (i, 0)),
        out_shape=jax.ShapeDtypeStruct((n, d), F32),
        compiler_params=_cparams("parallel"),
        name="final",
    )(h1, yk, gates, p2, g_ple, w_gate, w_proj)


def _layer(h, p, g_mix, w_in, b_f, g_qa, g_ka, g_qb, g_kb, w_o, g_ffn, w_router, b_router,
           w_gate_up, b_gate_up, w_down, b_down, g_ple, w_ple_gate, w_ple_proj):
    batch, seq, d = h.shape
    n = batch * seq
    for window, dil in DILATED_PATTERNS:
        assert window // dil == BLOCK and seq % (dil * BLOCK) == 0
    assert n % ROW_TILE == 0 and d % (2 * LANES) == 0
    x2 = h.reshape(n, d)

    qkv_cols = 3 * WIDTH_A + 3 * WIDTH_B
    w_qkv = w_in[:, :qkv_cols].astype(BF16)
    w_f = jnp.pad(w_in[:, qkv_cols:], ((0, 0), (0, LANES - N_HEADS_B))).astype(BF16)
    b_fp = jnp.pad(b_f.astype(F32), (0, LANES - N_HEADS_B)).reshape(1, LANES)
    scale = HEAD_DIM ** -0.5
    gains = jnp.stack([jnp.tile(g_qa, N_HEADS_A) * scale, jnp.tile(g_ka, N_HEADS_A),
                       jnp.tile(g_qb, N_HEADS_B) * scale, jnp.tile(g_kb, N_HEADS_B)]).astype(F32)
    hid = jnp.arange(2 * LANES) // HEAD_DIM
    bd = (hid[:, None] == hid[None, :]).astype(BF16)

    z, logf = _in_proj(x2, g_mix.reshape(1, d), w_qkv, w_f, b_fp, gains, bd)
    ccol, crow = _cumsum(logf, batch, seq)
    crow2 = crow.reshape(batch * (N_HEADS_B // PAIR), PAIR, seq)

    slopes = 2.0 ** (-8.0 * jnp.arange(1, N_HEADS_A + 1, dtype=F32) / N_HEADS_A)
    mix_a = _dilated(z, slopes, batch, seq)
    mix_b = _fox(z, ccol, crow2, batch, seq)

    w_r = jnp.pad(w_router.astype(F32), ((0, 0), (0, LANES - N_EXPERTS)))
    b_r = jnp.concatenate([b_router.astype(F32), jnp.full((LANES - N_EXPERTS,), NEG_INF, F32)]).reshape(1, LANES)
    h1, u_packed, top_idx, gates, rank, counts = _post_attn(
        mix_a, mix_b, x2, w_o.astype(BF16), g_ffn.reshape(1, d), w_r, b_r)

    counts = counts[0, :N_EXPERTS].astype(jnp.int32)
    tiles_per = (counts + GMM_TILE - 1) // GMM_TILE
    tile_end = jnp.cumsum(tiles_per)
    starts = (tile_end - tiles_per) * GMM_TILE
    n_tiles = n * TOP_K // GMM_TILE + N_EXPERTS
    tile_ids = jnp.arange(n_tiles, dtype=jnp.int32)
    tile_used = (tile_ids < tile_end[-1]).astype(jnp.int32)
    tile_expert = jnp.searchsorted(tile_end, jnp.minimum(tile_ids, tile_end[-1] - 1), side="right")
    tile_expert = jnp.minimum(tile_expert, N_EXPERTS - 1).astype(jnp.int32)
    idx4 = top_idx[:, :TOP_K]
    pos = starts[idx4] + rank[:, :TOP_K]
    src = jnp.zeros((n_tiles * GMM_TILE,), jnp.int32).at[pos.reshape(-1)].set(
        jnp.repeat(jnp.arange(n, dtype=jnp.int32), TOP_K))
    xs = u_packed[src]

    de = w_down.shape[1]
    ys = _gmm(tile_expert, tile_used, xs,
              w_gate_up[:, :, 0::2].astype(BF16), w_gate_up[:, :, 1::2].astype(BF16),
              b_gate_up[:, 0::2].reshape(N_EXPERTS, 1, de).astype(F32),
              b_gate_up[:, 1::2].reshape(N_EXPERTS, 1, de).astype(F32),
              w_down.astype(BF16), b_down.reshape(N_EXPERTS, 1, d).astype(F32))
    yk = ys[pos.T]

    out = _final(h1, yk, gates, p.reshape(n, -1), g_ple.reshape(1, d),
                 w_ple_gate.astype(BF16), w_ple_proj.astype(BF16))
    return out.reshape(batch, seq, d)


def kernel(x, p, g_mix, w_in, b_f, g_qa, g_ka, g_qb, g_kb, w_o, g_ffn, w_router, b_router,
           w_gate_up, b_gate_up, w_down, b_down, g_ple, w_ple_gate, w_ple_proj):
    h = x
    for i in range(g_mix.shape[0]):
        h = _layer(h, p[i], g_mix[i], w_in[i], b_f[i], g_qa[i], g_ka[i], g_qb[i], g_kb[i], w_o[i],
                   g_ffn[i], w_router[i], b_router[i], w_gate_up[i], b_gate_up[i], w_down[i],
                   b_down[i], g_ple[i], w_ple_gate[i], w_ple_proj[i])
    return h
```

```python
import functools

import jax
import jax.numpy as jnp
from jax import lax
from jax.experimental import pallas as pl
from jax.experimental.pallas import tpu as pltpu

HEAD_DIM = 64
N_HEADS_A = 8
N_HEADS_B = 8
WIDTH_A = N_HEADS_A * HEAD_DIM
WIDTH_B = N_HEADS_B * HEAD_DIM
DILATED_PATTERNS = ((128, 1), (512, 4), (2048, 16))
BLOCK = 128
N_EXPERTS = 32
TOP_K = 4
SWIGLU_LIMIT = 7.0
SWIGLU_ALPHA = 1.702
NORM_EPS = 1e-6

LANES = 128
PAIR = LANES // HEAD_DIM
ROW_TILE = 512
GMM_TILE = 512
FOX_TILE = 512
UNITS_PER_STEP = 8
VMEM_LIMIT = 56 * 1024 * 1024

F32 = jnp.float32
BF16 = jnp.bfloat16
NEG_INF = float("-inf")
NT_DIMS = (((1,), (1,)), ((), ()))


def _cparams(*sem):
    return pltpu.CompilerParams(dimension_semantics=sem, vmem_limit_bytes=VMEM_LIMIT)


def _rms(x, g):
    return x * lax.rsqrt(jnp.mean(x * x, axis=-1, keepdims=True) + NORM_EPS) * g


def _lane_iota():
    return lax.broadcasted_iota(jnp.int32, (1, LANES), 1)


def _head_lane_mask(h):
    lane = _lane_iota()
    return (lane >= h * HEAD_DIM) & (lane < (h + 1) * HEAD_DIM)


def _merge_heads(acc0, acc1):
    first = _head_lane_mask(0)
    num = jnp.where(first, acc0, acc1)
    den = pltpu.roll(jnp.where(first, acc1, acc0), HEAD_DIM, axis=1)
    return num, den


def _in_proj_kernel(x_ref, g_ref, w_ref, wf_ref, bf_ref, gain_ref, bd_ref, z_ref, lf_ref):
    u = _rms(x_ref[...], g_ref[...]).astype(BF16)
    chunk = WIDTH_A
    normed = {0: 0, 1: 1, 3: 2, 4: 3}
    for c in range(6):
        acc = jnp.dot(u, w_ref[:, c * chunk:(c + 1) * chunk], preferred_element_type=F32)
        if c in normed:
            sq = (acc * acc).astype(BF16)
            half = chunk // 2
            ss = jnp.concatenate(
                [jnp.dot(sq[:, j * half:(j + 1) * half], bd_ref[...], preferred_element_type=F32)
                 for j in range(2)], axis=1)
            r = normed[c]
            acc = acc * lax.rsqrt(ss * (1.0 / HEAD_DIM) + NORM_EPS) * gain_ref[r:r + 1, :]
        z_ref[:, c * chunk:(c + 1) * chunk] = acc.astype(BF16)
    zf = jnp.dot(u, wf_ref[...], preferred_element_type=F32) + bf_ref[...]
    lf_ref[...] = jax.nn.log_sigmoid(zf)


def _in_proj(x2, g_mix, w_qkv, w_f, b_f, gains, bd):
    n, d = x2.shape
    cols = w_qkv.shape[1]
    tm = ROW_TILE
    const = lambda i: (0, 0)
    return pl.pallas_call(
        _in_proj_kernel,
        grid=(n // tm,),
        in_specs=[
            pl.BlockSpec((tm, d), lambda i: (i, 0)),
            pl.BlockSpec((1, d), const),
            pl.BlockSpec((d, cols), const),
            pl.BlockSpec((d, LANES), const),
            pl.BlockSpec((1, LANES), const),
            pl.BlockSpec(gains.shape, const),
            pl.BlockSpec(bd.shape, const),
        ],
        out_specs=[
            pl.BlockSpec((tm, cols), lambda i: (i, 0)),
            pl.BlockSpec((tm, LANES), lambda i: (i, 0)),
        ],
        out_shape=[
            jax.ShapeDtypeStruct((n, cols), BF16),
            jax.ShapeDtypeStruct((n, LANES), F32),
        ],
        compiler_params=_cparams("parallel"),
        name="in_proj",
    )(x2, g_mix, w_qkv, w_f, b_f, gains, bd)


def _cumsum_kernel(lf_ref, tri_ref, ccol_ref):
    s = lf_ref.shape[0]
    carry = jnp.zeros((1, LANES), F32)
    for blk in range(s // BLOCK):
        rows = slice(blk * BLOCK, (blk + 1) * BLOCK)
        part = jnp.dot(tri_ref[...], lf_ref[rows, :], precision=lax.Precision.HIGHEST,
                       preferred_element_type=F32) + carry
        ccol_ref[rows, :] = part
        carry = part[BLOCK - 1:BLOCK, :]


def _cumsum(logf, batch, seq):
    tri = (lax.broadcasted_iota(jnp.int32, (BLOCK, BLOCK), 0)
           >= lax.broadcasted_iota(jnp.int32, (BLOCK, BLOCK), 1)).astype(F32)
    return pl.pallas_call(
        _cumsum_kernel,
        grid=(batch,),
        in_specs=[
            pl.BlockSpec((seq, LANES), lambda b: (b, 0)),
            pl.BlockSpec((BLOCK, BLOCK), lambda b: (0, 0)),
        ],
        out_specs=pl.BlockSpec((seq, LANES), lambda b: (b, 0)),
        out_shape=jax.ShapeDtypeStruct((batch * seq, LANES), F32),
        compiler_params=_cparams("parallel"),
        name="cumsum",
    )(logf, tri)


def _split3(c):
    hi = c.astype(BF16)
    r1 = c - hi.astype(F32)
    mid = r1.astype(BF16)
    lo = (r1 - mid.astype(F32)).astype(BF16)
    return hi, mid, lo


def _fox_operand(x, c, h, key_side):
    lane = _lane_iota()
    base = HEAD_DIM * (1 - h)
    hi, mid, lo = (piece.astype(F32) for piece in _split3(-c if key_side else c))
    own = (base + 3) if key_side else base
    other = base if key_side else (base + 3)
    feat = jnp.where(lane == own, hi, jnp.where(lane == own + 1, mid, jnp.where(lane == own + 2, lo,
           jnp.where((lane >= other) & (lane < other + 3), 1.0, 0.0))))
    return jnp.where(_head_lane_mask(h), x, feat.astype(BF16))


def _fox_kernel(q_ref, k_ref, v_ref, c_ref, o_ref, kf, vf, s_scr, m_scr, acc_scr, *, tile):
    pair = pl.program_id(1)
    i = pl.program_id(2)
    lane = _lane_iota()

    def gate_column(c_tile, h):
        return jnp.sum(jnp.where(lane == PAIR * pair + h, c_tile, 0.0), axis=-1, keepdims=True)

    @pl.when(i == 0)
    def _():
        c_all = c_ref[...]
        for h in range(PAIR):
            kf[h] = _fox_operand(k_ref[...], gate_column(c_all, h), h, True)
            vf[h] = jnp.where(_head_lane_mask(h), v_ref[...], jnp.ones_like(v_ref[...]))

    row0 = pl.multiple_of(i * tile, tile)
    c_tile = c_ref[pl.ds(row0, tile), :]
    q = q_ref[...]
    qf = [_fox_operand(q, gate_column(c_tile, h), h, False) for h in range(PAIR)]

    def lane_groups_max(s):
        m = s[:, :LANES]
        for g in range(1, tile // LANES):
            m = jnp.maximum(m, s[:, g * LANES:(g + 1) * LANES])
        return m

    def scores(j, h):
        off = pl.multiple_of(j * tile, tile)
        return lax.dot_general(qf[h], kf[h, pl.ds(off, tile), :], NT_DIMS, preferred_element_type=F32)

    row = lax.broadcasted_iota(jnp.int32, (tile, tile), 0)
    col = lax.broadcasted_iota(jnp.int32, (tile, tile), 1)
    for h in range(PAIR):
        s = jnp.where(col <= row, scores(i, h), NEG_INF)
        s_scr[h, :, pl.ds(row0, tile)] = s
        m_scr[h] = lane_groups_max(s)

    def pass1(j, _):
        off = pl.multiple_of(j * tile, tile)
        for h in range(PAIR):
            s = scores(j, h)
            s_scr[h, :, pl.ds(off, tile)] = s
            m_scr[h] = jnp.maximum(m_scr[h], lane_groups_max(s))
        return 0
    lax.fori_loop(0, i, pass1, 0)

    m = [jnp.max(m_scr[h], axis=-1, keepdims=True) for h in range(PAIR)]
    acc_scr[...] = jnp.zeros_like(acc_scr)

    def pass2(j, _):
        off = pl.multiple_of(j * tile, tile)
        for h in range(PAIR):
            p = jnp.exp(s_scr[h, :, pl.ds(off, tile)] - m[h]).astype(BF16)
            acc_scr[h] += jnp.dot(p, vf[h, pl.ds(off, tile), :], preferred_element_type=F32)
        return 0
    lax.fori_loop(0, i + 1, pass2, 0)

    num, den = _merge_heads(acc_scr[0], acc_scr[1])
    o_ref[...] = (num / den).astype(o_ref.dtype)


def _fox(z, ccol, batch, seq):
    n = z.shape[0]
    tile = FOX_TILE
    nq = seq // tile
    npair = N_HEADS_B // PAIR
    base = 3 * WIDTH_A // LANES
    qcol, kcol, vcol = base, base + WIDTH_B // LANES, base + 2 * WIDTH_B // LANES
    return pl.pallas_call(
        functools.partial(_fox_kernel, tile=tile),
        grid=(batch, npair, nq),
        in_specs=[
            pl.BlockSpec((tile, LANES), lambda b, p, i: (b * nq + i, qcol + p)),
            pl.BlockSpec((seq, LANES), lambda b, p, i: (b, kcol + p)),
            pl.BlockSpec((seq, LANES), lambda b, p, i: (b, vcol + p)),
            pl.BlockSpec((seq, LANES), lambda b, p, i: (b, 0)),
        ],
        out_specs=pl.BlockSpec((tile, LANES), lambda b, p, i: (b * nq + i, p)),
        out_shape=jax.ShapeDtypeStruct((n, WIDTH_B), BF16),
        scratch_shapes=[
            pltpu.VMEM((PAIR, seq, LANES), BF16),
            pltpu.VMEM((PAIR, seq, LANES), BF16),
            pltpu.VMEM((PAIR, tile, seq), F32),
            pltpu.VMEM((PAIR, tile, LANES), F32),
            pltpu.VMEM((PAIR, tile, LANES), F32),
        ],
        compiler_params=_cparams("parallel", "parallel", "arbitrary"),
        name="fox",
    )(z, z, z, ccol)


def _dilated_kernel(slope_ref, q_ref, k_ref, v_ref, o_ref,
                    natf, p4f, p4b, p16b, qfeat, kfeat, vals, dens, maxs, *, seq):
    pair = pl.program_id(1)
    lane = _lane_iota()
    first = _head_lane_mask(0)
    quarter = seq // 4
    units = seq // BLOCK

    def deinterleave(src, t, span_start, span):
        return [src[t, pl.ds(span_start + r, span // 4, stride=4), :] for r in range(4)]

    for t, ref in enumerate((q_ref, k_ref, v_ref)):
        natf[t] = ref[...].astype(F32)
        for r, part in enumerate(deinterleave(natf, t, 0, seq)):
            p4f[t, pl.ds(r * quarter, quarter), :] = part
            p4b[t, pl.ds(r * quarter, quarter), :] = part.astype(BF16)
        for r4 in range(4):
            for r, part in enumerate(deinterleave(p4f, t, r4 * quarter, quarter)):
                p16b[t, pl.ds(r4 * quarter + r * (quarter // 4), quarter // 4), :] = part.astype(BF16)

    qi = lax.broadcasted_iota(jnp.int32, (BLOCK, LANES), 0).astype(F32)
    kj = lax.broadcasted_iota(jnp.int32, (2 * BLOCK, LANES), 0).astype(F32)
    for p, (_, dil) in enumerate(DILATED_PATTERNS):
        for h in range(PAIR):
            sd = slope_ref[PAIR * pair + h] * float(dil)
            base = HEAD_DIM * (1 - h)
            qfeat[p * PAIR + h] = jnp.where(lane == base, -(qi + float(BLOCK)) * sd,
                                            jnp.where(lane == base + 1, 1.0, 0.0)).astype(BF16)
            kfeat[p * PAIR + h] = jnp.where(lane == base, 1.0,
                                            jnp.where(lane == base + 1, kj * sd, 0.0)).astype(BF16)

    bq = lax.broadcasted_iota(jnp.int32, (BLOCK, 2 * BLOCK), 0)
    bk = lax.broadcasted_iota(jnp.int32, (BLOCK, 2 * BLOCK), 1)
    rel = bq + BLOCK - bk
    band = (rel >= 0) & (rel <= BLOCK)

    def unit(p, srcs, u, prev_valid):
        qs, ks, vs = srcs
        start = pl.multiple_of(u * BLOCK, BLOCK)
        prev = pl.multiple_of(jnp.maximum(start - BLOCK, 0), BLOCK)
        qb = qs[pl.ds(start, BLOCK), :]
        kk = jnp.concatenate([ks[pl.ds(prev, BLOCK), :], ks[pl.ds(start, BLOCK), :]], axis=0)
        vv = jnp.concatenate([vs[pl.ds(prev, BLOCK), :], vs[pl.ds(start, BLOCK), :]], axis=0)
        if prev_valid is True:
            ok = band
        elif prev_valid is False:
            ok = band & (bk >= BLOCK)
        else:
            ok = band & ((bk >= BLOCK) | prev_valid)
        accs, ms = [], []
        for h in range(PAIR):
            in_h = _head_lane_mask(h)
            qh = jnp.where(in_h, qb, qfeat[p * PAIR + h])
            kh = jnp.where(in_h, kk, kfeat[p * PAIR + h])
            vh = jnp.where(in_h, vv, jnp.ones_like(vv))
            s = lax.dot_general(qh, kh, NT_DIMS, preferred_element_type=F32)
            s = jnp.where(ok, s, NEG_INF)
            m = jnp.max(s, axis=-1, keepdims=True)
            pr = jnp.exp(s - m).astype(BF16)
            accs.append(jnp.dot(pr, vh, preferred_element_type=F32))
            ms.append(m)
        num, den = _merge_heads(*accs)
        vals[p, pl.ds(start, BLOCK), :] = num
        dens[p, pl.ds(start, BLOCK), :] = den
        maxs[p, pl.ds(start, BLOCK), :] = jnp.where(first, ms[0], ms[1])

    group = UNITS_PER_STEP
    sources = ((q_ref, k_ref, v_ref), tuple(p4b.at[t] for t in range(3)), tuple(p16b.at[t] for t in range(3)))
    for p, (_, dil) in enumerate(DILATED_PATTERNS):
        per_class = units // dil

        def step(g, _, p=p, per_class=per_class):
            for e in range(group):
                u = g * group + e
                if per_class >= group:
                    prev_valid = (u % per_class != 0) if e == 0 else True
                else:
                    prev_valid = e % per_class != 0
                unit(p, sources[p], u, prev_valid)
            return 0
        lax.fori_loop(0, units // group, step, 0)

    def interleave(src, s_t, dst, d_t, span_start, span):
        for r in range(4):
            dst[d_t, pl.ds(span_start + r, span // 4, stride=4), :] = \
                src[s_t, pl.ds(span_start + r * (span // 4), span // 4), :]

    for t, arr in enumerate((vals, dens, maxs)):
        for r4 in range(4):
            interleave(arr, 2, p4f, t, r4 * quarter, quarter)
        interleave(p4f, t, arr, 2, 0, seq)
        natf[t] = arr[1]
        interleave(natf, t, arr, 1, 0, seq)

    m_all = jnp.maximum(jnp.maximum(maxs[0], maxs[1]), maxs[2])
    num = jnp.zeros((seq, LANES), F32)
    den = jnp.zeros((seq, LANES), F32)
    for p in range(3):
        e = jnp.exp(maxs[p] - m_all)
        num = num + e * vals[p]
        den = den + e * dens[p]
    o_ref[...] = (num / den).astype(o_ref.dtype)


def _dilated(z, slopes, batch, seq):
    n = z.shape[0]
    npair = N_HEADS_A // PAIR
    npat = len(DILATED_PATTERNS)
    qcol, kcol, vcol = 0, WIDTH_A // LANES, 2 * WIDTH_A // LANES
    blk = lambda c0: pl.BlockSpec((seq, LANES), lambda b, p: (b, c0 + p))
    return pl.pallas_call(
        functools.partial(_dilated_kernel, seq=seq),
        grid=(batch, npair),
        in_specs=[pl.BlockSpec(memory_space=pltpu.SMEM), blk(qcol), blk(kcol), blk(vcol)],
        out_specs=pl.BlockSpec((seq, LANES), lambda b, p: (b, p)),
        out_shape=jax.ShapeDtypeStruct((n, WIDTH_A), BF16),
        scratch_shapes=[
            pltpu.VMEM((3, seq, LANES), F32),
            pltpu.VMEM((3, seq, LANES), F32),
            pltpu.VMEM((3, seq, LANES), BF16),
            pltpu.VMEM((3, seq, LANES), BF16),
            pltpu.VMEM((npat * PAIR, BLOCK, LANES), BF16),
            pltpu.VMEM((npat * PAIR, 2 * BLOCK, LANES), BF16),
            pltpu.VMEM((npat, seq, LANES), F32),
            pltpu.VMEM((npat, seq, LANES), F32),
            pltpu.VMEM((npat, seq, LANES), F32),
        ],
        compiler_params=_cparams("parallel", "parallel"),
        name="dilated",
    )(slopes, z, z, z)


def _pack_bf16_pairs(a, b):
    hi = pltpu.bitcast(a.astype(BF16).astype(F32), jnp.int32)
    lo = pltpu.bitcast(b.astype(BF16).astype(F32), jnp.int32)
    return (hi & jnp.int32(-65536)) | lax.shift_right_logical(lo, jnp.int32(16))


def _unpack_bf16_pairs(w):
    a = pltpu.bitcast(w & jnp.int32(-65536), F32)
    b = pltpu.bitcast(lax.shift_left(w, jnp.int32(16)), F32)
    return a, b


def _post_attn_kernel(ma_ref, mb_ref, x_ref, wo_ref, g_ref, wr_ref, br_ref, tri_ref,
                      h_ref, up_ref, idx_ref, gate_ref, rank_ref, cnt_ref, carry):
    @pl.when(pl.program_id(0) == 0)
    def _():
        carry[...] = jnp.zeros_like(carry)

    y = jnp.dot(ma_ref[...], wo_ref[:WIDTH_A, :], preferred_element_type=F32)
    y = y + jnp.dot(mb_ref[...], wo_ref[WIDTH_A:, :], preferred_element_type=F32)
    h = x_ref[...] + y
    h_ref[...] = h
    u = _rms(h, g_ref[...])
    half = u.shape[1] // 2
    up_ref[...] = _pack_bf16_pairs(u[:, :half], u[:, half:])

    logits = jnp.dot(u, wr_ref[...], precision=lax.Precision.HIGHEST,
                     preferred_element_type=F32) + br_ref[...]
    lane = lax.broadcasted_iota(jnp.int32, logits.shape, 1).astype(F32)
    work = logits
    idxs, tops = [], []
    for _ in range(TOP_K):
        top = jnp.max(work, axis=-1, keepdims=True)
        idx = jnp.min(jnp.where(work == top, lane, float(LANES)), axis=-1, keepdims=True)
        work = jnp.where(lane == idx, NEG_INF, work)
        idxs.append(idx)
        tops.append(top)
    exps = [jnp.exp(t - tops[0]) for t in tops]
    total = exps[0] + exps[1] + exps[2] + exps[3]

    onehot = jnp.zeros(logits.shape, F32)
    for idx in idxs:
        onehot = onehot + (lane == idx).astype(F32)
    before = jnp.dot(tri_ref[...], onehot.astype(BF16), preferred_element_type=F32) + carry[...]
    carry[...] = carry[...] + jnp.sum(onehot, axis=0, keepdims=True)
    cnt_ref[...] = carry[...]

    idx_out = jnp.zeros(logits.shape, F32)
    gate_out = jnp.zeros(logits.shape, F32)
    rank_out = jnp.zeros(logits.shape, F32)
    for k in range(TOP_K):
        rank_k = jnp.sum(jnp.where(lane == idxs[k], before, 0.0), axis=-1, keepdims=True)
        idx_out = jnp.where(lane == float(k), idxs[k], idx_out)
        gate_out = jnp.where(lane == float(k), exps[k] / total, gate_out)
        rank_out = jnp.where(lane == float(k), rank_k, rank_out)
    idx_ref[...] = idx_out.astype(jnp.int32)
    gate_ref[...] = gate_out
    rank_ref[...] = rank_out.astype(jnp.int32)


def _post_attn(mix_a, mix_b, x2, w_o, g_ffn, w_r, b_r):
    n, d = x2.shape
    tm = ROW_TILE
    tri = (lax.broadcasted_iota(jnp.int32, (tm, tm), 0)
           > lax.broadcasted_iota(jnp.int32, (tm, tm), 1)).astype(BF16)
    const = lambda i: (0, 0)
    row = lambda w: pl.BlockSpec((tm, w), lambda i: (i, 0))
    return pl.pallas_call(
        _post_attn_kernel,
        grid=(n // tm,),
        in_specs=[
            row(WIDTH_A), row(WIDTH_B), row(d),
            pl.BlockSpec(w_o.shape, const),
            pl.BlockSpec((1, d), const),
            pl.BlockSpec((d, LANES), const),
            pl.BlockSpec((1, LANES), const),
            pl.BlockSpec((tm, tm), const),
        ],
        out_specs=[row(d), row(d // 2), row(LANES), row(LANES), row(LANES),
                   pl.BlockSpec((1, LANES), const)],
        out_shape=[
            jax.ShapeDtypeStruct((n, d), F32),
            jax.ShapeDtypeStruct((n, d // 2), jnp.int32),
            jax.ShapeDtypeStruct((n, LANES), jnp.int32),
            jax.ShapeDtypeStruct((n, LANES), F32),
            jax.ShapeDtypeStruct((n, LANES), jnp.int32),
            jax.ShapeDtypeStruct((1, LANES), F32),
        ],
        scratch_shapes=[pltpu.VMEM((1, LANES), F32)],
        compiler_params=_cparams("arbitrary"),
        name="post_attn",
    )(mix_a, mix_b, x2, w_o, g_ffn, w_r, b_r, tri)


def _wprep_kernel(w_ref, wg_ref, wl_ref, wt):
    d, cols = w_ref.shape[1:]
    de = cols // 2
    for j in range(d // LANES):
        lanes = slice(j * LANES, (j + 1) * LANES)
        wt[j] = w_ref[0, lanes, :].T
        wg_ref[0, :, lanes] = wt[j, pl.ds(0, de, stride=2), :].astype(BF16)
        wl_ref[0, :, lanes] = wt[j, pl.ds(1, de, stride=2), :].astype(BF16)


def _wprep(w_gate_up):
    ne, d, cols = w_gate_up.shape
    de = cols // 2
    out = pl.BlockSpec((1, de, d), lambda e: (e, 0, 0))
    return pl.pallas_call(
        _wprep_kernel,
        grid=(ne,),
        in_specs=[pl.BlockSpec((1, d, cols), lambda e: (e, 0, 0))],
        out_specs=[out, out],
        out_shape=[jax.ShapeDtypeStruct((ne, de, d), BF16)] * 2,
        scratch_shapes=[pltpu.VMEM((d // LANES, cols, LANES), F32)],
        compiler_params=_cparams("parallel"),
        name="wprep",
    )(w_gate_up)


def _gmm_kernel(te_ref, used_ref, xs_ref, wg_ref, wl_ref, bg_ref, bl_ref, wd_ref, bd_ref, ys_ref):
    i = pl.program_id(0)

    @pl.when(used_ref[i] > 0)
    def _():
        a, b = _unpack_bf16_pairs(xs_ref[...])
        x = jnp.concatenate([a, b], axis=1).astype(BF16)
        hg = lax.dot_general(x, wg_ref[0], NT_DIMS, preferred_element_type=F32) + bg_ref[0]
        hl = lax.dot_general(x, wl_ref[0], NT_DIMS, preferred_element_type=F32) + bl_ref[0]
        xg = jnp.minimum(hg, SWIGLU_LIMIT)
        xl = jnp.clip(hl, -SWIGLU_LIMIT, SWIGLU_LIMIT)
        act = xg * jax.nn.sigmoid(SWIGLU_ALPHA * xg) * (xl + 1.0)
        ys_ref[...] = jnp.dot(act.astype(BF16), wd_ref[0].astype(BF16),
                              preferred_element_type=F32) + bd_ref[0]

    @pl.when(used_ref[i] == 0)
    def _():
        ys_ref[...] = jnp.zeros_like(ys_ref)


def _gmm(tile_expert, tile_used, xs, wg_t, wl_t, b_glu, b_lin, w_down, b_down):
    rows, half = xs.shape
    d = 2 * half
    de = wg_t.shape[1]
    tm = GMM_TILE
    wspec = lambda shape: pl.BlockSpec((1,) + shape, lambda i, te, used: (te[i], 0, 0))
    grid_spec = pltpu.PrefetchScalarGridSpec(
        num_scalar_prefetch=2,
        grid=(rows // tm,),
        in_specs=[
            pl.BlockSpec((tm, half), lambda i, te, used: (i, 0)),
            wspec((de, d)), wspec((de, d)), wspec((1, de)), wspec((1, de)),
            wspec((de, d)), wspec((1, d)),
        ],
        out_specs=pl.BlockSpec((tm, d), lambda i, te, used: (i, 0)),
    )
    return pl.pallas_call(
        _gmm_kernel,
        grid_spec=grid_spec,
        out_shape=jax.ShapeDtypeStruct((rows, d), F32),
        compiler_params=_cparams("arbitrary"),
        name="gmm",
    )(tile_expert, tile_used, xs, wg_t, wl_t, b_glu, b_lin, w_down, b_down)


def _final_kernel(h_ref, yk_ref, gate_ref, p_ref, g_ref, wg_ref, wp_ref, o_ref):
    gates = gate_ref[...]
    h = h_ref[...]
    for k in range(TOP_K):
        h = h + gates[:, k:k + 1] * yk_ref[k]
    u = _rms(h, g_ref[...]).astype(BF16)
    gate = jax.nn.sigmoid(jnp.dot(u, wg_ref[...], preferred_element_type=F32))
    proj = jnp.dot(p_ref[...].astype(BF16), wp_ref[...], preferred_element_type=F32)
    o_ref[...] = h + gate * proj


def _final(h1, yk, gates, p2, g_ple, w_gate, w_proj):
    n, d = h1.shape
    tm = ROW_TILE
    const = lambda i: (0, 0)
    return pl.pallas_call(
        _final_kernel,
        grid=(n // tm,),
        in_specs=[
            pl.BlockSpec((tm, d), lambda i: (i, 0)),
            pl.BlockSpec((TOP_K, tm, d), lambda i: (0, i, 0)),
            pl.BlockSpec((tm, LANES), lambda i: (i, 0)),
            pl.BlockSpec((tm, p2.shape[1]), lambda i: (i, 0)),
            pl.BlockSpec((1, d), const),
            pl.BlockSpec(w_gate.shape, const),
            pl.BlockSpec(w_proj.shape, const),
        ],
        out_specs=pl.BlockSpec((tm, d), lambda i: (i, 0)),
        out_shape=jax.ShapeDtypeStruct((n, d), F32),
        compiler_params=_cparams("parallel"),
        name="final",
    )(h1, yk, gates, p2, g_ple, w_gate, w_proj)


def _layer(h, p, g_mix, w_in, b_f, g_qa, g_ka, g_qb, g_kb, w_o, g_ffn, w_router, b_router,
           w_gate_up, b_gate_up, w_down, b_down, g_ple, w_ple_gate, w_ple_proj):
    batch, seq, d = h.shape
    n = batch * seq
    assert tuple(dil for _, dil in DILATED_PATTERNS) == (1, 4, 16)
    for window, dil in DILATED_PATTERNS:
        per_class = seq // BLOCK // dil
        assert window // dil == BLOCK and seq % (dil * BLOCK) == 0
        assert per_class % UNITS_PER_STEP == 0 or UNITS_PER_STEP % per_class == 0
    assert n % ROW_TILE == 0 and d % (2 * LANES) == 0 and seq % FOX_TILE == 0
    x2 = h.reshape(n, d)

    qkv_cols = 3 * WIDTH_A + 3 * WIDTH_B
    w_qkv = w_in[:, :qkv_cols].astype(BF16)
    w_f = jnp.pad(w_in[:, qkv_cols:], ((0, 0), (0, LANES - N_HEADS_B))).astype(BF16)
    b_fp = jnp.pad(b_f.astype(F32), (0, LANES - N_HEADS_B)).reshape(1, LANES)
    scale = HEAD_DIM ** -0.5
    gains = jnp.stack([jnp.tile(g_qa, N_HEADS_A) * scale, jnp.tile(g_ka, N_HEADS_A),
                       jnp.tile(g_qb, N_HEADS_B) * scale, jnp.tile(g_kb, N_HEADS_B)]).astype(F32)
    hid = jnp.arange(2 * LANES) // HEAD_DIM
    bd = (hid[:, None] == hid[None, :]).astype(BF16)

    z, logf = _in_proj(x2, g_mix.reshape(1, d), w_qkv, w_f, b_fp, gains, bd)
    ccol = _cumsum(logf, batch, seq)

    slopes = 2.0 ** (-8.0 * jnp.arange(1, N_HEADS_A + 1, dtype=F32) / N_HEADS_A)
    mix_a = _dilated(z, slopes, batch, seq)
    mix_b = _fox(z, ccol, batch, seq)

    w_r = jnp.pad(w_router.astype(F32), ((0, 0), (0, LANES - N_EXPERTS)))
    b_r = jnp.concatenate([b_router.astype(F32), jnp.full((LANES - N_EXPERTS,), NEG_INF, F32)]).reshape(1, LANES)
    h1, u_packed, top_idx, gates, rank, counts = _post_attn(
        mix_a, mix_b, x2, w_o.astype(BF16), g_ffn.reshape(1, d), w_r, b_r)

    counts = counts[0, :N_EXPERTS].astype(jnp.int32)
    tiles_per = (counts + GMM_TILE - 1) // GMM_TILE
    tile_end = jnp.cumsum(tiles_per)
    starts = (tile_end - tiles_per) * GMM_TILE
    n_tiles = n * TOP_K // GMM_TILE + N_EXPERTS
    tile_ids = jnp.arange(n_tiles, dtype=jnp.int32)
    tile_used = (tile_ids < tile_end[-1]).astype(jnp.int32)
    last_used = jnp.minimum(tile_ids, tile_end[-1] - 1)
    tile_expert = jnp.sum((last_used[:, None] >= tile_end[None, :]).astype(jnp.int32), axis=1)
    tile_expert = jnp.minimum(tile_expert, N_EXPERTS - 1)
    idx4 = top_idx[:, :TOP_K]
    pos = starts[idx4] + rank[:, :TOP_K]
    src = jnp.zeros((n_tiles * GMM_TILE,), jnp.int32).at[pos.reshape(-1)].set(
        jnp.repeat(jnp.arange(n, dtype=jnp.int32), TOP_K))
    xs = u_packed[src]

    de = w_down.shape[1]
    wg_t, wl_t = _wprep(w_gate_up)
    ys = _gmm(tile_expert, tile_used, xs, wg_t, wl_t,
              b_gate_up[:, 0::2].reshape(N_EXPERTS, 1, de).astype(F32),
              b_gate_up[:, 1::2].reshape(N_EXPERTS, 1, de).astype(F32),
              w_down, b_down.reshape(N_EXPERTS, 1, d).astype(F32))
    yk = ys[pos.T]

    out = _final(h1, yk, gates, p.reshape(n, -1), g_ple.reshape(1, d),
                 w_ple_gate.astype(BF16), w_ple_proj.astype(BF16))
    return out.reshape(batch, seq, d)


def kernel(x, p, g_mix, w_in, b_f, g_qa, g_ka, g_qb, g_kb, w_o, g_ffn, w_router, b_router,
           w_gate_up, b_gate_up, w_down, b_down, g_ple, w_ple_gate, w_ple_proj):
    h = x
    for i in range(g_mix.shape[0]):
        h = _layer(h, p[i], g_mix[i], w_in[i], b_f[i], g_qa[i], g_ka[i], g_qb[i], g_kb[i], w_o[i],
                   g_ffn[i], w_router[i], b_router[i], w_gate_up[i], b_gate_up[i], w_down[i],
                   b_down[i], g_ple[i], w_ple_gate[i], w_ple_proj[i])
    return h
```

```python
import functools

import jax
import jax.numpy as jnp
from jax import lax
from jax.experimental import pallas as pl
from jax.experimental.pallas import tpu as pltpu
from jax.experimental.pallas import tpu_sc as plsc

HEAD_DIM = 64
N_HEADS_A = 8
N_HEADS_B = 8
WIDTH_A = N_HEADS_A * HEAD_DIM
WIDTH_B = N_HEADS_B * HEAD_DIM
DILATED_PATTERNS = ((128, 1), (512, 4), (2048, 16))
BLOCK = 128
N_EXPERTS = 32
TOP_K = 4
SWIGLU_LIMIT = 7.0
SWIGLU_ALPHA = 1.702
NORM_EPS = 1e-6

LANES = 128
PAIR = LANES // HEAD_DIM
ROW_TILE = 512
GMM_TILE = 512
FOX_TILE = 512
UNITS_PER_STEP = 8
SC_CHUNK = 32
SC_DEPTH = 4
VMEM_LIMIT = 56 * 1024 * 1024

F32 = jnp.float32
BF16 = jnp.bfloat16
NEG_INF = float("-inf")
NT_DIMS = (((1,), (1,)), ((), ()))


def _cparams(*sem):
    return pltpu.CompilerParams(dimension_semantics=sem, vmem_limit_bytes=VMEM_LIMIT)


def _rms(x, g):
    return x * lax.rsqrt(jnp.mean(x * x, axis=-1, keepdims=True) + NORM_EPS) * g


def _lane_iota():
    return lax.broadcasted_iota(jnp.int32, (1, LANES), 1)


def _head_lane_mask(h):
    lane = _lane_iota()
    return (lane >= h * HEAD_DIM) & (lane < (h + 1) * HEAD_DIM)


def _merge_heads(acc0, acc1):
    first = _head_lane_mask(0)
    num = jnp.where(first, acc0, acc1)
    den = pltpu.roll(jnp.where(first, acc1, acc0), HEAD_DIM, axis=1)
    return num, den


def _in_proj_kernel(x_ref, g_ref, w_ref, wf_ref, bf_ref, gain_ref, bd_ref, z_ref, lf_ref):
    u = _rms(x_ref[...], g_ref[...]).astype(BF16)
    chunk = WIDTH_A
    normed = {0: 0, 1: 1, 3: 2, 4: 3}
    for c in range(6):
        acc = jnp.dot(u, w_ref[:, c * chunk:(c + 1) * chunk], preferred_element_type=F32)
        if c in normed:
            sq = (acc * acc).astype(BF16)
            half = chunk // 2
            ss = jnp.concatenate(
                [jnp.dot(sq[:, j * half:(j + 1) * half], bd_ref[...], preferred_element_type=F32)
                 for j in range(2)], axis=1)
            r = normed[c]
            acc = acc * lax.rsqrt(ss * (1.0 / HEAD_DIM) + NORM_EPS) * gain_ref[r:r + 1, :]
        z_ref[:, c * chunk:(c + 1) * chunk] = acc.astype(BF16)
    zf = jnp.dot(u, wf_ref[...], preferred_element_type=F32) + bf_ref[...]
    lf_ref[...] = jax.nn.log_sigmoid(zf)


def _in_proj(x2, g_mix, w_qkv, w_f, b_f, gains, bd):
    n, d = x2.shape
    cols = w_qkv.shape[1]
    tm = ROW_TILE
    const = lambda i: (0, 0)
    return pl.pallas_call(
        _in_proj_kernel,
        grid=(n // tm,),
        in_specs=[
            pl.BlockSpec((tm, d), lambda i: (i, 0)),
            pl.BlockSpec((1, d), const),
            pl.BlockSpec((d, cols), const),
            pl.BlockSpec((d, LANES), const),
            pl.BlockSpec((1, LANES), const),
            pl.BlockSpec(gains.shape, const),
            pl.BlockSpec(bd.shape, const),
        ],
        out_specs=[
            pl.BlockSpec((tm, cols), lambda i: (i, 0)),
            pl.BlockSpec((tm, LANES), lambda i: (i, 0)),
        ],
        out_shape=[
            jax.ShapeDtypeStruct((n, cols), BF16),
            jax.ShapeDtypeStruct((n, LANES), F32),
        ],
        compiler_params=_cparams("parallel"),
        name="in_proj",
    )(x2, g_mix, w_qkv, w_f, b_f, gains, bd)


def _cumsum_kernel(lf_ref, tri_ref, ccol_ref):
    s = lf_ref.shape[0]
    carry = jnp.zeros((1, LANES), F32)
    for blk in range(s // BLOCK):
        rows = slice(blk * BLOCK, (blk + 1) * BLOCK)
        part = jnp.dot(tri_ref[...], lf_ref[rows, :], precision=lax.Precision.HIGHEST,
                       preferred_element_type=F32) + carry
        ccol_ref[rows, :] = part
        carry = part[BLOCK - 1:BLOCK, :]


def _cumsum(logf, batch, seq):
    tri = (lax.broadcasted_iota(jnp.int32, (BLOCK, BLOCK), 0)
           >= lax.broadcasted_iota(jnp.int32, (BLOCK, BLOCK), 1)).astype(F32)
    return pl.pallas_call(
        _cumsum_kernel,
        grid=(batch,),
        in_specs=[
            pl.BlockSpec((seq, LANES), lambda b: (b, 0)),
            pl.BlockSpec((BLOCK, BLOCK), lambda b: (0, 0)),
        ],
        out_specs=pl.BlockSpec((seq, LANES), lambda b: (b, 0)),
        out_shape=jax.ShapeDtypeStruct((batch * seq, LANES), F32),
        compiler_params=_cparams("parallel"),
        name="cumsum",
    )(logf, tri)


def _split3(c):
    hi = c.astype(BF16)
    r1 = c - hi.astype(F32)
    mid = r1.astype(BF16)
    lo = (r1 - mid.astype(F32)).astype(BF16)
    return hi, mid, lo


def _fox_operand(x, c, h, key_side):
    lane = _lane_iota()
    base = HEAD_DIM * (1 - h)
    hi, mid, lo = (piece.astype(F32) for piece in _split3(-c if key_side else c))
    own = (base + 3) if key_side else base
    other = base if key_side else (base + 3)
    feat = jnp.where(lane == own, hi, jnp.where(lane == own + 1, mid, jnp.where(lane == own + 2, lo,
           jnp.where((lane >= other) & (lane < other + 3), 1.0, 0.0))))
    return jnp.where(_head_lane_mask(h), x, feat.astype(BF16))


def _fox_kernel(q_ref, k_ref, v_ref, c_ref, o_ref, kf, vf, s_scr, m_scr, acc_scr, *, tile):
    pair = pl.program_id(1)
    i = pl.program_id(2)
    lane = _lane_iota()

    def gate_column(c_tile, h):
        return jnp.sum(jnp.where(lane == PAIR * pair + h, c_tile, 0.0), axis=-1, keepdims=True)

    @pl.when(i == 0)
    def _():
        c_all = c_ref[...]
        for h in range(PAIR):
            kf[h] = _fox_operand(k_ref[...], gate_column(c_all, h), h, True)
            vf[h] = jnp.where(_head_lane_mask(h), v_ref[...], jnp.ones_like(v_ref[...]))

    row0 = pl.multiple_of(i * tile, tile)
    c_tile = c_ref[pl.ds(row0, tile), :]
    q = q_ref[...]
    qf = [_fox_operand(q, gate_column(c_tile, h), h, False) for h in range(PAIR)]

    def lane_groups_max(s):
        m = s[:, :LANES]
        for g in range(1, tile // LANES):
            m = jnp.maximum(m, s[:, g * LANES:(g + 1) * LANES])
        return m

    def scores(j, h):
        off = pl.multiple_of(j * tile, tile)
        return lax.dot_general(qf[h], kf[h, pl.ds(off, tile), :], NT_DIMS, preferred_element_type=F32)

    row = lax.broadcasted_iota(jnp.int32, (tile, tile), 0)
    col = lax.broadcasted_iota(jnp.int32, (tile, tile), 1)
    for h in range(PAIR):
        s = jnp.where(col <= row, scores(i, h), NEG_INF)
        s_scr[h, :, pl.ds(row0, tile)] = s
        m_scr[h] = lane_groups_max(s)

    def pass1(j, _):
        off = pl.multiple_of(j * tile, tile)
        for h in range(PAIR):
            s = scores(j, h)
            s_scr[h, :, pl.ds(off, tile)] = s
            m_scr[h] = jnp.maximum(m_scr[h], lane_groups_max(s))
        return 0
    lax.fori_loop(0, i, pass1, 0)

    m = [jnp.max(m_scr[h], axis=-1, keepdims=True) for h in range(PAIR)]
    acc_scr[...] = jnp.zeros_like(acc_scr)

    def pass2(j, _):
        off = pl.multiple_of(j * tile, tile)
        for h in range(PAIR):
            p = jnp.exp(s_scr[h, :, pl.ds(off, tile)] - m[h]).astype(BF16)
            acc_scr[h] += jnp.dot(p, vf[h, pl.ds(off, tile), :], preferred_element_type=F32)
        return 0
    lax.fori_loop(0, i + 1, pass2, 0)

    num, den = _merge_heads(acc_scr[0], acc_scr[1])
    o_ref[...] = (num / den).astype(o_ref.dtype)


def _fox(z, ccol, batch, seq):
    n = z.shape[0]
    tile = FOX_TILE
    nq = seq // tile
    npair = N_HEADS_B // PAIR
    base = 3 * WIDTH_A // LANES
    qcol, kcol, vcol = base, base + WIDTH_B // LANES, base + 2 * WIDTH_B // LANES
    return pl.pallas_call(
        functools.partial(_fox_kernel, tile=tile),
        grid=(batch, npair, nq),
        in_specs=[
            pl.BlockSpec((tile, LANES), lambda b, p, i: (b * nq + i, qcol + p)),
            pl.BlockSpec((seq, LANES), lambda b, p, i: (b, kcol + p)),
            pl.BlockSpec((seq, LANES), lambda b, p, i: (b, vcol + p)),
            pl.BlockSpec((seq, LANES), lambda b, p, i: (b, 0)),
        ],
        out_specs=pl.BlockSpec((tile, LANES), lambda b, p, i: (b * nq + i, p)),
        out_shape=jax.ShapeDtypeStruct((n, WIDTH_B), BF16),
        scratch_shapes=[
            pltpu.VMEM((PAIR, seq, LANES), BF16),
            pltpu.VMEM((PAIR, seq, LANES), BF16),
            pltpu.VMEM((PAIR, tile, seq), F32),
            pltpu.VMEM((PAIR, tile, LANES), F32),
            pltpu.VMEM((PAIR, tile, LANES), F32),
        ],
        compiler_params=_cparams("parallel", "parallel", "arbitrary"),
        name="fox",
    )(z, z, z, ccol)


def _dilated_kernel(slope_ref, q_ref, k_ref, v_ref, o_ref,
                    natf, p4f, p4b, p16b, qfeat, kfeat, vals, dens, maxs, *, seq):
    pair = pl.program_id(1)
    lane = _lane_iota()
    first = _head_lane_mask(0)
    quarter = seq // 4
    units = seq // BLOCK

    def deinterleave(src, t, span_start, span):
        return [src[t, pl.ds(span_start + r, span // 4, stride=4), :] for r in range(4)]

    for t, ref in enumerate((q_ref, k_ref, v_ref)):
        natf[t] = ref[...].astype(F32)
        for r, part in enumerate(deinterleave(natf, t, 0, seq)):
            p4f[t, pl.ds(r * quarter, quarter), :] = part
            p4b[t, pl.ds(r * quarter, quarter), :] = part.astype(BF16)
        for r4 in range(4):
            for r, part in enumerate(deinterleave(p4f, t, r4 * quarter, quarter)):
                p16b[t, pl.ds(r4 * quarter + r * (quarter // 4), quarter // 4), :] = part.astype(BF16)

    qi = lax.broadcasted_iota(jnp.int32, (BLOCK, LANES), 0).astype(F32)
    kj = lax.broadcasted_iota(jnp.int32, (2 * BLOCK, LANES), 0).astype(F32)
    for p, (_, dil) in enumerate(DILATED_PATTERNS):
        for h in range(PAIR):
            sd = slope_ref[PAIR * pair + h] * float(dil)
            base = HEAD_DIM * (1 - h)
            qfeat[p * PAIR + h] = jnp.where(lane == base, -(qi + float(BLOCK)) * sd,
                                            jnp.where(lane == base + 1, 1.0, 0.0)).astype(BF16)
            kfeat[p * PAIR + h] = jnp.where(lane == base, 1.0,
                                            jnp.where(lane == base + 1, kj * sd, 0.0)).astype(BF16)

    bq = lax.broadcasted_iota(jnp.int32, (BLOCK, 2 * BLOCK), 0)
    bk = lax.broadcasted_iota(jnp.int32, (BLOCK, 2 * BLOCK), 1)
    rel = bq + BLOCK - bk
    band = (rel >= 0) & (rel <= BLOCK)

    def unit(p, srcs, u, prev_valid):
        qs, ks, vs = srcs
        start = pl.multiple_of(u * BLOCK, BLOCK)
        prev = pl.multiple_of(jnp.maximum(start - BLOCK, 0), BLOCK)
        qb = qs[pl.ds(start, BLOCK), :]
        kk = jnp.concatenate([ks[pl.ds(prev, BLOCK), :], ks[pl.ds(start, BLOCK), :]], axis=0)
        vv = jnp.concatenate([vs[pl.ds(prev, BLOCK), :], vs[pl.ds(start, BLOCK), :]], axis=0)
        if prev_valid is True:
            ok = band
        elif prev_valid is False:
            ok = band & (bk >= BLOCK)
        else:
            ok = band & ((bk >= BLOCK) | prev_valid)
        accs, ms = [], []
        for h in range(PAIR):
            in_h = _head_lane_mask(h)
            qh = jnp.where(in_h, qb, qfeat[p * PAIR + h])
            kh = jnp.where(in_h, kk, kfeat[p * PAIR + h])
            vh = jnp.where(in_h, vv, jnp.ones_like(vv))
            s = lax.dot_general(qh, kh, NT_DIMS, preferred_element_type=F32)
            s = jnp.where(ok, s, NEG_INF)
            m = jnp.max(s, axis=-1, keepdims=True)
            pr = jnp.exp(s - m).astype(BF16)
            accs.append(jnp.dot(pr, vh, preferred_element_type=F32))
            ms.append(m)
        num, den = _merge_heads(*accs)
        vals[p, pl.ds(start, BLOCK), :] = num
        dens[p, pl.ds(start, BLOCK), :] = den
        maxs[p, pl.ds(start, BLOCK), :] = jnp.where(first, ms[0], ms[1])

    group = UNITS_PER_STEP
    sources = ((q_ref, k_ref, v_ref), tuple(p4b.at[t] for t in range(3)), tuple(p16b.at[t] for t in range(3)))
    for p, (_, dil) in enumerate(DILATED_PATTERNS):
        per_class = units // dil

        def step(g, _, p=p, per_class=per_class):
            for e in range(group):
                u = g * group + e
                if per_class >= group:
                    prev_valid = (u % per_class != 0) if e == 0 else True
                else:
                    prev_valid = e % per_class != 0
                unit(p, sources[p], u, prev_valid)
            return 0
        lax.fori_loop(0, units // group, step, 0)

    def interleave(src, s_t, dst, d_t, span_start, span):
        for r in range(4):
            dst[d_t, pl.ds(span_start + r, span // 4, stride=4), :] = \
                src[s_t, pl.ds(span_start + r * (span // 4), span // 4), :]

    for t, arr in enumerate((vals, dens, maxs)):
        for r4 in range(4):
            interleave(arr, 2, p4f, t, r4 * quarter, quarter)
        interleave(p4f, t, arr, 2, 0, seq)
        natf[t] = arr[1]
        interleave(natf, t, arr, 1, 0, seq)

    m_all = jnp.maximum(jnp.maximum(maxs[0], maxs[1]), maxs[2])
    num = jnp.zeros((seq, LANES), F32)
    den = jnp.zeros((seq, LANES), F32)
    for p in range(3):
        e = jnp.exp(maxs[p] - m_all)
        num = num + e * vals[p]
        den = den + e * dens[p]
    o_ref[...] = (num / den).astype(o_ref.dtype)


def _dilated(z, slopes, batch, seq):
    n = z.shape[0]
    npair = N_HEADS_A // PAIR
    npat = len(DILATED_PATTERNS)
    qcol, kcol, vcol = 0, WIDTH_A // LANES, 2 * WIDTH_A // LANES
    blk = lambda c0: pl.BlockSpec((seq, LANES), lambda b, p: (b, c0 + p))
    return pl.pallas_call(
        functools.partial(_dilated_kernel, seq=seq),
        grid=(batch, npair),
        in_specs=[pl.BlockSpec(memory_space=pltpu.SMEM), blk(qcol), blk(kcol), blk(vcol)],
        out_specs=pl.BlockSpec((seq, LANES), lambda b, p: (b, p)),
        out_shape=jax.ShapeDtypeStruct((n, WIDTH_A), BF16),
        scratch_shapes=[
            pltpu.VMEM((3, seq, LANES), F32),
            pltpu.VMEM((3, seq, LANES), F32),
            pltpu.VMEM((3, seq, LANES), BF16),
            pltpu.VMEM((3, seq, LANES), BF16),
            pltpu.VMEM((npat * PAIR, BLOCK, LANES), BF16),
            pltpu.VMEM((npat * PAIR, 2 * BLOCK, LANES), BF16),
            pltpu.VMEM((npat, seq, LANES), F32),
            pltpu.VMEM((npat, seq, LANES), F32),
            pltpu.VMEM((npat, seq, LANES), F32),
        ],
        compiler_params=_cparams("parallel", "parallel"),
        name="dilated",
    )(slopes, z, z, z)


def _pack_bf16_pairs(a, b):
    hi = pltpu.bitcast(a.astype(BF16).astype(F32), jnp.int32)
    lo = pltpu.bitcast(b.astype(BF16).astype(F32), jnp.int32)
    return (hi & jnp.int32(-65536)) | lax.shift_right_logical(lo, jnp.int32(16))


def _unpack_bf16_pairs(w):
    a = pltpu.bitcast(w & jnp.int32(-65536), F32)
    b = pltpu.bitcast(lax.shift_left(w, jnp.int32(16)), F32)
    return a, b


def _post_attn_kernel(ma_ref, mb_ref, x_ref, wo_ref, g_ref, wr_ref, br_ref, tri_ref,
                      h_ref, up_ref, idx_ref, gate_ref, rank_ref, cnt_ref, carry):
    @pl.when(pl.program_id(0) == 0)
    def _():
        carry[...] = jnp.zeros_like(carry)

    y = jnp.dot(ma_ref[...], wo_ref[:WIDTH_A, :], preferred_element_type=F32)
    y = y + jnp.dot(mb_ref[...], wo_ref[WIDTH_A:, :], preferred_element_type=F32)
    h = x_ref[...] + y
    h_ref[...] = h
    u = _rms(h, g_ref[...])
    half = u.shape[1] // 2
    up_ref[...] = _pack_bf16_pairs(u[:, :half], u[:, half:])

    logits = jnp.dot(u, wr_ref[...], precision=lax.Precision.HIGHEST,
                     preferred_element_type=F32) + br_ref[...]
    lane = lax.broadcasted_iota(jnp.int32, logits.shape, 1).astype(F32)
    work = logits
    idxs, tops = [], []
    for _ in range(TOP_K):
        top = jnp.max(work, axis=-1, keepdims=True)
        idx = jnp.min(jnp.where(work == top, lane, float(LANES)), axis=-1, keepdims=True)
        work = jnp.where(lane == idx, NEG_INF, work)
        idxs.append(idx)
        tops.append(top)
    exps = [jnp.exp(t - tops[0]) for t in tops]
    total = exps[0] + exps[1] + exps[2] + exps[3]

    onehot = jnp.zeros(logits.shape, F32)
    for idx in idxs:
        onehot = onehot + (lane == idx).astype(F32)
    before = jnp.dot(tri_ref[...], onehot.astype(BF16), preferred_element_type=F32) + carry[...]
    carry[...] = carry[...] + jnp.sum(onehot, axis=0, keepdims=True)
    cnt_ref[...] = carry[...]

    idx_out = jnp.zeros(logits.shape, F32)
    gate_out = jnp.zeros(logits.shape, F32)
    rank_out = jnp.zeros(logits.shape, F32)
    for k in range(TOP_K):
        rank_k = jnp.sum(jnp.where(lane == idxs[k], before, 0.0), axis=-1, keepdims=True)
        idx_out = jnp.where(lane == float(k), idxs[k], idx_out)
        gate_out = jnp.where(lane == float(k), exps[k] / total, gate_out)
        rank_out = jnp.where(lane == float(k), rank_k, rank_out)
    idx_ref[...] = idx_out.astype(jnp.int32)
    gate_ref[...] = gate_out
    rank_ref[...] = rank_out.astype(jnp.int32)


def _post_attn(mix_a, mix_b, x2, w_o, g_ffn, w_r, b_r):
    n, d = x2.shape
    tm = ROW_TILE
    tri = (lax.broadcasted_iota(jnp.int32, (tm, tm), 0)
           > lax.broadcasted_iota(jnp.int32, (tm, tm), 1)).astype(BF16)
    const = lambda i: (0, 0)
    row = lambda w: pl.BlockSpec((tm, w), lambda i: (i, 0))
    return pl.pallas_call(
        _post_attn_kernel,
        grid=(n // tm,),
        in_specs=[
            row(WIDTH_A), row(WIDTH_B), row(d),
            pl.BlockSpec(w_o.shape, const),
            pl.BlockSpec((1, d), const),
            pl.BlockSpec((d, LANES), const),
            pl.BlockSpec((1, LANES), const),
            pl.BlockSpec((tm, tm), const),
        ],
        out_specs=[row(d), row(d // 2), row(LANES), row(LANES), row(LANES),
                   pl.BlockSpec((1, LANES), const)],
        out_shape=[
            jax.ShapeDtypeStruct((n, d), F32),
            jax.ShapeDtypeStruct((n, d // 2), jnp.int32),
            jax.ShapeDtypeStruct((n, LANES), jnp.int32),
            jax.ShapeDtypeStruct((n, LANES), F32),
            jax.ShapeDtypeStruct((n, LANES), jnp.int32),
            jax.ShapeDtypeStruct((1, LANES), F32),
        ],
        scratch_shapes=[pltpu.VMEM((1, LANES), F32)],
        compiler_params=_cparams("arbitrary"),
        name="post_attn",
    )(mix_a, mix_b, x2, w_o, g_ffn, w_r, b_r, tri)


def _wprep_kernel(w_ref, wg_ref, wl_ref, wt):
    d, cols = w_ref.shape[1:]
    de = cols // 2
    for j in range(d // LANES):
        lanes = slice(j * LANES, (j + 1) * LANES)
        wt[j] = w_ref[0, lanes, :].T
        wg_ref[0, :, lanes] = wt[j, pl.ds(0, de, stride=2), :].astype(BF16)
        wl_ref[0, :, lanes] = wt[j, pl.ds(1, de, stride=2), :].astype(BF16)


def _wprep(w_gate_up):
    ne, d, cols = w_gate_up.shape
    de = cols // 2
    out = pl.BlockSpec((1, de, d), lambda e: (e, 0, 0))
    return pl.pallas_call(
        _wprep_kernel,
        grid=(ne,),
        in_specs=[pl.BlockSpec((1, d, cols), lambda e: (e, 0, 0))],
        out_specs=[out, out],
        out_shape=[jax.ShapeDtypeStruct((ne, de, d), BF16)] * 2,
        scratch_shapes=[pltpu.VMEM((d // LANES, cols, LANES), F32)],
        compiler_params=_cparams("parallel"),
        name="wprep",
    )(w_gate_up)


def _sc_gather(table, idx):
    info = plsc.get_sparse_core_info()
    workers = info.num_cores * info.num_subcores
    rows, width = idx.shape[0], table.shape[1]
    chunk, depth = SC_CHUNK, SC_DEPTH
    assert rows % (workers * chunk * depth) == 0
    per_worker = rows // workers
    nchunks = per_worker // chunk
    mesh = plsc.VectorSubcoreMesh(core_axis_name="c", subcore_axis_name="s")

    @functools.partial(
        pl.kernel, mesh=mesh,
        out_type=jax.ShapeDtypeStruct((rows, width), table.dtype),
        scratch_types=[
            pltpu.VMEM((nchunks, chunk), jnp.int32),
            pltpu.VMEM((depth, chunk, width), table.dtype),
            pltpu.SemaphoreType.DMA((depth,)),
            pltpu.SemaphoreType.DMA((depth,)),
        ],
    )
    def gather_kernel(table_hbm, idx_hbm, out_hbm, idx_v, rows_v, gsem, wsem):
        wid = lax.axis_index("s") * info.num_cores + lax.axis_index("c")
        base = wid * per_worker
        pltpu.sync_copy(idx_hbm.at[wid], idx_v)

        def gather(c, b):
            return pltpu.make_async_copy(table_hbm.at[idx_v.at[c]], rows_v.at[b], gsem.at[b])

        def write(c, b):
            off = pl.multiple_of(base + c * chunk, chunk)
            return pltpu.make_async_copy(rows_v.at[b], out_hbm.at[pl.ds(off, chunk)], wsem.at[b])

        @pl.loop(0, nchunks, step=depth)
        def _(c0):
            for b in range(depth):
                gather(c0 + b, b).start()
            for b in range(depth):
                gather(c0 + b, b).wait()
                write(c0 + b, b).start()
            for b in range(depth):
                write(c0 + b, b).wait()

    return gather_kernel(table, idx.reshape(workers, nchunks, chunk))


def _gmm_kernel(te_ref, used_ref, xs_ref, wg_ref, wl_ref, bg_ref, bl_ref, wd_ref, bd_ref, ys_ref):
    i = pl.program_id(0)

    @pl.when(used_ref[i] > 0)
    def _():
        a, b = _unpack_bf16_pairs(xs_ref[...])
        x = jnp.concatenate([a, b], axis=1).astype(BF16)
        hg = lax.dot_general(x, wg_ref[0], NT_DIMS, preferred_element_type=F32) + bg_ref[0]
        hl = lax.dot_general(x, wl_ref[0], NT_DIMS, preferred_element_type=F32) + bl_ref[0]
        xg = jnp.minimum(hg, SWIGLU_LIMIT)
        xl = jnp.clip(hl, -SWIGLU_LIMIT, SWIGLU_LIMIT)
        act = xg * jax.nn.sigmoid(SWIGLU_ALPHA * xg) * (xl + 1.0)
        out = jnp.dot(act.astype(BF16), wd_ref[0].astype(BF16), preferred_element_type=F32) + bd_ref[0]
        half = out.shape[1] // 2
        ys_ref[...] = _pack_bf16_pairs(out[:, :half], out[:, half:])

    @pl.when(used_ref[i] == 0)
    def _():
        ys_ref[...] = jnp.zeros_like(ys_ref)


def _gmm(tile_expert, tile_used, xs, wg_t, wl_t, b_glu, b_lin, w_down, b_down):
    rows, half = xs.shape
    d = 2 * half
    de = wg_t.shape[1]
    tm = GMM_TILE
    wspec = lambda shape: pl.BlockSpec((1,) + shape, lambda i, te, used: (te[i], 0, 0))
    grid_spec = pltpu.PrefetchScalarGridSpec(
        num_scalar_prefetch=2,
        grid=(rows // tm,),
        in_specs=[
            pl.BlockSpec((tm, half), lambda i, te, used: (i, 0)),
            wspec((de, d)), wspec((de, d)), wspec((1, de)), wspec((1, de)),
            wspec((de, d)), wspec((1, d)),
        ],
        out_specs=pl.BlockSpec((tm, half), lambda i, te, used: (i, 0)),
    )
    return pl.pallas_call(
        _gmm_kernel,
        grid_spec=grid_spec,
        out_shape=jax.ShapeDtypeStruct((rows, half), jnp.int32),
        compiler_params=_cparams("arbitrary"),
        name="gmm",
    )(tile_expert, tile_used, xs, wg_t, wl_t, b_glu, b_lin, w_down, b_down)


def _final_kernel(h_ref, yk_ref, gate_ref, p_ref, g_ref, wg_ref, wp_ref, o_ref):
    gates = gate_ref[...]
    h = h_ref[...]
    for k in range(TOP_K):
        h = h + gates[:, k:k + 1] * jnp.concatenate(_unpack_bf16_pairs(yk_ref[k]), axis=1)
    u = _rms(h, g_ref[...]).astype(BF16)
    gate = jax.nn.sigmoid(jnp.dot(u, wg_ref[...], preferred_element_type=F32))
    proj = jnp.dot(p_ref[...].astype(BF16), wp_ref[...], preferred_element_type=F32)
    o_ref[...] = h + gate * proj


def _final(h1, yk, gates, p2, g_ple, w_gate, w_proj):
    n, d = h1.shape
    tm = ROW_TILE
    const = lambda i: (0, 0)
    return pl.pallas_call(
        _final_kernel,
        grid=(n // tm,),
        in_specs=[
            pl.BlockSpec((tm, d), lambda i: (i, 0)),
            pl.BlockSpec((TOP_K, tm, d // 2), lambda i: (0, i, 0)),
            pl.BlockSpec((tm, LANES), lambda i: (i, 0)),
            pl.BlockSpec((tm, p2.shape[1]), lambda i: (i, 0)),
            pl.BlockSpec((1, d), const),
            pl.BlockSpec(w_gate.shape, const),
            pl.BlockSpec(w_proj.shape, const),
        ],
        out_specs=pl.BlockSpec((tm, d), lambda i: (i, 0)),
        out_shape=jax.ShapeDtypeStruct((n, d), F32),
        compiler_params=_cparams("parallel"),
        name="final",
    )(h1, yk, gates, p2, g_ple, w_gate, w_proj)


def _layer(h, p, g_mix, w_in, b_f, g_qa, g_ka, g_qb, g_kb, w_o, g_ffn, w_router, b_router,
           w_gate_up, b_gate_up, w_down, b_down, g_ple, w_ple_gate, w_ple_proj):
    batch, seq, d = h.shape
    n = batch * seq
    assert tuple(dil for _, dil in DILATED_PATTERNS) == (1, 4, 16)
    for window, dil in DILATED_PATTERNS:
        per_class = seq // BLOCK // dil
        assert window // dil == BLOCK and seq % (dil * BLOCK) == 0
        assert per_class % UNITS_PER_STEP == 0 or UNITS_PER_STEP % per_class == 0
    assert n % ROW_TILE == 0 and d % (2 * LANES) == 0 and seq % FOX_TILE == 0
    x2 = h.reshape(n, d)

    qkv_cols = 3 * WIDTH_A + 3 * WIDTH_B
    w_qkv = w_in[:, :qkv_cols].astype(BF16)
    w_f = jnp.pad(w_in[:, qkv_cols:], ((0, 0), (0, LANES - N_HEADS_B))).astype(BF16)
    b_fp = jnp.pad(b_f.astype(F32), (0, LANES - N_HEADS_B)).reshape(1, LANES)
    scale = HEAD_DIM ** -0.5
    gains = jnp.stack([jnp.tile(g_qa, N_HEADS_A) * scale, jnp.tile(g_ka, N_HEADS_A),
                       jnp.tile(g_qb, N_HEADS_B) * scale, jnp.tile(g_kb, N_HEADS_B)]).astype(F32)
    hid = jnp.arange(2 * LANES) // HEAD_DIM
    bd = (hid[:, None] == hid[None, :]).astype(BF16)

    z, logf = _in_proj(x2, g_mix.reshape(1, d), w_qkv, w_f, b_fp, gains, bd)
    ccol = _cumsum(logf, batch, seq)

    slopes = 2.0 ** (-8.0 * jnp.arange(1, N_HEADS_A + 1, dtype=F32) / N_HEADS_A)
    mix_a = _dilated(z, slopes, batch, seq)
    mix_b = _fox(z, ccol, batch, seq)

    w_r = jnp.pad(w_router.astype(F32), ((0, 0), (0, LANES - N_EXPERTS)))
    b_r = jnp.concatenate([b_router.astype(F32), jnp.full((LANES - N_EXPERTS,), NEG_INF, F32)]).reshape(1, LANES)
    h1, u_packed, top_idx, gates, rank, counts = _post_attn(
        mix_a, mix_b, x2, w_o.astype(BF16), g_ffn.reshape(1, d), w_r, b_r)

    counts = counts[0, :N_EXPERTS].astype(jnp.int32)
    tiles_per = (counts + GMM_TILE - 1) // GMM_TILE
    tile_end = jnp.cumsum(tiles_per)
    starts = (tile_end - tiles_per) * GMM_TILE
    n_tiles = n * TOP_K // GMM_TILE + N_EXPERTS
    tile_ids = jnp.arange(n_tiles, dtype=jnp.int32)
    tile_used = (tile_ids < tile_end[-1]).astype(jnp.int32)
    last_used = jnp.minimum(tile_ids, tile_end[-1] - 1)
    tile_expert = jnp.sum((last_used[:, None] >= tile_end[None, :]).astype(jnp.int32), axis=1)
    tile_expert = jnp.minimum(tile_expert, N_EXPERTS - 1)
    idx4 = top_idx[:, :TOP_K]
    pos = starts[idx4] + rank[:, :TOP_K]
    tok_sorted = (jnp.argsort(pos.reshape(-1)) // TOP_K).astype(jnp.int32)
    within = jnp.arange(n_tiles * GMM_TILE, dtype=jnp.int32) - jnp.repeat(starts[tile_expert], GMM_TILE)
    compact = jnp.repeat((jnp.cumsum(counts) - counts)[tile_expert], GMM_TILE) + within
    valid = (within < jnp.repeat(counts[tile_expert], GMM_TILE)) & (jnp.repeat(tile_used, GMM_TILE) > 0)
    src = jnp.where(valid, tok_sorted[jnp.clip(compact, 0, n * TOP_K - 1)], 0)
    xs = _sc_gather(u_packed, src)

    de = w_down.shape[1]
    wg_t, wl_t = _wprep(w_gate_up)
    ys = _gmm(tile_expert, tile_used, xs, wg_t, wl_t,
              b_gate_up[:, 0::2].reshape(N_EXPERTS, 1, de).astype(F32),
              b_gate_up[:, 1::2].reshape(N_EXPERTS, 1, de).astype(F32),
              w_down, b_down.reshape(N_EXPERTS, 1, d).astype(F32))
    yk = _sc_gather(ys, pos.T.reshape(-1)).reshape(TOP_K, n, d // 2)

    out = _final(h1, yk, gates, p.reshape(n, -1), g_ple.reshape(1, d),
                 w_ple_gate.astype(BF16), w_ple_proj.astype(BF16))
    return out.reshape(batch, seq, d)


def kernel(x, p, g_mix, w_in, b_f, g_qa, g_ka, g_qb, g_kb, w_o, g_ffn, w_router, b_router,
           w_gate_up, b_gate_up, w_down, b_down, g_ple, w_ple_gate, w_ple_proj):
    h = x
    for i in range(g_mix.shape[0]):
        h = _layer(h, p[i], g_mix[i], w_in[i], b_f[i], g_qa[i], g_ka[i], g_qb[i], g_kb[i], w_o[i],
                   g_ffn[i], w_router[i], b_router[i], w_gate_up[i], b_gate_up[i], w_down[i],
                   b_down[i], g_ple[i], w_ple_gate[i], w_ple_proj[i])
    return h
```

```python
import functools

import jax
import jax.numpy as jnp
from jax import lax
from jax.experimental import pallas as pl
from jax.experimental.pallas import tpu as pltpu
from jax.experimental.pallas import tpu_sc as plsc

HEAD_DIM = 64
N_HEADS_A = 8
N_HEADS_B = 8
WIDTH_A = N_HEADS_A * HEAD_DIM
WIDTH_B = N_HEADS_B * HEAD_DIM
DILATED_PATTERNS = ((128, 1), (512, 4), (2048, 16))
BLOCK = 128
N_EXPERTS = 32
TOP_K = 4
SWIGLU_LIMIT = 7.0
SWIGLU_ALPHA = 1.702
NORM_EPS = 1e-6

LANES = 128
PAIR = LANES // HEAD_DIM
ROW_TILE = 512
GMM_TILE = 512
FOX_TILE = 512
UNITS_PER_STEP = 8
SC_CHUNK = 32
SC_DEPTH = 4
VMEM_LIMIT = 56 * 1024 * 1024

F32 = jnp.float32
BF16 = jnp.bfloat16
NEG_INF = float("-inf")
NT_DIMS = (((1,), (1,)), ((), ()))
LOG2E = 1.4426950408889634


def _cparams(*sem):
    return pltpu.CompilerParams(dimension_semantics=sem, vmem_limit_bytes=VMEM_LIMIT)


def _rms(x, g):
    return x * lax.rsqrt(jnp.mean(x * x, axis=-1, keepdims=True) + NORM_EPS) * g


def _lane_iota():
    return lax.broadcasted_iota(jnp.int32, (1, LANES), 1)


def _head_lane_mask(h):
    lane = _lane_iota()
    return (lane >= h * HEAD_DIM) & (lane < (h + 1) * HEAD_DIM)


def _merge_heads(acc0, acc1):
    first = _head_lane_mask(0)
    num = jnp.where(first, acc0, acc1)
    den = pltpu.roll(jnp.where(first, acc1, acc0), HEAD_DIM, axis=1)
    return num, den


def _in_proj_kernel(x_ref, g_ref, w_ref, wf_ref, bf_ref, gain_ref, bd_ref, z_ref, lf_ref):
    u = _rms(x_ref[...], g_ref[...]).astype(BF16)
    chunk = WIDTH_A
    normed = {0: 0, 1: 1, 3: 2, 4: 3}
    for c in range(6):
        acc = jnp.dot(u, w_ref[:, c * chunk:(c + 1) * chunk], preferred_element_type=F32)
        if c in normed:
            sq = (acc * acc).astype(BF16)
            half = chunk // 2
            ss = jnp.concatenate(
                [jnp.dot(sq[:, j * half:(j + 1) * half], bd_ref[...], preferred_element_type=F32)
                 for j in range(2)], axis=1)
            r = normed[c]
            acc = acc * lax.rsqrt(ss * (1.0 / HEAD_DIM) + NORM_EPS) * gain_ref[r:r + 1, :]
        z_ref[:, c * chunk:(c + 1) * chunk] = acc.astype(BF16)
    zf = jnp.dot(u, wf_ref[...], preferred_element_type=F32) + bf_ref[...]
    lf_ref[...] = jax.nn.log_sigmoid(zf)


def _in_proj(x2, g_mix, w_qkv, w_f, b_f, gains, bd):
    n, d = x2.shape
    cols = w_qkv.shape[1]
    tm = ROW_TILE
    const = lambda i: (0, 0)
    return pl.pallas_call(
        _in_proj_kernel,
        grid=(n // tm,),
        in_specs=[
            pl.BlockSpec((tm, d), lambda i: (i, 0)),
            pl.BlockSpec((1, d), const),
            pl.BlockSpec((d, cols), const),
            pl.BlockSpec((d, LANES), const),
            pl.BlockSpec((1, LANES), const),
            pl.BlockSpec(gains.shape, const),
            pl.BlockSpec(bd.shape, const),
        ],
        out_specs=[
            pl.BlockSpec((tm, cols), lambda i: (i, 0)),
            pl.BlockSpec((tm, LANES), lambda i: (i, 0)),
        ],
        out_shape=[
            jax.ShapeDtypeStruct((n, cols), BF16),
            jax.ShapeDtypeStruct((n, LANES), F32),
        ],
        compiler_params=_cparams("parallel"),
        name="in_proj",
    )(x2, g_mix, w_qkv, w_f, b_f, gains, bd)


def _cumsum_kernel(lf_ref, tri_ref, ccol_ref):
    s = lf_ref.shape[0]
    carry = jnp.zeros((1, LANES), F32)
    for blk in range(s // BLOCK):
        rows = slice(blk * BLOCK, (blk + 1) * BLOCK)
        part = jnp.dot(tri_ref[...], lf_ref[rows, :], precision=lax.Precision.HIGHEST,
                       preferred_element_type=F32) + carry
        ccol_ref[rows, :] = part
        carry = part[BLOCK - 1:BLOCK, :]


def _cumsum(logf, batch, seq):
    tri = (lax.broadcasted_iota(jnp.int32, (BLOCK, BLOCK), 0)
           >= lax.broadcasted_iota(jnp.int32, (BLOCK, BLOCK), 1)).astype(F32)
    return pl.pallas_call(
        _cumsum_kernel,
        grid=(batch,),
        in_specs=[
            pl.BlockSpec((seq, LANES), lambda b: (b, 0)),
            pl.BlockSpec((BLOCK, BLOCK), lambda b: (0, 0)),
        ],
        out_specs=pl.BlockSpec((seq, LANES), lambda b: (b, 0)),
        out_shape=jax.ShapeDtypeStruct((batch * seq, LANES), F32),
        compiler_params=_cparams("parallel"),
        name="cumsum",
    )(logf, tri)


def _split3(c):
    hi = c.astype(BF16)
    r1 = c - hi.astype(F32)
    mid = r1.astype(BF16)
    lo = (r1 - mid.astype(F32)).astype(BF16)
    return hi, mid, lo


def _fox_operand(x, c, h, key_side):
    lane = _lane_iota()
    base = HEAD_DIM * (1 - h)
    hi, mid, lo = (piece.astype(F32) for piece in _split3(-c if key_side else c))
    own = (base + 3) if key_side else base
    other = base if key_side else (base + 3)
    feat = jnp.where(lane == own, hi, jnp.where(lane == own + 1, mid, jnp.where(lane == own + 2, lo,
           jnp.where((lane >= other) & (lane < other + 3), 1.0, 0.0))))
    return jnp.where(_head_lane_mask(h), x, feat.astype(BF16))


def _fox_kernel(q_ref, k_ref, v_ref, c_ref, o_ref, kf, vf, s_scr, m_scr, acc_scr, *, tile):
    pair = pl.program_id(1)
    i = pl.program_id(2)
    lane = _lane_iota()

    def gate_column(c_tile, h):
        return jnp.sum(jnp.where(lane == PAIR * pair + h, c_tile, 0.0), axis=-1, keepdims=True)

    @pl.when(i == 0)
    def _():
        c_all = c_ref[...] * LOG2E
        for h in range(PAIR):
            kf[h] = _fox_operand(k_ref[...], gate_column(c_all, h), h, True)
            vf[h] = jnp.where(_head_lane_mask(h), v_ref[...], jnp.ones_like(v_ref[...]))

    row0 = pl.multiple_of(i * tile, tile)
    c_tile = c_ref[pl.ds(row0, tile), :] * LOG2E
    q = q_ref[...]
    qf = [_fox_operand(q, gate_column(c_tile, h), h, False) for h in range(PAIR)]

    def lane_groups_max(s):
        m = s[:, :LANES]
        for g in range(1, tile // LANES):
            m = jnp.maximum(m, s[:, g * LANES:(g + 1) * LANES])
        return m

    def scores(j, h):
        off = pl.multiple_of(j * tile, tile)
        return lax.dot_general(qf[h], kf[h, pl.ds(off, tile), :], NT_DIMS, preferred_element_type=F32)

    row = lax.broadcasted_iota(jnp.int32, (tile, tile), 0)
    col = lax.broadcasted_iota(jnp.int32, (tile, tile), 1)
    for h in range(PAIR):
        s = jnp.where(col <= row, scores(i, h), NEG_INF)
        s_scr[h, :, pl.ds(row0, tile)] = s
        m_scr[h] = lane_groups_max(s)

    def pass1(j, _):
        off = pl.multiple_of(j * tile, tile)
        for h in range(PAIR):
            s = scores(j, h)
            s_scr[h, :, pl.ds(off, tile)] = s
            m_scr[h] = jnp.maximum(m_scr[h], lane_groups_max(s))
        return 0
    lax.fori_loop(0, i, pass1, 0)

    m = [jnp.max(m_scr[h], axis=-1, keepdims=True) for h in range(PAIR)]
    acc_scr[...] = jnp.zeros_like(acc_scr)

    def pass2(j, _):
        off = pl.multiple_of(j * tile, tile)
        for h in range(PAIR):
            p = jnp.exp2(s_scr[h, :, pl.ds(off, tile)] - m[h]).astype(BF16)
            acc_scr[h] += jnp.dot(p, vf[h, pl.ds(off, tile), :], preferred_element_type=F32)
        return 0
    lax.fori_loop(0, i + 1, pass2, 0)

    num, den = _merge_heads(acc_scr[0], acc_scr[1])
    o_ref[...] = (num / den).astype(o_ref.dtype)


def _fox(z, ccol, batch, seq):
    n = z.shape[0]
    tile = FOX_TILE
    nq = seq // tile
    npair = N_HEADS_B // PAIR
    base = 3 * WIDTH_A // LANES
    qcol, kcol, vcol = base, base + WIDTH_B // LANES, base + 2 * WIDTH_B // LANES
    return pl.pallas_call(
        functools.partial(_fox_kernel, tile=tile),
        grid=(batch, npair, nq),
        in_specs=[
            pl.BlockSpec((tile, LANES), lambda b, p, i: (b * nq + i, qcol + p)),
            pl.BlockSpec((seq, LANES), lambda b, p, i: (b, kcol + p)),
            pl.BlockSpec((seq, LANES), lambda b, p, i: (b, vcol + p)),
            pl.BlockSpec((seq, LANES), lambda b, p, i: (b, 0)),
        ],
        out_specs=pl.BlockSpec((tile, LANES), lambda b, p, i: (b * nq + i, p)),
        out_shape=jax.ShapeDtypeStruct((n, WIDTH_B), BF16),
        scratch_shapes=[
            pltpu.VMEM((PAIR, seq, LANES), BF16),
            pltpu.VMEM((PAIR, seq, LANES), BF16),
            pltpu.VMEM((PAIR, tile, seq), F32),
            pltpu.VMEM((PAIR, tile, LANES), F32),
            pltpu.VMEM((PAIR, tile, LANES), F32),
        ],
        compiler_params=_cparams("parallel", "parallel", "arbitrary"),
        name="fox",
    )(z, z, z, ccol)


def _dilated_kernel(slope_ref, q_ref, k_ref, v_ref, o_ref,
                    natf, p4f, p4b, p16b, qfeat, kfeat, vals, dens, maxs, *, seq):
    pair = pl.program_id(1)
    lane = _lane_iota()
    first = _head_lane_mask(0)
    quarter = seq // 4
    units = seq // BLOCK

    def deinterleave(src, t, span_start, span):
        return [src[t, pl.ds(span_start + r, span // 4, stride=4), :] for r in range(4)]

    for t, ref in enumerate((q_ref, k_ref, v_ref)):
        natf[t] = ref[...].astype(F32)
        for r, part in enumerate(deinterleave(natf, t, 0, seq)):
            p4f[t, pl.ds(r * quarter, quarter), :] = part
            p4b[t, pl.ds(r * quarter, quarter), :] = part.astype(BF16)
        for r4 in range(4):
            for r, part in enumerate(deinterleave(p4f, t, r4 * quarter, quarter)):
                p16b[t, pl.ds(r4 * quarter + r * (quarter // 4), quarter // 4), :] = part.astype(BF16)

    qi = lax.broadcasted_iota(jnp.int32, (BLOCK, LANES), 0).astype(F32)
    kj = lax.broadcasted_iota(jnp.int32, (2 * BLOCK, LANES), 0).astype(F32)
    for p, (_, dil) in enumerate(DILATED_PATTERNS):
        for h in range(PAIR):
            sd = slope_ref[PAIR * pair + h] * float(dil)
            base = HEAD_DIM * (1 - h)
            qfeat[p * PAIR + h] = jnp.where(lane == base, -(qi + float(BLOCK)) * sd,
                                            jnp.where(lane == base + 1, 1.0, 0.0)).astype(BF16)
            kfeat[p * PAIR + h] = jnp.where(lane == base, 1.0,
                                            jnp.where(lane == base + 1, kj * sd, 0.0)).astype(BF16)

    bq = lax.broadcasted_iota(jnp.int32, (BLOCK, 2 * BLOCK), 0)
    bk = lax.broadcasted_iota(jnp.int32, (BLOCK, 2 * BLOCK), 1)
    rel = bq + BLOCK - bk
    band = (rel >= 0) & (rel <= BLOCK)

    def unit(p, srcs, u, prev_valid):
        qs, ks, vs = srcs
        start = pl.multiple_of(u * BLOCK, BLOCK)
        prev = pl.multiple_of(jnp.maximum(start - BLOCK, 0), BLOCK)
        qb = qs[pl.ds(start, BLOCK), :]
        kk = jnp.concatenate([ks[pl.ds(prev, BLOCK), :], ks[pl.ds(start, BLOCK), :]], axis=0)
        vv = jnp.concatenate([vs[pl.ds(prev, BLOCK), :], vs[pl.ds(start, BLOCK), :]], axis=0)
        if prev_valid is True:
            ok = band
        elif prev_valid is False:
            ok = band & (bk >= BLOCK)
        else:
            ok = band & ((bk >= BLOCK) | prev_valid)
        accs, ms = [], []
        for h in range(PAIR):
            in_h = _head_lane_mask(h)
            qh = jnp.where(in_h, qb, qfeat[p * PAIR + h])
            kh = jnp.where(in_h, kk, kfeat[p * PAIR + h])
            vh = jnp.where(in_h, vv, jnp.ones_like(vv))
            s = lax.dot_general(qh, kh, NT_DIMS, preferred_element_type=F32)
            s = jnp.where(ok, s, NEG_INF)
            m = jnp.max(s, axis=-1, keepdims=True)
            pr = jnp.exp(s - m).astype(BF16)
            accs.append(jnp.dot(pr, vh, preferred_element_type=F32))
            ms.append(m)
        num, den = _merge_heads(*accs)
        vals[p, pl.ds(start, BLOCK), :] = num
        dens[p, pl.ds(start, BLOCK), :] = den
        maxs[p, pl.ds(start, BLOCK), :] = jnp.where(first, ms[0], ms[1])

    group = UNITS_PER_STEP
    sources = ((q_ref, k_ref, v_ref), tuple(p4b.at[t] for t in range(3)), tuple(p16b.at[t] for t in range(3)))
    for p, (_, dil) in enumerate(DILATED_PATTERNS):
        per_class = units // dil

        def step(g, _, p=p, per_class=per_class):
            for e in range(group):
                u = g * group + e
                if per_class >= group:
                    prev_valid = (u % per_class != 0) if e == 0 else True
                else:
                    prev_valid = e % per_class != 0
                unit(p, sources[p], u, prev_valid)
            return 0
        lax.fori_loop(0, units // group, step, 0)

    def interleave(src, s_t, dst, d_t, span_start, span):
        for r in range(4):
            dst[d_t, pl.ds(span_start + r, span // 4, stride=4), :] = \
                src[s_t, pl.ds(span_start + r * (span // 4), span // 4), :]

    for t, arr in enumerate((vals, dens, maxs)):
        for r4 in range(4):
            interleave(arr, 2, p4f, t, r4 * quarter, quarter)
        interleave(p4f, t, arr, 2, 0, seq)
        natf[t] = arr[1]
        interleave(natf, t, arr, 1, 0, seq)

    m_all = jnp.maximum(jnp.maximum(maxs[0], maxs[1]), maxs[2])
    num = jnp.zeros((seq, LANES), F32)
    den = jnp.zeros((seq, LANES), F32)
    for p in range(3):
        e = jnp.exp(maxs[p] - m_all)
        num = num + e * vals[p]
        den = den + e * dens[p]
    o_ref[...] = (num / den).astype(o_ref.dtype)


def _dilated(z, slopes, batch, seq):
    n = z.shape[0]
    npair = N_HEADS_A // PAIR
    npat = len(DILATED_PATTERNS)
    qcol, kcol, vcol = 0, WIDTH_A // LANES, 2 * WIDTH_A // LANES
    blk = lambda c0: pl.BlockSpec((seq, LANES), lambda b, p: (b, c0 + p))
    return pl.pallas_call(
        functools.partial(_dilated_kernel, seq=seq),
        grid=(batch, npair),
        in_specs=[pl.BlockSpec(memory_space=pltpu.SMEM), blk(qcol), blk(kcol), blk(vcol)],
        out_specs=pl.BlockSpec((seq, LANES), lambda b, p: (b, p)),
        out_shape=jax.ShapeDtypeStruct((n, WIDTH_A), BF16),
        scratch_shapes=[
            pltpu.VMEM((3, seq, LANES), F32),
            pltpu.VMEM((3, seq, LANES), F32),
            pltpu.VMEM((3, seq, LANES), BF16),
            pltpu.VMEM((3, seq, LANES), BF16),
            pltpu.VMEM((npat * PAIR, BLOCK, LANES), BF16),
            pltpu.VMEM((npat * PAIR, 2 * BLOCK, LANES), BF16),
            pltpu.VMEM((npat, seq, LANES), F32),
            pltpu.VMEM((npat, seq, LANES), F32),
            pltpu.VMEM((npat, seq, LANES), F32),
        ],
        compiler_params=_cparams("parallel", "parallel"),
        name="dilated",
    )(slopes, z, z, z)


def _pack_bf16_pairs(a, b):
    hi = pltpu.bitcast(a.astype(BF16).astype(F32), jnp.int32)
    lo = pltpu.bitcast(b.astype(BF16).astype(F32), jnp.int32)
    return (hi & jnp.int32(-65536)) | lax.shift_right_logical(lo, jnp.int32(16))


def _unpack_bf16_pairs(w):
    a = pltpu.bitcast(w & jnp.int32(-65536), F32)
    b = pltpu.bitcast(lax.shift_left(w, jnp.int32(16)), F32)
    return a, b


def _post_attn_kernel(ma_ref, mb_ref, x_ref, wo_ref, g_ref, wr_ref, br_ref, tri_ref,
                      h_ref, up_ref, idx_ref, gate_ref, rank_ref, cnt_ref, carry):
    @pl.when(pl.program_id(0) == 0)
    def _():
        carry[...] = jnp.zeros_like(carry)

    y = jnp.dot(ma_ref[...], wo_ref[:WIDTH_A, :], preferred_element_type=F32)
    y = y + jnp.dot(mb_ref[...], wo_ref[WIDTH_A:, :], preferred_element_type=F32)
    h = x_ref[...] + y
    h_ref[...] = h
    u = _rms(h, g_ref[...])
    half = u.shape[1] // 2
    up_ref[...] = _pack_bf16_pairs(u[:, :half], u[:, half:])

    u_hi = u.astype(BF16)
    u_lo = (u - u_hi.astype(F32)).astype(BF16)
    hi_terms = jnp.dot(u_hi, wr_ref[...], preferred_element_type=F32)
    logits = (hi_terms[:, :LANES] + hi_terms[:, LANES:]
              + jnp.dot(u_lo, wr_ref[:, :LANES], preferred_element_type=F32)) + br_ref[...]
    lane = lax.broadcasted_iota(jnp.int32, logits.shape, 1).astype(F32)
    work = logits
    idxs, tops = [], []
    for _ in range(TOP_K):
        top = jnp.max(work, axis=-1, keepdims=True)
        idx = jnp.min(jnp.where(work == top, lane, float(LANES)), axis=-1, keepdims=True)
        work = jnp.where(lane == idx, NEG_INF, work)
        idxs.append(idx)
        tops.append(top)
    exps = [jnp.exp(t - tops[0]) for t in tops]
    total = exps[0] + exps[1] + exps[2] + exps[3]

    onehot = jnp.zeros(logits.shape, F32)
    for idx in idxs:
        onehot = onehot + (lane == idx).astype(F32)
    before = jnp.dot(tri_ref[...], onehot.astype(BF16), preferred_element_type=F32) + carry[...]
    carry[...] = carry[...] + jnp.sum(onehot, axis=0, keepdims=True)
    cnt_ref[...] = carry[...]

    idx_out = jnp.zeros(logits.shape, F32)
    gate_out = jnp.zeros(logits.shape, F32)
    rank_out = jnp.zeros(logits.shape, F32)
    for k in range(TOP_K):
        rank_k = jnp.sum(jnp.where(lane == idxs[k], before, 0.0), axis=-1, keepdims=True)
        idx_out = jnp.where(lane == float(k), idxs[k], idx_out)
        gate_out = jnp.where(lane == float(k), exps[k] / total, gate_out)
        rank_out = jnp.where(lane == float(k), rank_k, rank_out)
    idx_ref[...] = idx_out.astype(jnp.int32)
    gate_ref[...] = gate_out
    rank_ref[...] = rank_out.astype(jnp.int32)


def _post_attn(mix_a, mix_b, x2, w_o, g_ffn, w_r, b_r):
    n, d = x2.shape
    tm = ROW_TILE
    tri = (lax.broadcasted_iota(jnp.int32, (tm, tm), 0)
           > lax.broadcasted_iota(jnp.int32, (tm, tm), 1)).astype(BF16)
    const = lambda i: (0, 0)
    row = lambda w: pl.BlockSpec((tm, w), lambda i: (i, 0))
    return pl.pallas_call(
        _post_attn_kernel,
        grid=(n // tm,),
        in_specs=[
            row(WIDTH_A), row(WIDTH_B), row(d),
            pl.BlockSpec(w_o.shape, const),
            pl.BlockSpec((1, d), const),
            pl.BlockSpec(w_r.shape, const),
            pl.BlockSpec((1, LANES), const),
            pl.BlockSpec((tm, tm), const),
        ],
        out_specs=[row(d), row(d // 2), row(LANES), row(LANES), row(LANES),
                   pl.BlockSpec((1, LANES), const)],
        out_shape=[
            jax.ShapeDtypeStruct((n, d), F32),
            jax.ShapeDtypeStruct((n, d // 2), jnp.int32),
            jax.ShapeDtypeStruct((n, LANES), jnp.int32),
            jax.ShapeDtypeStruct((n, LANES), F32),
            jax.ShapeDtypeStruct((n, LANES), jnp.int32),
            jax.ShapeDtypeStruct((1, LANES), F32),
        ],
        scratch_shapes=[pltpu.VMEM((1, LANES), F32)],
        compiler_params=_cparams("arbitrary"),
        name="post_attn",
    )(mix_a, mix_b, x2, w_o, g_ffn, w_r, b_r, tri)


def _wprep_kernel(w_ref, wg_ref, wl_ref, wt):
    d, cols = w_ref.shape[1:]
    de = cols // 2
    for j in range(d // LANES):
        lanes = slice(j * LANES, (j + 1) * LANES)
        wt[j] = w_ref[0, lanes, :].T
        wg_ref[0, :, lanes] = wt[j, pl.ds(0, de, stride=2), :].astype(BF16)
        wl_ref[0, :, lanes] = wt[j, pl.ds(1, de, stride=2), :].astype(BF16)


def _wprep(w_gate_up):
    ne, d, cols = w_gate_up.shape
    de = cols // 2
    out = pl.BlockSpec((1, de, d), lambda e: (e, 0, 0))
    return pl.pallas_call(
        _wprep_kernel,
        grid=(ne,),
        in_specs=[pl.BlockSpec((1, d, cols), lambda e: (e, 0, 0))],
        out_specs=[out, out],
        out_shape=[jax.ShapeDtypeStruct((ne, de, d), BF16)] * 2,
        scratch_shapes=[pltpu.VMEM((d // LANES, cols, LANES), F32)],
        compiler_params=_cparams("parallel"),
        name="wprep",
    )(w_gate_up)


def _sc_gather(table, idx):
    info = plsc.get_sparse_core_info()
    workers = info.num_cores * info.num_subcores
    rows, width = idx.shape[0], table.shape[1]
    chunk, depth = SC_CHUNK, SC_DEPTH
    assert rows % (workers * chunk * depth) == 0
    per_worker = rows // workers
    nchunks = per_worker // chunk
    mesh = plsc.VectorSubcoreMesh(core_axis_name="c", subcore_axis_name="s")

    @functools.partial(
        pl.kernel, mesh=mesh,
        out_type=jax.ShapeDtypeStruct((rows, width), table.dtype),
        scratch_types=[
            pltpu.VMEM((nchunks, chunk), jnp.int32),
            pltpu.VMEM((depth, chunk, width), table.dtype),
            pltpu.SemaphoreType.DMA((depth,)),
            pltpu.SemaphoreType.DMA((depth,)),
        ],
    )
    def gather_kernel(table_hbm, idx_hbm, out_hbm, idx_v, rows_v, gsem, wsem):
        wid = lax.axis_index("s") * info.num_cores + lax.axis_index("c")
        base = wid * per_worker
        pltpu.sync_copy(idx_hbm.at[wid], idx_v)

        def gather(c, b):
            return pltpu.make_async_copy(table_hbm.at[idx_v.at[c]], rows_v.at[b], gsem.at[b])

        def write(c, b):
            off = pl.multiple_of(base + c * chunk, chunk)
            return pltpu.make_async_copy(rows_v.at[b], out_hbm.at[pl.ds(off, chunk)], wsem.at[b])

        @pl.loop(0, nchunks, step=depth)
        def _(c0):
            for b in range(depth):
                gather(c0 + b, b).start()
            for b in range(depth):
                gather(c0 + b, b).wait()
                write(c0 + b, b).start()
            for b in range(depth):
                write(c0 + b, b).wait()

    return gather_kernel(table, idx.reshape(workers, nchunks, chunk))


def _gmm_kernel(te_ref, used_ref, xs_ref, wg_ref, wl_ref, bg_ref, bl_ref, wd_ref, bd_ref, ys_ref):
    i = pl.program_id(0)

    @pl.when(used_ref[i] > 0)
    def _():
        a, b = _unpack_bf16_pairs(xs_ref[...])
        x = jnp.concatenate([a, b], axis=1).astype(BF16)
        hg = lax.dot_general(x, wg_ref[0], NT_DIMS, preferred_element_type=F32) + bg_ref[0]
        hl = lax.dot_general(x, wl_ref[0], NT_DIMS, preferred_element_type=F32) + bl_ref[0]
        xg = jnp.minimum(hg, SWIGLU_LIMIT)
        xl = jnp.clip(hl, -SWIGLU_LIMIT, SWIGLU_LIMIT)
        act = xg * jax.nn.sigmoid(SWIGLU_ALPHA * xg) * (xl + 1.0)
        out = jnp.dot(act.astype(BF16), wd_ref[0].astype(BF16), preferred_element_type=F32) + bd_ref[0]
        half = out.shape[1] // 2
        ys_ref[...] = _pack_bf16_pairs(out[:, :half], out[:, half:])

    @pl.when(used_ref[i] == 0)
    def _():
        ys_ref[...] = jnp.zeros_like(ys_ref)


def _gmm(tile_expert, tile_used, xs, wg_t, wl_t, b_glu, b_lin, w_down, b_down):
    rows, half = xs.shape
    d = 2 * half
    de = wg_t.shape[1]
    tm = GMM_TILE
    wspec = lambda shape: pl.BlockSpec((1,) + shape, lambda i, te, used: (te[i], 0, 0))
    grid_spec = pltpu.PrefetchScalarGridSpec(
        num_scalar_prefetch=2,
        grid=(rows // tm,),
        in_specs=[
            pl.BlockSpec((tm, half), lambda i, te, used: (i, 0)),
            wspec((de, d)), wspec((de, d)), wspec((1, de)), wspec((1, de)),
            wspec((de, d)), wspec((1, d)),
        ],
        out_specs=pl.BlockSpec((tm, half), lambda i, te, used: (i, 0)),
    )
    return pl.pallas_call(
        _gmm_kernel,
        grid_spec=grid_spec,
        out_shape=jax.ShapeDtypeStruct((rows, half), jnp.int32),
        compiler_params=_cparams("arbitrary"),
        name="gmm",
    )(tile_expert, tile_used, xs, wg_t, wl_t, b_glu, b_lin, w_down, b_down)


def _final_kernel(h_ref, yk_ref, gate_ref, p_ref, g_ref, wg_ref, wp_ref, o_ref):
    gates = gate_ref[...]
    h = h_ref[...]
    for k in range(TOP_K):
        h = h + gates[:, k:k + 1] * jnp.concatenate(_unpack_bf16_pairs(yk_ref[k]), axis=1)
    u = _rms(h, g_ref[...]).astype(BF16)
    gate = jax.nn.sigmoid(jnp.dot(u, wg_ref[...], preferred_element_type=F32))
    proj = jnp.dot(p_ref[...].astype(BF16), wp_ref[...], preferred_element_type=F32)
    o_ref[...] = h + gate * proj


def _final(h1, yk, gates, p2, g_ple, w_gate, w_proj):
    n, d = h1.shape
    tm = ROW_TILE
    const = lambda i: (0, 0)
    return pl.pallas_call(
        _final_kernel,
        grid=(n // tm,),
        in_specs=[
            pl.BlockSpec((tm, d), lambda i: (i, 0)),
            pl.BlockSpec((TOP_K, tm, d // 2), lambda i: (0, i, 0)),
            pl.BlockSpec((tm, LANES), lambda i: (i, 0)),
            pl.BlockSpec((tm, p2.shape[1]), lambda i: (i, 0)),
            pl.BlockSpec((1, d), const),
            pl.BlockSpec(w_gate.shape, const),
            pl.BlockSpec(w_proj.shape, const),
        ],
        out_specs=pl.BlockSpec((tm, d), lambda i: (i, 0)),
        out_shape=jax.ShapeDtypeStruct((n, d), F32),
        compiler_params=_cparams("parallel"),
        name="final",
    )(h1, yk, gates, p2, g_ple, w_gate, w_proj)


def _layer(h, p, g_mix, w_in, b_f, g_qa, g_ka, g_qb, g_kb, w_o, g_ffn, w_router, b_router,
           w_gate_up, b_gate_up, w_down, b_down, g_ple, w_ple_gate, w_ple_proj):
    batch, seq, d = h.shape
    n = batch * seq
    assert tuple(dil for _, dil in DILATED_PATTERNS) == (1, 4, 16)
    for window, dil in DILATED_PATTERNS:
        per_class = seq // BLOCK // dil
        assert window // dil == BLOCK and seq % (dil * BLOCK) == 0
        assert per_class % UNITS_PER_STEP == 0 or UNITS_PER_STEP % per_class == 0
    assert n % ROW_TILE == 0 and d % (2 * LANES) == 0 and seq % FOX_TILE == 0
    x2 = h.reshape(n, d)

    qkv_cols = 3 * WIDTH_A + 3 * WIDTH_B
    w_qkv = w_in[:, :qkv_cols].astype(BF16)
    w_f = jnp.pad(w_in[:, qkv_cols:], ((0, 0), (0, LANES - N_HEADS_B))).astype(BF16)
    b_fp = jnp.pad(b_f.astype(F32), (0, LANES - N_HEADS_B)).reshape(1, LANES)
    scale = HEAD_DIM ** -0.5
    gains = jnp.stack([jnp.tile(g_qa, N_HEADS_A) * scale, jnp.tile(g_ka, N_HEADS_A),
                       jnp.tile(g_qb, N_HEADS_B) * (scale * LOG2E), jnp.tile(g_kb, N_HEADS_B)]).astype(F32)
    hid = jnp.arange(2 * LANES) // HEAD_DIM
    bd = (hid[:, None] == hid[None, :]).astype(BF16)

    z, logf = _in_proj(x2, g_mix.reshape(1, d), w_qkv, w_f, b_fp, gains, bd)
    ccol = _cumsum(logf, batch, seq)

    slopes = 2.0 ** (-8.0 * jnp.arange(1, N_HEADS_A + 1, dtype=F32) / N_HEADS_A)
    mix_a = _dilated(z, slopes, batch, seq)
    mix_b = _fox(z, ccol, batch, seq)

    w_r = jnp.pad(w_router.astype(F32), ((0, 0), (0, LANES - N_EXPERTS)))
    w_r_hi = w_r.astype(BF16)
    w_r = jnp.concatenate([w_r_hi, (w_r - w_r_hi.astype(F32)).astype(BF16)], axis=1)
    b_r = jnp.concatenate([b_router.astype(F32), jnp.full((LANES - N_EXPERTS,), NEG_INF, F32)]).reshape(1, LANES)
    h1, u_packed, top_idx, gates, rank, counts = _post_attn(
        mix_a, mix_b, x2, w_o.astype(BF16), g_ffn.reshape(1, d), w_r, b_r)

    counts = counts[0, :N_EXPERTS].astype(jnp.int32)
    tiles_per = (counts + GMM_TILE - 1) // GMM_TILE
    tile_end = jnp.cumsum(tiles_per)
    starts = (tile_end - tiles_per) * GMM_TILE
    n_tiles = n * TOP_K // GMM_TILE + N_EXPERTS
    tile_ids = jnp.arange(n_tiles, dtype=jnp.int32)
    tile_used = (tile_ids < tile_end[-1]).astype(jnp.int32)
    last_used = jnp.minimum(tile_ids, tile_end[-1] - 1)
    tile_expert = jnp.sum((last_used[:, None] >= tile_end[None, :]).astype(jnp.int32), axis=1)
    tile_expert = jnp.minimum(tile_expert, N_EXPERTS - 1)
    idx4 = top_idx[:, :TOP_K]
    pos = starts[idx4] + rank[:, :TOP_K]
    tok_sorted = (jnp.argsort(pos.reshape(-1)) // TOP_K).astype(jnp.int32)
    sorted_rows = jnp.arange(n_tiles * GMM_TILE, dtype=jnp.int32)
    within = sorted_rows - jnp.repeat(starts[tile_expert], GMM_TILE)
    compact = jnp.repeat((jnp.cumsum(counts) - counts)[tile_expert], GMM_TILE) + within
    valid = (within < jnp.repeat(counts[tile_expert], GMM_TILE)) & (jnp.repeat(tile_used, GMM_TILE) > 0)
    src = jnp.where(valid, tok_sorted[jnp.clip(compact, 0, n * TOP_K - 1)], sorted_rows % n)
    xs = _sc_gather(u_packed, src)

    de = w_down.shape[1]
    wg_t, wl_t = _wprep(w_gate_up)
    ys = _gmm(tile_expert, tile_used, xs, wg_t, wl_t,
              b_gate_up[:, 0::2].reshape(N_EXPERTS, 1, de).astype(F32),
              b_gate_up[:, 1::2].reshape(N_EXPERTS, 1, de).astype(F32),
              w_down, b_down.reshape(N_EXPERTS, 1, d).astype(F32))
    yk = _sc_gather(ys, pos.T.reshape(-1)).reshape(TOP_K, n, d // 2)

    out = _final(h1, yk, gates, p.reshape(n, -1), g_ple.reshape(1, d),
                 w_ple_gate.astype(BF16), w_ple_proj.astype(BF16))
    return out.reshape(batch, seq, d)


def kernel(x, p, g_mix, w_in, b_f, g_qa, g_ka, g_qb, g_kb, w_o, g_ffn, w_router, b_router,
           w_gate_up, b_gate_up, w_down, b_down, g_ple, w_ple_gate, w_ple_proj):
    h = x
    for i in range(g_mix.shape[0]):
        h = _layer(h, p[i], g_mix[i], w_in[i], b_f[i], g_qa[i], g_ka[i], g_qb[i], g_kb[i], w_o[i],
                   g_ffn[i], w_router[i], b_router[i], w_gate_up[i], b_gate_up[i], w_down[i],
                   b_down[i], g_ple[i], w_ple_gate[i], w_ple_proj[i])
    return h
```

```python
import functools

import jax
import jax.numpy as jnp
from jax import lax
from jax.experimental import pallas as pl
from jax.experimental.pallas import tpu as pltpu
from jax.experimental.pallas import tpu_sc as plsc

HEAD_DIM = 64
N_HEADS_A = 8
N_HEADS_B = 8
WIDTH_A = N_HEADS_A * HEAD_DIM
WIDTH_B = N_HEADS_B * HEAD_DIM
DILATED_PATTERNS = ((128, 1), (512, 4), (2048, 16))
BLOCK = 128
N_EXPERTS = 32
TOP_K = 4
SWIGLU_LIMIT = 7.0
SWIGLU_ALPHA = 1.702
NORM_EPS = 1e-6

LANES = 128
PAIR = LANES // HEAD_DIM
ROW_TILE = 512
GMM_TILE = 512
FOX_TILE = 512
UNITS_PER_STEP = 8
SC_CHUNK = 32
SC_DEPTH = 4
VMEM_LIMIT = 56 * 1024 * 1024

F32 = jnp.float32
BF16 = jnp.bfloat16
NEG_INF = float("-inf")
NT_DIMS = (((1,), (1,)), ((), ()))
LOG2E = 1.4426950408889634


def _cparams(*sem):
    return pltpu.CompilerParams(dimension_semantics=sem, vmem_limit_bytes=VMEM_LIMIT)


def _rms(x, g):
    return x * lax.rsqrt(jnp.mean(x * x, axis=-1, keepdims=True) + NORM_EPS) * g


def _lane_iota():
    return lax.broadcasted_iota(jnp.int32, (1, LANES), 1)


def _head_lane_mask(h):
    lane = _lane_iota()
    return (lane >= h * HEAD_DIM) & (lane < (h + 1) * HEAD_DIM)


def _merge_heads(acc0, acc1):
    first = _head_lane_mask(0)
    num = jnp.where(first, acc0, acc1)
    den = pltpu.roll(jnp.where(first, acc1, acc0), HEAD_DIM, axis=1)
    return num, den


def _in_proj_kernel(x_ref, g_ref, w_ref, wf_ref, bf_ref, gain_ref, bd_ref, z_ref, lf_ref):
    u = _rms(x_ref[...], g_ref[...]).astype(BF16)
    chunk = WIDTH_A
    normed = {0: 0, 1: 1, 3: 2, 4: 3}
    for c in range(6):
        acc = jnp.dot(u, w_ref[:, c * chunk:(c + 1) * chunk], preferred_element_type=F32)
        if c in normed:
            sq = (acc * acc).astype(BF16)
            half = chunk // 2
            ss = jnp.concatenate(
                [jnp.dot(sq[:, j * half:(j + 1) * half], bd_ref[...], preferred_element_type=F32)
                 for j in range(2)], axis=1)
            r = normed[c]
            acc = acc * lax.rsqrt(ss * (1.0 / HEAD_DIM) + NORM_EPS) * gain_ref[r:r + 1, :]
        z_ref[:, c * chunk:(c + 1) * chunk] = acc.astype(BF16)
    zf = jnp.dot(u, wf_ref[...], preferred_element_type=F32) + bf_ref[...]
    lf_ref[...] = jax.nn.log_sigmoid(zf)


def _in_proj(x2, g_mix, w_qkv, w_f, b_f, gains, bd):
    n, d = x2.shape
    cols = w_qkv.shape[1]
    tm = ROW_TILE
    const = lambda i: (0, 0)
    return pl.pallas_call(
        _in_proj_kernel,
        grid=(n // tm,),
        in_specs=[
            pl.BlockSpec((tm, d), lambda i: (i, 0)),
            pl.BlockSpec((1, d), const),
            pl.BlockSpec((d, cols), const),
            pl.BlockSpec((d, LANES), const),
            pl.BlockSpec((1, LANES), const),
            pl.BlockSpec(gains.shape, const),
            pl.BlockSpec(bd.shape, const),
        ],
        out_specs=[
            pl.BlockSpec((tm, cols), lambda i: (i, 0)),
            pl.BlockSpec((tm, LANES), lambda i: (i, 0)),
        ],
        out_shape=[
            jax.ShapeDtypeStruct((n, cols), BF16),
            jax.ShapeDtypeStruct((n, LANES), F32),
        ],
        compiler_params=_cparams("parallel"),
        name="in_proj",
    )(x2, g_mix, w_qkv, w_f, b_f, gains, bd)


def _cumsum_kernel(lf_ref, tri_ref, cpk_ref):
    s = lf_ref.shape[0]
    lane = _lane_iota()
    carry = jnp.zeros((1, LANES), F32)
    for blk in range(s // BLOCK):
        rows = slice(blk * BLOCK, (blk + 1) * BLOCK)
        part = jnp.dot(tri_ref[...], lf_ref[rows, :], precision=lax.Precision.HIGHEST,
                       preferred_element_type=F32) + carry
        carry = part[BLOCK - 1:BLOCK, :]
        c = part * LOG2E
        hi = c.astype(BF16).astype(F32)
        r1 = c - hi
        mid = r1.astype(BF16).astype(F32)
        lo = r1 - mid
        packed = jnp.where(lane < N_HEADS_B, hi,
                 jnp.where(lane < 2 * N_HEADS_B, pltpu.roll(mid, N_HEADS_B, axis=1),
                 jnp.where(lane < 3 * N_HEADS_B, pltpu.roll(lo, 2 * N_HEADS_B, axis=1),
                 jnp.where(lane == 3 * N_HEADS_B, 1.0, 0.0))))
        cpk_ref[rows, :] = packed.astype(BF16)


def _cumsum(logf, batch, seq):
    tri = (lax.broadcasted_iota(jnp.int32, (BLOCK, BLOCK), 0)
           >= lax.broadcasted_iota(jnp.int32, (BLOCK, BLOCK), 1)).astype(F32)
    return pl.pallas_call(
        _cumsum_kernel,
        grid=(batch,),
        in_specs=[
            pl.BlockSpec((seq, LANES), lambda b: (b, 0)),
            pl.BlockSpec((BLOCK, BLOCK), lambda b: (0, 0)),
        ],
        out_specs=pl.BlockSpec((seq, LANES), lambda b: (b, 0)),
        out_shape=jax.ShapeDtypeStruct((batch * seq, LANES), BF16),
        compiler_params=_cparams("parallel"),
        name="cumsum",
    )(logf, tri)


def _fox_features(cpk, pair, key_side):
    assert PAIR == 2
    r = lax.broadcasted_iota(jnp.int32, (LANES, PAIR * LANES), 0)
    c = lax.broadcasted_iota(jnp.int32, (LANES, PAIR * LANES), 1)
    hh = jnp.where(c >= LANES, 1, 0)
    slot = c - hh * LANES - HEAD_DIM * (1 - hh)
    head = PAIR * pair + hh
    piece_slot = slot - 3 if key_side else slot
    ones_slot = slot if key_side else slot - 3
    piece = (piece_slot >= 0) & (piece_slot < 3) & (r == N_HEADS_B * piece_slot + head)
    ones = (ones_slot >= 0) & (ones_slot < 3) & (r == 3 * N_HEADS_B)
    place = jnp.where(piece, -1.0 if key_side else 1.0, jnp.where(ones, 1.0, 0.0)).astype(BF16)
    return jnp.dot(cpk, place, preferred_element_type=F32).astype(BF16)


def _fox_kernel(q_ref, k_ref, v_ref, c_ref, o_ref, kf, vf, s_scr, m_scr, acc_scr, *, tile):
    pair = pl.program_id(1)
    i = pl.program_id(2)
    half = tile // 2
    in_head = [_head_lane_mask(h) for h in range(PAIR)]
    block = lambda feat, h: feat[:, h * LANES:(h + 1) * LANES]

    @pl.when(i == 0)
    def _():
        feat = _fox_features(c_ref[...], pair, True)
        for h in range(PAIR):
            kf[h] = jnp.where(in_head[h], k_ref[...], block(feat, h))
            vf[h] = jnp.where(in_head[h], v_ref[...], jnp.ones_like(v_ref[...]))

    row0 = pl.multiple_of(i * tile, tile)
    feat_q = _fox_features(c_ref[pl.ds(row0, tile), :], pair, False)
    q = q_ref[...]
    qf = [jnp.where(in_head[h], q, block(feat_q, h)) for h in range(PAIR)]

    def lane_groups_max(s):
        m = s[:, :LANES]
        for g in range(1, s.shape[1] // LANES):
            m = jnp.maximum(m, s[:, g * LANES:(g + 1) * LANES])
        return m

    def scores(qrows, off, width, h):
        return lax.dot_general(qrows, kf[h, pl.ds(off, width), :], NT_DIMS, preferred_element_type=F32)

    up_r = lax.broadcasted_iota(jnp.int32, (half, half), 0)
    up_c = lax.broadcasted_iota(jnp.int32, (half, half), 1)
    lo_r = lax.broadcasted_iota(jnp.int32, (half, tile), 0)
    lo_c = lax.broadcasted_iota(jnp.int32, (half, tile), 1)
    for h in range(PAIR):
        s_up = jnp.where(up_c <= up_r, scores(qf[h][:half], row0, half, h), NEG_INF)
        s_lo = jnp.where(lo_c <= lo_r + half, scores(qf[h][half:], row0, tile, h), NEG_INF)
        s_scr[h, :half, pl.ds(row0, half)] = s_up
        s_scr[h, half:, pl.ds(row0, tile)] = s_lo
        m_scr[h, :half] = lane_groups_max(s_up)
        m_scr[h, half:] = lane_groups_max(s_lo)

    def pass1(j, _):
        off = pl.multiple_of(j * tile, tile)
        for h in range(PAIR):
            s = scores(qf[h], off, tile, h)
            s_scr[h, :, pl.ds(off, tile)] = s
            m_scr[h] = jnp.maximum(m_scr[h], lane_groups_max(s))
        return 0
    lax.fori_loop(0, i, pass1, 0)

    m = [jnp.max(m_scr[h], axis=-1, keepdims=True) for h in range(PAIR)]
    for h in range(PAIR):
        p_up = jnp.exp2(s_scr[h, :half, pl.ds(row0, half)] - m[h][:half]).astype(BF16)
        p_lo = jnp.exp2(s_scr[h, half:, pl.ds(row0, tile)] - m[h][half:]).astype(BF16)
        acc_scr[h, :half] = jnp.dot(p_up, vf[h, pl.ds(row0, half), :], preferred_element_type=F32)
        acc_scr[h, half:] = jnp.dot(p_lo, vf[h, pl.ds(row0, tile), :], preferred_element_type=F32)

    def pass2(j, _):
        off = pl.multiple_of(j * tile, tile)
        for h in range(PAIR):
            p = jnp.exp2(s_scr[h, :, pl.ds(off, tile)] - m[h]).astype(BF16)
            acc_scr[h] += jnp.dot(p, vf[h, pl.ds(off, tile), :], preferred_element_type=F32)
        return 0
    lax.fori_loop(0, i, pass2, 0)

    num, den = _merge_heads(acc_scr[0], acc_scr[1])
    o_ref[...] = (num / den).astype(o_ref.dtype)


def _fox(z, ccol, batch, seq):
    n = z.shape[0]
    tile = FOX_TILE
    nq = seq // tile
    npair = N_HEADS_B // PAIR
    base = 3 * WIDTH_A // LANES
    qcol, kcol, vcol = base, base + WIDTH_B // LANES, base + 2 * WIDTH_B // LANES
    return pl.pallas_call(
        functools.partial(_fox_kernel, tile=tile),
        grid=(batch, npair, nq),
        in_specs=[
            pl.BlockSpec((tile, LANES), lambda b, p, i: (b * nq + i, qcol + p)),
            pl.BlockSpec((seq, LANES), lambda b, p, i: (b, kcol + p)),
            pl.BlockSpec((seq, LANES), lambda b, p, i: (b, vcol + p)),
            pl.BlockSpec((seq, LANES), lambda b, p, i: (b, 0)),
        ],
        out_specs=pl.BlockSpec((tile, LANES), lambda b, p, i: (b * nq + i, p)),
        out_shape=jax.ShapeDtypeStruct((n, WIDTH_B), BF16),
        scratch_shapes=[
            pltpu.VMEM((PAIR, seq, LANES), BF16),
            pltpu.VMEM((PAIR, seq, LANES), BF16),
            pltpu.VMEM((PAIR, tile, seq), F32),
            pltpu.VMEM((PAIR, tile, LANES), F32),
            pltpu.VMEM((PAIR, tile, LANES), F32),
        ],
        compiler_params=_cparams("parallel", "parallel", "arbitrary"),
        name="fox",
    )(z, z, z, ccol)


def _dilated_kernel(slope_ref, q_ref, k_ref, v_ref, o_ref,
                    natf, p4f, p4b, p16b, qfeat, kfeat, vals, dens, maxs, *, seq):
    pair = pl.program_id(1)
    lane = _lane_iota()
    first = _head_lane_mask(0)
    quarter = seq // 4
    units = seq // BLOCK

    def deinterleave(src, t, span_start, span):
        return [src[t, pl.ds(span_start + r, span // 4, stride=4), :] for r in range(4)]

    for t, ref in enumerate((q_ref, k_ref, v_ref)):
        natf[t] = ref[...].astype(F32)
        for r, part in enumerate(deinterleave(natf, t, 0, seq)):
            p4f[t, pl.ds(r * quarter, quarter), :] = part
            p4b[t, pl.ds(r * quarter, quarter), :] = part.astype(BF16)
        for r4 in range(4):
            for r, part in enumerate(deinterleave(p4f, t, r4 * quarter, quarter)):
                p16b[t, pl.ds(r4 * quarter + r * (quarter // 4), quarter // 4), :] = part.astype(BF16)

    qi = lax.broadcasted_iota(jnp.int32, (BLOCK, LANES), 0).astype(F32)
    kj = lax.broadcasted_iota(jnp.int32, (2 * BLOCK, LANES), 0).astype(F32)
    for p, (_, dil) in enumerate(DILATED_PATTERNS):
        for h in range(PAIR):
            sd = slope_ref[PAIR * pair + h] * float(dil)
            base = HEAD_DIM * (1 - h)
            qfeat[p * PAIR + h] = jnp.where(lane == base, -(qi + float(BLOCK)) * sd,
                                            jnp.where(lane == base + 1, 1.0, 0.0)).astype(BF16)
            kfeat[p * PAIR + h] = jnp.where(lane == base, 1.0,
                                            jnp.where(lane == base + 1, kj * sd, 0.0)).astype(BF16)

    bq = lax.broadcasted_iota(jnp.int32, (BLOCK, 2 * BLOCK), 0)
    bk = lax.broadcasted_iota(jnp.int32, (BLOCK, 2 * BLOCK), 1)
    rel = bq + BLOCK - bk
    band = (rel >= 0) & (rel <= BLOCK)

    def unit(p, srcs, u, prev_valid):
        qs, ks, vs = srcs
        start = pl.multiple_of(u * BLOCK, BLOCK)
        prev = pl.multiple_of(jnp.maximum(start - BLOCK, 0), BLOCK)
        qb = qs[pl.ds(start, BLOCK), :]
        kk = jnp.concatenate([ks[pl.ds(prev, BLOCK), :], ks[pl.ds(start, BLOCK), :]], axis=0)
        vv = jnp.concatenate([vs[pl.ds(prev, BLOCK), :], vs[pl.ds(start, BLOCK), :]], axis=0)
        if prev_valid is True:
            ok = band
        elif prev_valid is False:
            ok = band & (bk >= BLOCK)
        else:
            ok = band & ((bk >= BLOCK) | prev_valid)
        accs, ms = [], []
        for h in range(PAIR):
            in_h = _head_lane_mask(h)
            qh = jnp.where(in_h, qb, qfeat[p * PAIR + h])
            kh = jnp.where(in_h, kk, kfeat[p * PAIR + h])
            vh = jnp.where(in_h, vv, jnp.ones_like(vv))
            s = lax.dot_general(qh, kh, NT_DIMS, preferred_element_type=F32)
            s = jnp.where(ok, s, NEG_INF)
            m = jnp.max(s, axis=-1, keepdims=True)
            pr = jnp.exp(s - m).astype(BF16)
            accs.append(jnp.dot(pr, vh, preferred_element_type=F32))
            ms.append(m)
        num, den = _merge_heads(*accs)
        vals[p, pl.ds(start, BLOCK), :] = num
        dens[p, pl.ds(start, BLOCK), :] = den
        maxs[p, pl.ds(start, BLOCK), :] = jnp.where(first, ms[0], ms[1])

    group = UNITS_PER_STEP
    sources = ((q_ref, k_ref, v_ref), tuple(p4b.at[t] for t in range(3)), tuple(p16b.at[t] for t in range(3)))
    for p, (_, dil) in enumerate(DILATED_PATTERNS):
        per_class = units // dil

        def step(g, _, p=p, per_class=per_class):
            for e in range(group):
                u = g * group + e
                if per_class >= group:
                    prev_valid = (u % per_class != 0) if e == 0 else True
                else:
                    prev_valid = e % per_class != 0
                unit(p, sources[p], u, prev_valid)
            return 0
        lax.fori_loop(0, units // group, step, 0)

    for t, arr in enumerate((vals, dens, maxs)):
        for r4 in range(4):
            for r in range(4):
                p4f[t, pl.ds(r4 * quarter + r, quarter // 4, stride=4), :] = \
                    arr[2, pl.ds(r4 * quarter + r * (quarter // 4), quarter // 4), :]

    for r in range(4):
        grouped = pl.ds(r * quarter, quarter)
        natural = pl.ds(r, quarter, stride=4)
        ms = (maxs[0, natural, :], maxs[1, grouped, :], p4f[2, grouped, :])
        vs = (vals[0, natural, :], vals[1, grouped, :], p4f[0, grouped, :])
        ds = (dens[0, natural, :], dens[1, grouped, :], p4f[1, grouped, :])
        m_all = jnp.maximum(jnp.maximum(ms[0], ms[1]), ms[2])
        num = jnp.zeros((quarter, LANES), F32)
        den = jnp.zeros((quarter, LANES), F32)
        for p in range(3):
            e = jnp.exp(ms[p] - m_all)
            num = num + e * vs[p]
            den = den + e * ds[p]
        natf[0, natural, :] = num / den
    o_ref[...] = natf[0].astype(o_ref.dtype)


def _dilated(z, slopes, batch, seq):
    n = z.shape[0]
    npair = N_HEADS_A // PAIR
    npat = len(DILATED_PATTERNS)
    qcol, kcol, vcol = 0, WIDTH_A // LANES, 2 * WIDTH_A // LANES
    blk = lambda c0: pl.BlockSpec((seq, LANES), lambda b, p: (b, c0 + p))
    return pl.pallas_call(
        functools.partial(_dilated_kernel, seq=seq),
        grid=(batch, npair),
        in_specs=[pl.BlockSpec(memory_space=pltpu.SMEM), blk(qcol), blk(kcol), blk(vcol)],
        out_specs=pl.BlockSpec((seq, LANES), lambda b, p: (b, p)),
        out_shape=jax.ShapeDtypeStruct((n, WIDTH_A), BF16),
        scratch_shapes=[
            pltpu.VMEM((3, seq, LANES), F32),
            pltpu.VMEM((3, seq, LANES), F32),
            pltpu.VMEM((3, seq, LANES), BF16),
            pltpu.VMEM((3, seq, LANES), BF16),
            pltpu.VMEM((npat * PAIR, BLOCK, LANES), BF16),
            pltpu.VMEM((npat * PAIR, 2 * BLOCK, LANES), BF16),
            pltpu.VMEM((npat, seq, LANES), F32),
            pltpu.VMEM((npat, seq, LANES), F32),
            pltpu.VMEM((npat, seq, LANES), F32),
        ],
        compiler_params=_cparams("parallel", "parallel"),
        name="dilated",
    )(slopes, z, z, z)


def _pack_bf16_pairs(a, b):
    hi = pltpu.bitcast(a.astype(BF16).astype(F32), jnp.int32)
    lo = pltpu.bitcast(b.astype(BF16).astype(F32), jnp.int32)
    return (hi & jnp.int32(-65536)) | lax.shift_right_logical(lo, jnp.int32(16))


def _unpack_bf16_pairs(w):
    a = pltpu.bitcast(w & jnp.int32(-65536), F32)
    b = pltpu.bitcast(lax.shift_left(w, jnp.int32(16)), F32)
    return a, b


def _post_attn_kernel(ma_ref, mb_ref, x_ref, wo_ref, g_ref, wr_ref, br_ref, tri_ref,
                      h_ref, up_ref, idx_ref, gate_ref, rank_ref, cnt_ref, carry):
    @pl.when(pl.program_id(0) == 0)
    def _():
        carry[...] = jnp.zeros_like(carry)

    y = jnp.dot(ma_ref[...], wo_ref[:WIDTH_A, :], preferred_element_type=F32)
    y = y + jnp.dot(mb_ref[...], wo_ref[WIDTH_A:, :], preferred_element_type=F32)
    h = x_ref[...] + y
    h_ref[...] = h
    u = _rms(h, g_ref[...])
    half = u.shape[1] // 2
    up_ref[...] = _pack_bf16_pairs(u[:, :half], u[:, half:])

    u_hi = u.astype(BF16)
    u_lo = (u - u_hi.astype(F32)).astype(BF16)
    hi_terms = jnp.dot(u_hi, wr_ref[...], preferred_element_type=F32)
    logits = (hi_terms[:, :LANES] + hi_terms[:, LANES:]
              + jnp.dot(u_lo, wr_ref[:, :LANES], preferred_element_type=F32)) + br_ref[...]
    lane = lax.broadcasted_iota(jnp.int32, logits.shape, 1).astype(F32)
    work = logits
    idxs, tops = [], []
    for _ in range(TOP_K):
        top = jnp.max(work, axis=-1, keepdims=True)
        idx = jnp.min(jnp.where(work == top, lane, float(LANES)), axis=-1, keepdims=True)
        work = jnp.where(lane == idx, NEG_INF, work)
        idxs.append(idx)
        tops.append(top)
    exps = [jnp.exp(t - tops[0]) for t in tops]
    total = exps[0] + exps[1] + exps[2] + exps[3]

    onehot = jnp.zeros(logits.shape, F32)
    for idx in idxs:
        onehot = onehot + (lane == idx).astype(F32)
    before = jnp.dot(tri_ref[...], onehot.astype(BF16), preferred_element_type=F32) + carry[...]
    carry[...] = carry[...] + jnp.sum(onehot, axis=0, keepdims=True)
    cnt_ref[...] = carry[...]

    idx_out = jnp.zeros(logits.shape, F32)
    gate_out = jnp.zeros(logits.shape, F32)
    rank_out = jnp.zeros(logits.shape, F32)
    for k in range(TOP_K):
        rank_k = jnp.sum(jnp.where(lane == idxs[k], before, 0.0), axis=-1, keepdims=True)
        idx_out = jnp.where(lane == float(k), idxs[k], idx_out)
        gate_out = jnp.where(lane == float(k), exps[k] / total, gate_out)
        rank_out = jnp.where(lane == float(k), rank_k, rank_out)
    idx_ref[...] = idx_out.astype(jnp.int32)
    gate_ref[...] = gate_out
    rank_ref[...] = rank_out.astype(jnp.int32)


def _post_attn(mix_a, mix_b, x2, w_o, g_ffn, w_r, b_r):
    n, d = x2.shape
    tm = ROW_TILE
    tri = (lax.broadcasted_iota(jnp.int32, (tm, tm), 0)
           > lax.broadcasted_iota(jnp.int32, (tm, tm), 1)).astype(BF16)
    const = lambda i: (0, 0)
    row = lambda w: pl.BlockSpec((tm, w), lambda i: (i, 0))
    return pl.pallas_call(
        _post_attn_kernel,
        grid=(n // tm,),
        in_specs=[
            row(WIDTH_A), row(WIDTH_B), row(d),
            pl.BlockSpec(w_o.shape, const),
            pl.BlockSpec((1, d), const),
            pl.BlockSpec(w_r.shape, const),
            pl.BlockSpec((1, LANES), const),
            pl.BlockSpec((tm, tm), const),
        ],
        out_specs=[row(d), row(d // 2), row(LANES), row(LANES), row(LANES),
                   pl.BlockSpec((1, LANES), const)],
        out_shape=[
            jax.ShapeDtypeStruct((n, d), F32),
            jax.ShapeDtypeStruct((n, d // 2), jnp.int32),
            jax.ShapeDtypeStruct((n, LANES), jnp.int32),
            jax.ShapeDtypeStruct((n, LANES), F32),
            jax.ShapeDtypeStruct((n, LANES), jnp.int32),
            jax.ShapeDtypeStruct((1, LANES), F32),
        ],
        scratch_shapes=[pltpu.VMEM((1, LANES), F32)],
        compiler_params=_cparams("arbitrary"),
        name="post_attn",
    )(mix_a, mix_b, x2, w_o, g_ffn, w_r, b_r, tri)


def _wprep_kernel(w_ref, wg_ref, wl_ref, wt):
    d, cols = w_ref.shape[1:]
    de = cols // 2
    for j in range(d // LANES):
        lanes = slice(j * LANES, (j + 1) * LANES)
        wt[j] = w_ref[0, lanes, :].T
        wg_ref[0, :, lanes] = wt[j, pl.ds(0, de, stride=2), :].astype(BF16)
        wl_ref[0, :, lanes] = wt[j, pl.ds(1, de, stride=2), :].astype(BF16)


def _wprep(w_gate_up):
    ne, d, cols = w_gate_up.shape
    de = cols // 2
    out = pl.BlockSpec((1, de, d), lambda e: (e, 0, 0))
    return pl.pallas_call(
        _wprep_kernel,
        grid=(ne,),
        in_specs=[pl.BlockSpec((1, d, cols), lambda e: (e, 0, 0))],
        out_specs=[out, out],
        out_shape=[jax.ShapeDtypeStruct((ne, de, d), BF16)] * 2,
        scratch_shapes=[pltpu.VMEM((d // LANES, cols, LANES), F32)],
        compiler_params=_cparams("parallel"),
        name="wprep",
    )(w_gate_up)


def _sc_gather(table, idx):
    info = plsc.get_sparse_core_info()
    workers = info.num_cores * info.num_subcores
    rows, width = idx.shape[0], table.shape[1]
    chunk, depth = SC_CHUNK, SC_DEPTH
    assert rows % (workers * chunk * depth) == 0
    per_worker = rows // workers
    nchunks = per_worker // chunk
    mesh = plsc.VectorSubcoreMesh(core_axis_name="c", subcore_axis_name="s")

    @functools.partial(
        pl.kernel, mesh=mesh,
        out_type=jax.ShapeDtypeStruct((rows, width), table.dtype),
        scratch_types=[
            pltpu.VMEM((nchunks, chunk), jnp.int32),
            pltpu.VMEM((depth, chunk, width), table.dtype),
            pltpu.SemaphoreType.DMA((depth,)),
            pltpu.SemaphoreType.DMA((depth,)),
        ],
    )
    def gather_kernel(table_hbm, idx_hbm, out_hbm, idx_v, rows_v, gsem, wsem):
        wid = lax.axis_index("s") * info.num_cores + lax.axis_index("c")
        base = wid * per_worker
        pltpu.sync_copy(idx_hbm.at[wid], idx_v)

        def gather(c, b):
            return pltpu.make_async_copy(table_hbm.at[idx_v.at[c]], rows_v.at[b], gsem.at[b])

        def write(c, b):
            off = pl.multiple_of(base + c * chunk, chunk)
            return pltpu.make_async_copy(rows_v.at[b], out_hbm.at[pl.ds(off, chunk)], wsem.at[b])

        @pl.loop(0, nchunks, step=depth)
        def _(c0):
            for b in range(depth):
                gather(c0 + b, b).start()
            for b in range(depth):
                gather(c0 + b, b).wait()
                write(c0 + b, b).start()
            for b in range(depth):
                write(c0 + b, b).wait()

    return gather_kernel(table, idx.reshape(workers, nchunks, chunk))


def _gmm_kernel(te_ref, used_ref, xs_ref, wg_ref, wl_ref, bg_ref, bl_ref, wd_ref, bd_ref, ys_ref):
    i = pl.program_id(0)

    @pl.when(used_ref[i] > 0)
    def _():
        a, b = _unpack_bf16_pairs(xs_ref[...])
        x = jnp.concatenate([a, b], axis=1).astype(BF16)
        hg = lax.dot_general(x, wg_ref[0], NT_DIMS, preferred_element_type=F32) + bg_ref[0]
        hl = lax.dot_general(x, wl_ref[0], NT_DIMS, preferred_element_type=F32) + bl_ref[0]
        xg = jnp.minimum(hg, SWIGLU_LIMIT)
        xl = jnp.clip(hl, -SWIGLU_LIMIT, SWIGLU_LIMIT)
        act = xg * jax.nn.sigmoid(SWIGLU_ALPHA * xg) * (xl + 1.0)
        out = jnp.dot(act.astype(BF16), wd_ref[0].astype(BF16), preferred_element_type=F32) + bd_ref[0]
        half = out.shape[1] // 2
        ys_ref[...] = _pack_bf16_pairs(out[:, :half], out[:, half:])

    @pl.when(used_ref[i] == 0)
    def _():
        ys_ref[...] = jnp.zeros_like(ys_ref)


def _gmm(tile_expert, tile_used, xs, wg_t, wl_t, b_glu, b_lin, w_down, b_down):
    rows, half = xs.shape
    d = 2 * half
    de = wg_t.shape[1]
    tm = GMM_TILE
    wspec = lambda shape: pl.BlockSpec((1,) + shape, lambda i, te, used: (te[i], 0, 0))
    grid_spec = pltpu.PrefetchScalarGridSpec(
        num_scalar_prefetch=2,
        grid=(rows // tm,),
        in_specs=[
            pl.BlockSpec((tm, half), lambda i, te, used: (i, 0)),
            wspec((de, d)), wspec((de, d)), wspec((1, de)), wspec((1, de)),
            wspec((de, d)), wspec((1, d)),
        ],
        out_specs=pl.BlockSpec((tm, half), lambda i, te, used: (i, 0)),
    )
    return pl.pallas_call(
        _gmm_kernel,
        grid_spec=grid_spec,
        out_shape=jax.ShapeDtypeStruct((rows, half), jnp.int32),
        compiler_params=_cparams("arbitrary"),
        name="gmm",
    )(tile_expert, tile_used, xs, wg_t, wl_t, b_glu, b_lin, w_down, b_down)


def _final_kernel(h_ref, yk_ref, gate_ref, p_ref, g_ref, wg_ref, wp_ref, o_ref):
    gates = gate_ref[...]
    h = h_ref[...]
    for k in range(TOP_K):
        h = h + gates[:, k:k + 1] * jnp.concatenate(_unpack_bf16_pairs(yk_ref[k]), axis=1)
    u = _rms(h, g_ref[...]).astype(BF16)
    gate = jax.nn.sigmoid(jnp.dot(u, wg_ref[...], preferred_element_type=F32))
    proj = jnp.dot(p_ref[...].astype(BF16), wp_ref[...], preferred_element_type=F32)
    o_ref[...] = h + gate * proj


def _final(h1, yk, gates, p2, g_ple, w_gate, w_proj):
    n, d = h1.shape
    tm = ROW_TILE
    const = lambda i: (0, 0)
    return pl.pallas_call(
        _final_kernel,
        grid=(n // tm,),
        in_specs=[
            pl.BlockSpec((tm, d), lambda i: (i, 0)),
            pl.BlockSpec((TOP_K, tm, d // 2), lambda i: (0, i, 0)),
            pl.BlockSpec((tm, LANES), lambda i: (i, 0)),
            pl.BlockSpec((tm, p2.shape[1]), lambda i: (i, 0)),
            pl.BlockSpec((1, d), const),
            pl.BlockSpec(w_gate.shape, const),
            pl.BlockSpec(w_proj.shape, const),
        ],
        out_specs=pl.BlockSpec((tm, d), lambda i: (i, 0)),
        out_shape=jax.ShapeDtypeStruct((n, d), F32),
        compiler_params=_cparams("parallel"),
        name="final",
    )(h1, yk, gates, p2, g_ple, w_gate, w_proj)


def _layer(h, p, g_mix, w_in, b_f, g_qa, g_ka, g_qb, g_kb, w_o, g_ffn, w_router, b_router,
           w_gate_up, b_gate_up, w_down, b_down, g_ple, w_ple_gate, w_ple_proj):
    batch, seq, d = h.shape
    n = batch * seq
    assert tuple(dil for _, dil in DILATED_PATTERNS) == (1, 4, 16)
    for window, dil in DILATED_PATTERNS:
        per_class = seq // BLOCK // dil
        assert window // dil == BLOCK and seq % (dil * BLOCK) == 0
        assert per_class % UNITS_PER_STEP == 0 or UNITS_PER_STEP % per_class == 0
    assert n % ROW_TILE == 0 and d % (2 * LANES) == 0 and seq % FOX_TILE == 0
    x2 = h.reshape(n, d)

    qkv_cols = 3 * WIDTH_A + 3 * WIDTH_B
    w_qkv = w_in[:, :qkv_cols].astype(BF16)
    w_f = jnp.pad(w_in[:, qkv_cols:], ((0, 0), (0, LANES - N_HEADS_B))).astype(BF16)
    b_fp = jnp.pad(b_f.astype(F32), (0, LANES - N_HEADS_B)).reshape(1, LANES)
    scale = HEAD_DIM ** -0.5
    gains = jnp.stack([jnp.tile(g_qa, N_HEADS_A) * scale, jnp.tile(g_ka, N_HEADS_A),
                       jnp.tile(g_qb, N_HEADS_B) * (scale * LOG2E), jnp.tile(g_kb, N_HEADS_B)]).astype(F32)
    hid = jnp.arange(2 * LANES) // HEAD_DIM
    bd = (hid[:, None] == hid[None, :]).astype(BF16)

    z, logf = _in_proj(x2, g_mix.reshape(1, d), w_qkv, w_f, b_fp, gains, bd)
    ccol = _cumsum(logf, batch, seq)

    slopes = 2.0 ** (-8.0 * jnp.arange(1, N_HEADS_A + 1, dtype=F32) / N_HEADS_A)
    mix_a = _dilated(z, slopes, batch, seq)
    mix_b = _fox(z, ccol, batch, seq)

    w_r = jnp.pad(w_router.astype(F32), ((0, 0), (0, LANES - N_EXPERTS)))
    w_r_hi = w_r.astype(BF16)
    w_r = jnp.concatenate([w_r_hi, (w_r - w_r_hi.astype(F32)).astype(BF16)], axis=1)
    b_r = jnp.concatenate([b_router.astype(F32), jnp.full((LANES - N_EXPERTS,), NEG_INF, F32)]).reshape(1, LANES)
    h1, u_packed, top_idx, gates, rank, counts = _post_attn(
        mix_a, mix_b, x2, w_o.astype(BF16), g_ffn.reshape(1, d), w_r, b_r)

    counts = counts[0, :N_EXPERTS].astype(jnp.int32)
    tiles_per = (counts + GMM_TILE - 1) // GMM_TILE
    tile_end = jnp.cumsum(tiles_per)
    starts = (tile_end - tiles_per) * GMM_TILE
    n_tiles = n * TOP_K // GMM_TILE + N_EXPERTS
    tile_ids = jnp.arange(n_tiles, dtype=jnp.int32)
    tile_used = (tile_ids < tile_end[-1]).astype(jnp.int32)
    last_used = jnp.minimum(tile_ids, tile_end[-1] - 1)
    tile_expert = jnp.sum((last_used[:, None] >= tile_end[None, :]).astype(jnp.int32), axis=1)
    tile_expert = jnp.minimum(tile_expert, N_EXPERTS - 1)
    idx4 = top_idx[:, :TOP_K]
    pos = starts[idx4] + rank[:, :TOP_K]
    tok_sorted = (jnp.argsort(pos.reshape(-1)) // TOP_K).astype(jnp.int32)
    sorted_rows = jnp.arange(n_tiles * GMM_TILE, dtype=jnp.int32)
    within = sorted_rows - jnp.repeat(starts[tile_expert], GMM_TILE)
    compact = jnp.repeat((jnp.cumsum(counts) - counts)[tile_expert], GMM_TILE) + within
    valid = (within < jnp.repeat(counts[tile_expert], GMM_TILE)) & (jnp.repeat(tile_used, GMM_TILE) > 0)
    src = jnp.where(valid, tok_sorted[jnp.clip(compact, 0, n * TOP_K - 1)], sorted_rows % n)
    xs = _sc_gather(u_packed, src)

    de = w_down.shape[1]
    wg_t, wl_t = _wprep(w_gate_up)
    ys = _gmm(tile_expert, tile_used, xs, wg_t, wl_t,
              b_gate_up[:, 0::2].reshape(N_EXPERTS, 1, de).astype(F32),
              b_gate_up[:, 1::2].reshape(N_EXPERTS, 1, de).astype(F32),
              w_down, b_down.reshape(N_EXPERTS, 1, d).astype(F32))
    yk = _sc_gather(ys, pos.T.reshape(-1)).reshape(TOP_K, n, d // 2)

    out = _final(h1, yk, gates, p.reshape(n, -1), g_ple.reshape(1, d),
                 w_ple_gate.astype(BF16), w_ple_proj.astype(BF16))
    return out.reshape(batch, seq, d)


def kernel(x, p, g_mix, w_in, b_f, g_qa, g_ka, g_qb, g_kb, w_o, g_ffn, w_router, b_router,
           w_gate_up, b_gate_up, w_down, b_down, g_ple, w_ple_gate, w_ple_proj):
    h = x
    for i in range(g_mix.shape[0]):
        h = _layer(h, p[i], g_mix[i], w_in[i], b_f[i], g_qa[i], g_ka[i], g_qb[i], g_kb[i], w_o[i],
                   g_ffn[i], w_router[i], b_router[i], w_gate_up[i], b_gate_up[i], w_down[i],
                   b_down[i], g_ple[i], w_ple_gate[i], w_ple_proj[i])
    return h
```

```python
import functools

import jax
import jax.numpy as jnp
from jax import lax
from jax.experimental import pallas as pl
from jax.experimental.pallas import tpu as pltpu
from jax.experimental.pallas import tpu_sc as plsc

HEAD_DIM = 64
N_HEADS_A = 8
N_HEADS_B = 8
WIDTH_A = N_HEADS_A * HEAD_DIM
WIDTH_B = N_HEADS_B * HEAD_DIM
DILATED_PATTERNS = ((128, 1), (512, 4), (2048, 16))
BLOCK = 128
N_EXPERTS = 32
TOP_K = 4
SWIGLU_LIMIT = 7.0
SWIGLU_ALPHA = 1.702
NORM_EPS = 1e-6

LANES = 128
PAIR = LANES // HEAD_DIM
ROW_TILE = 512
GMM_TILE = 512
FOX_TILE = 512
UNITS_PER_STEP = 8
SC_CHUNK = 32
SC_DEPTH = 4
VMEM_LIMIT = 56 * 1024 * 1024

F32 = jnp.float32
BF16 = jnp.bfloat16
NEG_INF = float("-inf")
NT_DIMS = (((1,), (1,)), ((), ()))
LOG2E = 1.4426950408889634


def _cparams(*sem):
    return pltpu.CompilerParams(dimension_semantics=sem, vmem_limit_bytes=VMEM_LIMIT)


def _rms(x, g):
    return x * lax.rsqrt(jnp.mean(x * x, axis=-1, keepdims=True) + NORM_EPS) * g


def _lane_iota():
    return lax.broadcasted_iota(jnp.int32, (1, LANES), 1)


def _head_lane_mask(h):
    lane = _lane_iota()
    return (lane >= h * HEAD_DIM) & (lane < (h + 1) * HEAD_DIM)


def _merge_heads(acc0, acc1):
    first = _head_lane_mask(0)
    num = jnp.where(first, acc0, acc1)
    den = pltpu.roll(jnp.where(first, acc1, acc0), HEAD_DIM, axis=1)
    return num, den


def _in_proj_kernel(x_ref, g_ref, w_ref, wf_ref, bf_ref, gain_ref, bd_ref, z_ref, lf_ref):
    u = _rms(x_ref[...], g_ref[...]).astype(BF16)
    chunk = WIDTH_A
    normed = {0: 0, 1: 1, 3: 2, 4: 3}
    for c in range(6):
        acc = jnp.dot(u, w_ref[:, c * chunk:(c + 1) * chunk], preferred_element_type=F32)
        if c in normed:
            sq = (acc * acc).astype(BF16)
            half = chunk // 2
            ss = jnp.concatenate(
                [jnp.dot(sq[:, j * half:(j + 1) * half], bd_ref[...], preferred_element_type=F32)
                 for j in range(2)], axis=1)
            r = normed[c]
            acc = acc * lax.rsqrt(ss * (1.0 / HEAD_DIM) + NORM_EPS) * gain_ref[r:r + 1, :]
        z_ref[:, c * chunk:(c + 1) * chunk] = acc.astype(BF16)
    zf = jnp.dot(u, wf_ref[...], preferred_element_type=F32) + bf_ref[...]
    lf_ref[...] = jax.nn.log_sigmoid(zf)


def _in_proj(x2, g_mix, w_qkv, w_f, b_f, gains, bd):
    n, d = x2.shape
    cols = w_qkv.shape[1]
    tm = ROW_TILE
    const = lambda i: (0, 0)
    return pl.pallas_call(
        _in_proj_kernel,
        grid=(n // tm,),
        in_specs=[
            pl.BlockSpec((tm, d), lambda i: (i, 0)),
            pl.BlockSpec((1, d), const),
            pl.BlockSpec((d, cols), const),
            pl.BlockSpec((d, LANES), const),
            pl.BlockSpec((1, LANES), const),
            pl.BlockSpec(gains.shape, const),
            pl.BlockSpec(bd.shape, const),
        ],
        out_specs=[
            pl.BlockSpec((tm, cols), lambda i: (i, 0)),
            pl.BlockSpec((tm, LANES), lambda i: (i, 0)),
        ],
        out_shape=[
            jax.ShapeDtypeStruct((n, cols), BF16),
            jax.ShapeDtypeStruct((n, LANES), F32),
        ],
        compiler_params=_cparams("parallel"),
        name="in_proj",
    )(x2, g_mix, w_qkv, w_f, b_f, gains, bd)


def _cumsum_kernel(lf_ref, tri_ref, cpk_ref):
    s = lf_ref.shape[0]
    lane = _lane_iota()
    carry = jnp.zeros((1, LANES), F32)
    for blk in range(s // BLOCK):
        rows = slice(blk * BLOCK, (blk + 1) * BLOCK)
        part = jnp.dot(tri_ref[...], lf_ref[rows, :], precision=lax.Precision.HIGHEST,
                       preferred_element_type=F32) + carry
        carry = part[BLOCK - 1:BLOCK, :]
        c = part * LOG2E
        hi = c.astype(BF16).astype(F32)
        r1 = c - hi
        mid = r1.astype(BF16).astype(F32)
        lo = r1 - mid
        packed = jnp.where(lane < N_HEADS_B, hi,
                 jnp.where(lane < 2 * N_HEADS_B, pltpu.roll(mid, N_HEADS_B, axis=1),
                 jnp.where(lane < 3 * N_HEADS_B, pltpu.roll(lo, 2 * N_HEADS_B, axis=1),
                 jnp.where(lane == 3 * N_HEADS_B, 1.0, 0.0))))
        cpk_ref[rows, :] = packed.astype(BF16)


def _cumsum(logf, batch, seq):
    tri = (lax.broadcasted_iota(jnp.int32, (BLOCK, BLOCK), 0)
           >= lax.broadcasted_iota(jnp.int32, (BLOCK, BLOCK), 1)).astype(F32)
    return pl.pallas_call(
        _cumsum_kernel,
        grid=(batch,),
        in_specs=[
            pl.BlockSpec((seq, LANES), lambda b: (b, 0)),
            pl.BlockSpec((BLOCK, BLOCK), lambda b: (0, 0)),
        ],
        out_specs=pl.BlockSpec((seq, LANES), lambda b: (b, 0)),
        out_shape=jax.ShapeDtypeStruct((batch * seq, LANES), BF16),
        compiler_params=_cparams("parallel"),
        name="cumsum",
    )(logf, tri)


def _fox_features(cpk, pair, key_side):
    assert PAIR == 2
    r = lax.broadcasted_iota(jnp.int32, (LANES, PAIR * LANES), 0)
    c = lax.broadcasted_iota(jnp.int32, (LANES, PAIR * LANES), 1)
    hh = jnp.where(c >= LANES, 1, 0)
    slot = c - hh * LANES - HEAD_DIM * (1 - hh)
    head = PAIR * pair + hh
    piece_slot = slot - 3 if key_side else slot
    ones_slot = slot if key_side else slot - 3
    piece = (piece_slot >= 0) & (piece_slot < 3) & (r == N_HEADS_B * piece_slot + head)
    ones = (ones_slot >= 0) & (ones_slot < 3) & (r == 3 * N_HEADS_B)
    place = jnp.where(piece, -1.0 if key_side else 1.0, jnp.where(ones, 1.0, 0.0)).astype(BF16)
    return jnp.dot(cpk, place, preferred_element_type=F32).astype(BF16)


def _fox_kernel(q_ref, k_ref, v_ref, c_ref, o_ref, kf, vf, s_scr, m_scr, acc_scr, *, tile):
    pair = pl.program_id(1)
    i = pl.program_id(2)
    half = tile // 2
    in_head = [_head_lane_mask(h) for h in range(PAIR)]
    block = lambda feat, h: feat[:, h * LANES:(h + 1) * LANES]

    @pl.when(i == 0)
    def _():
        feat = _fox_features(c_ref[...], pair, True)
        for h in range(PAIR):
            kf[h] = jnp.where(in_head[h], k_ref[...], block(feat, h))
            vf[h] = jnp.where(in_head[h], v_ref[...], jnp.ones_like(v_ref[...]))

    row0 = pl.multiple_of(i * tile, tile)
    feat_q = _fox_features(c_ref[pl.ds(row0, tile), :], pair, False)
    q = q_ref[...]
    qf = [jnp.where(in_head[h], q, block(feat_q, h)) for h in range(PAIR)]

    def lane_groups_max(s):
        m = s[:, :LANES]
        for g in range(1, s.shape[1] // LANES):
            m = jnp.maximum(m, s[:, g * LANES:(g + 1) * LANES])
        return m

    def scores(qrows, off, width, h):
        return lax.dot_general(qrows, kf[h, pl.ds(off, width), :], NT_DIMS, preferred_element_type=F32)

    up_r = lax.broadcasted_iota(jnp.int32, (half, half), 0)
    up_c = lax.broadcasted_iota(jnp.int32, (half, half), 1)
    lo_r = lax.broadcasted_iota(jnp.int32, (half, tile), 0)
    lo_c = lax.broadcasted_iota(jnp.int32, (half, tile), 1)
    for h in range(PAIR):
        s_up = jnp.where(up_c <= up_r, scores(qf[h][:half], row0, half, h), NEG_INF)
        s_lo = jnp.where(lo_c <= lo_r + half, scores(qf[h][half:], row0, tile, h), NEG_INF)
        s_scr[h, :half, pl.ds(row0, half)] = s_up
        s_scr[h, half:, pl.ds(row0, tile)] = s_lo
        m_scr[h, :half] = lane_groups_max(s_up)
        m_scr[h, half:] = lane_groups_max(s_lo)

    def pass1(j, _):
        off = pl.multiple_of(j * tile, tile)
        for h in range(PAIR):
            s = scores(qf[h], off, tile, h)
            s_scr[h, :, pl.ds(off, tile)] = s
            m_scr[h] = jnp.maximum(m_scr[h], lane_groups_max(s))
        return 0
    lax.fori_loop(0, i, pass1, 0)

    m = [jnp.max(m_scr[h], axis=-1, keepdims=True) for h in range(PAIR)]
    for h in range(PAIR):
        p_up = jnp.exp2(s_scr[h, :half, pl.ds(row0, half)] - m[h][:half]).astype(BF16)
        p_lo = jnp.exp2(s_scr[h, half:, pl.ds(row0, tile)] - m[h][half:]).astype(BF16)
        acc_scr[h, :half] = jnp.dot(p_up, vf[h, pl.ds(row0, half), :], preferred_element_type=F32)
        acc_scr[h, half:] = jnp.dot(p_lo, vf[h, pl.ds(row0, tile), :], preferred_element_type=F32)

    def pass2(j, _):
        off = pl.multiple_of(j * tile, tile)
        for h in range(PAIR):
            p = jnp.exp2(s_scr[h, :, pl.ds(off, tile)] - m[h]).astype(BF16)
            acc_scr[h] += jnp.dot(p, vf[h, pl.ds(off, tile), :], preferred_element_type=F32)
        return 0
    lax.fori_loop(0, i, pass2, 0)

    num, den = _merge_heads(acc_scr[0], acc_scr[1])
    o_ref[...] = (num / den).astype(o_ref.dtype)


def _fox(z, ccol, batch, seq):
    n = z.shape[0]
    tile = FOX_TILE
    nq = seq // tile
    npair = N_HEADS_B // PAIR
    base = 3 * WIDTH_A // LANES
    qcol, kcol, vcol = base, base + WIDTH_B // LANES, base + 2 * WIDTH_B // LANES
    return pl.pallas_call(
        functools.partial(_fox_kernel, tile=tile),
        grid=(batch, npair, nq),
        in_specs=[
            pl.BlockSpec((tile, LANES), lambda b, p, i: (b * nq + i, qcol + p)),
            pl.BlockSpec((seq, LANES), lambda b, p, i: (b, kcol + p)),
            pl.BlockSpec((seq, LANES), lambda b, p, i: (b, vcol + p)),
            pl.BlockSpec((seq, LANES), lambda b, p, i: (b, 0)),
        ],
        out_specs=pl.BlockSpec((tile, LANES), lambda b, p, i: (b * nq + i, p)),
        out_shape=jax.ShapeDtypeStruct((n, WIDTH_B), BF16),
        scratch_shapes=[
            pltpu.VMEM((PAIR, seq, LANES), BF16),
            pltpu.VMEM((PAIR, seq, LANES), BF16),
            pltpu.VMEM((PAIR, tile, seq), F32),
            pltpu.VMEM((PAIR, tile, LANES), F32),
            pltpu.VMEM((PAIR, tile, LANES), F32),
        ],
        compiler_params=_cparams("parallel", "parallel", "arbitrary"),
        name="fox",
    )(z, z, z, ccol)


def _dilated_kernel(slope_ref, q_ref, k_ref, v_ref, o_ref,
                    natf, p4f, p4b, p16b, qfeat, kfeat, vals, dens, maxs, *, seq):
    pair = pl.program_id(1)
    lane = _lane_iota()
    first = _head_lane_mask(0)
    quarter = seq // 4
    units = seq // BLOCK

    def deinterleave(src, t, span_start, span):
        return [src[t, pl.ds(span_start + r, span // 4, stride=4), :] for r in range(4)]

    for t, ref in enumerate((q_ref, k_ref, v_ref)):
        natf[t] = ref[...].astype(F32)
        for r, part in enumerate(deinterleave(natf, t, 0, seq)):
            p4f[t, pl.ds(r * quarter, quarter), :] = part
            p4b[t, pl.ds(r * quarter, quarter), :] = part.astype(BF16)
        for r4 in range(4):
            for r, part in enumerate(deinterleave(p4f, t, r4 * quarter, quarter)):
                p16b[t, pl.ds(r4 * quarter + r * (quarter // 4), quarter // 4), :] = part.astype(BF16)

    qi = lax.broadcasted_iota(jnp.int32, (BLOCK, LANES), 0).astype(F32)
    kj = lax.broadcasted_iota(jnp.int32, (2 * BLOCK, LANES), 0).astype(F32)
    for p, (_, dil) in enumerate(DILATED_PATTERNS):
        for h in range(PAIR):
            sd = slope_ref[PAIR * pair + h] * float(dil)
            base = HEAD_DIM * (1 - h)
            qfeat[p * PAIR + h] = jnp.where(lane == base, -(qi + float(BLOCK)) * sd,
                                            jnp.where(lane == base + 1, 1.0, 0.0)).astype(BF16)
            kfeat[p * PAIR + h] = jnp.where(lane == base, 1.0,
                                            jnp.where(lane == base + 1, kj * sd, 0.0)).astype(BF16)

    bq = lax.broadcasted_iota(jnp.int32, (BLOCK, 2 * BLOCK), 0)
    bk = lax.broadcasted_iota(jnp.int32, (BLOCK, 2 * BLOCK), 1)
    rel = bq + BLOCK - bk
    band = (rel >= 0) & (rel <= BLOCK)

    def unit(p, srcs, u, prev_valid):
        qs, ks, vs = srcs
        start = pl.multiple_of(u * BLOCK, BLOCK)
        prev = pl.multiple_of(jnp.maximum(start - BLOCK, 0), BLOCK)
        qb = qs[pl.ds(start, BLOCK), :]
        kk = jnp.concatenate([ks[pl.ds(prev, BLOCK), :], ks[pl.ds(start, BLOCK), :]], axis=0)
        vv = jnp.concatenate([vs[pl.ds(prev, BLOCK), :], vs[pl.ds(start, BLOCK), :]], axis=0)
        if prev_valid is True:
            ok = band
        elif prev_valid is False:
            ok = band & (bk >= BLOCK)
        else:
            ok = band & ((bk >= BLOCK) | prev_valid)
        accs, ms = [], []
        for h in range(PAIR):
            in_h = _head_lane_mask(h)
            qh = jnp.where(in_h, qb, qfeat[p * PAIR + h])
            kh = jnp.where(in_h, kk, kfeat[p * PAIR + h])
            vh = jnp.where(in_h, vv, jnp.ones_like(vv))
            s = lax.dot_general(qh, kh, NT_DIMS, preferred_element_type=F32)
            s = jnp.where(ok, s, NEG_INF)
            m = jnp.max(s, axis=-1, keepdims=True)
            pr = jnp.exp(s - m).astype(BF16)
            accs.append(jnp.dot(pr, vh, preferred_element_type=F32))
            ms.append(m)
        num, den = _merge_heads(*accs)
        vals[p, pl.ds(start, BLOCK), :] = num
        dens[p, pl.ds(start, BLOCK), :] = den
        maxs[p, pl.ds(start, BLOCK), :] = jnp.where(first, ms[0], ms[1])

    group = UNITS_PER_STEP
    sources = ((q_ref, k_ref, v_ref), tuple(p4b.at[t] for t in range(3)), tuple(p16b.at[t] for t in range(3)))
    for p, (_, dil) in enumerate(DILATED_PATTERNS):
        per_class = units // dil

        def step(g, _, p=p, per_class=per_class):
            for e in range(group):
                u = g * group + e
                if per_class >= group:
                    prev_valid = (u % per_class != 0) if e == 0 else True
                else:
                    prev_valid = e % per_class != 0
                unit(p, sources[p], u, prev_valid)
            return 0
        lax.fori_loop(0, units // group, step, 0)

    for t, arr in enumerate((vals, dens, maxs)):
        for r4 in range(4):
            for r in range(4):
                p4f[t, pl.ds(r4 * quarter + r, quarter // 4, stride=4), :] = \
                    arr[2, pl.ds(r4 * quarter + r * (quarter // 4), quarter // 4), :]

    for r in range(4):
        grouped = pl.ds(r * quarter, quarter)
        natural = pl.ds(r, quarter, stride=4)
        ms = (maxs[0, natural, :], maxs[1, grouped, :], p4f[2, grouped, :])
        vs = (vals[0, natural, :], vals[1, grouped, :], p4f[0, grouped, :])
        ds = (dens[0, natural, :], dens[1, grouped, :], p4f[1, grouped, :])
        m_all = jnp.maximum(jnp.maximum(ms[0], ms[1]), ms[2])
        num = jnp.zeros((quarter, LANES), F32)
        den = jnp.zeros((quarter, LANES), F32)
        for p in range(3):
            e = jnp.exp(ms[p] - m_all)
            num = num + e * vs[p]
            den = den + e * ds[p]
        natf[0, natural, :] = num / den
    o_ref[...] = natf[0].astype(o_ref.dtype)


def _dilated(z, slopes, batch, seq):
    n = z.shape[0]
    npair = N_HEADS_A // PAIR
    npat = len(DILATED_PATTERNS)
    qcol, kcol, vcol = 0, WIDTH_A // LANES, 2 * WIDTH_A // LANES
    blk = lambda c0: pl.BlockSpec((seq, LANES), lambda b, p: (b, c0 + p))
    return pl.pallas_call(
        functools.partial(_dilated_kernel, seq=seq),
        grid=(batch, npair),
        in_specs=[pl.BlockSpec(memory_space=pltpu.SMEM), blk(qcol), blk(kcol), blk(vcol)],
        out_specs=pl.BlockSpec((seq, LANES), lambda b, p: (b, p)),
        out_shape=jax.ShapeDtypeStruct((n, WIDTH_A), BF16),
        scratch_shapes=[
            pltpu.VMEM((3, seq, LANES), F32),
            pltpu.VMEM((3, seq, LANES), F32),
            pltpu.VMEM((3, seq, LANES), BF16),
            pltpu.VMEM((3, seq, LANES), BF16),
            pltpu.VMEM((npat * PAIR, BLOCK, LANES), BF16),
            pltpu.VMEM((npat * PAIR, 2 * BLOCK, LANES), BF16),
            pltpu.VMEM((npat, seq, LANES), F32),
            pltpu.VMEM((npat, seq, LANES), F32),
            pltpu.VMEM((npat, seq, LANES), F32),
        ],
        compiler_params=_cparams("parallel", "parallel"),
        name="dilated",
    )(slopes, z, z, z)


def _pack_bf16_pairs(a, b):
    hi = pltpu.bitcast(a.astype(BF16).astype(F32), jnp.int32)
    lo = pltpu.bitcast(b.astype(BF16).astype(F32), jnp.int32)
    return (hi & jnp.int32(-65536)) | lax.shift_right_logical(lo, jnp.int32(16))


def _unpack_bf16_pairs(w):
    a = pltpu.bitcast(w & jnp.int32(-65536), F32)
    b = pltpu.bitcast(lax.shift_left(w, jnp.int32(16)), F32)
    return a, b


def _post_attn_kernel(ma_ref, mb_ref, x_ref, wo_ref, g_ref, wr_ref, br_ref, tri_ref,
                      h_ref, up_ref, idx_ref, gate_ref, rank_ref, cnt_ref, carry):
    @pl.when(pl.program_id(0) == 0)
    def _():
        carry[...] = jnp.zeros_like(carry)

    y = jnp.dot(ma_ref[...], wo_ref[:WIDTH_A, :], preferred_element_type=F32)
    y = y + jnp.dot(mb_ref[...], wo_ref[WIDTH_A:, :], preferred_element_type=F32)
    h = x_ref[...] + y
    h_ref[...] = h
    u = _rms(h, g_ref[...])
    half = u.shape[1] // 2
    up_ref[...] = _pack_bf16_pairs(u[:, :half], u[:, half:])

    u_hi = u.astype(BF16)
    u_lo = (u - u_hi.astype(F32)).astype(BF16)
    hi_terms = jnp.dot(u_hi, wr_ref[...], preferred_element_type=F32)
    logits = (hi_terms[:, :LANES] + hi_terms[:, LANES:]
              + jnp.dot(u_lo, wr_ref[:, :LANES], preferred_element_type=F32)) + br_ref[...]
    lane = lax.broadcasted_iota(jnp.int32, logits.shape, 1).astype(F32)
    work = logits
    idxs, tops = [], []
    for _ in range(TOP_K):
        top = jnp.max(work, axis=-1, keepdims=True)
        idx = jnp.min(jnp.where(work == top, lane, float(LANES)), axis=-1, keepdims=True)
        work = jnp.where(lane == idx, NEG_INF, work)
        idxs.append(idx)
        tops.append(top)
    exps = [jnp.exp(t - tops[0]) for t in tops]
    total = exps[0] + exps[1] + exps[2] + exps[3]

    onehot = jnp.zeros(logits.shape, F32)
    for idx in idxs:
        onehot = onehot + (lane == idx).astype(F32)
    before = jnp.dot(tri_ref[...], onehot.astype(BF16), preferred_element_type=F32) + carry[...]
    carry[...] = carry[...] + jnp.sum(onehot, axis=0, keepdims=True)
    cnt_ref[...] = carry[...]

    idx_out = jnp.zeros(logits.shape, F32)
    gate_out = jnp.zeros(logits.shape, F32)
    rank_out = jnp.zeros(logits.shape, F32)
    for k in range(TOP_K):
        rank_k = jnp.sum(jnp.where(lane == idxs[k], before, 0.0), axis=-1, keepdims=True)
        idx_out = jnp.where(lane == float(k), idxs[k], idx_out)
        gate_out = jnp.where(lane == float(k), exps[k] / total, gate_out)
        rank_out = jnp.where(lane == float(k), rank_k, rank_out)
    idx_ref[...] = idx_out.astype(jnp.int32)
    gate_ref[...] = gate_out
    rank_ref[...] = rank_out.astype(jnp.int32)


def _post_attn(mix_a, mix_b, x2, w_o, g_ffn, w_r, b_r):
    n, d = x2.shape
    tm = ROW_TILE
    tri = (lax.broadcasted_iota(jnp.int32, (tm, tm), 0)
           > lax.broadcasted_iota(jnp.int32, (tm, tm), 1)).astype(BF16)
    const = lambda i: (0, 0)
    row = lambda w: pl.BlockSpec((tm, w), lambda i: (i, 0))
    return pl.pallas_call(
        _post_attn_kernel,
        grid=(n // tm,),
        in_specs=[
            row(WIDTH_A), row(WIDTH_B), row(d),
            pl.BlockSpec(w_o.shape, const),
            pl.BlockSpec((1, d), const),
            pl.BlockSpec(w_r.shape, const),
            pl.BlockSpec((1, LANES), const),
            pl.BlockSpec((tm, tm), const),
        ],
        out_specs=[row(d), row(d // 2), row(LANES), row(LANES), row(LANES),
                   pl.BlockSpec((1, LANES), const)],
        out_shape=[
            jax.ShapeDtypeStruct((n, d), F32),
            jax.ShapeDtypeStruct((n, d // 2), jnp.int32),
            jax.ShapeDtypeStruct((n, LANES), jnp.int32),
            jax.ShapeDtypeStruct((n, LANES), F32),
            jax.ShapeDtypeStruct((n, LANES), jnp.int32),
            jax.ShapeDtypeStruct((1, LANES), F32),
        ],
        scratch_shapes=[pltpu.VMEM((1, LANES), F32)],
        compiler_params=_cparams("arbitrary"),
        name="post_attn",
    )(mix_a, mix_b, x2, w_o, g_ffn, w_r, b_r, tri)


def _wprep_kernel(w_ref, wg_ref, wl_ref, wt):
    d, cols = w_ref.shape[1:]
    de = cols // 2
    for j in range(d // LANES):
        lanes = slice(j * LANES, (j + 1) * LANES)
        wt[j] = w_ref[0, lanes, :].T
        wg_ref[0, :, lanes] = wt[j, pl.ds(0, de, stride=2), :].astype(BF16)
        wl_ref[0, :, lanes] = wt[j, pl.ds(1, de, stride=2), :].astype(BF16)


def _wprep(w_gate_up):
    ne, d, cols = w_gate_up.shape
    de = cols // 2
    out = pl.BlockSpec((1, de, d), lambda e: (e, 0, 0))
    return pl.pallas_call(
        _wprep_kernel,
        grid=(ne,),
        in_specs=[pl.BlockSpec((1, d, cols), lambda e: (e, 0, 0))],
        out_specs=[out, out],
        out_shape=[jax.ShapeDtypeStruct((ne, de, d), BF16)] * 2,
        scratch_shapes=[pltpu.VMEM((d // LANES, cols, LANES), F32)],
        compiler_params=_cparams("parallel"),
        name="wprep",
    )(w_gate_up)


def _sc_gather(table, idx):
    info = plsc.get_sparse_core_info()
    workers = info.num_cores * info.num_subcores
    rows, width = idx.shape[0], table.shape[1]
    chunk, depth = SC_CHUNK, SC_DEPTH
    assert rows % (workers * chunk * depth) == 0
    per_worker = rows // workers
    nchunks = per_worker // chunk
    mesh = plsc.VectorSubcoreMesh(core_axis_name="c", subcore_axis_name="s")

    @functools.partial(
        pl.kernel, mesh=mesh,
        out_type=jax.ShapeDtypeStruct((rows, width), table.dtype),
        scratch_types=[
            pltpu.VMEM((nchunks, chunk), jnp.int32),
            pltpu.VMEM((depth, chunk, width), table.dtype),
            pltpu.SemaphoreType.DMA((depth,)),
            pltpu.SemaphoreType.DMA((depth,)),
        ],
    )
    def gather_kernel(table_hbm, idx_hbm, out_hbm, idx_v, rows_v, gsem, wsem):
        wid = lax.axis_index("s") * info.num_cores + lax.axis_index("c")
        base = wid * per_worker
        pltpu.sync_copy(idx_hbm.at[wid], idx_v)

        def gather(c, b):
            return pltpu.make_async_copy(table_hbm.at[idx_v.at[c]], rows_v.at[b], gsem.at[b])

        def write(c, b):
            off = pl.multiple_of(base + c * chunk, chunk)
            return pltpu.make_async_copy(rows_v.at[b], out_hbm.at[pl.ds(off, chunk)], wsem.at[b])

        @pl.loop(0, nchunks, step=depth)
        def _(c0):
            for b in range(depth):
                gather(c0 + b, b).start()
            for b in range(depth):
                gather(c0 + b, b).wait()
                write(c0 + b, b).start()
            for b in range(depth):
                write(c0 + b, b).wait()

    return gather_kernel(table, idx.reshape(workers, nchunks, chunk))


def _gmm_kernel(te_ref, used_ref, xs_ref, wg_ref, wl_ref, bg_ref, bl_ref, wd_ref, bd_ref, ys_ref):
    i = pl.program_id(0)

    @pl.when(used_ref[i] > 0)
    def _():
        a, b = _unpack_bf16_pairs(xs_ref[...])
        x = jnp.concatenate([a, b], axis=1).astype(BF16)
        hg = lax.dot_general(x, wg_ref[0], NT_DIMS, preferred_element_type=F32) + bg_ref[0]
        hl = lax.dot_general(x, wl_ref[0], NT_DIMS, preferred_element_type=F32) + bl_ref[0]
        xg = jnp.minimum(hg, SWIGLU_LIMIT)
        xl = jnp.clip(hl, -SWIGLU_LIMIT, SWIGLU_LIMIT)
        act = xg * jax.nn.sigmoid(SWIGLU_ALPHA * xg) * (xl + 1.0)
        out = jnp.dot(act.astype(BF16), wd_ref[0].astype(BF16), preferred_element_type=F32) + bd_ref[0]
        half = out.shape[1] // 2
        ys_ref[...] = _pack_bf16_pairs(out[:, :half], out[:, half:])

    @pl.when(used_ref[i] == 0)
    def _():
        ys_ref[...] = jnp.zeros_like(ys_ref)


def _gmm(tile_expert, tile_used, xs, wg_t, wl_t, b_glu, b_lin, w_down, b_down):
    rows, half = xs.shape
    d = 2 * half
    de = wg_t.shape[1]
    tm = GMM_TILE
    wspec = lambda shape: pl.BlockSpec((1,) + shape, lambda i, te, used: (te[i], 0, 0))
    grid_spec = pltpu.PrefetchScalarGridSpec(
        num_scalar_prefetch=2,
        grid=(rows // tm,),
        in_specs=[
            pl.BlockSpec((tm, half), lambda i, te, used: (i, 0)),
            wspec((de, d)), wspec((de, d)), wspec((1, de)), wspec((1, de)),
            wspec((de, d)), wspec((1, d)),
        ],
        out_specs=pl.BlockSpec((tm, half), lambda i, te, used: (i, 0)),
    )
    return pl.pallas_call(
        _gmm_kernel,
        grid_spec=grid_spec,
        out_shape=jax.ShapeDtypeStruct((rows, half), jnp.int32),
        compiler_params=_cparams("arbitrary"),
        name="gmm",
    )(tile_expert, tile_used, xs, wg_t, wl_t, b_glu, b_lin, w_down, b_down)


def _final_kernel(h_ref, yk_ref, gate_ref, p_ref, g_ref, wg_ref, wp_ref, o_ref):
    gates = gate_ref[...]
    h = h_ref[...]
    for k in range(TOP_K):
        h = h + gates[:, k:k + 1] * jnp.concatenate(_unpack_bf16_pairs(yk_ref[k]), axis=1)
    u = _rms(h, g_ref[...]).astype(BF16)
    gate = jax.nn.sigmoid(jnp.dot(u, wg_ref[...], preferred_element_type=F32))
    proj = jnp.dot(p_ref[...].astype(BF16), wp_ref[...], preferred_element_type=F32)
    o_ref[...] = h + gate * proj


def _final(h1, yk, gates, p2, g_ple, w_gate, w_proj, first_tile, out_so_far=None):
    n, d = h1.shape
    tm = ROW_TILE
    const = lambda i: (0, 0)
    rows = lambda width: pl.BlockSpec((tm, width), lambda i: (i + first_tile, 0))
    in_specs = [
        rows(d),
        pl.BlockSpec((TOP_K, tm, d // 2), lambda i: (0, i, 0)),
        rows(LANES),
        rows(p2.shape[1]),
        pl.BlockSpec((1, d), const),
        pl.BlockSpec(w_gate.shape, const),
        pl.BlockSpec(w_proj.shape, const),
    ]
    args = [h1, yk, gates, p2, g_ple, w_gate, w_proj]
    kernel_fn, aliases = _final_kernel, {}
    if out_so_far is not None:
        in_specs.append(pl.BlockSpec(memory_space=pl.ANY))
        args.append(out_so_far)
        aliases = {len(args) - 1: 0}
        kernel_fn = lambda *refs: _final_kernel(*refs[:7], refs[8])
    return pl.pallas_call(
        kernel_fn,
        grid=(yk.shape[1] // tm,),
        in_specs=in_specs,
        out_specs=rows(d),
        out_shape=jax.ShapeDtypeStruct((n, d), F32),
        input_output_aliases=aliases,
        compiler_params=_cparams("parallel"),
        name="final",
    )(*args)


def _layer(h, p, g_mix, w_in, b_f, g_qa, g_ka, g_qb, g_kb, w_o, g_ffn, w_router, b_router,
           w_gate_up, b_gate_up, w_down, b_down, g_ple, w_ple_gate, w_ple_proj):
    batch, seq, d = h.shape
    n = batch * seq
    assert tuple(dil for _, dil in DILATED_PATTERNS) == (1, 4, 16)
    for window, dil in DILATED_PATTERNS:
        per_class = seq // BLOCK // dil
        assert window // dil == BLOCK and seq % (dil * BLOCK) == 0
        assert per_class % UNITS_PER_STEP == 0 or UNITS_PER_STEP % per_class == 0
    assert n % ROW_TILE == 0 and d % (2 * LANES) == 0 and seq % FOX_TILE == 0
    x2 = h.reshape(n, d)

    qkv_cols = 3 * WIDTH_A + 3 * WIDTH_B
    w_qkv = w_in[:, :qkv_cols].astype(BF16)
    w_f = jnp.pad(w_in[:, qkv_cols:], ((0, 0), (0, LANES - N_HEADS_B))).astype(BF16)
    b_fp = jnp.pad(b_f.astype(F32), (0, LANES - N_HEADS_B)).reshape(1, LANES)
    scale = HEAD_DIM ** -0.5
    gains = jnp.stack([jnp.tile(g_qa, N_HEADS_A) * scale, jnp.tile(g_ka, N_HEADS_A),
                       jnp.tile(g_qb, N_HEADS_B) * (scale * LOG2E), jnp.tile(g_kb, N_HEADS_B)]).astype(F32)
    hid = jnp.arange(2 * LANES) // HEAD_DIM
    bd = (hid[:, None] == hid[None, :]).astype(BF16)

    z, logf = _in_proj(x2, g_mix.reshape(1, d), w_qkv, w_f, b_fp, gains, bd)
    ccol = _cumsum(logf, batch, seq)

    slopes = 2.0 ** (-8.0 * jnp.arange(1, N_HEADS_A + 1, dtype=F32) / N_HEADS_A)
    mix_a = _dilated(z, slopes, batch, seq)
    mix_b = _fox(z, ccol, batch, seq)

    w_r = jnp.pad(w_router.astype(F32), ((0, 0), (0, LANES - N_EXPERTS)))
    w_r_hi = w_r.astype(BF16)
    w_r = jnp.concatenate([w_r_hi, (w_r - w_r_hi.astype(F32)).astype(BF16)], axis=1)
    b_r = jnp.concatenate([b_router.astype(F32), jnp.full((LANES - N_EXPERTS,), NEG_INF, F32)]).reshape(1, LANES)
    h1, u_packed, top_idx, gates, rank, counts = _post_attn(
        mix_a, mix_b, x2, w_o.astype(BF16), g_ffn.reshape(1, d), w_r, b_r)

    counts = counts[0, :N_EXPERTS].astype(jnp.int32)
    tiles_per = (counts + GMM_TILE - 1) // GMM_TILE
    tile_end = jnp.cumsum(tiles_per)
    starts = (tile_end - tiles_per) * GMM_TILE
    n_tiles = n * TOP_K // GMM_TILE + N_EXPERTS
    tile_ids = jnp.arange(n_tiles, dtype=jnp.int32)
    tile_used = (tile_ids < tile_end[-1]).astype(jnp.int32)
    last_used = jnp.minimum(tile_ids, tile_end[-1] - 1)
    tile_expert = jnp.sum((last_used[:, None] >= tile_end[None, :]).astype(jnp.int32), axis=1)
    tile_expert = jnp.minimum(tile_expert, N_EXPERTS - 1)
    idx4 = top_idx[:, :TOP_K]
    pos = starts[idx4] + rank[:, :TOP_K]
    filler = jnp.arange(GMM_TILE, dtype=jnp.int32)[None, :]
    need = (tiles_per * GMM_TILE - counts)[:, None]
    spare_before = (jnp.cumsum(GMM_TILE - need[:, 0]) - (GMM_TILE - need[:, 0]))[:, None]
    filler_key = jnp.where(filler < need, (starts + counts)[:, None] + filler,
                           tile_end[-1] * GMM_TILE + spare_before + filler - need)
    filler_tok = (jnp.arange(N_EXPERTS * GMM_TILE, dtype=jnp.int32) % n)
    keys = jnp.concatenate([pos.reshape(-1), filler_key.reshape(-1).astype(jnp.int32)])
    toks = jnp.concatenate([jnp.repeat(jnp.arange(n, dtype=jnp.int32), TOP_K), filler_tok])
    src = lax.sort((keys, toks), num_keys=1)[1]
    xs = _sc_gather(u_packed, src)

    de = w_down.shape[1]
    wg_t, wl_t = _wprep(w_gate_up)
    ys = _gmm(tile_expert, tile_used, xs, wg_t, wl_t,
              b_gate_up[:, 0::2].reshape(N_EXPERTS, 1, de).astype(F32),
              b_gate_up[:, 1::2].reshape(N_EXPERTS, 1, de).astype(F32),
              w_down, b_down.reshape(N_EXPERTS, 1, d).astype(F32))
    assert n % (2 * ROW_TILE) == 0
    half_n = n // 2
    out = None
    for part in range(2):
        part_pos = pos[part * half_n:(part + 1) * half_n].T.reshape(-1)
        yk = _sc_gather(ys, part_pos).reshape(TOP_K, half_n, d // 2)
        out = _final(h1, yk, gates, p.reshape(n, -1), g_ple.reshape(1, d), w_ple_gate.astype(BF16),
                     w_ple_proj.astype(BF16), part * (half_n // ROW_TILE), out)
    return out.reshape(batch, seq, d)


def kernel(x, p, g_mix, w_in, b_f, g_qa, g_ka, g_qb, g_kb, w_o, g_ffn, w_router, b_router,
           w_gate_up, b_gate_up, w_down, b_down, g_ple, w_ple_gate, w_ple_proj):
    h = x
    for i in range(g_mix.shape[0]):
        h = _layer(h, p[i], g_mix[i], w_in[i], b_f[i], g_qa[i], g_ka[i], g_qb[i], g_kb[i], w_o[i],
                   g_ffn[i], w_router[i], b_router[i], w_gate_up[i], b_gate_up[i], w_down[i],
                   b_down[i], g_ple[i], w_ple_gate[i], w_ple_proj[i])
    return h
```

```python
import functools

import jax
import jax.numpy as jnp
from jax import lax
from jax.experimental import pallas as pl
from jax.experimental.pallas import tpu as pltpu
from jax.experimental.pallas import tpu_sc as plsc

HEAD_DIM = 64
N_HEADS_A = 8
N_HEADS_B = 8
WIDTH_A = N_HEADS_A * HEAD_DIM
WIDTH_B = N_HEADS_B * HEAD_DIM
DILATED_PATTERNS = ((128, 1), (512, 4), (2048, 16))
BLOCK = 128
N_EXPERTS = 32
TOP_K = 4
SWIGLU_LIMIT = 7.0
SWIGLU_ALPHA = 1.702
NORM_EPS = 1e-6

LANES = 128
PAIR = LANES // HEAD_DIM
ROW_TILE = 512
GMM_TILE = 512
FOX_TILE = 512
UNITS_PER_STEP = 8
SC_CHUNK = 32
SC_DEPTH = 4
VMEM_LIMIT = 56 * 1024 * 1024

F32 = jnp.float32
BF16 = jnp.bfloat16
NEG_INF = float("-inf")
NT_DIMS = (((1,), (1,)), ((), ()))
LOG2E = 1.4426950408889634


def _cparams(*sem):
    return pltpu.CompilerParams(dimension_semantics=sem, vmem_limit_bytes=VMEM_LIMIT)


def _rms(x, g):
    return x * lax.rsqrt(jnp.mean(x * x, axis=-1, keepdims=True) + NORM_EPS) * g


def _lane_iota():
    return lax.broadcasted_iota(jnp.int32, (1, LANES), 1)


def _head_lane_mask(h):
    lane = _lane_iota()
    return (lane >= h * HEAD_DIM) & (lane < (h + 1) * HEAD_DIM)


def _merge_heads(acc0, acc1):
    first = _head_lane_mask(0)
    num = jnp.where(first, acc0, acc1)
    den = pltpu.roll(jnp.where(first, acc1, acc0), HEAD_DIM, axis=1)
    return num, den


def _in_proj_kernel(x_ref, g_ref, w_ref, wf_ref, bf_ref, gain_ref, bd_ref, z_ref, lf_ref):
    u = _rms(x_ref[...], g_ref[...]).astype(BF16)
    chunk = WIDTH_A
    normed = {0: 0, 1: 1, 3: 2, 4: 3}
    for c in range(6):
        acc = jnp.dot(u, w_ref[:, c * chunk:(c + 1) * chunk], preferred_element_type=F32)
        if c in normed:
            sq = (acc * acc).astype(BF16)
            half = chunk // 2
            ss = jnp.concatenate(
                [jnp.dot(sq[:, j * half:(j + 1) * half], bd_ref[...], preferred_element_type=F32)
                 for j in range(2)], axis=1)
            r = normed[c]
            acc = acc * lax.rsqrt(ss * (1.0 / HEAD_DIM) + NORM_EPS) * gain_ref[r:r + 1, :]
        z_ref[:, c * chunk:(c + 1) * chunk] = acc.astype(BF16)
    zf = jnp.dot(u, wf_ref[...], preferred_element_type=F32) + bf_ref[...]
    lf_ref[...] = jax.nn.log_sigmoid(zf)


def _in_proj(x2, g_mix, w_qkv, w_f, b_f, gains, bd):
    n, d = x2.shape
    cols = w_qkv.shape[1]
    tm = ROW_TILE
    const = lambda i: (0, 0)
    return pl.pallas_call(
        _in_proj_kernel,
        grid=(n // tm,),
        in_specs=[
            pl.BlockSpec((tm, d), lambda i: (i, 0)),
            pl.BlockSpec((1, d), const),
            pl.BlockSpec((d, cols), const),
            pl.BlockSpec((d, LANES), const),
            pl.BlockSpec((1, LANES), const),
            pl.BlockSpec(gains.shape, const),
            pl.BlockSpec(bd.shape, const),
        ],
        out_specs=[
            pl.BlockSpec((tm, cols), lambda i: (i, 0)),
            pl.BlockSpec((tm, LANES), lambda i: (i, 0)),
        ],
        out_shape=[
            jax.ShapeDtypeStruct((n, cols), BF16),
            jax.ShapeDtypeStruct((n, LANES), F32),
        ],
        compiler_params=_cparams("parallel"),
        name="in_proj",
    )(x2, g_mix, w_qkv, w_f, b_f, gains, bd)


def _cumsum_kernel(lf_ref, tri_ref, cpk_ref):
    s = lf_ref.shape[0]
    lane = _lane_iota()
    carry = jnp.zeros((1, LANES), F32)
    for blk in range(s // BLOCK):
        rows = slice(blk * BLOCK, (blk + 1) * BLOCK)
        part = jnp.dot(tri_ref[...], lf_ref[rows, :], precision=lax.Precision.HIGHEST,
                       preferred_element_type=F32) + carry
        carry = part[BLOCK - 1:BLOCK, :]
        c = part * LOG2E
        hi = c.astype(BF16).astype(F32)
        r1 = c - hi
        mid = r1.astype(BF16).astype(F32)
        lo = r1 - mid
        packed = jnp.where(lane < N_HEADS_B, hi,
                 jnp.where(lane < 2 * N_HEADS_B, pltpu.roll(mid, N_HEADS_B, axis=1),
                 jnp.where(lane < 3 * N_HEADS_B, pltpu.roll(lo, 2 * N_HEADS_B, axis=1),
                 jnp.where(lane == 3 * N_HEADS_B, 1.0, 0.0))))
        cpk_ref[rows, :] = packed.astype(BF16)


def _cumsum(logf, batch, seq):
    tri = (lax.broadcasted_iota(jnp.int32, (BLOCK, BLOCK), 0)
           >= lax.broadcasted_iota(jnp.int32, (BLOCK, BLOCK), 1)).astype(F32)
    return pl.pallas_call(
        _cumsum_kernel,
        grid=(batch,),
        in_specs=[
            pl.BlockSpec((seq, LANES), lambda b: (b, 0)),
            pl.BlockSpec((BLOCK, BLOCK), lambda b: (0, 0)),
        ],
        out_specs=pl.BlockSpec((seq, LANES), lambda b: (b, 0)),
        out_shape=jax.ShapeDtypeStruct((batch * seq, LANES), BF16),
        compiler_params=_cparams("parallel"),
        name="cumsum",
    )(logf, tri)


def _fox_features(cpk, pair, key_side):
    assert PAIR == 2
    r = lax.broadcasted_iota(jnp.int32, (LANES, PAIR * LANES), 0)
    c = lax.broadcasted_iota(jnp.int32, (LANES, PAIR * LANES), 1)
    hh = jnp.where(c >= LANES, 1, 0)
    slot = c - hh * LANES - HEAD_DIM * (1 - hh)
    head = PAIR * pair + hh
    piece_slot = slot - 3 if key_side else slot
    ones_slot = slot if key_side else slot - 3
    piece = (piece_slot >= 0) & (piece_slot < 3) & (r == N_HEADS_B * piece_slot + head)
    ones = (ones_slot >= 0) & (ones_slot < 3) & (r == 3 * N_HEADS_B)
    place = jnp.where(piece, -1.0 if key_side else 1.0, jnp.where(ones, 1.0, 0.0)).astype(BF16)
    return jnp.dot(cpk, place, preferred_element_type=F32).astype(BF16)


def _fox_kernel(q_ref, k_ref, v_ref, c_ref, o_ref, kf, vf, s_scr, m_scr, acc_scr, *, tile):
    pair = pl.program_id(1)
    i = pl.program_id(2)
    half = tile // 2
    in_head = [_head_lane_mask(h) for h in range(PAIR)]
    block = lambda feat, h: feat[:, h * LANES:(h + 1) * LANES]

    @pl.when(i == 0)
    def _():
        feat = _fox_features(c_ref[...], pair, True)
        for h in range(PAIR):
            kf[h] = jnp.where(in_head[h], k_ref[...], block(feat, h))
            vf[h] = jnp.where(in_head[h], v_ref[...], jnp.ones_like(v_ref[...]))

    row0 = pl.multiple_of(i * tile, tile)
    feat_q = _fox_features(c_ref[pl.ds(row0, tile), :], pair, False)
    q = q_ref[...]
    qf = [jnp.where(in_head[h], q, block(feat_q, h)) for h in range(PAIR)]

    def lane_groups_max(s):
        m = s[:, :LANES]
        for g in range(1, s.shape[1] // LANES):
            m = jnp.maximum(m, s[:, g * LANES:(g + 1) * LANES])
        return m

    def scores(qrows, off, width, h):
        return lax.dot_general(qrows, kf[h, pl.ds(off, width), :], NT_DIMS, preferred_element_type=F32)

    up_r = lax.broadcasted_iota(jnp.int32, (half, half), 0)
    up_c = lax.broadcasted_iota(jnp.int32, (half, half), 1)
    lo_r = lax.broadcasted_iota(jnp.int32, (half, tile), 0)
    lo_c = lax.broadcasted_iota(jnp.int32, (half, tile), 1)
    for h in range(PAIR):
        s_up = jnp.where(up_c <= up_r, scores(qf[h][:half], row0, half, h), NEG_INF)
        s_lo = jnp.where(lo_c <= lo_r + half, scores(qf[h][half:], row0, tile, h), NEG_INF)
        s_scr[h, :half, pl.ds(row0, half)] = s_up
        s_scr[h, half:, pl.ds(row0, tile)] = s_lo
        m_scr[h, :half] = lane_groups_max(s_up)
        m_scr[h, half:] = lane_groups_max(s_lo)

    def pass1(j, _):
        off = pl.multiple_of(j * tile, tile)
        for h in range(PAIR):
            s = scores(qf[h], off, tile, h)
            s_scr[h, :, pl.ds(off, tile)] = s
            m_scr[h] = jnp.maximum(m_scr[h], lane_groups_max(s))
        return 0
    lax.fori_loop(0, i, pass1, 0)

    m = [jnp.max(m_scr[h], axis=-1, keepdims=True) for h in range(PAIR)]
    for h in range(PAIR):
        p_up = jnp.exp2(s_scr[h, :half, pl.ds(row0, half)] - m[h][:half]).astype(BF16)
        p_lo = jnp.exp2(s_scr[h, half:, pl.ds(row0, tile)] - m[h][half:]).astype(BF16)
        acc_scr[h, :half] = jnp.dot(p_up, vf[h, pl.ds(row0, half), :], preferred_element_type=F32)
        acc_scr[h, half:] = jnp.dot(p_lo, vf[h, pl.ds(row0, tile), :], preferred_element_type=F32)

    def pass2(j, _):
        off = pl.multiple_of(j * tile, tile)
        for h in range(PAIR):
            p = jnp.exp2(s_scr[h, :, pl.ds(off, tile)] - m[h]).astype(BF16)
            acc_scr[h] += jnp.dot(p, vf[h, pl.ds(off, tile), :], preferred_element_type=F32)
        return 0
    lax.fori_loop(0, i, pass2, 0)

    num, den = _merge_heads(acc_scr[0], acc_scr[1])
    o_ref[...] = (num / den).astype(o_ref.dtype)


def _fox(z, ccol, batch, seq):
    n = z.shape[0]
    tile = FOX_TILE
    nq = seq // tile
    npair = N_HEADS_B // PAIR
    base = 3 * WIDTH_A // LANES
    qcol, kcol, vcol = base, base + WIDTH_B // LANES, base + 2 * WIDTH_B // LANES
    return pl.pallas_call(
        functools.partial(_fox_kernel, tile=tile),
        grid=(batch, npair, nq),
        in_specs=[
            pl.BlockSpec((tile, LANES), lambda b, p, i: (b * nq + i, qcol + p)),
            pl.BlockSpec((seq, LANES), lambda b, p, i: (b, kcol + p)),
            pl.BlockSpec((seq, LANES), lambda b, p, i: (b, vcol + p)),
            pl.BlockSpec((seq, LANES), lambda b, p, i: (b, 0)),
        ],
        out_specs=pl.BlockSpec((tile, LANES), lambda b, p, i: (b * nq + i, p)),
        out_shape=jax.ShapeDtypeStruct((n, WIDTH_B), BF16),
        scratch_shapes=[
            pltpu.VMEM((PAIR, seq, LANES), BF16),
            pltpu.VMEM((PAIR, seq, LANES), BF16),
            pltpu.VMEM((PAIR, tile, seq), F32),
            pltpu.VMEM((PAIR, tile, LANES), F32),
            pltpu.VMEM((PAIR, tile, LANES), F32),
        ],
        compiler_params=_cparams("parallel", "parallel", "arbitrary"),
        name="fox",
    )(z, z, z, ccol)


def _dilated_kernel(slope_ref, q_ref, k_ref, v_ref, o_ref,
                    natf, p4f, p4b, p16b, qfeat, kfeat, vals, dens, maxs, *, seq):
    pair = pl.program_id(1)
    lane = _lane_iota()
    first = _head_lane_mask(0)
    quarter = seq // 4
    units = seq // BLOCK

    def deinterleave(src, t, span_start, span):
        return [src[t, pl.ds(span_start + r, span // 4, stride=4), :] for r in range(4)]

    for t, ref in enumerate((q_ref, k_ref, v_ref)):
        natf[t] = ref[...].astype(F32)
        for r, part in enumerate(deinterleave(natf, t, 0, seq)):
            p4f[t, pl.ds(r * quarter, quarter), :] = part
            p4b[t, pl.ds(r * quarter, quarter), :] = part.astype(BF16)
        for r4 in range(4):
            for r, part in enumerate(deinterleave(p4f, t, r4 * quarter, quarter)):
                p16b[t, pl.ds(r4 * quarter + r * (quarter // 4), quarter // 4), :] = part.astype(BF16)

    qi = lax.broadcasted_iota(jnp.int32, (BLOCK, LANES), 0).astype(F32)
    kj = lax.broadcasted_iota(jnp.int32, (2 * BLOCK, LANES), 0).astype(F32)
    for p, (_, dil) in enumerate(DILATED_PATTERNS):
        for h in range(PAIR):
            sd = slope_ref[PAIR * pair + h] * float(dil)
            base = HEAD_DIM * (1 - h)
            def pieces_and_ones(value, first_piece_lane, first_one_lane):
                hi = value.astype(BF16).astype(F32)
                mid = (value - hi).astype(BF16).astype(F32)
                lo = value - hi - mid
                return jnp.where(lane == first_piece_lane, hi,
                       jnp.where(lane == first_piece_lane + 1, mid,
                       jnp.where(lane == first_piece_lane + 2, lo,
                       jnp.where((lane >= first_one_lane) & (lane < first_one_lane + 3), 1.0, 0.0)))).astype(BF16)
            qfeat[p * PAIR + h] = pieces_and_ones(-(qi + float(BLOCK)) * (sd * LOG2E), base, base + 3)
            kfeat[p * PAIR + h] = pieces_and_ones(kj * (sd * LOG2E), base + 3, base)

    bq = lax.broadcasted_iota(jnp.int32, (BLOCK, 2 * BLOCK), 0)
    bk = lax.broadcasted_iota(jnp.int32, (BLOCK, 2 * BLOCK), 1)
    rel = bq + BLOCK - bk
    band = (rel >= 0) & (rel <= BLOCK)

    def unit(p, srcs, u, prev_valid):
        qs, ks, vs = srcs
        start = pl.multiple_of(u * BLOCK, BLOCK)
        prev = pl.multiple_of(jnp.maximum(start - BLOCK, 0), BLOCK)
        qb = qs[pl.ds(start, BLOCK), :]
        kk = jnp.concatenate([ks[pl.ds(prev, BLOCK), :], ks[pl.ds(start, BLOCK), :]], axis=0)
        vv = jnp.concatenate([vs[pl.ds(prev, BLOCK), :], vs[pl.ds(start, BLOCK), :]], axis=0)
        if prev_valid is True:
            ok = band
        elif prev_valid is False:
            ok = band & (bk >= BLOCK)
        else:
            ok = band & ((bk >= BLOCK) | prev_valid)
        accs, ms = [], []
        for h in range(PAIR):
            in_h = _head_lane_mask(h)
            qh = jnp.where(in_h, qb, qfeat[p * PAIR + h])
            kh = jnp.where(in_h, kk, kfeat[p * PAIR + h])
            vh = jnp.where(in_h, vv, jnp.ones_like(vv))
            s = lax.dot_general(qh, kh, NT_DIMS, preferred_element_type=F32)
            s = jnp.where(ok, s, NEG_INF)
            m = jnp.max(s, axis=-1, keepdims=True)
            pr = jnp.exp2(s - m).astype(BF16)
            accs.append(jnp.dot(pr, vh, preferred_element_type=F32))
            ms.append(m)
        num, den = _merge_heads(*accs)
        vals[p, pl.ds(start, BLOCK), :] = num
        dens[p, pl.ds(start, BLOCK), :] = den
        maxs[p, pl.ds(start, BLOCK), :] = jnp.where(first, ms[0], ms[1])

    group = UNITS_PER_STEP
    sources = ((q_ref, k_ref, v_ref), tuple(p4b.at[t] for t in range(3)), tuple(p16b.at[t] for t in range(3)))
    for p, (_, dil) in enumerate(DILATED_PATTERNS):
        per_class = units // dil

        def step(g, _, p=p, per_class=per_class):
            for e in range(group):
                u = g * group + e
                if per_class >= group:
                    prev_valid = (u % per_class != 0) if e == 0 else True
                else:
                    prev_valid = e % per_class != 0
                unit(p, sources[p], u, prev_valid)
            return 0
        lax.fori_loop(0, units // group, step, 0)

    for t, arr in enumerate((vals, dens, maxs)):
        for r4 in range(4):
            for r in range(4):
                p4f[t, pl.ds(r4 * quarter + r, quarter // 4, stride=4), :] = \
                    arr[2, pl.ds(r4 * quarter + r * (quarter // 4), quarter // 4), :]

    for r in range(4):
        grouped = pl.ds(r * quarter, quarter)
        natural = pl.ds(r, quarter, stride=4)
        ms = (maxs[0, natural, :], maxs[1, grouped, :], p4f[2, grouped, :])
        vs = (vals[0, natural, :], vals[1, grouped, :], p4f[0, grouped, :])
        ds = (dens[0, natural, :], dens[1, grouped, :], p4f[1, grouped, :])
        m_all = jnp.maximum(jnp.maximum(ms[0], ms[1]), ms[2])
        num = jnp.zeros((quarter, LANES), F32)
        den = jnp.zeros((quarter, LANES), F32)
        for p in range(3):
            e = jnp.exp2(ms[p] - m_all)
            num = num + e * vs[p]
            den = den + e * ds[p]
        natf[0, natural, :] = num / den
    o_ref[...] = natf[0].astype(o_ref.dtype)


def _dilated(z, slopes, batch, seq):
    n = z.shape[0]
    npair = N_HEADS_A // PAIR
    npat = len(DILATED_PATTERNS)
    qcol, kcol, vcol = 0, WIDTH_A // LANES, 2 * WIDTH_A // LANES
    blk = lambda c0: pl.BlockSpec((seq, LANES), lambda b, p: (b, c0 + p))
    return pl.pallas_call(
        functools.partial(_dilated_kernel, seq=seq),
        grid=(batch, npair),
        in_specs=[pl.BlockSpec(memory_space=pltpu.SMEM), blk(qcol), blk(kcol), blk(vcol)],
        out_specs=pl.BlockSpec((seq, LANES), lambda b, p: (b, p)),
        out_shape=jax.ShapeDtypeStruct((n, WIDTH_A), BF16),
        scratch_shapes=[
            pltpu.VMEM((3, seq, LANES), F32),
            pltpu.VMEM((3, seq, LANES), F32),
            pltpu.VMEM((3, seq, LANES), BF16),
            pltpu.VMEM((3, seq, LANES), BF16),
            pltpu.VMEM((npat * PAIR, BLOCK, LANES), BF16),
            pltpu.VMEM((npat * PAIR, 2 * BLOCK, LANES), BF16),
            pltpu.VMEM((npat, seq, LANES), F32),
            pltpu.VMEM((npat, seq, LANES), F32),
            pltpu.VMEM((npat, seq, LANES), F32),
        ],
        compiler_params=_cparams("parallel", "parallel"),
        name="dilated",
    )(slopes, z, z, z)


def _pack_bf16_pairs(a, b):
    hi = pltpu.bitcast(a.astype(BF16).astype(F32), jnp.int32)
    lo = pltpu.bitcast(b.astype(BF16).astype(F32), jnp.int32)
    return (hi & jnp.int32(-65536)) | lax.shift_right_logical(lo, jnp.int32(16))


def _unpack_bf16_pairs(w):
    a = pltpu.bitcast(w & jnp.int32(-65536), F32)
    b = pltpu.bitcast(lax.shift_left(w, jnp.int32(16)), F32)
    return a, b


def _post_attn_kernel(ma_ref, mb_ref, x_ref, wo_ref, g_ref, wr_ref, br_ref, tri_ref,
                      h_ref, up_ref, idx_ref, gate_ref, rank_ref, cnt_ref, carry):
    @pl.when(pl.program_id(0) == 0)
    def _():
        carry[...] = jnp.zeros_like(carry)

    y = jnp.dot(ma_ref[...], wo_ref[:WIDTH_A, :], preferred_element_type=F32)
    y = y + jnp.dot(mb_ref[...], wo_ref[WIDTH_A:, :], preferred_element_type=F32)
    h = x_ref[...] + y
    h_ref[...] = h
    u = _rms(h, g_ref[...])
    half = u.shape[1] // 2
    up_ref[...] = _pack_bf16_pairs(u[:, :half], u[:, half:])

    u_hi = u.astype(BF16)
    u_lo = (u - u_hi.astype(F32)).astype(BF16)
    hi_terms = jnp.dot(u_hi, wr_ref[...], preferred_element_type=F32)
    logits = (hi_terms[:, :LANES] + hi_terms[:, LANES:]
              + jnp.dot(u_lo, wr_ref[:, :LANES], preferred_element_type=F32)) + br_ref[...]
    lane = lax.broadcasted_iota(jnp.int32, logits.shape, 1).astype(F32)
    work = logits
    idxs, tops = [], []
    for _ in range(TOP_K):
        top = jnp.max(work, axis=-1, keepdims=True)
        idx = jnp.min(jnp.where(work == top, lane, float(LANES)), axis=-1, keepdims=True)
        work = jnp.where(lane == idx, NEG_INF, work)
        idxs.append(idx)
        tops.append(top)
    exps = [jnp.exp(t - tops[0]) for t in tops]
    total = exps[0] + exps[1] + exps[2] + exps[3]

    onehot = jnp.zeros(logits.shape, F32)
    for idx in idxs:
        onehot = onehot + (lane == idx).astype(F32)
    before = jnp.dot(tri_ref[...], onehot.astype(BF16), preferred_element_type=F32) + carry[...]
    carry[...] = carry[...] + jnp.sum(onehot, axis=0, keepdims=True)
    cnt_ref[...] = carry[...]

    idx_out = jnp.zeros(logits.shape, F32)
    gate_out = jnp.zeros(logits.shape, F32)
    rank_out = jnp.zeros(logits.shape, F32)
    for k in range(TOP_K):
        rank_k = jnp.sum(jnp.where(lane == idxs[k], before, 0.0), axis=-1, keepdims=True)
        idx_out = jnp.where(lane == float(k), idxs[k], idx_out)
        gate_out = jnp.where(lane == float(k), exps[k] / total, gate_out)
        rank_out = jnp.where(lane == float(k), rank_k, rank_out)
    idx_ref[...] = idx_out.T[:8, :].astype(jnp.int32)
    gate_ref[...] = gate_out
    rank_ref[...] = rank_out.T[:8, :].astype(jnp.int32)


def _post_attn(mix_a, mix_b, x2, w_o, g_ffn, w_r, b_r):
    n, d = x2.shape
    tm = ROW_TILE
    tri = (lax.broadcasted_iota(jnp.int32, (tm, tm), 0)
           > lax.broadcasted_iota(jnp.int32, (tm, tm), 1)).astype(BF16)
    const = lambda i: (0, 0)
    row = lambda w: pl.BlockSpec((tm, w), lambda i: (i, 0))
    lanes_t = pl.BlockSpec((8, tm), lambda i: (0, i))
    return pl.pallas_call(
        _post_attn_kernel,
        grid=(n // tm,),
        in_specs=[
            row(WIDTH_A), row(WIDTH_B), row(d),
            pl.BlockSpec(w_o.shape, const),
            pl.BlockSpec((1, d), const),
            pl.BlockSpec(w_r.shape, const),
            pl.BlockSpec((1, LANES), const),
            pl.BlockSpec((tm, tm), const),
        ],
        out_specs=[row(d), row(d // 2), lanes_t, row(LANES), lanes_t,
                   pl.BlockSpec((1, LANES), const)],
        out_shape=[
            jax.ShapeDtypeStruct((n, d), F32),
            jax.ShapeDtypeStruct((n, d // 2), jnp.int32),
            jax.ShapeDtypeStruct((8, n), jnp.int32),
            jax.ShapeDtypeStruct((n, LANES), F32),
            jax.ShapeDtypeStruct((8, n), jnp.int32),
            jax.ShapeDtypeStruct((1, LANES), F32),
        ],
        scratch_shapes=[pltpu.VMEM((1, LANES), F32)],
        compiler_params=_cparams("arbitrary"),
        name="post_attn",
    )(mix_a, mix_b, x2, w_o, g_ffn, w_r, b_r, tri)


def _wprep_kernel(w_ref, wg_ref, wl_ref, wt):
    d, cols = w_ref.shape[1:]
    de = cols // 2
    for j in range(d // LANES):
        lanes = slice(j * LANES, (j + 1) * LANES)
        wt[j] = w_ref[0, lanes, :].T
        wg_ref[0, :, lanes] = wt[j, pl.ds(0, de, stride=2), :].astype(BF16)
        wl_ref[0, :, lanes] = wt[j, pl.ds(1, de, stride=2), :].astype(BF16)


def _wprep(w_gate_up):
    ne, d, cols = w_gate_up.shape
    de = cols // 2
    out = pl.BlockSpec((1, de, d), lambda e: (e, 0, 0))
    return pl.pallas_call(
        _wprep_kernel,
        grid=(ne,),
        in_specs=[pl.BlockSpec((1, d, cols), lambda e: (e, 0, 0))],
        out_specs=[out, out],
        out_shape=[jax.ShapeDtypeStruct((ne, de, d), BF16)] * 2,
        scratch_shapes=[pltpu.VMEM((d // LANES, cols, LANES), F32)],
        compiler_params=_cparams("parallel"),
        name="wprep",
    )(w_gate_up)


def _sc_gather(table, idx):
    info = plsc.get_sparse_core_info()
    workers = info.num_cores * info.num_subcores
    rows, width = idx.shape[0], table.shape[1]
    chunk, depth = SC_CHUNK, SC_DEPTH
    assert rows % (workers * chunk * depth) == 0
    per_worker = rows // workers
    nchunks = per_worker // chunk
    mesh = plsc.VectorSubcoreMesh(core_axis_name="c", subcore_axis_name="s")

    @functools.partial(
        pl.kernel, mesh=mesh,
        out_type=jax.ShapeDtypeStruct((rows, width), table.dtype),
        scratch_types=[
            pltpu.VMEM((nchunks, chunk), jnp.int32),
            pltpu.VMEM((depth, chunk, width), table.dtype),
            pltpu.SemaphoreType.DMA((depth,)),
            pltpu.SemaphoreType.DMA((depth,)),
        ],
    )
    def gather_kernel(table_hbm, idx_hbm, out_hbm, idx_v, rows_v, gsem, wsem):
        wid = lax.axis_index("s") * info.num_cores + lax.axis_index("c")
        base = wid * per_worker
        pltpu.sync_copy(idx_hbm.at[wid], idx_v)

        def gather(c, b):
            return pltpu.make_async_copy(table_hbm.at[idx_v.at[c]], rows_v.at[b], gsem.at[b])

        def write(c, b):
            off = pl.multiple_of(base + c * chunk, chunk)
            return pltpu.make_async_copy(rows_v.at[b], out_hbm.at[pl.ds(off, chunk)], wsem.at[b])

        @pl.loop(0, nchunks, step=depth)
        def _(c0):
            for b in range(depth):
                gather(c0 + b, b).start()
            for b in range(depth):
                gather(c0 + b, b).wait()
                write(c0 + b, b).start()
            for b in range(depth):
                write(c0 + b, b).wait()

    return gather_kernel(table, idx.reshape(workers, nchunks, chunk))


def _gmm_kernel(te_ref, used_ref, xs_ref, wg_ref, wl_ref, bg_ref, bl_ref, wd_ref, bd_ref, ys_ref):
    i = pl.program_id(0)

    @pl.when(used_ref[i] > 0)
    def _():
        a, b = _unpack_bf16_pairs(xs_ref[...])
        x = jnp.concatenate([a, b], axis=1).astype(BF16)
        hg = lax.dot_general(x, wg_ref[0], NT_DIMS, preferred_element_type=F32) + bg_ref[0]
        hl = lax.dot_general(x, wl_ref[0], NT_DIMS, preferred_element_type=F32) + bl_ref[0]
        xg = jnp.minimum(hg, SWIGLU_LIMIT)
        xl = jnp.clip(hl, -SWIGLU_LIMIT, SWIGLU_LIMIT)
        act = xg * jax.nn.sigmoid(SWIGLU_ALPHA * xg) * (xl + 1.0)
        out = jnp.dot(act.astype(BF16), wd_ref[0].astype(BF16), preferred_element_type=F32) + bd_ref[0]
        half = out.shape[1] // 2
        ys_ref[...] = _pack_bf16_pairs(out[:, :half], out[:, half:])

    @pl.when(used_ref[i] == 0)
    def _():
        ys_ref[...] = jnp.zeros_like(ys_ref)


def _gmm(tile_expert, tile_used, xs, wg_t, wl_t, b_glu, b_lin, w_down, b_down):
    rows, half = xs.shape
    d = 2 * half
    de = wg_t.shape[1]
    tm = GMM_TILE
    wspec = lambda shape: pl.BlockSpec((1,) + shape, lambda i, te, used: (te[i], 0, 0))
    grid_spec = pltpu.PrefetchScalarGridSpec(
        num_scalar_prefetch=2,
        grid=(rows // tm,),
        in_specs=[
            pl.BlockSpec((tm, half), lambda i, te, used: (i, 0)),
            wspec((de, d)), wspec((de, d)), wspec((1, de)), wspec((1, de)),
            wspec((de, d)), wspec((1, d)),
        ],
        out_specs=pl.BlockSpec((tm, half), lambda i, te, used: (i, 0)),
    )
    return pl.pallas_call(
        _gmm_kernel,
        grid_spec=grid_spec,
        out_shape=jax.ShapeDtypeStruct((rows, half), jnp.int32),
        compiler_params=_cparams("arbitrary"),
        name="gmm",
    )(tile_expert, tile_used, xs, wg_t, wl_t, b_glu, b_lin, w_down, b_down)


def _final_kernel(h_ref, yk_ref, gate_ref, p_ref, g_ref, wg_ref, wp_ref, o_ref):
    gates = gate_ref[...]
    h = h_ref[...]
    for k in range(TOP_K):
        h = h + gates[:, k:k + 1] * jnp.concatenate(_unpack_bf16_pairs(yk_ref[k]), axis=1)
    u = _rms(h, g_ref[...]).astype(BF16)
    gate = jax.nn.sigmoid(jnp.dot(u, wg_ref[...], preferred_element_type=F32))
    proj = jnp.dot(p_ref[...].astype(BF16), wp_ref[...], preferred_element_type=F32)
    o_ref[...] = h + gate * proj


def _final(h1, yk, gates, p2, g_ple, w_gate, w_proj, first_tile, out_so_far=None):
    n, d = h1.shape
    tm = ROW_TILE
    const = lambda i: (0, 0)
    rows = lambda width: pl.BlockSpec((tm, width), lambda i: (i + first_tile, 0))
    in_specs = [
        rows(d),
        pl.BlockSpec((TOP_K, tm, d // 2), lambda i: (0, i, 0)),
        rows(LANES),
        rows(p2.shape[1]),
        pl.BlockSpec((1, d), const),
        pl.BlockSpec(w_gate.shape, const),
        pl.BlockSpec(w_proj.shape, const),
    ]
    args = [h1, yk, gates, p2, g_ple, w_gate, w_proj]
    kernel_fn, aliases = _final_kernel, {}
    if out_so_far is not None:
        in_specs.append(pl.BlockSpec(memory_space=pl.ANY))
        args.append(out_so_far)
        aliases = {len(args) - 1: 0}
        kernel_fn = lambda *refs: _final_kernel(*refs[:7], refs[8])
    return pl.pallas_call(
        kernel_fn,
        grid=(yk.shape[1] // tm,),
        in_specs=in_specs,
        out_specs=rows(d),
        out_shape=jax.ShapeDtypeStruct((n, d), F32),
        input_output_aliases=aliases,
        compiler_params=_cparams("parallel"),
        name="final",
    )(*args)


def _layer(h, p, g_mix, w_in, b_f, g_qa, g_ka, g_qb, g_kb, w_o, g_ffn, w_router, b_router,
           w_gate_up, b_gate_up, w_down, b_down, g_ple, w_ple_gate, w_ple_proj):
    batch, seq, d = h.shape
    n = batch * seq
    assert tuple(dil for _, dil in DILATED_PATTERNS) == (1, 4, 16)
    for window, dil in DILATED_PATTERNS:
        per_class = seq // BLOCK // dil
        assert window // dil == BLOCK and seq % (dil * BLOCK) == 0
        assert per_class % UNITS_PER_STEP == 0 or UNITS_PER_STEP % per_class == 0
    assert n % ROW_TILE == 0 and d % (2 * LANES) == 0 and seq % FOX_TILE == 0
    x2 = h.reshape(n, d)

    qkv_cols = 3 * WIDTH_A + 3 * WIDTH_B
    w_qkv = w_in[:, :qkv_cols].astype(BF16)
    w_f = jnp.pad(w_in[:, qkv_cols:], ((0, 0), (0, LANES - N_HEADS_B))).astype(BF16)
    b_fp = jnp.pad(b_f.astype(F32), (0, LANES - N_HEADS_B)).reshape(1, LANES)
    scale = HEAD_DIM ** -0.5
    gains = jnp.stack([jnp.tile(g_qa, N_HEADS_A) * (scale * LOG2E), jnp.tile(g_ka, N_HEADS_A),
                       jnp.tile(g_qb, N_HEADS_B) * (scale * LOG2E), jnp.tile(g_kb, N_HEADS_B)]).astype(F32)
    hid = jnp.arange(2 * LANES) // HEAD_DIM
    bd = (hid[:, None] == hid[None, :]).astype(BF16)

    z, logf = _in_proj(x2, g_mix.reshape(1, d), w_qkv, w_f, b_fp, gains, bd)
    ccol = _cumsum(logf, batch, seq)

    slopes = 2.0 ** (-8.0 * jnp.arange(1, N_HEADS_A + 1, dtype=F32) / N_HEADS_A)
    mix_a = _dilated(z, slopes, batch, seq)
    mix_b = _fox(z, ccol, batch, seq)

    w_r = jnp.pad(w_router.astype(F32), ((0, 0), (0, LANES - N_EXPERTS)))
    w_r_hi = w_r.astype(BF16)
    w_r = jnp.concatenate([w_r_hi, (w_r - w_r_hi.astype(F32)).astype(BF16)], axis=1)
    b_r = jnp.concatenate([b_router.astype(F32), jnp.full((LANES - N_EXPERTS,), NEG_INF, F32)]).reshape(1, LANES)
    h1, u_packed, top_idx, gates, rank, counts = _post_attn(
        mix_a, mix_b, x2, w_o.astype(BF16), g_ffn.reshape(1, d), w_r, b_r)

    counts = counts[0, :N_EXPERTS].astype(jnp.int32)
    tiles_per = (counts + GMM_TILE - 1) // GMM_TILE
    tile_end = jnp.cumsum(tiles_per)
    starts = (tile_end - tiles_per) * GMM_TILE
    n_tiles = n * TOP_K // GMM_TILE + N_EXPERTS
    tile_ids = jnp.arange(n_tiles, dtype=jnp.int32)
    tile_used = (tile_ids < tile_end[-1]).astype(jnp.int32)
    last_used = jnp.minimum(tile_ids, tile_end[-1] - 1)
    tile_expert = jnp.sum((last_used[:, None] >= tile_end[None, :]).astype(jnp.int32), axis=1)
    tile_expert = jnp.minimum(tile_expert, N_EXPERTS - 1)
    pos_t = starts[top_idx[:TOP_K]] + rank[:TOP_K]
    filler = jnp.arange(GMM_TILE, dtype=jnp.int32)[None, :]
    need = (tiles_per * GMM_TILE - counts)[:, None]
    spare_before = (jnp.cumsum(GMM_TILE - need[:, 0]) - (GMM_TILE - need[:, 0]))[:, None]
    filler_key = jnp.where(filler < need, (starts + counts)[:, None] + filler,
                           tile_end[-1] * GMM_TILE + spare_before + filler - need)
    filler_tok = (jnp.arange(N_EXPERTS * GMM_TILE, dtype=jnp.int32) % n)
    keys = jnp.concatenate([pos_t.reshape(-1), filler_key.reshape(-1).astype(jnp.int32)])
    toks = jnp.concatenate([jnp.tile(jnp.arange(n, dtype=jnp.int32), TOP_K), filler_tok])
    src = lax.sort((keys, toks), num_keys=1)[1]
    xs = _sc_gather(u_packed, src)

    de = w_down.shape[1]
    wg_t, wl_t = _wprep(w_gate_up)
    ys = _gmm(tile_expert, tile_used, xs, wg_t, wl_t,
              b_gate_up[:, 0::2].reshape(N_EXPERTS, 1, de).astype(F32),
              b_gate_up[:, 1::2].reshape(N_EXPERTS, 1, de).astype(F32),
              w_down, b_down.reshape(N_EXPERTS, 1, d).astype(F32))
    assert n % (2 * ROW_TILE) == 0
    half_n = n // 2
    out = None
    for part in range(2):
        part_pos = pos_t[:, part * half_n:(part + 1) * half_n].reshape(-1)
        yk = _sc_gather(ys, part_pos).reshape(TOP_K, half_n, d // 2)
        out = _final(h1, yk, gates, p.reshape(n, -1), g_ple.reshape(1, d), w_ple_gate.astype(BF16),
                     w_ple_proj.astype(BF16), part * (half_n // ROW_TILE), out)
    return out.reshape(batch, seq, d)


def kernel(x, p, g_mix, w_in, b_f, g_qa, g_ka, g_qb, g_kb, w_o, g_ffn, w_router, b_router,
           w_gate_up, b_gate_up, w_down, b_down, g_ple, w_ple_gate, w_ple_proj):
    h = x
    for i in range(g_mix.shape[0]):
        h = _layer(h, p[i], g_mix[i], w_in[i], b_f[i], g_qa[i], g_ka[i], g_qb[i], g_kb[i], w_o[i],
                   g_ffn[i], w_router[i], b_router[i], w_gate_up[i], b_gate_up[i], w_down[i],
                   b_down[i], g_ple[i], w_ple_gate[i], w_ple_proj[i])
    return h
```

```python
import functools

import jax
import jax.numpy as jnp
from jax import lax
from jax.experimental import pallas as pl
from jax.experimental.pallas import tpu as pltpu
from jax.experimental.pallas import tpu_sc as plsc

HEAD_DIM = 64
N_HEADS_A = 8
N_HEADS_B = 8
WIDTH_A = N_HEADS_A * HEAD_DIM
WIDTH_B = N_HEADS_B * HEAD_DIM
DILATED_PATTERNS = ((128, 1), (512, 4), (2048, 16))
BLOCK = 128
N_EXPERTS = 32
TOP_K = 4
SWIGLU_LIMIT = 7.0
SWIGLU_ALPHA = 1.702
NORM_EPS = 1e-6

LANES = 128
PAIR = LANES // HEAD_DIM
ROW_TILE = 512
GMM_TILE = 512
FOX_TILE = 512
UNITS_PER_STEP = 8
SC_CHUNK = 32
SC_DEPTH = 4
VMEM_LIMIT = 56 * 1024 * 1024

F32 = jnp.float32
BF16 = jnp.bfloat16
NEG_INF = float("-inf")
NT_DIMS = (((1,), (1,)), ((), ()))
LOG2E = 1.4426950408889634


def _cparams(*sem):
    return pltpu.CompilerParams(dimension_semantics=sem, vmem_limit_bytes=VMEM_LIMIT)


def _rms(x, g):
    return x * lax.rsqrt(jnp.mean(x * x, axis=-1, keepdims=True) + NORM_EPS) * g


def _lane_iota():
    return lax.broadcasted_iota(jnp.int32, (1, LANES), 1)


def _head_lane_mask(h):
    lane = _lane_iota()
    return (lane >= h * HEAD_DIM) & (lane < (h + 1) * HEAD_DIM)


def _merge_heads(acc0, acc1):
    first = _head_lane_mask(0)
    num = jnp.where(first, acc0, acc1)
    den = pltpu.roll(jnp.where(first, acc1, acc0), HEAD_DIM, axis=1)
    return num, den


def _in_proj_kernel(x_ref, g_ref, w_ref, wf_ref, bf_ref, gain_ref, bd_ref, z_ref, lf_ref):
    u = _rms(x_ref[...], g_ref[...]).astype(BF16)
    chunk = WIDTH_A
    normed = {0: 0, 1: 1, 3: 2, 4: 3}
    for c in range(6):
        acc = jnp.dot(u, w_ref[:, c * chunk:(c + 1) * chunk], preferred_element_type=F32)
        if c in normed:
            sq = (acc * acc).astype(BF16)
            half = chunk // 2
            ss = jnp.concatenate(
                [jnp.dot(sq[:, j * half:(j + 1) * half], bd_ref[...], preferred_element_type=F32)
                 for j in range(2)], axis=1)
            r = normed[c]
            acc = acc * lax.rsqrt(ss * (1.0 / HEAD_DIM) + NORM_EPS) * gain_ref[r:r + 1, :]
        z_ref[:, c * chunk:(c + 1) * chunk] = acc.astype(BF16)
    zf = jnp.dot(u, wf_ref[...], preferred_element_type=F32) + bf_ref[...]
    lf_ref[...] = jax.nn.log_sigmoid(zf)


def _in_proj(x2, g_mix, w_qkv, w_f, b_f, gains, bd):
    n, d = x2.shape
    cols = w_qkv.shape[1]
    tm = ROW_TILE
    const = lambda i: (0, 0)
    return pl.pallas_call(
        _in_proj_kernel,
        grid=(n // tm,),
        in_specs=[
            pl.BlockSpec((tm, d), lambda i: (i, 0)),
            pl.BlockSpec((1, d), const),
            pl.BlockSpec((d, cols), const),
            pl.BlockSpec((d, LANES), const),
            pl.BlockSpec((1, LANES), const),
            pl.BlockSpec(gains.shape, const),
            pl.BlockSpec(bd.shape, const),
        ],
        out_specs=[
            pl.BlockSpec((tm, cols), lambda i: (i, 0)),
            pl.BlockSpec((tm, LANES), lambda i: (i, 0)),
        ],
        out_shape=[
            jax.ShapeDtypeStruct((n, cols), BF16),
            jax.ShapeDtypeStruct((n, LANES), F32),
        ],
        compiler_params=_cparams("parallel"),
        name="in_proj",
    )(x2, g_mix, w_qkv, w_f, b_f, gains, bd)


def _cumsum_kernel(lf_ref, tri_ref, cpk_ref):
    s = lf_ref.shape[0]
    lane = _lane_iota()
    carry = jnp.zeros((1, LANES), F32)
    for blk in range(s // BLOCK):
        rows = slice(blk * BLOCK, (blk + 1) * BLOCK)
        part = jnp.dot(tri_ref[...], lf_ref[rows, :], precision=lax.Precision.HIGHEST,
                       preferred_element_type=F32) + carry
        carry = part[BLOCK - 1:BLOCK, :]
        c = part * LOG2E
        hi = c.astype(BF16).astype(F32)
        r1 = c - hi
        mid = r1.astype(BF16).astype(F32)
        lo = r1 - mid
        packed = jnp.where(lane < N_HEADS_B, hi,
                 jnp.where(lane < 2 * N_HEADS_B, pltpu.roll(mid, N_HEADS_B, axis=1),
                 jnp.where(lane < 3 * N_HEADS_B, pltpu.roll(lo, 2 * N_HEADS_B, axis=1),
                 jnp.where(lane == 3 * N_HEADS_B, 1.0, 0.0))))
        cpk_ref[rows, :] = packed.astype(BF16)


def _cumsum(logf, batch, seq):
    tri = (lax.broadcasted_iota(jnp.int32, (BLOCK, BLOCK), 0)
           >= lax.broadcasted_iota(jnp.int32, (BLOCK, BLOCK), 1)).astype(F32)
    return pl.pallas_call(
        _cumsum_kernel,
        grid=(batch,),
        in_specs=[
            pl.BlockSpec((seq, LANES), lambda b: (b, 0)),
            pl.BlockSpec((BLOCK, BLOCK), lambda b: (0, 0)),
        ],
        out_specs=pl.BlockSpec((seq, LANES), lambda b: (b, 0)),
        out_shape=jax.ShapeDtypeStruct((batch * seq, LANES), BF16),
        compiler_params=_cparams("parallel"),
        name="cumsum",
    )(logf, tri)


def _fox_features(cpk, pair, key_side):
    assert PAIR == 2
    r = lax.broadcasted_iota(jnp.int32, (LANES, PAIR * LANES), 0)
    c = lax.broadcasted_iota(jnp.int32, (LANES, PAIR * LANES), 1)
    hh = jnp.where(c >= LANES, 1, 0)
    slot = c - hh * LANES - HEAD_DIM * (1 - hh)
    head = PAIR * pair + hh
    piece_slot = slot - 3 if key_side else slot
    ones_slot = slot if key_side else slot - 3
    piece = (piece_slot >= 0) & (piece_slot < 3) & (r == N_HEADS_B * piece_slot + head)
    ones = (ones_slot >= 0) & (ones_slot < 3) & (r == 3 * N_HEADS_B)
    place = jnp.where(piece, -1.0 if key_side else 1.0, jnp.where(ones, 1.0, 0.0)).astype(BF16)
    return jnp.dot(cpk, place, preferred_element_type=F32).astype(BF16)


def _fox_kernel(q_ref, k_ref, v_ref, c_ref, o_ref, kf, vf, s_scr, m_scr, acc_scr, *, tile):
    pair = pl.program_id(1)
    i = pl.program_id(2)
    half = tile // 2
    in_head = [_head_lane_mask(h) for h in range(PAIR)]
    block = lambda feat, h: feat[:, h * LANES:(h + 1) * LANES]

    @pl.when(i == 0)
    def _():
        feat = _fox_features(c_ref[...], pair, True)
        for h in range(PAIR):
            kf[h] = jnp.where(in_head[h], k_ref[...], block(feat, h))
            vf[h] = jnp.where(in_head[h], v_ref[...], jnp.ones_like(v_ref[...]))

    row0 = pl.multiple_of(i * tile, tile)
    feat_q = _fox_features(c_ref[pl.ds(row0, tile), :], pair, False)
    q = q_ref[...]
    qf = [jnp.where(in_head[h], q, block(feat_q, h)) for h in range(PAIR)]

    def lane_groups_max(s):
        m = s[:, :LANES]
        for g in range(1, s.shape[1] // LANES):
            m = jnp.maximum(m, s[:, g * LANES:(g + 1) * LANES])
        return m

    def scores(qrows, off, width, h):
        return lax.dot_general(qrows, kf[h, pl.ds(off, width), :], NT_DIMS, preferred_element_type=F32)

    up_r = lax.broadcasted_iota(jnp.int32, (half, half), 0)
    up_c = lax.broadcasted_iota(jnp.int32, (half, half), 1)
    lo_r = lax.broadcasted_iota(jnp.int32, (half, tile), 0)
    lo_c = lax.broadcasted_iota(jnp.int32, (half, tile), 1)
    for h in range(PAIR):
        s_up = jnp.where(up_c <= up_r, scores(qf[h][:half], row0, half, h), NEG_INF)
        s_lo = jnp.where(lo_c <= lo_r + half, scores(qf[h][half:], row0, tile, h), NEG_INF)
        s_scr[h, :half, pl.ds(row0, half)] = s_up
        s_scr[h, half:, pl.ds(row0, tile)] = s_lo
        m_scr[h, :half] = lane_groups_max(s_up)
        m_scr[h, half:] = lane_groups_max(s_lo)

    def pass1(j, _):
        off = pl.multiple_of(j * tile, tile)
        for h in range(PAIR):
            s = scores(qf[h], off, tile, h)
            s_scr[h, :, pl.ds(off, tile)] = s
            m_scr[h] = jnp.maximum(m_scr[h], lane_groups_max(s))
        return 0
    lax.fori_loop(0, i, pass1, 0)

    m = [jnp.max(m_scr[h], axis=-1, keepdims=True) for h in range(PAIR)]
    for h in range(PAIR):
        p_up = jnp.exp2(s_scr[h, :half, pl.ds(row0, half)] - m[h][:half]).astype(BF16)
        p_lo = jnp.exp2(s_scr[h, half:, pl.ds(row0, tile)] - m[h][half:]).astype(BF16)
        acc_scr[h, :half] = jnp.dot(p_up, vf[h, pl.ds(row0, half), :], preferred_element_type=F32)
        acc_scr[h, half:] = jnp.dot(p_lo, vf[h, pl.ds(row0, tile), :], preferred_element_type=F32)

    def pass2(j, _):
        off = pl.multiple_of(j * tile, tile)
        for h in range(PAIR):
            p = jnp.exp2(s_scr[h, :, pl.ds(off, tile)] - m[h]).astype(BF16)
            acc_scr[h] += jnp.dot(p, vf[h, pl.ds(off, tile), :], preferred_element_type=F32)
        return 0
    lax.fori_loop(0, i, pass2, 0)

    num, den = _merge_heads(acc_scr[0], acc_scr[1])
    o_ref[...] = (num / den).astype(o_ref.dtype)


def _fox(z, ccol, batch, seq):
    n = z.shape[0]
    tile = FOX_TILE
    nq = seq // tile
    npair = N_HEADS_B // PAIR
    base = 3 * WIDTH_A // LANES
    qcol, kcol, vcol = base, base + WIDTH_B // LANES, base + 2 * WIDTH_B // LANES
    return pl.pallas_call(
        functools.partial(_fox_kernel, tile=tile),
        grid=(batch, npair, nq),
        in_specs=[
            pl.BlockSpec((tile, LANES), lambda b, p, i: (b * nq + i, qcol + p)),
            pl.BlockSpec((seq, LANES), lambda b, p, i: (b, kcol + p)),
            pl.BlockSpec((seq, LANES), lambda b, p, i: (b, vcol + p)),
            pl.BlockSpec((seq, LANES), lambda b, p, i: (b, 0)),
        ],
        out_specs=pl.BlockSpec((tile, LANES), lambda b, p, i: (b * nq + i, p)),
        out_shape=jax.ShapeDtypeStruct((n, WIDTH_B), BF16),
        scratch_shapes=[
            pltpu.VMEM((PAIR, seq, LANES), BF16),
            pltpu.VMEM((PAIR, seq, LANES), BF16),
            pltpu.VMEM((PAIR, tile, seq), F32),
            pltpu.VMEM((PAIR, tile, LANES), F32),
            pltpu.VMEM((PAIR, tile, LANES), F32),
        ],
        compiler_params=_cparams("parallel", "parallel", "arbitrary"),
        name="fox",
    )(z, z, z, ccol)


def _dilated_kernel(slope_ref, q_ref, k_ref, v_ref, o_ref,
                    natf, p4f, p4b, p16b, qfeat, kfeat, vals, dens, maxs, *, seq):
    pair = pl.program_id(1)
    lane = _lane_iota()
    first = _head_lane_mask(0)
    quarter = seq // 4
    units = seq // BLOCK

    def deinterleave(src, t, span_start, span):
        return [src[t, pl.ds(span_start + r, span // 4, stride=4), :] for r in range(4)]

    for t, ref in enumerate((q_ref, k_ref, v_ref)):
        natf[t] = ref[...].astype(F32)
        for r, part in enumerate(deinterleave(natf, t, 0, seq)):
            p4f[t, pl.ds(r * quarter, quarter), :] = part
            p4b[t, pl.ds(r * quarter, quarter), :] = part.astype(BF16)
        for r4 in range(4):
            for r, part in enumerate(deinterleave(p4f, t, r4 * quarter, quarter)):
                p16b[t, pl.ds(r4 * quarter + r * (quarter // 4), quarter // 4), :] = part.astype(BF16)

    qi = lax.broadcasted_iota(jnp.int32, (BLOCK, LANES), 0).astype(F32)
    kj = lax.broadcasted_iota(jnp.int32, (2 * BLOCK, LANES), 0).astype(F32)
    for p, (_, dil) in enumerate(DILATED_PATTERNS):
        for h in range(PAIR):
            sd = slope_ref[PAIR * pair + h] * float(dil)
            base = HEAD_DIM * (1 - h)
            def pieces_and_ones(value, first_piece_lane, first_one_lane):
                hi = value.astype(BF16).astype(F32)
                mid = (value - hi).astype(BF16).astype(F32)
                lo = value - hi - mid
                return jnp.where(lane == first_piece_lane, hi,
                       jnp.where(lane == first_piece_lane + 1, mid,
                       jnp.where(lane == first_piece_lane + 2, lo,
                       jnp.where((lane >= first_one_lane) & (lane < first_one_lane + 3), 1.0, 0.0)))).astype(BF16)
            qfeat[p * PAIR + h] = pieces_and_ones(-(qi + float(BLOCK)) * (sd * LOG2E), base, base + 3)
            kfeat[p * PAIR + h] = pieces_and_ones(kj * (sd * LOG2E), base + 3, base)

    bq = lax.broadcasted_iota(jnp.int32, (BLOCK, 2 * BLOCK), 0)
    bk = lax.broadcasted_iota(jnp.int32, (BLOCK, 2 * BLOCK), 1)
    rel = bq + BLOCK - bk
    band = (rel >= 0) & (rel <= BLOCK)

    def unit(p, srcs, u, prev_valid):
        qs, ks, vs = srcs
        start = pl.multiple_of(u * BLOCK, BLOCK)
        prev = pl.multiple_of(jnp.maximum(start - BLOCK, 0), BLOCK)
        qb = qs[pl.ds(start, BLOCK), :]
        kk = jnp.concatenate([ks[pl.ds(prev, BLOCK), :], ks[pl.ds(start, BLOCK), :]], axis=0)
        vv = jnp.concatenate([vs[pl.ds(prev, BLOCK), :], vs[pl.ds(start, BLOCK), :]], axis=0)
        if prev_valid is True:
            ok = band
        elif prev_valid is False:
            ok = band & (bk >= BLOCK)
        else:
            ok = band & ((bk >= BLOCK) | prev_valid)
        accs, ms = [], []
        for h in range(PAIR):
            in_h = _head_lane_mask(h)
            qh = jnp.where(in_h, qb, qfeat[p * PAIR + h])
            kh = jnp.where(in_h, kk, kfeat[p * PAIR + h])
            vh = jnp.where(in_h, vv, jnp.ones_like(vv))
            s = lax.dot_general(qh, kh, NT_DIMS, preferred_element_type=F32)
            s = jnp.where(ok, s, NEG_INF)
            m = jnp.max(s, axis=-1, keepdims=True)
            pr = jnp.exp2(s - m).astype(BF16)
            accs.append(jnp.dot(pr, vh, preferred_element_type=F32))
            ms.append(m)
        num, den = _merge_heads(*accs)
        vals[p, pl.ds(start, BLOCK), :] = num
        dens[p, pl.ds(start, BLOCK), :] = den
        maxs[p, pl.ds(start, BLOCK), :] = jnp.where(first, ms[0], ms[1])

    group = UNITS_PER_STEP
    sources = ((q_ref, k_ref, v_ref), tuple(p4b.at[t] for t in range(3)), tuple(p16b.at[t] for t in range(3)))
    for p, (_, dil) in enumerate(DILATED_PATTERNS):
        per_class = units // dil

        def step(g, _, p=p, per_class=per_class):
            for e in range(group):
                u = g * group + e
                if per_class >= group:
                    prev_valid = (u % per_class != 0) if e == 0 else True
                else:
                    prev_valid = e % per_class != 0
                unit(p, sources[p], u, prev_valid)
            return 0
        lax.fori_loop(0, units // group, step, 0)

    for t, arr in enumerate((vals, dens, maxs)):
        for r4 in range(4):
            for r in range(4):
                p4f[t, pl.ds(r4 * quarter + r, quarter // 4, stride=4), :] = \
                    arr[2, pl.ds(r4 * quarter + r * (quarter // 4), quarter // 4), :]

    for r in range(4):
        grouped = pl.ds(r * quarter, quarter)
        natural = pl.ds(r, quarter, stride=4)
        ms = (maxs[0, natural, :], maxs[1, grouped, :], p4f[2, grouped, :])
        vs = (vals[0, natural, :], vals[1, grouped, :], p4f[0, grouped, :])
        ds = (dens[0, natural, :], dens[1, grouped, :], p4f[1, grouped, :])
        m_all = jnp.maximum(jnp.maximum(ms[0], ms[1]), ms[2])
        num = jnp.zeros((quarter, LANES), F32)
        den = jnp.zeros((quarter, LANES), F32)
        for p in range(3):
            e = jnp.exp2(ms[p] - m_all)
            num = num + e * vs[p]
            den = den + e * ds[p]
        natf[0, natural, :] = num / den
    o_ref[...] = natf[0].astype(o_ref.dtype)


def _dilated(z, slopes, batch, seq):
    n = z.shape[0]
    npair = N_HEADS_A // PAIR
    npat = len(DILATED_PATTERNS)
    qcol, kcol, vcol = 0, WIDTH_A // LANES, 2 * WIDTH_A // LANES
    blk = lambda c0: pl.BlockSpec((seq, LANES), lambda b, p: (b, c0 + p))
    return pl.pallas_call(
        functools.partial(_dilated_kernel, seq=seq),
        grid=(batch, npair),
        in_specs=[pl.BlockSpec(memory_space=pltpu.SMEM), blk(qcol), blk(kcol), blk(vcol)],
        out_specs=pl.BlockSpec((seq, LANES), lambda b, p: (b, p)),
        out_shape=jax.ShapeDtypeStruct((n, WIDTH_A), BF16),
        scratch_shapes=[
            pltpu.VMEM((3, seq, LANES), F32),
            pltpu.VMEM((3, seq, LANES), F32),
            pltpu.VMEM((3, seq, LANES), BF16),
            pltpu.VMEM((3, seq, LANES), BF16),
            pltpu.VMEM((npat * PAIR, BLOCK, LANES), BF16),
            pltpu.VMEM((npat * PAIR, 2 * BLOCK, LANES), BF16),
            pltpu.VMEM((npat, seq, LANES), F32),
            pltpu.VMEM((npat, seq, LANES), F32),
            pltpu.VMEM((npat, seq, LANES), F32),
        ],
        compiler_params=_cparams("parallel", "parallel"),
        name="dilated",
    )(slopes, z, z, z)


def _pack_bf16_pairs(a, b):
    hi = pltpu.bitcast(a.astype(BF16).astype(F32), jnp.int32)
    lo = pltpu.bitcast(b.astype(BF16).astype(F32), jnp.int32)
    return (hi & jnp.int32(-65536)) | lax.shift_right_logical(lo, jnp.int32(16))


def _unpack_bf16_pairs(w):
    a = pltpu.bitcast(w & jnp.int32(-65536), F32)
    b = pltpu.bitcast(lax.shift_left(w, jnp.int32(16)), F32)
    return a, b


def _post_attn_kernel(ma_ref, mb_ref, x_ref, wo_ref, g_ref, wr_ref, br_ref, tri_ref,
                      h_ref, up_ref, idx_ref, gate_ref, rank_ref, cnt_ref, carry):
    @pl.when(pl.program_id(0) == 0)
    def _():
        carry[...] = jnp.zeros_like(carry)

    y = jnp.dot(ma_ref[...], wo_ref[:WIDTH_A, :], preferred_element_type=F32)
    y = y + jnp.dot(mb_ref[...], wo_ref[WIDTH_A:, :], preferred_element_type=F32)
    h = x_ref[...] + y
    h_ref[...] = h
    u = _rms(h, g_ref[...])
    half = u.shape[1] // 2
    up_ref[...] = _pack_bf16_pairs(u[:, :half], u[:, half:])

    u_hi = u.astype(BF16)
    u_lo = (u - u_hi.astype(F32)).astype(BF16)
    hi_terms = jnp.dot(u_hi, wr_ref[...], preferred_element_type=F32)
    logits = (hi_terms[:, :LANES] + hi_terms[:, LANES:]
              + jnp.dot(u_lo, wr_ref[:, :LANES], preferred_element_type=F32)) + br_ref[...]
    lane = lax.broadcasted_iota(jnp.int32, logits.shape, 1).astype(F32)
    work = logits
    idxs, tops = [], []
    for _ in range(TOP_K):
        top = jnp.max(work, axis=-1, keepdims=True)
        idx = jnp.min(jnp.where(work == top, lane, float(LANES)), axis=-1, keepdims=True)
        work = jnp.where(lane == idx, NEG_INF, work)
        idxs.append(idx)
        tops.append(top)
    exps = [jnp.exp(t - tops[0]) for t in tops]
    total = exps[0] + exps[1] + exps[2] + exps[3]

    onehot = jnp.zeros(logits.shape, F32)
    for idx in idxs:
        onehot = onehot + (lane == idx).astype(F32)
    before = jnp.dot(tri_ref[...], onehot.astype(BF16), preferred_element_type=F32) + carry[...]
    carry[...] = carry[...] + jnp.sum(onehot, axis=0, keepdims=True)
    cnt_ref[...] = carry[...]

    idx_out = jnp.zeros(logits.shape, F32)
    gate_out = jnp.zeros(logits.shape, F32)
    rank_out = jnp.zeros(logits.shape, F32)
    for k in range(TOP_K):
        rank_k = jnp.sum(jnp.where(lane == idxs[k], before, 0.0), axis=-1, keepdims=True)
        idx_out = jnp.where(lane == float(k), idxs[k], idx_out)
        gate_out = jnp.where(lane == float(k), exps[k] / total, gate_out)
        rank_out = jnp.where(lane == float(k), rank_k, rank_out)
    idx_ref[...] = idx_out.T[:8, :].astype(jnp.int32)
    gate_ref[...] = gate_out
    rank_ref[...] = rank_out.T[:8, :].astype(jnp.int32)


def _post_attn(mix_a, mix_b, x2, w_o, g_ffn, w_r, b_r):
    n, d = x2.shape
    tm = ROW_TILE
    tri = (lax.broadcasted_iota(jnp.int32, (tm, tm), 0)
           > lax.broadcasted_iota(jnp.int32, (tm, tm), 1)).astype(BF16)
    const = lambda i: (0, 0)
    row = lambda w: pl.BlockSpec((tm, w), lambda i: (i, 0))
    lanes_t = pl.BlockSpec((8, tm), lambda i: (0, i))
    return pl.pallas_call(
        _post_attn_kernel,
        grid=(n // tm,),
        in_specs=[
            row(WIDTH_A), row(WIDTH_B), row(d),
            pl.BlockSpec(w_o.shape, const),
            pl.BlockSpec((1, d), const),
            pl.BlockSpec(w_r.shape, const),
            pl.BlockSpec((1, LANES), const),
            pl.BlockSpec((tm, tm), const),
        ],
        out_specs=[row(d), row(d // 2), lanes_t, row(LANES), lanes_t,
                   pl.BlockSpec((1, LANES), const)],
        out_shape=[
            jax.ShapeDtypeStruct((n, d), F32),
            jax.ShapeDtypeStruct((n, d // 2), jnp.int32),
            jax.ShapeDtypeStruct((8, n), jnp.int32),
            jax.ShapeDtypeStruct((n, LANES), F32),
            jax.ShapeDtypeStruct((8, n), jnp.int32),
            jax.ShapeDtypeStruct((1, LANES), F32),
        ],
        scratch_shapes=[pltpu.VMEM((1, LANES), F32)],
        compiler_params=_cparams("arbitrary"),
        name="post_attn",
    )(mix_a, mix_b, x2, w_o, g_ffn, w_r, b_r, tri)


def _wprep_kernel(w_ref, wg_ref, wl_ref, wt):
    d, cols = w_ref.shape[1:]
    de = cols // 2
    for j in range(d // LANES):
        lanes = slice(j * LANES, (j + 1) * LANES)
        wt[j] = w_ref[0, lanes, :].T
        wg_ref[0, :, lanes] = wt[j, pl.ds(0, de, stride=2), :].astype(BF16)
        wl_ref[0, :, lanes] = wt[j, pl.ds(1, de, stride=2), :].astype(BF16)


def _wprep(w_gate_up):
    ne, d, cols = w_gate_up.shape
    de = cols // 2
    out = pl.BlockSpec((1, de, d), lambda e: (e, 0, 0))
    return pl.pallas_call(
        _wprep_kernel,
        grid=(ne,),
        in_specs=[pl.BlockSpec((1, d, cols), lambda e: (e, 0, 0))],
        out_specs=[out, out],
        out_shape=[jax.ShapeDtypeStruct((ne, de, d), BF16)] * 2,
        scratch_shapes=[pltpu.VMEM((d // LANES, cols, LANES), F32)],
        compiler_params=_cparams("parallel"),
        name="wprep",
    )(w_gate_up)


def _sc_gather(table, idx):
    info = plsc.get_sparse_core_info()
    workers = info.num_cores * info.num_subcores
    rows, width = idx.shape[0], table.shape[1]
    chunk, depth = SC_CHUNK, SC_DEPTH
    assert rows % (workers * chunk * depth) == 0
    per_worker = rows // workers
    nchunks = per_worker // chunk
    mesh = plsc.VectorSubcoreMesh(core_axis_name="c", subcore_axis_name="s")

    @functools.partial(
        pl.kernel, mesh=mesh,
        out_type=jax.ShapeDtypeStruct((rows, width), table.dtype),
        scratch_types=[
            pltpu.VMEM((nchunks, chunk), jnp.int32),
            pltpu.VMEM((depth, chunk, width), table.dtype),
            pltpu.SemaphoreType.DMA((depth,)),
            pltpu.SemaphoreType.DMA((depth,)),
        ],
    )
    def gather_kernel(table_hbm, idx_hbm, out_hbm, idx_v, rows_v, gsem, wsem):
        wid = lax.axis_index("s") * info.num_cores + lax.axis_index("c")
        base = wid * per_worker
        pltpu.sync_copy(idx_hbm.at[wid], idx_v)

        def gather(c, b):
            return pltpu.make_async_copy(table_hbm.at[idx_v.at[c]], rows_v.at[b], gsem.at[b])

        def write(c, b):
            off = pl.multiple_of(base + c * chunk, chunk)
            return pltpu.make_async_copy(rows_v.at[b], out_hbm.at[pl.ds(off, chunk)], wsem.at[b])

        @pl.loop(0, nchunks, step=depth)
        def _(c0):
            for b in range(depth):
                gather(c0 + b, b).start()
            for b in range(depth):
                gather(c0 + b, b).wait()
                write(c0 + b, b).start()
            for b in range(depth):
                write(c0 + b, b).wait()

    return gather_kernel(table, idx.reshape(workers, nchunks, chunk))


def _gmm_kernel(te_ref, used_ref, xs_ref, wg_ref, wl_ref, bg_ref, bl_ref, wd_ref, bd_ref, ys_ref):
    i = pl.program_id(0)

    @pl.when(used_ref[i] > 0)
    def _():
        a, b = _unpack_bf16_pairs(xs_ref[...])
        x = jnp.concatenate([a, b], axis=1).astype(BF16)
        hg = lax.dot_general(x, wg_ref[0], NT_DIMS, preferred_element_type=F32) + bg_ref[0]
        hl = lax.dot_general(x, wl_ref[0], NT_DIMS, preferred_element_type=F32) + bl_ref[0]
        xg = jnp.minimum(hg, SWIGLU_LIMIT)
        xl = jnp.clip(hl, -SWIGLU_LIMIT, SWIGLU_LIMIT)
        act = xg * jax.nn.sigmoid(SWIGLU_ALPHA * xg) * (xl + 1.0)
        out = jnp.dot(act.astype(BF16), wd_ref[0].astype(BF16), preferred_element_type=F32) + bd_ref[0]
        half = out.shape[1] // 2
        ys_ref[...] = _pack_bf16_pairs(out[:, :half], out[:, half:])

    @pl.when(used_ref[i] == 0)
    def _():
        ys_ref[...] = jnp.zeros_like(ys_ref)


def _gmm(tile_expert, tile_used, xs, wg_t, wl_t, b_glu, b_lin, w_down, b_down):
    rows, half = xs.shape
    d = 2 * half
    de = wg_t.shape[1]
    tm = GMM_TILE
    wspec = lambda shape: pl.BlockSpec((1,) + shape, lambda i, te, used: (te[i], 0, 0))
    grid_spec = pltpu.PrefetchScalarGridSpec(
        num_scalar_prefetch=2,
        grid=(rows // tm,),
        in_specs=[
            pl.BlockSpec((tm, half), lambda i, te, used: (i, 0)),
            wspec((de, d)), wspec((de, d)), wspec((1, de)), wspec((1, de)),
            wspec((de, d)), wspec((1, d)),
        ],
        out_specs=pl.BlockSpec((tm, half), lambda i, te, used: (i, 0)),
    )
    return pl.pallas_call(
        _gmm_kernel,
        grid_spec=grid_spec,
        out_shape=jax.ShapeDtypeStruct((rows, half), jnp.int32),
        compiler_params=_cparams("arbitrary"),
        name="gmm",
    )(tile_expert, tile_used, xs, wg_t, wl_t, b_glu, b_lin, w_down, b_down)


def _final_kernel(h_ref, yk_ref, gate_ref, p_ref, g_ref, wg_ref, wp_ref, o_ref):
    gates = gate_ref[...]
    h = h_ref[...]
    for k in range(TOP_K):
        h = h + gates[:, k:k + 1] * jnp.concatenate(_unpack_bf16_pairs(yk_ref[k]), axis=1)
    u = _rms(h, g_ref[...]).astype(BF16)
    gate = jax.nn.sigmoid(jnp.dot(u, wg_ref[...], preferred_element_type=F32))
    proj = jnp.dot(p_ref[...].astype(BF16), wp_ref[...], preferred_element_type=F32)
    o_ref[...] = h + gate * proj


def _final(h1, yk, gates, p2, g_ple, w_gate, w_proj, first_tile, out_so_far=None):
    n, d = h1.shape
    tm = ROW_TILE
    const = lambda i: (0, 0)
    rows = lambda width: pl.BlockSpec((tm, width), lambda i: (i + first_tile, 0))
    in_specs = [
        rows(d),
        pl.BlockSpec((TOP_K, tm, d // 2), lambda i: (0, i, 0)),
        rows(LANES),
        rows(p2.shape[1]),
        pl.BlockSpec((1, d), const),
        pl.BlockSpec(w_gate.shape, const),
        pl.BlockSpec(w_proj.shape, const),
    ]
    args = [h1, yk, gates, p2, g_ple, w_gate, w_proj]
    kernel_fn, aliases = _final_kernel, {}
    if out_so_far is not None:
        in_specs.append(pl.BlockSpec(memory_space=pl.ANY))
        args.append(out_so_far)
        aliases = {len(args) - 1: 0}
        kernel_fn = lambda *refs: _final_kernel(*refs[:7], refs[8])
    return pl.pallas_call(
        kernel_fn,
        grid=(yk.shape[1] // tm,),
        in_specs=in_specs,
        out_specs=rows(d),
        out_shape=jax.ShapeDtypeStruct((n, d), F32),
        input_output_aliases=aliases,
        compiler_params=_cparams("parallel"),
        name="final",
    )(*args)


def _layer(h, p, g_mix, w_in, b_f, g_qa, g_ka, g_qb, g_kb, w_o, g_ffn, w_router, b_router,
           w_gate_up, b_gate_up, w_down, b_down, g_ple, w_ple_gate, w_ple_proj):
    batch, seq, d = h.shape
    n = batch * seq
    assert tuple(dil for _, dil in DILATED_PATTERNS) == (1, 4, 16)
    for window, dil in DILATED_PATTERNS:
        per_class = seq // BLOCK // dil
        assert window // dil == BLOCK and seq % (dil * BLOCK) == 0
        assert per_class % UNITS_PER_STEP == 0 or UNITS_PER_STEP % per_class == 0
    assert n % ROW_TILE == 0 and d % (2 * LANES) == 0 and seq % FOX_TILE == 0
    x2 = h.reshape(n, d)

    qkv_cols = 3 * WIDTH_A + 3 * WIDTH_B
    w_qkv = w_in[:, :qkv_cols].astype(BF16)
    w_f = jnp.pad(w_in[:, qkv_cols:], ((0, 0), (0, LANES - N_HEADS_B))).astype(BF16)
    b_fp = jnp.pad(b_f.astype(F32), (0, LANES - N_HEADS_B)).reshape(1, LANES)
    scale = HEAD_DIM ** -0.5
    gains = jnp.stack([jnp.tile(g_qa, N_HEADS_A) * (scale * LOG2E), jnp.tile(g_ka, N_HEADS_A),
                       jnp.tile(g_qb, N_HEADS_B) * (scale * LOG2E), jnp.tile(g_kb, N_HEADS_B)]).astype(F32)
    hid = jnp.arange(2 * LANES) // HEAD_DIM
    bd = (hid[:, None] == hid[None, :]).astype(BF16)

    z, logf = _in_proj(x2, g_mix.reshape(1, d), w_qkv, w_f, b_fp, gains, bd)
    ccol = _cumsum(logf, batch, seq)

    slopes = 2.0 ** (-8.0 * jnp.arange(1, N_HEADS_A + 1, dtype=F32) / N_HEADS_A)
    mix_a = _dilated(z, slopes, batch, seq)
    mix_b = _fox(z, ccol, batch, seq)

    w_r = jnp.pad(w_router.astype(F32), ((0, 0), (0, LANES - N_EXPERTS)))
    w_r_hi = w_r.astype(BF16)
    w_r = jnp.concatenate([w_r_hi, (w_r - w_r_hi.astype(F32)).astype(BF16)], axis=1)
    b_r = jnp.concatenate([b_router.astype(F32), jnp.full((LANES - N_EXPERTS,), NEG_INF, F32)]).reshape(1, LANES)
    h1, u_packed, top_idx, gates, rank, counts = _post_attn(
        mix_a, mix_b, x2, w_o.astype(BF16), g_ffn.reshape(1, d), w_r, b_r)

    counts = counts[0, :N_EXPERTS].astype(jnp.int32)
    tiles_per = (counts + GMM_TILE - 1) // GMM_TILE
    tile_end = jnp.cumsum(tiles_per)
    starts = (tile_end - tiles_per) * GMM_TILE
    n_tiles = n * TOP_K // GMM_TILE + N_EXPERTS
    tile_ids = jnp.arange(n_tiles, dtype=jnp.int32)
    tile_used = (tile_ids < tile_end[-1]).astype(jnp.int32)
    last_used = jnp.minimum(tile_ids, tile_end[-1] - 1)
    tile_expert = jnp.sum((last_used[:, None] >= tile_end[None, :]).astype(jnp.int32), axis=1)
    tile_expert = jnp.minimum(tile_expert, N_EXPERTS - 1)
    experts = jnp.arange(N_EXPERTS, dtype=jnp.int32)[:, None, None]
    pos_t = rank[:TOP_K] + jnp.sum(jnp.where(top_idx[None, :TOP_K] == experts, starts[:, None, None], 0), axis=0)
    filler = jnp.arange(GMM_TILE, dtype=jnp.int32)[None, :]
    need = (tiles_per * GMM_TILE - counts)[:, None]
    spare_before = (jnp.cumsum(GMM_TILE - need[:, 0]) - (GMM_TILE - need[:, 0]))[:, None]
    filler_key = jnp.where(filler < need, (starts + counts)[:, None] + filler,
                           tile_end[-1] * GMM_TILE + spare_before + filler - need)
    filler_tok = (jnp.arange(N_EXPERTS * GMM_TILE, dtype=jnp.int32) % n)
    keys = jnp.concatenate([pos_t.reshape(-1), filler_key.reshape(-1).astype(jnp.int32)])
    toks = jnp.concatenate([jnp.tile(jnp.arange(n, dtype=jnp.int32), TOP_K), filler_tok])
    src = lax.sort((keys, toks), num_keys=1)[1]
    xs = _sc_gather(u_packed, src)

    de = w_down.shape[1]
    wg_t, wl_t = _wprep(w_gate_up)
    ys = _gmm(tile_expert, tile_used, xs, wg_t, wl_t,
              b_gate_up[:, 0::2].reshape(N_EXPERTS, 1, de).astype(F32),
              b_gate_up[:, 1::2].reshape(N_EXPERTS, 1, de).astype(F32),
              w_down, b_down.reshape(N_EXPERTS, 1, d).astype(F32))
    assert n % (2 * ROW_TILE) == 0
    half_n = n // 2
    out = None
    for part in range(2):
        part_pos = pos_t[:, part * half_n:(part + 1) * half_n].reshape(-1)
        yk = _sc_gather(ys, part_pos).reshape(TOP_K, half_n, d // 2)
        out = _final(h1, yk, gates, p.reshape(n, -1), g_ple.reshape(1, d), w_ple_gate.astype(BF16),
                     w_ple_proj.astype(BF16), part * (half_n // ROW_TILE), out)
    return out.reshape(batch, seq, d)


def kernel(x, p, g_mix, w_in, b_f, g_qa, g_ka, g_qb, g_kb, w_o, g_ffn, w_router, b_router,
           w_gate_up, b_gate_up, w_down, b_down, g_ple, w_ple_gate, w_ple_proj):
    h = x
    for i in range(g_mix.shape[0]):
        h = _layer(h, p[i], g_mix[i], w_in[i], b_f[i], g_qa[i], g_ka[i], g_qb[i], g_kb[i], w_o[i],
                   g_ffn[i], w_router[i], b_router[i], w_gate_up[i], b_gate_up[i], w_down[i],
                   b_down[i], g_ple[i], w_ple_gate[i], w_ple_proj[i])
    return h
```

```python
import dataclasses
import functools

import jax
import jax.numpy as jnp
from jax import lax
from jax.experimental import pallas as pl
from jax.experimental.pallas import tpu as pltpu
from jax.experimental.pallas import tpu_sc as plsc

HEAD_DIM = 64
N_HEADS_A = 8
N_HEADS_B = 8
WIDTH_A = N_HEADS_A * HEAD_DIM
WIDTH_B = N_HEADS_B * HEAD_DIM
DILATED_PATTERNS = ((128, 1), (512, 4), (2048, 16))
BLOCK = 128
N_EXPERTS = 32
TOP_K = 4
SWIGLU_LIMIT = 7.0
SWIGLU_ALPHA = 1.702
NORM_EPS = 1e-6

LANES = 128
PAIR = LANES // HEAD_DIM
ROW_TILE = 512
GMM_TILE = 512
FOX_TILE = 512
UNITS_PER_STEP = 8
SC_CHUNK = 32
SC_DEPTH = 4
SC_SCAN_CHUNK = 16384
SC_SCAN_UNROLL = 8
VMEM_LIMIT = 56 * 1024 * 1024

F32 = jnp.float32
BF16 = jnp.bfloat16
NEG_INF = float("-inf")
NT_DIMS = (((1,), (1,)), ((), ()))
LOG2E = 1.4426950408889634


def _cparams(*sem):
    return pltpu.CompilerParams(dimension_semantics=sem, vmem_limit_bytes=VMEM_LIMIT)


def _rms(x, g):
    return x * lax.rsqrt(jnp.mean(x * x, axis=-1, keepdims=True) + NORM_EPS) * g


def _lane_iota():
    return lax.broadcasted_iota(jnp.int32, (1, LANES), 1)


def _head_lane_mask(h):
    lane = _lane_iota()
    return (lane >= h * HEAD_DIM) & (lane < (h + 1) * HEAD_DIM)


def _merge_heads(acc0, acc1):
    first = _head_lane_mask(0)
    num = jnp.where(first, acc0, acc1)
    den = pltpu.roll(jnp.where(first, acc1, acc0), HEAD_DIM, axis=1)
    return num, den


def _in_proj_kernel(x_ref, g_ref, w_ref, wf_ref, bf_ref, gain_ref, bd_ref, z_ref, lf_ref):
    u = _rms(x_ref[...], g_ref[...]).astype(BF16)
    chunk = WIDTH_A
    normed = {0: 0, 1: 1, 3: 2, 4: 3}
    for c in range(6):
        acc = jnp.dot(u, w_ref[:, c * chunk:(c + 1) * chunk], preferred_element_type=F32)
        if c in normed:
            sq = (acc * acc).astype(BF16)
            half = chunk // 2
            ss = jnp.concatenate(
                [jnp.dot(sq[:, j * half:(j + 1) * half], bd_ref[...], preferred_element_type=F32)
                 for j in range(2)], axis=1)
            r = normed[c]
            acc = acc * lax.rsqrt(ss * (1.0 / HEAD_DIM) + NORM_EPS) * gain_ref[r:r + 1, :]
        z_ref[:, c * chunk:(c + 1) * chunk] = acc.astype(BF16)
    zf = jnp.dot(u, wf_ref[...], preferred_element_type=F32) + bf_ref[...]
    lf_ref[...] = jax.nn.log_sigmoid(zf)


def _in_proj(x2, g_mix, w_qkv, w_f, b_f, gains, bd):
    n, d = x2.shape
    cols = w_qkv.shape[1]
    tm = ROW_TILE
    const = lambda i: (0, 0)
    return pl.pallas_call(
        _in_proj_kernel,
        grid=(n // tm,),
        in_specs=[
            pl.BlockSpec((tm, d), lambda i: (i, 0)),
            pl.BlockSpec((1, d), const),
            pl.BlockSpec((d, cols), const),
            pl.BlockSpec((d, LANES), const),
            pl.BlockSpec((1, LANES), const),
            pl.BlockSpec(gains.shape, const),
            pl.BlockSpec(bd.shape, const),
        ],
        out_specs=[
            pl.BlockSpec((tm, cols), lambda i: (i, 0)),
            pl.BlockSpec((tm, LANES), lambda i: (i, 0)),
        ],
        out_shape=[
            jax.ShapeDtypeStruct((n, cols), BF16),
            jax.ShapeDtypeStruct((n, LANES), F32),
        ],
        compiler_params=_cparams("parallel"),
        name="in_proj",
    )(x2, g_mix, w_qkv, w_f, b_f, gains, bd)


def _cumsum_kernel(lf_ref, tri_ref, cpk_ref):
    s = lf_ref.shape[0]
    lane = _lane_iota()
    carry = jnp.zeros((1, LANES), F32)
    for blk in range(s // BLOCK):
        rows = slice(blk * BLOCK, (blk + 1) * BLOCK)
        part = jnp.dot(tri_ref[...], lf_ref[rows, :], precision=lax.Precision.HIGHEST,
                       preferred_element_type=F32) + carry
        carry = part[BLOCK - 1:BLOCK, :]
        c = part * LOG2E
        hi = c.astype(BF16).astype(F32)
        r1 = c - hi
        mid = r1.astype(BF16).astype(F32)
        lo = r1 - mid
        packed = jnp.where(lane < N_HEADS_B, hi,
                 jnp.where(lane < 2 * N_HEADS_B, pltpu.roll(mid, N_HEADS_B, axis=1),
                 jnp.where(lane < 3 * N_HEADS_B, pltpu.roll(lo, 2 * N_HEADS_B, axis=1),
                 jnp.where(lane == 3 * N_HEADS_B, 1.0, 0.0))))
        cpk_ref[rows, :] = packed.astype(BF16)


def _cumsum(logf, batch, seq):
    tri = (lax.broadcasted_iota(jnp.int32, (BLOCK, BLOCK), 0)
           >= lax.broadcasted_iota(jnp.int32, (BLOCK, BLOCK), 1)).astype(F32)
    return pl.pallas_call(
        _cumsum_kernel,
        grid=(batch,),
        in_specs=[
            pl.BlockSpec((seq, LANES), lambda b: (b, 0)),
            pl.BlockSpec((BLOCK, BLOCK), lambda b: (0, 0)),
        ],
        out_specs=pl.BlockSpec((seq, LANES), lambda b: (b, 0)),
        out_shape=jax.ShapeDtypeStruct((batch * seq, LANES), BF16),
        compiler_params=_cparams("parallel"),
        name="cumsum",
    )(logf, tri)


def _fox_features(cpk, pair, key_side):
    assert PAIR == 2
    r = lax.broadcasted_iota(jnp.int32, (LANES, PAIR * LANES), 0)
    c = lax.broadcasted_iota(jnp.int32, (LANES, PAIR * LANES), 1)
    hh = jnp.where(c >= LANES, 1, 0)
    slot = c - hh * LANES - HEAD_DIM * (1 - hh)
    head = PAIR * pair + hh
    piece_slot = slot - 3 if key_side else slot
    ones_slot = slot if key_side else slot - 3
    piece = (piece_slot >= 0) & (piece_slot < 3) & (r == N_HEADS_B * piece_slot + head)
    ones = (ones_slot >= 0) & (ones_slot < 3) & (r == 3 * N_HEADS_B)
    place = jnp.where(piece, -1.0 if key_side else 1.0, jnp.where(ones, 1.0, 0.0)).astype(BF16)
    return jnp.dot(cpk, place, preferred_element_type=F32).astype(BF16)


def _fox_kernel(q_ref, k_ref, v_ref, c_ref, o_ref, kf, vf, s_scr, m_scr, acc_scr, *, tile):
    pair = pl.program_id(1)
    i = pl.program_id(2)
    half = tile // 2
    in_head = [_head_lane_mask(h) for h in range(PAIR)]
    block = lambda feat, h: feat[:, h * LANES:(h + 1) * LANES]

    @pl.when(i == 0)
    def _():
        feat = _fox_features(c_ref[...], pair, True)
        for h in range(PAIR):
            kf[h] = jnp.where(in_head[h], k_ref[...], block(feat, h))
            vf[h] = jnp.where(in_head[h], v_ref[...], jnp.ones_like(v_ref[...]))

    row0 = pl.multiple_of(i * tile, tile)
    feat_q = _fox_features(c_ref[pl.ds(row0, tile), :], pair, False)
    q = q_ref[...]
    qf = [jnp.where(in_head[h], q, block(feat_q, h)) for h in range(PAIR)]

    def lane_groups_max(s):
        m = s[:, :LANES]
        for g in range(1, s.shape[1] // LANES):
            m = jnp.maximum(m, s[:, g * LANES:(g + 1) * LANES])
        return m

    def scores(qrows, off, width, h):
        return lax.dot_general(qrows, kf[h, pl.ds(off, width), :], NT_DIMS, preferred_element_type=F32)

    up_r = lax.broadcasted_iota(jnp.int32, (half, half), 0)
    up_c = lax.broadcasted_iota(jnp.int32, (half, half), 1)
    lo_r = lax.broadcasted_iota(jnp.int32, (half, tile), 0)
    lo_c = lax.broadcasted_iota(jnp.int32, (half, tile), 1)
    for h in range(PAIR):
        s_up = jnp.where(up_c <= up_r, scores(qf[h][:half], row0, half, h), NEG_INF)
        s_lo = jnp.where(lo_c <= lo_r + half, scores(qf[h][half:], row0, tile, h), NEG_INF)
        s_scr[h, :half, pl.ds(row0, half)] = s_up
        s_scr[h, half:, pl.ds(row0, tile)] = s_lo
        m_scr[h, :half] = lane_groups_max(s_up)
        m_scr[h, half:] = lane_groups_max(s_lo)

    def pass1(j, _):
        off = pl.multiple_of(j * tile, tile)
        for h in range(PAIR):
            s = scores(qf[h], off, tile, h)
            s_scr[h, :, pl.ds(off, tile)] = s
            m_scr[h] = jnp.maximum(m_scr[h], lane_groups_max(s))
        return 0
    lax.fori_loop(0, i, pass1, 0)

    m = [jnp.max(m_scr[h], axis=-1, keepdims=True) for h in range(PAIR)]
    for h in range(PAIR):
        p_up = jnp.exp2(s_scr[h, :half, pl.ds(row0, half)] - m[h][:half]).astype(BF16)
        p_lo = jnp.exp2(s_scr[h, half:, pl.ds(row0, tile)] - m[h][half:]).astype(BF16)
        acc_scr[h, :half] = jnp.dot(p_up, vf[h, pl.ds(row0, half), :], preferred_element_type=F32)
        acc_scr[h, half:] = jnp.dot(p_lo, vf[h, pl.ds(row0, tile), :], preferred_element_type=F32)

    def pass2(j, _):
        off = pl.multiple_of(j * tile, tile)
        for h in range(PAIR):
            p = jnp.exp2(s_scr[h, :, pl.ds(off, tile)] - m[h]).astype(BF16)
            acc_scr[h] += jnp.dot(p, vf[h, pl.ds(off, tile), :], preferred_element_type=F32)
        return 0
    lax.fori_loop(0, i, pass2, 0)

    num, den = _merge_heads(acc_scr[0], acc_scr[1])
    o_ref[...] = (num / den).astype(o_ref.dtype)


def _fox(z, ccol, batch, seq):
    n = z.shape[0]
    tile = FOX_TILE
    nq = seq // tile
    npair = N_HEADS_B // PAIR
    base = 3 * WIDTH_A // LANES
    qcol, kcol, vcol = base, base + WIDTH_B // LANES, base + 2 * WIDTH_B // LANES
    return pl.pallas_call(
        functools.partial(_fox_kernel, tile=tile),
        grid=(batch, npair, nq),
        in_specs=[
            pl.BlockSpec((tile, LANES), lambda b, p, i: (b * nq + i, qcol + p)),
            pl.BlockSpec((seq, LANES), lambda b, p, i: (b, kcol + p)),
            pl.BlockSpec((seq, LANES), lambda b, p, i: (b, vcol + p)),
            pl.BlockSpec((seq, LANES), lambda b, p, i: (b, 0)),
        ],
        out_specs=pl.BlockSpec((tile, LANES), lambda b, p, i: (b * nq + i, p)),
        out_shape=jax.ShapeDtypeStruct((n, WIDTH_B), BF16),
        scratch_shapes=[
            pltpu.VMEM((PAIR, seq, LANES), BF16),
            pltpu.VMEM((PAIR, seq, LANES), BF16),
            pltpu.VMEM((PAIR, tile, seq), F32),
            pltpu.VMEM((PAIR, tile, LANES), F32),
            pltpu.VMEM((PAIR, tile, LANES), F32),
        ],
        compiler_params=_cparams("parallel", "parallel", "arbitrary"),
        name="fox",
    )(z, z, z, ccol)


def _dilated_kernel(slope_ref, q_ref, k_ref, v_ref, o_ref,
                    natf, p4f, p4b, p16b, qfeat, kfeat, vals, dens, maxs, *, seq):
    pair = pl.program_id(1)
    lane = _lane_iota()
    first = _head_lane_mask(0)
    quarter = seq // 4
    units = seq // BLOCK

    def deinterleave(src, t, span_start, span):
        return [src[t, pl.ds(span_start + r, span // 4, stride=4), :] for r in range(4)]

    for t, ref in enumerate((q_ref, k_ref, v_ref)):
        natf[t] = ref[...].astype(F32)
        for r, part in enumerate(deinterleave(natf, t, 0, seq)):
            p4f[t, pl.ds(r * quarter, quarter), :] = part
            p4b[t, pl.ds(r * quarter, quarter), :] = part.astype(BF16)
        for r4 in range(4):
            for r, part in enumerate(deinterleave(p4f, t, r4 * quarter, quarter)):
                p16b[t, pl.ds(r4 * quarter + r * (quarter // 4), quarter // 4), :] = part.astype(BF16)

    qi = lax.broadcasted_iota(jnp.int32, (BLOCK, LANES), 0).astype(F32)
    kj = lax.broadcasted_iota(jnp.int32, (2 * BLOCK, LANES), 0).astype(F32)
    for p, (_, dil) in enumerate(DILATED_PATTERNS):
        for h in range(PAIR):
            sd = slope_ref[PAIR * pair + h] * float(dil)
            base = HEAD_DIM * (1 - h)
            def pieces_and_ones(value, first_piece_lane, first_one_lane):
                hi = value.astype(BF16).astype(F32)
                mid = (value - hi).astype(BF16).astype(F32)
                lo = value - hi - mid
                return jnp.where(lane == first_piece_lane, hi,
                       jnp.where(lane == first_piece_lane + 1, mid,
                       jnp.where(lane == first_piece_lane + 2, lo,
                       jnp.where((lane >= first_one_lane) & (lane < first_one_lane + 3), 1.0, 0.0)))).astype(BF16)
            qfeat[p * PAIR + h] = pieces_and_ones(-(qi + float(BLOCK)) * (sd * LOG2E), base, base + 3)
            kfeat[p * PAIR + h] = pieces_and_ones(kj * (sd * LOG2E), base + 3, base)

    bq = lax.broadcasted_iota(jnp.int32, (BLOCK, 2 * BLOCK), 0)
    bk = lax.broadcasted_iota(jnp.int32, (BLOCK, 2 * BLOCK), 1)
    rel = bq + BLOCK - bk
    band = (rel >= 0) & (rel <= BLOCK)

    def unit(p, srcs, u, prev_valid):
        qs, ks, vs = srcs
        start = pl.multiple_of(u * BLOCK, BLOCK)
        prev = pl.multiple_of(jnp.maximum(start - BLOCK, 0), BLOCK)
        qb = qs[pl.ds(start, BLOCK), :]
        kk = jnp.concatenate([ks[pl.ds(prev, BLOCK), :], ks[pl.ds(start, BLOCK), :]], axis=0)
        vv = jnp.concatenate([vs[pl.ds(prev, BLOCK), :], vs[pl.ds(start, BLOCK), :]], axis=0)
        if prev_valid is True:
            ok = band
        elif prev_valid is False:
            ok = band & (bk >= BLOCK)
        else:
            ok = band & ((bk >= BLOCK) | prev_valid)
        accs, ms = [], []
        for h in range(PAIR):
            in_h = _head_lane_mask(h)
            qh = jnp.where(in_h, qb, qfeat[p * PAIR + h])
            kh = jnp.where(in_h, kk, kfeat[p * PAIR + h])
            vh = jnp.where(in_h, vv, jnp.ones_like(vv))
            s = lax.dot_general(qh, kh, NT_DIMS, preferred_element_type=F32)
            s = jnp.where(ok, s, NEG_INF)
            m = jnp.max(s, axis=-1, keepdims=True)
            pr = jnp.exp2(s - m).astype(BF16)
            accs.append(jnp.dot(pr, vh, preferred_element_type=F32))
            ms.append(m)
        num, den = _merge_heads(*accs)
        vals[p, pl.ds(start, BLOCK), :] = num
        dens[p, pl.ds(start, BLOCK), :] = den
        maxs[p, pl.ds(start, BLOCK), :] = jnp.where(first, ms[0], ms[1])

    group = UNITS_PER_STEP
    sources = ((q_ref, k_ref, v_ref), tuple(p4b.at[t] for t in range(3)), tuple(p16b.at[t] for t in range(3)))
    for p, (_, dil) in enumerate(DILATED_PATTERNS):
        per_class = units // dil

        def step(g, _, p=p, per_class=per_class):
            for e in range(group):
                u = g * group + e
                if per_class >= group:
                    prev_valid = (u % per_class != 0) if e == 0 else True
                else:
                    prev_valid = e % per_class != 0
                unit(p, sources[p], u, prev_valid)
            return 0
        lax.fori_loop(0, units // group, step, 0)

    for t, arr in enumerate((vals, dens, maxs)):
        for r4 in range(4):
            for r in range(4):
                p4f[t, pl.ds(r4 * quarter + r, quarter // 4, stride=4), :] = \
                    arr[2, pl.ds(r4 * quarter + r * (quarter // 4), quarter // 4), :]

    for r in range(4):
        grouped = pl.ds(r * quarter, quarter)
        natural = pl.ds(r, quarter, stride=4)
        ms = (maxs[0, natural, :], maxs[1, grouped, :], p4f[2, grouped, :])
        vs = (vals[0, natural, :], vals[1, grouped, :], p4f[0, grouped, :])
        ds = (dens[0, natural, :], dens[1, grouped, :], p4f[1, grouped, :])
        m_all = jnp.maximum(jnp.maximum(ms[0], ms[1]), ms[2])
        num = jnp.zeros((quarter, LANES), F32)
        den = jnp.zeros((quarter, LANES), F32)
        for p in range(3):
            e = jnp.exp2(ms[p] - m_all)
            num = num + e * vs[p]
            den = den + e * ds[p]
        natf[0, natural, :] = num / den
    o_ref[...] = natf[0].astype(o_ref.dtype)


def _dilated(z, slopes, batch, seq):
    n = z.shape[0]
    npair = N_HEADS_A // PAIR
    npat = len(DILATED_PATTERNS)
    qcol, kcol, vcol = 0, WIDTH_A // LANES, 2 * WIDTH_A // LANES
    blk = lambda c0: pl.BlockSpec((seq, LANES), lambda b, p: (b, c0 + p))
    return pl.pallas_call(
        functools.partial(_dilated_kernel, seq=seq),
        grid=(batch, npair),
        in_specs=[pl.BlockSpec(memory_space=pltpu.SMEM), blk(qcol), blk(kcol), blk(vcol)],
        out_specs=pl.BlockSpec((seq, LANES), lambda b, p: (b, p)),
        out_shape=jax.ShapeDtypeStruct((n, WIDTH_A), BF16),
        scratch_shapes=[
            pltpu.VMEM((3, seq, LANES), F32),
            pltpu.VMEM((3, seq, LANES), F32),
            pltpu.VMEM((3, seq, LANES), BF16),
            pltpu.VMEM((3, seq, LANES), BF16),
            pltpu.VMEM((npat * PAIR, BLOCK, LANES), BF16),
            pltpu.VMEM((npat * PAIR, 2 * BLOCK, LANES), BF16),
            pltpu.VMEM((npat, seq, LANES), F32),
            pltpu.VMEM((npat, seq, LANES), F32),
            pltpu.VMEM((npat, seq, LANES), F32),
        ],
        compiler_params=_cparams("parallel", "parallel"),
        name="dilated",
    )(slopes, z, z, z)


def _pack_bf16_pairs(a, b):
    hi = pltpu.bitcast(a.astype(BF16).astype(F32), jnp.int32)
    lo = pltpu.bitcast(b.astype(BF16).astype(F32), jnp.int32)
    return (hi & jnp.int32(-65536)) | lax.shift_right_logical(lo, jnp.int32(16))


def _unpack_bf16_pairs(w):
    a = pltpu.bitcast(w & jnp.int32(-65536), F32)
    b = pltpu.bitcast(lax.shift_left(w, jnp.int32(16)), F32)
    return a, b


def _post_attn_kernel(ma_ref, mb_ref, x_ref, wo_ref, g_ref, wr_ref, br_ref, tri_ref,
                      h_ref, up_ref, idx_ref, gate_ref, rank_ref, cnt_ref, carry):
    @pl.when(pl.program_id(0) == 0)
    def _():
        carry[...] = jnp.zeros_like(carry)

    y = jnp.dot(ma_ref[...], wo_ref[:WIDTH_A, :], preferred_element_type=F32)
    y = y + jnp.dot(mb_ref[...], wo_ref[WIDTH_A:, :], preferred_element_type=F32)
    h = x_ref[...] + y
    h_ref[...] = h
    u = _rms(h, g_ref[...])
    half = u.shape[1] // 2
    up_ref[...] = _pack_bf16_pairs(u[:, :half], u[:, half:])

    u_hi = u.astype(BF16)
    u_lo = (u - u_hi.astype(F32)).astype(BF16)
    hi_terms = jnp.dot(u_hi, wr_ref[...], preferred_element_type=F32)
    logits = (hi_terms[:, :LANES] + hi_terms[:, LANES:]
              + jnp.dot(u_lo, wr_ref[:, :LANES], preferred_element_type=F32)) + br_ref[...]
    lane = lax.broadcasted_iota(jnp.int32, logits.shape, 1).astype(F32)
    work = logits
    idxs, tops = [], []
    for _ in range(TOP_K):
        top = jnp.max(work, axis=-1, keepdims=True)
        idx = jnp.min(jnp.where(work == top, lane, float(LANES)), axis=-1, keepdims=True)
        work = jnp.where(lane == idx, NEG_INF, work)
        idxs.append(idx)
        tops.append(top)
    exps = [jnp.exp(t - tops[0]) for t in tops]
    total = exps[0] + exps[1] + exps[2] + exps[3]

    onehot = jnp.zeros(logits.shape, F32)
    for idx in idxs:
        onehot = onehot + (lane == idx).astype(F32)
    before = jnp.dot(tri_ref[...], onehot.astype(BF16), preferred_element_type=F32) + carry[...]
    carry[...] = carry[...] + jnp.sum(onehot, axis=0, keepdims=True)
    cnt_ref[...] = carry[...]

    idx_out = jnp.zeros(logits.shape, F32)
    gate_out = jnp.zeros(logits.shape, F32)
    rank_out = jnp.zeros(logits.shape, F32)
    for k in range(TOP_K):
        rank_k = jnp.sum(jnp.where(lane == idxs[k], before, 0.0), axis=-1, keepdims=True)
        idx_out = jnp.where(lane == float(k), idxs[k], idx_out)
        gate_out = jnp.where(lane == float(k), exps[k] / total, gate_out)
        rank_out = jnp.where(lane == float(k), rank_k, rank_out)
    idx_ref[...] = idx_out.T[:8, :].astype(jnp.int32)
    gate_ref[...] = gate_out
    rank_ref[...] = rank_out.T[:8, :].astype(jnp.int32)


def _post_attn(mix_a, mix_b, x2, w_o, g_ffn, w_r, b_r):
    n, d = x2.shape
    tm = ROW_TILE
    tri = (lax.broadcasted_iota(jnp.int32, (tm, tm), 0)
           > lax.broadcasted_iota(jnp.int32, (tm, tm), 1)).astype(BF16)
    const = lambda i: (0, 0)
    row = lambda w: pl.BlockSpec((tm, w), lambda i: (i, 0))
    lanes_t = pl.BlockSpec((8, tm), lambda i: (0, i))
    return pl.pallas_call(
        _post_attn_kernel,
        grid=(n // tm,),
        in_specs=[
            row(WIDTH_A), row(WIDTH_B), row(d),
            pl.BlockSpec(w_o.shape, const),
            pl.BlockSpec((1, d), const),
            pl.BlockSpec(w_r.shape, const),
            pl.BlockSpec((1, LANES), const),
            pl.BlockSpec((tm, tm), const),
        ],
        out_specs=[row(d), row(d // 2), lanes_t, row(LANES), lanes_t,
                   pl.BlockSpec((1, LANES), const)],
        out_shape=[
            jax.ShapeDtypeStruct((n, d), F32),
            jax.ShapeDtypeStruct((n, d // 2), jnp.int32),
            jax.ShapeDtypeStruct((8, n), jnp.int32),
            jax.ShapeDtypeStruct((n, LANES), F32),
            jax.ShapeDtypeStruct((8, n), jnp.int32),
            jax.ShapeDtypeStruct((1, LANES), F32),
        ],
        scratch_shapes=[pltpu.VMEM((1, LANES), F32)],
        compiler_params=_cparams("arbitrary"),
        name="post_attn",
    )(mix_a, mix_b, x2, w_o, g_ffn, w_r, b_r, tri)


def _wprep_kernel(w_ref, wg_ref, wl_ref, wt):
    d, cols = w_ref.shape[1:]
    de = cols // 2
    for j in range(d // LANES):
        lanes = slice(j * LANES, (j + 1) * LANES)
        wt[j] = w_ref[0, lanes, :].T
        wg_ref[0, :, lanes] = wt[j, pl.ds(0, de, stride=2), :].astype(BF16)
        wl_ref[0, :, lanes] = wt[j, pl.ds(1, de, stride=2), :].astype(BF16)


def _wprep(w_gate_up):
    ne, d, cols = w_gate_up.shape
    de = cols // 2
    out = pl.BlockSpec((1, de, d), lambda e: (e, 0, 0))
    return pl.pallas_call(
        _wprep_kernel,
        grid=(ne,),
        in_specs=[pl.BlockSpec((1, d, cols), lambda e: (e, 0, 0))],
        out_specs=[out, out],
        out_shape=[jax.ShapeDtypeStruct((ne, de, d), BF16)] * 2,
        scratch_shapes=[pltpu.VMEM((d // LANES, cols, LANES), F32)],
        compiler_params=_cparams("parallel"),
        name="wprep",
    )(w_gate_up)


def _sc_invert(keys, n_tokens, fill):
    info = plsc.get_sparse_core_info()
    workers = info.num_cores * info.num_subcores
    lanes = info.num_lanes
    total, = fill.shape
    count, = keys.shape
    chunk, unroll = SC_SCAN_CHUNK, SC_SCAN_UNROLL
    assert total % (workers * lanes) == 0 and count % (2 * chunk) == 0 and n_tokens % chunk == 0
    own = total // workers
    nchunks = count // chunk
    chunks_per_pass = n_tokens // chunk
    mesh = plsc.VectorSubcoreMesh(core_axis_name="c", subcore_axis_name="s")
    params = pltpu.CompilerParams()
    if "needs_layout_passes" in pltpu.CompilerParams.__dataclass_fields__:
        params = dataclasses.replace(params, needs_layout_passes=False)

    @functools.partial(
        pl.kernel, mesh=mesh, compiler_params=params,
        out_type=jax.ShapeDtypeStruct((total,), jnp.int32),
        scratch_types=[
            pltpu.VMEM((own,), jnp.int32),
            pltpu.VMEM((2, chunk), jnp.int32),
            pltpu.SemaphoreType.DMA((2,)),
        ],
    )
    def invert_kernel(keys_hbm, fill_hbm, out_hbm, own_v, key_v, sem):
        wid = lax.axis_index("s") * info.num_cores + lax.axis_index("c")
        lo = wid * own
        lane_ids = lax.broadcasted_iota(jnp.int32, (lanes,), 0)

        def fetch(c, b):
            off = pl.multiple_of(c * chunk, chunk)
            return pltpu.make_async_copy(keys_hbm.at[pl.ds(off, chunk)], key_v.at[b], sem.at[b])

        fetch(0, 0).start()
        pltpu.sync_copy(fill_hbm.at[pl.ds(lo, own)], own_v)

        @pl.loop(0, nchunks, step=2)
        def _(c0):
            for b in range(2):
                c = c0 + b

                @pl.when(c + 1 < nchunks)
                def _():
                    fetch(c + 1, 1 - b).start()

                fetch(c, b).wait()
                first_token = (c % chunks_per_pass) * chunk

                @pl.loop(0, chunk // lanes, step=unroll)
                def _(j0):
                    for u in range(unroll):
                        start = pl.multiple_of((j0 + u) * lanes, lanes)
                        row = key_v[b, pl.ds(start, lanes)] - lo
                        mine = (row >= 0) & (row < own)
                        plsc.store_scatter(own_v, [jnp.where(mine, row, 0)], first_token + start + lane_ids,
                                           mask=mine)

        pltpu.sync_copy(own_v, out_hbm.at[pl.ds(lo, own)])

    return invert_kernel(keys, fill)


def _sc_gather(table, idx):
    info = plsc.get_sparse_core_info()
    workers = info.num_cores * info.num_subcores
    rows, width = idx.shape[0], table.shape[1]
    chunk, depth = SC_CHUNK, SC_DEPTH
    assert rows % (workers * chunk * depth) == 0
    per_worker = rows // workers
    nchunks = per_worker // chunk
    mesh = plsc.VectorSubcoreMesh(core_axis_name="c", subcore_axis_name="s")

    @functools.partial(
        pl.kernel, mesh=mesh,
        out_type=jax.ShapeDtypeStruct((rows, width), table.dtype),
        scratch_types=[
            pltpu.VMEM((nchunks, chunk), jnp.int32),
            pltpu.VMEM((depth, chunk, width), table.dtype),
            pltpu.SemaphoreType.DMA((depth,)),
            pltpu.SemaphoreType.DMA((depth,)),
        ],
    )
    def gather_kernel(table_hbm, idx_hbm, out_hbm, idx_v, rows_v, gsem, wsem):
        wid = lax.axis_index("s") * info.num_cores + lax.axis_index("c")
        base = wid * per_worker
        pltpu.sync_copy(idx_hbm.at[wid], idx_v)

        def gather(c, b):
            return pltpu.make_async_copy(table_hbm.at[idx_v.at[c]], rows_v.at[b], gsem.at[b])

        def write(c, b):
            off = pl.multiple_of(base + c * chunk, chunk)
            return pltpu.make_async_copy(rows_v.at[b], out_hbm.at[pl.ds(off, chunk)], wsem.at[b])

        @pl.loop(0, nchunks, step=depth)
        def _(c0):
            for b in range(depth):
                gather(c0 + b, b).start()
            for b in range(depth):
                gather(c0 + b, b).wait()
                write(c0 + b, b).start()
            for b in range(depth):
                write(c0 + b, b).wait()

    return gather_kernel(table, idx.reshape(workers, nchunks, chunk))


def _gmm_kernel(te_ref, used_ref, xs_ref, wg_ref, wl_ref, bg_ref, bl_ref, wd_ref, bd_ref, ys_ref):
    i = pl.program_id(0)

    @pl.when(used_ref[i] > 0)
    def _():
        a, b = _unpack_bf16_pairs(xs_ref[...])
        x = jnp.concatenate([a, b], axis=1).astype(BF16)
        hg = lax.dot_general(x, wg_ref[0], NT_DIMS, preferred_element_type=F32) + bg_ref[0]
        hl = lax.dot_general(x, wl_ref[0], NT_DIMS, preferred_element_type=F32) + bl_ref[0]
        xg = jnp.minimum(hg, SWIGLU_LIMIT)
        xl = jnp.clip(hl, -SWIGLU_LIMIT, SWIGLU_LIMIT)
        act = xg * jax.nn.sigmoid(SWIGLU_ALPHA * xg) * (xl + 1.0)
        out = jnp.dot(act.astype(BF16), wd_ref[0].astype(BF16), preferred_element_type=F32) + bd_ref[0]
        half = out.shape[1] // 2
        ys_ref[...] = _pack_bf16_pairs(out[:, :half], out[:, half:])

    @pl.when(used_ref[i] == 0)
    def _():
        ys_ref[...] = jnp.zeros_like(ys_ref)


def _gmm(tile_expert, tile_used, xs, wg_t, wl_t, b_glu, b_lin, w_down, b_down):
    rows, half = xs.shape
    d = 2 * half
    de = wg_t.shape[1]
    tm = GMM_TILE
    wspec = lambda shape: pl.BlockSpec((1,) + shape, lambda i, te, used: (te[i], 0, 0))
    grid_spec = pltpu.PrefetchScalarGridSpec(
        num_scalar_prefetch=2,
        grid=(rows // tm,),
        in_specs=[
            pl.BlockSpec((tm, half), lambda i, te, used: (i, 0)),
            wspec((de, d)), wspec((de, d)), wspec((1, de)), wspec((1, de)),
            wspec((de, d)), wspec((1, d)),
        ],
        out_specs=pl.BlockSpec((tm, half), lambda i, te, used: (i, 0)),
    )
    return pl.pallas_call(
        _gmm_kernel,
        grid_spec=grid_spec,
        out_shape=jax.ShapeDtypeStruct((rows, half), jnp.int32),
        compiler_params=_cparams("arbitrary"),
        name="gmm",
    )(tile_expert, tile_used, xs, wg_t, wl_t, b_glu, b_lin, w_down, b_down)


def _final_kernel(h_ref, yk_ref, gate_ref, p_ref, g_ref, wg_ref, wp_ref, o_ref):
    gates = gate_ref[...]
    h = h_ref[...]
    for k in range(TOP_K):
        h = h + gates[:, k:k + 1] * jnp.concatenate(_unpack_bf16_pairs(yk_ref[k]), axis=1)
    u = _rms(h, g_ref[...]).astype(BF16)
    gate = jax.nn.sigmoid(jnp.dot(u, wg_ref[...], preferred_element_type=F32))
    proj = jnp.dot(p_ref[...].astype(BF16), wp_ref[...], preferred_element_type=F32)
    o_ref[...] = h + gate * proj


def _final(h1, yk, gates, p2, g_ple, w_gate, w_proj, first_tile, out_so_far=None):
    n, d = h1.shape
    tm = ROW_TILE
    const = lambda i: (0, 0)
    rows = lambda width: pl.BlockSpec((tm, width), lambda i: (i + first_tile, 0))
    in_specs = [
        rows(d),
        pl.BlockSpec((TOP_K, tm, d // 2), lambda i: (0, i, 0)),
        rows(LANES),
        rows(p2.shape[1]),
        pl.BlockSpec((1, d), const),
        pl.BlockSpec(w_gate.shape, const),
        pl.BlockSpec(w_proj.shape, const),
    ]
    args = [h1, yk, gates, p2, g_ple, w_gate, w_proj]
    kernel_fn, aliases = _final_kernel, {}
    if out_so_far is not None:
        in_specs.append(pl.BlockSpec(memory_space=pl.ANY))
        args.append(out_so_far)
        aliases = {len(args) - 1: 0}
        kernel_fn = lambda *refs: _final_kernel(*refs[:7], refs[8])
    return pl.pallas_call(
        kernel_fn,
        grid=(yk.shape[1] // tm,),
        in_specs=in_specs,
        out_specs=rows(d),
        out_shape=jax.ShapeDtypeStruct((n, d), F32),
        input_output_aliases=aliases,
        compiler_params=_cparams("parallel"),
        name="final",
    )(*args)


def _layer(h, p, g_mix, w_in, b_f, g_qa, g_ka, g_qb, g_kb, w_o, g_ffn, w_router, b_router,
           w_gate_up, b_gate_up, w_down, b_down, g_ple, w_ple_gate, w_ple_proj):
    batch, seq, d = h.shape
    n = batch * seq
    assert tuple(dil for _, dil in DILATED_PATTERNS) == (1, 4, 16)
    for window, dil in DILATED_PATTERNS:
        per_class = seq // BLOCK // dil
        assert window // dil == BLOCK and seq % (dil * BLOCK) == 0
        assert per_class % UNITS_PER_STEP == 0 or UNITS_PER_STEP % per_class == 0
    assert n % ROW_TILE == 0 and d % (2 * LANES) == 0 and seq % FOX_TILE == 0
    x2 = h.reshape(n, d)

    qkv_cols = 3 * WIDTH_A + 3 * WIDTH_B
    w_qkv = w_in[:, :qkv_cols].astype(BF16)
    w_f = jnp.pad(w_in[:, qkv_cols:], ((0, 0), (0, LANES - N_HEADS_B))).astype(BF16)
    b_fp = jnp.pad(b_f.astype(F32), (0, LANES - N_HEADS_B)).reshape(1, LANES)
    scale = HEAD_DIM ** -0.5
    gains = jnp.stack([jnp.tile(g_qa, N_HEADS_A) * (scale * LOG2E), jnp.tile(g_ka, N_HEADS_A),
                       jnp.tile(g_qb, N_HEADS_B) * (scale * LOG2E), jnp.tile(g_kb, N_HEADS_B)]).astype(F32)
    hid = jnp.arange(2 * LANES) // HEAD_DIM
    bd = (hid[:, None] == hid[None, :]).astype(BF16)

    z, logf = _in_proj(x2, g_mix.reshape(1, d), w_qkv, w_f, b_fp, gains, bd)
    ccol = _cumsum(logf, batch, seq)

    slopes = 2.0 ** (-8.0 * jnp.arange(1, N_HEADS_A + 1, dtype=F32) / N_HEADS_A)
    mix_a = _dilated(z, slopes, batch, seq)
    mix_b = _fox(z, ccol, batch, seq)

    w_r = jnp.pad(w_router.astype(F32), ((0, 0), (0, LANES - N_EXPERTS)))
    w_r_hi = w_r.astype(BF16)
    w_r = jnp.concatenate([w_r_hi, (w_r - w_r_hi.astype(F32)).astype(BF16)], axis=1)
    b_r = jnp.concatenate([b_router.astype(F32), jnp.full((LANES - N_EXPERTS,), NEG_INF, F32)]).reshape(1, LANES)
    h1, u_packed, top_idx, gates, rank, counts = _post_attn(
        mix_a, mix_b, x2, w_o.astype(BF16), g_ffn.reshape(1, d), w_r, b_r)

    counts = counts[0, :N_EXPERTS].astype(jnp.int32)
    tiles_per = (counts + GMM_TILE - 1) // GMM_TILE
    tile_end = jnp.cumsum(tiles_per)
    starts = (tile_end - tiles_per) * GMM_TILE
    n_tiles = n * TOP_K // GMM_TILE + N_EXPERTS
    tile_ids = jnp.arange(n_tiles, dtype=jnp.int32)
    tile_used = (tile_ids < tile_end[-1]).astype(jnp.int32)
    last_used = jnp.minimum(tile_ids, tile_end[-1] - 1)
    tile_expert = jnp.sum((last_used[:, None] >= tile_end[None, :]).astype(jnp.int32), axis=1)
    tile_expert = jnp.minimum(tile_expert, N_EXPERTS - 1)
    experts = jnp.arange(N_EXPERTS, dtype=jnp.int32)[:, None, None]
    pos_t = rank[:TOP_K] + jnp.sum(jnp.where(top_idx[None, :TOP_K] == experts, starts[:, None, None], 0), axis=0)
    padding_tokens = jnp.arange(n_tiles * GMM_TILE, dtype=jnp.int32) % n
    src = _sc_invert(pos_t.reshape(-1), n, padding_tokens)
    xs = _sc_gather(u_packed, src)

    de = w_down.shape[1]
    wg_t, wl_t = _wprep(w_gate_up)
    ys = _gmm(tile_expert, tile_used, xs, wg_t, wl_t,
              b_gate_up[:, 0::2].reshape(N_EXPERTS, 1, de).astype(F32),
              b_gate_up[:, 1::2].reshape(N_EXPERTS, 1, de).astype(F32),
              w_down, b_down.reshape(N_EXPERTS, 1, d).astype(F32))
    assert n % (2 * ROW_TILE) == 0
    half_n = n // 2
    out = None
    for part in range(2):
        part_pos = pos_t[:, part * half_n:(part + 1) * half_n].reshape(-1)
        yk = _sc_gather(ys, part_pos).reshape(TOP_K, half_n, d // 2)
        out = _final(h1, yk, gates, p.reshape(n, -1), g_ple.reshape(1, d), w_ple_gate.astype(BF16),
                     w_ple_proj.astype(BF16), part * (half_n // ROW_TILE), out)
    return out.reshape(batch, seq, d)


def kernel(x, p, g_mix, w_in, b_f, g_qa, g_ka, g_qb, g_kb, w_o, g_ffn, w_router, b_router,
           w_gate_up, b_gate_up, w_down, b_down, g_ple, w_ple_gate, w_ple_proj):
    h = x
    for i in range(g_mix.shape[0]):
        h = _layer(h, p[i], g_mix[i], w_in[i], b_f[i], g_qa[i], g_ka[i], g_qb[i], g_kb[i], w_o[i],
                   g_ffn[i], w_router[i], b_router[i], w_gate_up[i], b_gate_up[i], w_down[i],
                   b_down[i], g_ple[i], w_ple_gate[i], w_ple_proj[i])
    return h
```

```python
import dataclasses
import functools

import jax
import jax.numpy as jnp
from jax import lax
from jax.experimental import pallas as pl
from jax.experimental.pallas import tpu as pltpu
from jax.experimental.pallas import tpu_sc as plsc

HEAD_DIM = 64
N_HEADS_A = 8
N_HEADS_B = 8
WIDTH_A = N_HEADS_A * HEAD_DIM
WIDTH_B = N_HEADS_B * HEAD_DIM
DILATED_PATTERNS = ((128, 1), (512, 4), (2048, 16))
BLOCK = 128
N_EXPERTS = 32
TOP_K = 4
SWIGLU_LIMIT = 7.0
SWIGLU_ALPHA = 1.702
NORM_EPS = 1e-6

LANES = 128
PAIR = LANES // HEAD_DIM
ROW_TILE = 512
GMM_TILE = 512
FOX_TILE = 512
UNITS_PER_STEP = 8
SC_CHUNK = 32
SC_DEPTH = 4
SC_SCAN_CHUNK = 16384
SC_SCAN_UNROLL = 8
VMEM_LIMIT = 56 * 1024 * 1024

F32 = jnp.float32
BF16 = jnp.bfloat16
NEG_INF = float("-inf")
NT_DIMS = (((1,), (1,)), ((), ()))
LOG2E = 1.4426950408889634


def _cparams(*sem):
    return pltpu.CompilerParams(dimension_semantics=sem, vmem_limit_bytes=VMEM_LIMIT)


def _rms(x, g):
    return x * lax.rsqrt(jnp.mean(x * x, axis=-1, keepdims=True) + NORM_EPS) * g


def _lane_iota():
    return lax.broadcasted_iota(jnp.int32, (1, LANES), 1)


def _head_lane_mask(h):
    lane = _lane_iota()
    return (lane >= h * HEAD_DIM) & (lane < (h + 1) * HEAD_DIM)


def _merge_heads(acc0, acc1):
    first = _head_lane_mask(0)
    num = jnp.where(first, acc0, acc1)
    den = pltpu.roll(jnp.where(first, acc1, acc0), HEAD_DIM, axis=1)
    return num, den


def _in_proj_kernel(x_ref, g_ref, w_ref, wf_ref, bf_ref, gain_ref, bd_ref, z_ref, lf_ref):
    u = _rms(x_ref[...], g_ref[...]).astype(BF16)
    chunk = WIDTH_A
    normed = {0: 0, 1: 1, 3: 2, 4: 3}
    for c in range(6):
        acc = jnp.dot(u, w_ref[:, c * chunk:(c + 1) * chunk], preferred_element_type=F32)
        if c in normed:
            sq = (acc * acc).astype(BF16)
            half = chunk // 2
            ss = jnp.concatenate(
                [jnp.dot(sq[:, j * half:(j + 1) * half], bd_ref[...], preferred_element_type=F32)
                 for j in range(2)], axis=1)
            r = normed[c]
            acc = acc * lax.rsqrt(ss * (1.0 / HEAD_DIM) + NORM_EPS) * gain_ref[r:r + 1, :]
        z_ref[:, c * chunk:(c + 1) * chunk] = acc.astype(BF16)
    zf = jnp.dot(u, wf_ref[...], preferred_element_type=F32) + bf_ref[...]
    lf_ref[...] = jax.nn.log_sigmoid(zf)


def _in_proj(x2, g_mix, w_qkv, w_f, b_f, gains, bd):
    n, d = x2.shape
    cols = w_qkv.shape[1]
    tm = ROW_TILE
    const = lambda i: (0, 0)
    return pl.pallas_call(
        _in_proj_kernel,
        grid=(n // tm,),
        in_specs=[
            pl.BlockSpec((tm, d), lambda i: (i, 0)),
            pl.BlockSpec((1, d), const),
            pl.BlockSpec((d, cols), const),
            pl.BlockSpec((d, LANES), const),
            pl.BlockSpec((1, LANES), const),
            pl.BlockSpec(gains.shape, const),
            pl.BlockSpec(bd.shape, const),
        ],
        out_specs=[
            pl.BlockSpec((tm, cols), lambda i: (i, 0)),
            pl.BlockSpec((tm, LANES), lambda i: (i, 0)),
        ],
        out_shape=[
            jax.ShapeDtypeStruct((n, cols), BF16),
            jax.ShapeDtypeStruct((n, LANES), F32),
        ],
        compiler_params=_cparams("parallel"),
        name="in_proj",
    )(x2, g_mix, w_qkv, w_f, b_f, gains, bd)


def _cumsum_kernel(lf_ref, tri_ref, cpk_ref):
    s = lf_ref.shape[0]
    lane = _lane_iota()
    carry = jnp.zeros((1, LANES), F32)
    for blk in range(s // BLOCK):
        rows = slice(blk * BLOCK, (blk + 1) * BLOCK)
        part = jnp.dot(tri_ref[...], lf_ref[rows, :], precision=lax.Precision.HIGHEST,
                       preferred_element_type=F32) + carry
        carry = part[BLOCK - 1:BLOCK, :]
        c = part * LOG2E
        hi = c.astype(BF16).astype(F32)
        r1 = c - hi
        mid = r1.astype(BF16).astype(F32)
        lo = r1 - mid
        packed = jnp.where(lane < N_HEADS_B, hi,
                 jnp.where(lane < 2 * N_HEADS_B, pltpu.roll(mid, N_HEADS_B, axis=1),
                 jnp.where(lane < 3 * N_HEADS_B, pltpu.roll(lo, 2 * N_HEADS_B, axis=1),
                 jnp.where(lane == 3 * N_HEADS_B, 1.0, 0.0))))
        cpk_ref[rows, :] = packed.astype(BF16)


def _cumsum(logf, batch, seq):
    tri = (lax.broadcasted_iota(jnp.int32, (BLOCK, BLOCK), 0)
           >= lax.broadcasted_iota(jnp.int32, (BLOCK, BLOCK), 1)).astype(F32)
    return pl.pallas_call(
        _cumsum_kernel,
        grid=(batch,),
        in_specs=[
            pl.BlockSpec((seq, LANES), lambda b: (b, 0)),
            pl.BlockSpec((BLOCK, BLOCK), lambda b: (0, 0)),
        ],
        out_specs=pl.BlockSpec((seq, LANES), lambda b: (b, 0)),
        out_shape=jax.ShapeDtypeStruct((batch * seq, LANES), BF16),
        compiler_params=_cparams("parallel"),
        name="cumsum",
    )(logf, tri)


def _fox_features(cpk, pair, key_side):
    assert PAIR == 2
    r = lax.broadcasted_iota(jnp.int32, (LANES, PAIR * LANES), 0)
    c = lax.broadcasted_iota(jnp.int32, (LANES, PAIR * LANES), 1)
    hh = jnp.where(c >= LANES, 1, 0)
    slot = c - hh * LANES - HEAD_DIM * (1 - hh)
    head = PAIR * pair + hh
    piece_slot = slot - 3 if key_side else slot
    ones_slot = slot if key_side else slot - 3
    piece = (piece_slot >= 0) & (piece_slot < 3) & (r == N_HEADS_B * piece_slot + head)
    ones = (ones_slot >= 0) & (ones_slot < 3) & (r == 3 * N_HEADS_B)
    place = jnp.where(piece, -1.0 if key_side else 1.0, jnp.where(ones, 1.0, 0.0)).astype(BF16)
    return jnp.dot(cpk, place, preferred_element_type=F32).astype(BF16)


def _fox_kernel(q_ref, k_ref, v_ref, c_ref, o_ref, kf, vf, s_scr, *, tile):
    pair = pl.program_id(1)
    i = pl.program_id(2)
    half = tile // 2
    in_head = [_head_lane_mask(h) for h in range(PAIR)]
    block = lambda feat, h: feat[:, h * LANES:(h + 1) * LANES]

    @pl.when(i == 0)
    def _():
        feat = _fox_features(c_ref[...], pair, True)
        for h in range(PAIR):
            kf[h] = jnp.where(in_head[h], k_ref[...], block(feat, h))
            vf[h] = jnp.where(in_head[h], v_ref[...], jnp.ones_like(v_ref[...]))

    def lane_groups_max(s):
        m = s[:, :LANES]
        for g in range(1, s.shape[1] // LANES):
            m = jnp.maximum(m, s[:, g * LANES:(g + 1) * LANES])
        return m

    def scores(qrows, off, width, h):
        return lax.dot_general(qrows, kf[h, off:off + width, :], NT_DIMS, preferred_element_type=F32)

    up_r = lax.broadcasted_iota(jnp.int32, (half, half), 0)
    up_c = lax.broadcasted_iota(jnp.int32, (half, half), 1)
    lo_r = lax.broadcasted_iota(jnp.int32, (half, tile), 0)
    lo_c = lax.broadcasted_iota(jnp.int32, (half, tile), 1)

    def query_tile(nq):
        row0 = nq * tile
        feat_q = _fox_features(c_ref[row0:row0 + tile, :], pair, False)
        q = q_ref[...]
        qf = [jnp.where(in_head[h], q, block(feat_q, h)) for h in range(PAIR)]

        row_max = []
        for h in range(PAIR):
            s_up = jnp.where(up_c <= up_r, scores(qf[h][:half], row0, half, h), NEG_INF)
            s_lo = jnp.where(lo_c <= lo_r + half, scores(qf[h][half:], row0, tile, h), NEG_INF)
            s_scr[h, :half, row0:row0 + half] = s_up
            s_scr[h, half:, row0:row0 + tile] = s_lo
            m_h = jnp.concatenate([lane_groups_max(s_up), lane_groups_max(s_lo)], axis=0)
            for j in range(nq):
                s = scores(qf[h], j * tile, tile, h)
                s_scr[h, :, j * tile:(j + 1) * tile] = s
                m_h = jnp.maximum(m_h, lane_groups_max(s))
            row_max.append(jnp.max(m_h, axis=-1, keepdims=True))

        accs = []
        for h in range(PAIR):
            m = row_max[h]
            p_up = jnp.exp2(s_scr[h, :half, row0:row0 + half] - m[:half]).astype(BF16)
            p_lo = jnp.exp2(s_scr[h, half:, row0:row0 + tile] - m[half:]).astype(BF16)
            acc = jnp.concatenate([
                jnp.dot(p_up, vf[h, row0:row0 + half, :], preferred_element_type=F32),
                jnp.dot(p_lo, vf[h, row0:row0 + tile, :], preferred_element_type=F32)], axis=0)
            for j in range(nq):
                p = jnp.exp2(s_scr[h, :, j * tile:(j + 1) * tile] - m).astype(BF16)
                acc = acc + jnp.dot(p, vf[h, j * tile:(j + 1) * tile, :], preferred_element_type=F32)
            accs.append(acc)
        num, den = _merge_heads(*accs)
        o_ref[...] = (num / den).astype(o_ref.dtype)

    for nq in range(s_scr.shape[2] // tile):
        pl.when(i == nq)(functools.partial(query_tile, nq))


def _fox(z, ccol, batch, seq):
    n = z.shape[0]
    tile = FOX_TILE
    nq = seq // tile
    npair = N_HEADS_B // PAIR
    base = 3 * WIDTH_A // LANES
    qcol, kcol, vcol = base, base + WIDTH_B // LANES, base + 2 * WIDTH_B // LANES
    return pl.pallas_call(
        functools.partial(_fox_kernel, tile=tile),
        grid=(batch, npair, nq),
        in_specs=[
            pl.BlockSpec((tile, LANES), lambda b, p, i: (b * nq + i, qcol + p)),
            pl.BlockSpec((seq, LANES), lambda b, p, i: (b, kcol + p)),
            pl.BlockSpec((seq, LANES), lambda b, p, i: (b, vcol + p)),
            pl.BlockSpec((seq, LANES), lambda b, p, i: (b, 0)),
        ],
        out_specs=pl.BlockSpec((tile, LANES), lambda b, p, i: (b * nq + i, p)),
        out_shape=jax.ShapeDtypeStruct((n, WIDTH_B), BF16),
        scratch_shapes=[
            pltpu.VMEM((PAIR, seq, LANES), BF16),
            pltpu.VMEM((PAIR, seq, LANES), BF16),
            pltpu.VMEM((PAIR, tile, seq), F32),
        ],
        compiler_params=_cparams("parallel", "parallel", "arbitrary"),
        name="fox",
    )(z, z, z, ccol)


def _dilated_kernel(slope_ref, q_ref, k_ref, v_ref, o_ref,
                    natf, p4f, p4b, p16b, qfeat, kfeat, vals, dens, maxs, *, seq):
    pair = pl.program_id(1)
    lane = _lane_iota()
    first = _head_lane_mask(0)
    quarter = seq // 4
    units = seq // BLOCK

    def deinterleave(src, t, span_start, span):
        return [src[t, pl.ds(span_start + r, span // 4, stride=4), :] for r in range(4)]

    for t, ref in enumerate((q_ref, k_ref, v_ref)):
        natf[t] = ref[...].astype(F32)
        for r, part in enumerate(deinterleave(natf, t, 0, seq)):
            p4f[t, pl.ds(r * quarter, quarter), :] = part
            p4b[t, pl.ds(r * quarter, quarter), :] = part.astype(BF16)
        for r4 in range(4):
            for r, part in enumerate(deinterleave(p4f, t, r4 * quarter, quarter)):
                p16b[t, pl.ds(r4 * quarter + r * (quarter // 4), quarter // 4), :] = part.astype(BF16)

    qi = lax.broadcasted_iota(jnp.int32, (BLOCK, LANES), 0).astype(F32)
    kj = lax.broadcasted_iota(jnp.int32, (2 * BLOCK, LANES), 0).astype(F32)
    for p, (_, dil) in enumerate(DILATED_PATTERNS):
        for h in range(PAIR):
            sd = slope_ref[PAIR * pair + h] * float(dil)
            base = HEAD_DIM * (1 - h)
            def pieces_and_ones(value, first_piece_lane, first_one_lane):
                hi = value.astype(BF16).astype(F32)
                mid = (value - hi).astype(BF16).astype(F32)
                lo = value - hi - mid
                return jnp.where(lane == first_piece_lane, hi,
                       jnp.where(lane == first_piece_lane + 1, mid,
                       jnp.where(lane == first_piece_lane + 2, lo,
                       jnp.where((lane >= first_one_lane) & (lane < first_one_lane + 3), 1.0, 0.0)))).astype(BF16)
            qfeat[p * PAIR + h] = pieces_and_ones(-(qi + float(BLOCK)) * (sd * LOG2E), base, base + 3)
            kfeat[p * PAIR + h] = pieces_and_ones(kj * (sd * LOG2E), base + 3, base)

    bq = lax.broadcasted_iota(jnp.int32, (BLOCK, 2 * BLOCK), 0)
    bk = lax.broadcasted_iota(jnp.int32, (BLOCK, 2 * BLOCK), 1)
    rel = bq + BLOCK - bk
    band = (rel >= 0) & (rel <= BLOCK)

    def unit(p, srcs, u, prev_valid):
        qs, ks, vs = srcs
        start = pl.multiple_of(u * BLOCK, BLOCK)
        prev = pl.multiple_of(jnp.maximum(start - BLOCK, 0), BLOCK)
        qb = qs[pl.ds(start, BLOCK), :]
        kk = jnp.concatenate([ks[pl.ds(prev, BLOCK), :], ks[pl.ds(start, BLOCK), :]], axis=0)
        vv = jnp.concatenate([vs[pl.ds(prev, BLOCK), :], vs[pl.ds(start, BLOCK), :]], axis=0)
        if prev_valid is True:
            ok = band
        elif prev_valid is False:
            ok = band & (bk >= BLOCK)
        else:
            ok = band & ((bk >= BLOCK) | prev_valid)
        accs, ms = [], []
        for h in range(PAIR):
            in_h = _head_lane_mask(h)
            qh = jnp.where(in_h, qb, qfeat[p * PAIR + h])
            kh = jnp.where(in_h, kk, kfeat[p * PAIR + h])
            vh = jnp.where(in_h, vv, jnp.ones_like(vv))
            s = lax.dot_general(qh, kh, NT_DIMS, preferred_element_type=F32)
            s = jnp.where(ok, s, NEG_INF)
            m = jnp.max(s, axis=-1, keepdims=True)
            pr = jnp.exp2(s - m).astype(BF16)
            accs.append(jnp.dot(pr, vh, preferred_element_type=F32))
            ms.append(m)
        num, den = _merge_heads(*accs)
        vals[p, pl.ds(start, BLOCK), :] = num
        dens[p, pl.ds(start, BLOCK), :] = den
        maxs[p, pl.ds(start, BLOCK), :] = jnp.where(first, ms[0], ms[1])

    group = UNITS_PER_STEP
    sources = ((q_ref, k_ref, v_ref), tuple(p4b.at[t] for t in range(3)), tuple(p16b.at[t] for t in range(3)))
    for p, (_, dil) in enumerate(DILATED_PATTERNS):
        per_class = units // dil

        def step(g, _, p=p, per_class=per_class):
            for e in range(group):
                u = g * group + e
                if per_class >= group:
                    prev_valid = (u % per_class != 0) if e == 0 else True
                else:
                    prev_valid = e % per_class != 0
                unit(p, sources[p], u, prev_valid)
            return 0
        lax.fori_loop(0, units // group, step, 0)

    for t, arr in enumerate((vals, dens, maxs)):
        for r4 in range(4):
            for r in range(4):
                p4f[t, pl.ds(r4 * quarter + r, quarter // 4, stride=4), :] = \
                    arr[2, pl.ds(r4 * quarter + r * (quarter // 4), quarter // 4), :]

    for r in range(4):
        grouped = pl.ds(r * quarter, quarter)
        natural = pl.ds(r, quarter, stride=4)
        ms = (maxs[0, natural, :], maxs[1, grouped, :], p4f[2, grouped, :])
        vs = (vals[0, natural, :], vals[1, grouped, :], p4f[0, grouped, :])
        ds = (dens[0, natural, :], dens[1, grouped, :], p4f[1, grouped, :])
        m_all = jnp.maximum(jnp.maximum(ms[0], ms[1]), ms[2])
        num = jnp.zeros((quarter, LANES), F32)
        den = jnp.zeros((quarter, LANES), F32)
        for p in range(3):
            e = jnp.exp2(ms[p] - m_all)
            num = num + e * vs[p]
            den = den + e * ds[p]
        natf[0, natural, :] = num / den
    o_ref[...] = natf[0].astype(o_ref.dtype)


def _dilated(z, slopes, batch, seq):
    n = z.shape[0]
    npair = N_HEADS_A // PAIR
    npat = len(DILATED_PATTERNS)
    qcol, kcol, vcol = 0, WIDTH_A // LANES, 2 * WIDTH_A // LANES
    blk = lambda c0: pl.BlockSpec((seq, LANES), lambda b, p: (b, c0 + p))
    return pl.pallas_call(
        functools.partial(_dilated_kernel, seq=seq),
        grid=(batch, npair),
        in_specs=[pl.BlockSpec(memory_space=pltpu.SMEM), blk(qcol), blk(kcol), blk(vcol)],
        out_specs=pl.BlockSpec((seq, LANES), lambda b, p: (b, p)),
        out_shape=jax.ShapeDtypeStruct((n, WIDTH_A), BF16),
        scratch_shapes=[
            pltpu.VMEM((3, seq, LANES), F32),
            pltpu.VMEM((3, seq, LANES), F32),
            pltpu.VMEM((3, seq, LANES), BF16),
            pltpu.VMEM((3, seq, LANES), BF16),
            pltpu.VMEM((npat * PAIR, BLOCK, LANES), BF16),
            pltpu.VMEM((npat * PAIR, 2 * BLOCK, LANES), BF16),
            pltpu.VMEM((npat, seq, LANES), F32),
            pltpu.VMEM((npat, seq, LANES), F32),
            pltpu.VMEM((npat, seq, LANES), F32),
        ],
        compiler_params=_cparams("parallel", "parallel"),
        name="dilated",
    )(slopes, z, z, z)


def _pack_bf16_pairs(a, b):
    hi = pltpu.bitcast(a.astype(BF16).astype(F32), jnp.int32)
    lo = pltpu.bitcast(b.astype(BF16).astype(F32), jnp.int32)
    return (hi & jnp.int32(-65536)) | lax.shift_right_logical(lo, jnp.int32(16))


def _unpack_bf16_pairs(w):
    a = pltpu.bitcast(w & jnp.int32(-65536), F32)
    b = pltpu.bitcast(lax.shift_left(w, jnp.int32(16)), F32)
    return a, b


def _post_attn_kernel(ma_ref, mb_ref, x_ref, wo_ref, g_ref, wr_ref, br_ref, tri_ref,
                      h_ref, up_ref, idx_ref, gate_ref, rank_ref, cnt_ref, carry):
    @pl.when(pl.program_id(0) == 0)
    def _():
        carry[...] = jnp.zeros_like(carry)

    y = jnp.dot(ma_ref[...], wo_ref[:WIDTH_A, :], preferred_element_type=F32)
    y = y + jnp.dot(mb_ref[...], wo_ref[WIDTH_A:, :], preferred_element_type=F32)
    h = x_ref[...] + y
    h_ref[...] = h
    u = _rms(h, g_ref[...])
    half = u.shape[1] // 2
    up_ref[...] = _pack_bf16_pairs(u[:, :half], u[:, half:])

    u_hi = u.astype(BF16)
    u_lo = (u - u_hi.astype(F32)).astype(BF16)
    hi_terms = jnp.dot(u_hi, wr_ref[...], preferred_element_type=F32)
    logits = (hi_terms[:, :LANES] + hi_terms[:, LANES:]
              + jnp.dot(u_lo, wr_ref[:, :LANES], preferred_element_type=F32)) + br_ref[...]
    lane = lax.broadcasted_iota(jnp.int32, logits.shape, 1).astype(F32)
    work = logits
    idxs, tops = [], []
    for _ in range(TOP_K):
        top = jnp.max(work, axis=-1, keepdims=True)
        idx = jnp.min(jnp.where(work == top, lane, float(LANES)), axis=-1, keepdims=True)
        work = jnp.where(lane == idx, NEG_INF, work)
        idxs.append(idx)
        tops.append(top)
    exps = [jnp.exp(t - tops[0]) for t in tops]
    total = exps[0] + exps[1] + exps[2] + exps[3]

    onehot = jnp.zeros(logits.shape, F32)
    for idx in idxs:
        onehot = onehot + (lane == idx).astype(F32)
    before = jnp.dot(tri_ref[...], onehot.astype(BF16), preferred_element_type=F32) + carry[...]
    carry[...] = carry[...] + jnp.sum(onehot, axis=0, keepdims=True)
    cnt_ref[...] = carry[...]

    idx_out = jnp.zeros(logits.shape, F32)
    gate_out = jnp.zeros(logits.shape, F32)
    rank_out = jnp.zeros(logits.shape, F32)
    for k in range(TOP_K):
        rank_k = jnp.sum(jnp.where(lane == idxs[k], before, 0.0), axis=-1, keepdims=True)
        idx_out = jnp.where(lane == float(k), idxs[k], idx_out)
        gate_out = jnp.where(lane == float(k), exps[k] / total, gate_out)
        rank_out = jnp.where(lane == float(k), rank_k, rank_out)
    idx_ref[...] = idx_out.T[:8, :].astype(jnp.int32)
    gate_ref[...] = gate_out
    rank_ref[...] = rank_out.T[:8, :].astype(jnp.int32)


def _post_attn(mix_a, mix_b, x2, w_o, g_ffn, w_r, b_r):
    n, d = x2.shape
    tm = ROW_TILE
    tri = (lax.broadcasted_iota(jnp.int32, (tm, tm), 0)
           > lax.broadcasted_iota(jnp.int32, (tm, tm), 1)).astype(BF16)
    const = lambda i: (0, 0)
    row = lambda w: pl.BlockSpec((tm, w), lambda i: (i, 0))
    lanes_t = pl.BlockSpec((8, tm), lambda i: (0, i))
    return pl.pallas_call(
        _post_attn_kernel,
        grid=(n // tm,),
        in_specs=[
            row(WIDTH_A), row(WIDTH_B), row(d),
            pl.BlockSpec(w_o.shape, const),
            pl.BlockSpec((1, d), const),
            pl.BlockSpec(w_r.shape, const),
            pl.BlockSpec((1, LANES), const),
            pl.BlockSpec((tm, tm), const),
        ],
        out_specs=[row(d), row(d // 2), lanes_t, row(LANES), lanes_t,
                   pl.BlockSpec((1, LANES), const)],
        out_shape=[
            jax.ShapeDtypeStruct((n, d), F32),
            jax.ShapeDtypeStruct((n, d // 2), jnp.int32),
            jax.ShapeDtypeStruct((8, n), jnp.int32),
            jax.ShapeDtypeStruct((n, LANES), F32),
            jax.ShapeDtypeStruct((8, n), jnp.int32),
            jax.ShapeDtypeStruct((1, LANES), F32),
        ],
        scratch_shapes=[pltpu.VMEM((1, LANES), F32)],
        compiler_params=_cparams("arbitrary"),
        name="post_attn",
    )(mix_a, mix_b, x2, w_o, g_ffn, w_r, b_r, tri)


def _wprep_kernel(w_ref, wg_ref, wl_ref, wt):
    d, cols = w_ref.shape[1:]
    de = cols // 2
    for j in range(d // LANES):
        lanes = slice(j * LANES, (j + 1) * LANES)
        wt[j] = w_ref[0, lanes, :].T
        wg_ref[0, :, lanes] = wt[j, pl.ds(0, de, stride=2), :].astype(BF16)
        wl_ref[0, :, lanes] = wt[j, pl.ds(1, de, stride=2), :].astype(BF16)


def _wprep(w_gate_up):
    ne, d, cols = w_gate_up.shape
    de = cols // 2
    out = pl.BlockSpec((1, de, d), lambda e: (e, 0, 0))
    return pl.pallas_call(
        _wprep_kernel,
        grid=(ne,),
        in_specs=[pl.BlockSpec((1, d, cols), lambda e: (e, 0, 0))],
        out_specs=[out, out],
        out_shape=[jax.ShapeDtypeStruct((ne, de, d), BF16)] * 2,
        scratch_shapes=[pltpu.VMEM((d // LANES, cols, LANES), F32)],
        compiler_params=_cparams("parallel"),
        name="wprep",
    )(w_gate_up)


def _sc_invert(keys, n_tokens, fill):
    info = plsc.get_sparse_core_info()
    workers = info.num_cores * info.num_subcores
    lanes = info.num_lanes
    total, = fill.shape
    count, = keys.shape
    chunk, unroll = SC_SCAN_CHUNK, SC_SCAN_UNROLL
    assert total % (workers * lanes) == 0 and count % (2 * chunk) == 0 and n_tokens % chunk == 0
    own = total // workers
    nchunks = count // chunk
    chunks_per_pass = n_tokens // chunk
    mesh = plsc.VectorSubcoreMesh(core_axis_name="c", subcore_axis_name="s")
    params = pltpu.CompilerParams()
    if "needs_layout_passes" in pltpu.CompilerParams.__dataclass_fields__:
        params = dataclasses.replace(params, needs_layout_passes=False)

    @functools.partial(
        pl.kernel, mesh=mesh, compiler_params=params,
        out_type=jax.ShapeDtypeStruct((total,), jnp.int32),
        scratch_types=[
            pltpu.VMEM((own,), jnp.int32),
            pltpu.VMEM((2, chunk), jnp.int32),
            pltpu.SemaphoreType.DMA((2,)),
        ],
    )
    def invert_kernel(keys_hbm, fill_hbm, out_hbm, own_v, key_v, sem):
        wid = lax.axis_index("s") * info.num_cores + lax.axis_index("c")
        lo = wid * own
        lane_ids = lax.broadcasted_iota(jnp.int32, (lanes,), 0)

        def fetch(c, b):
            off = pl.multiple_of(c * chunk, chunk)
            return pltpu.make_async_copy(keys_hbm.at[pl.ds(off, chunk)], key_v.at[b], sem.at[b])

        fetch(0, 0).start()
        pltpu.sync_copy(fill_hbm.at[pl.ds(lo, own)], own_v)

        @pl.loop(0, nchunks, step=2)
        def _(c0):
            for b in range(2):
                c = c0 + b

                @pl.when(c + 1 < nchunks)
                def _():
                    fetch(c + 1, 1 - b).start()

                fetch(c, b).wait()
                first_token = (c % chunks_per_pass) * chunk

                @pl.loop(0, chunk // lanes, step=unroll)
                def _(j0):
                    for u in range(unroll):
                        start = pl.multiple_of((j0 + u) * lanes, lanes)
                        row = key_v[b, pl.ds(start, lanes)] - lo
                        mine = (row >= 0) & (row < own)
                        plsc.store_scatter(own_v, [jnp.where(mine, row, 0)], first_token + start + lane_ids,
                                           mask=mine)

        pltpu.sync_copy(own_v, out_hbm.at[pl.ds(lo, own)])

    return invert_kernel(keys, fill)


def _sc_gather(table, idx):
    info = plsc.get_sparse_core_info()
    workers = info.num_cores * info.num_subcores
    rows, width = idx.shape[0], table.shape[1]
    chunk, depth = SC_CHUNK, SC_DEPTH
    assert rows % (workers * chunk * depth) == 0
    per_worker = rows // workers
    nchunks = per_worker // chunk
    mesh = plsc.VectorSubcoreMesh(core_axis_name="c", subcore_axis_name="s")

    @functools.partial(
        pl.kernel, mesh=mesh,
        out_type=jax.ShapeDtypeStruct((rows, width), table.dtype),
        scratch_types=[
            pltpu.VMEM((nchunks, chunk), jnp.int32),
            pltpu.VMEM((depth, chunk, width), table.dtype),
            pltpu.SemaphoreType.DMA((depth,)),
            pltpu.SemaphoreType.DMA((depth,)),
        ],
    )
    def gather_kernel(table_hbm, idx_hbm, out_hbm, idx_v, rows_v, gsem, wsem):
        wid = lax.axis_index("s") * info.num_cores + lax.axis_index("c")
        base = wid * per_worker
        pltpu.sync_copy(idx_hbm.at[wid], idx_v)

        def gather(c, b):
            return pltpu.make_async_copy(table_hbm.at[idx_v.at[c]], rows_v.at[b], gsem.at[b])

        def write(c, b):
            off = pl.multiple_of(base + c * chunk, chunk)
            return pltpu.make_async_copy(rows_v.at[b], out_hbm.at[pl.ds(off, chunk)], wsem.at[b])

        @pl.loop(0, nchunks, step=depth)
        def _(c0):
            for b in range(depth):
                gather(c0 + b, b).start()
            for b in range(depth):
                gather(c0 + b, b).wait()
                write(c0 + b, b).start()
            for b in range(depth):
                write(c0 + b, b).wait()

    return gather_kernel(table, idx.reshape(workers, nchunks, chunk))


def _gmm_kernel(te_ref, used_ref, xs_ref, wg_ref, wl_ref, bg_ref, bl_ref, wd_ref, bd_ref, ys_ref):
    i = pl.program_id(0)

    @pl.when(used_ref[i] > 0)
    def _():
        a, b = _unpack_bf16_pairs(xs_ref[...])
        x = jnp.concatenate([a, b], axis=1).astype(BF16)
        hg = lax.dot_general(x, wg_ref[0], NT_DIMS, preferred_element_type=F32) + bg_ref[0]
        hl = lax.dot_general(x, wl_ref[0], NT_DIMS, preferred_element_type=F32) + bl_ref[0]
        xg = jnp.minimum(hg, SWIGLU_LIMIT)
        xl = jnp.clip(hl, -SWIGLU_LIMIT, SWIGLU_LIMIT)
        act = xg * jax.nn.sigmoid(SWIGLU_ALPHA * xg) * (xl + 1.0)
        out = jnp.dot(act.astype(BF16), wd_ref[0].astype(BF16), preferred_element_type=F32) + bd_ref[0]
        half = out.shape[1] // 2
        ys_ref[...] = _pack_bf16_pairs(out[:, :half], out[:, half:])

    @pl.when(used_ref[i] == 0)
    def _():
        ys_ref[...] = jnp.zeros_like(ys_ref)


def _gmm(tile_expert, tile_used, xs, wg_t, wl_t, b_glu, b_lin, w_down, b_down):
    rows, half = xs.shape
    d = 2 * half
    de = wg_t.shape[1]
    tm = GMM_TILE
    wspec = lambda shape: pl.BlockSpec((1,) + shape, lambda i, te, used: (te[i], 0, 0))
    grid_spec = pltpu.PrefetchScalarGridSpec(
        num_scalar_prefetch=2,
        grid=(rows // tm,),
        in_specs=[
            pl.BlockSpec((tm, half), lambda i, te, used: (i, 0)),
            wspec((de, d)), wspec((de, d)), wspec((1, de)), wspec((1, de)),
            wspec((de, d)), wspec((1, d)),
        ],
        out_specs=pl.BlockSpec((tm, half), lambda i, te, used: (i, 0)),
    )
    return pl.pallas_call(
        _gmm_kernel,
        grid_spec=grid_spec,
        out_shape=jax.ShapeDtypeStruct((rows, half), jnp.int32),
        compiler_params=_cparams("arbitrary"),
        name="gmm",
    )(tile_expert, tile_used, xs, wg_t, wl_t, b_glu, b_lin, w_down, b_down)


def _final_kernel(h_ref, yk_ref, gate_ref, p_ref, g_ref, wg_ref, wp_ref, o_ref):
    gates = gate_ref[...]
    h = h_ref[...]
    for k in range(TOP_K):
        h = h + gates[:, k:k + 1] * jnp.concatenate(_unpack_bf16_pairs(yk_ref[k]), axis=1)
    u = _rms(h, g_ref[...]).astype(BF16)
    gate = jax.nn.sigmoid(jnp.dot(u, wg_ref[...], preferred_element_type=F32))
    proj = jnp.dot(p_ref[...].astype(BF16), wp_ref[...], preferred_element_type=F32)
    o_ref[...] = h + gate * proj


def _final(h1, yk, gates, p2, g_ple, w_gate, w_proj, first_tile, out_so_far=None):
    n, d = h1.shape
    tm = ROW_TILE
    const = lambda i: (0, 0)
    rows = lambda width: pl.BlockSpec((tm, width), lambda i: (i + first_tile, 0))
    in_specs = [
        rows(d),
        pl.BlockSpec((TOP_K, tm, d // 2), lambda i: (0, i, 0)),
        rows(LANES),
        rows(p2.shape[1]),
        pl.BlockSpec((1, d), const),
        pl.BlockSpec(w_gate.shape, const),
        pl.BlockSpec(w_proj.shape, const),
    ]
    args = [h1, yk, gates, p2, g_ple, w_gate, w_proj]
    kernel_fn, aliases = _final_kernel, {}
    if out_so_far is not None:
        in_specs.append(pl.BlockSpec(memory_space=pl.ANY))
        args.append(out_so_far)
        aliases = {len(args) - 1: 0}
        kernel_fn = lambda *refs: _final_kernel(*refs[:7], refs[8])
    return pl.pallas_call(
        kernel_fn,
        grid=(yk.shape[1] // tm,),
        in_specs=in_specs,
        out_specs=rows(d),
        out_shape=jax.ShapeDtypeStruct((n, d), F32),
        input_output_aliases=aliases,
        compiler_params=_cparams("parallel"),
        name="final",
    )(*args)


def _layer(h, p, g_mix, w_in, b_f, g_qa, g_ka, g_qb, g_kb, w_o, g_ffn, w_router, b_router,
           w_gate_up, b_gate_up, w_down, b_down, g_ple, w_ple_gate, w_ple_proj):
    batch, seq, d = h.shape
    n = batch * seq
    assert tuple(dil for _, dil in DILATED_PATTERNS) == (1, 4, 16)
    for window, dil in DILATED_PATTERNS:
        per_class = seq // BLOCK // dil
        assert window // dil == BLOCK and seq % (dil * BLOCK) == 0
        assert per_class % UNITS_PER_STEP == 0 or UNITS_PER_STEP % per_class == 0
    assert n % ROW_TILE == 0 and d % (2 * LANES) == 0 and seq % FOX_TILE == 0
    x2 = h.reshape(n, d)

    qkv_cols = 3 * WIDTH_A + 3 * WIDTH_B
    w_qkv = w_in[:, :qkv_cols].astype(BF16)
    w_f = jnp.pad(w_in[:, qkv_cols:], ((0, 0), (0, LANES - N_HEADS_B))).astype(BF16)
    b_fp = jnp.pad(b_f.astype(F32), (0, LANES - N_HEADS_B)).reshape(1, LANES)
    scale = HEAD_DIM ** -0.5
    gains = jnp.stack([jnp.tile(g_qa, N_HEADS_A) * (scale * LOG2E), jnp.tile(g_ka, N_HEADS_A),
                       jnp.tile(g_qb, N_HEADS_B) * (scale * LOG2E), jnp.tile(g_kb, N_HEADS_B)]).astype(F32)
    hid = jnp.arange(2 * LANES) // HEAD_DIM
    bd = (hid[:, None] == hid[None, :]).astype(BF16)

    z, logf = _in_proj(x2, g_mix.reshape(1, d), w_qkv, w_f, b_fp, gains, bd)
    ccol = _cumsum(logf, batch, seq)

    slopes = 2.0 ** (-8.0 * jnp.arange(1, N_HEADS_A + 1, dtype=F32) / N_HEADS_A)
    mix_a = _dilated(z, slopes, batch, seq)
    mix_b = _fox(z, ccol, batch, seq)

    w_r = jnp.pad(w_router.astype(F32), ((0, 0), (0, LANES - N_EXPERTS)))
    w_r_hi = w_r.astype(BF16)
    w_r = jnp.concatenate([w_r_hi, (w_r - w_r_hi.astype(F32)).astype(BF16)], axis=1)
    b_r = jnp.concatenate([b_router.astype(F32), jnp.full((LANES - N_EXPERTS,), NEG_INF, F32)]).reshape(1, LANES)
    h1, u_packed, top_idx, gates, rank, counts = _post_attn(
        mix_a, mix_b, x2, w_o.astype(BF16), g_ffn.reshape(1, d), w_r, b_r)

    counts = counts[0, :N_EXPERTS].astype(jnp.int32)
    tiles_per = (counts + GMM_TILE - 1) // GMM_TILE
    tile_end = jnp.cumsum(tiles_per)
    starts = (tile_end - tiles_per) * GMM_TILE
    n_tiles = n * TOP_K // GMM_TILE + N_EXPERTS
    tile_ids = jnp.arange(n_tiles, dtype=jnp.int32)
    tile_used = (tile_ids < tile_end[-1]).astype(jnp.int32)
    last_used = jnp.minimum(tile_ids, tile_end[-1] - 1)
    tile_expert = jnp.sum((last_used[:, None] >= tile_end[None, :]).astype(jnp.int32), axis=1)
    tile_expert = jnp.minimum(tile_expert, N_EXPERTS - 1)
    experts = jnp.arange(N_EXPERTS, dtype=jnp.int32)[:, None, None]
    pos_t = rank[:TOP_K] + jnp.sum(jnp.where(top_idx[None, :TOP_K] == experts, starts[:, None, None], 0), axis=0)
    padding_tokens = jnp.arange(n_tiles * GMM_TILE, dtype=jnp.int32) % n
    src = _sc_invert(pos_t.reshape(-1), n, padding_tokens)
    xs = _sc_gather(u_packed, src)

    de = w_down.shape[1]
    wg_t, wl_t = _wprep(w_gate_up)
    ys = _gmm(tile_expert, tile_used, xs, wg_t, wl_t,
              b_gate_up[:, 0::2].reshape(N_EXPERTS, 1, de).astype(F32),
              b_gate_up[:, 1::2].reshape(N_EXPERTS, 1, de).astype(F32),
              w_down, b_down.reshape(N_EXPERTS, 1, d).astype(F32))
    assert n % (2 * ROW_TILE) == 0
    half_n = n // 2
    out = None
    for part in range(2):
        part_pos = pos_t[:, part * half_n:(part + 1) * half_n].reshape(-1)
        yk = _sc_gather(ys, part_pos).reshape(TOP_K, half_n, d // 2)
        out = _final(h1, yk, gates, p.reshape(n, -1), g_ple.reshape(1, d), w_ple_gate.astype(BF16),
                     w_ple_proj.astype(BF16), part * (half_n // ROW_TILE), out)
    return out.reshape(batch, seq, d)


def kernel(x, p, g_mix, w_in, b_f, g_qa, g_ka, g_qb, g_kb, w_o, g_ffn, w_router, b_router,
           w_gate_up, b_gate_up, w_down, b_down, g_ple, w_ple_gate, w_ple_proj):
    h = x
    for i in range(g_mix.shape[0]):
        h = _layer(h, p[i], g_mix[i], w_in[i], b_f[i], g_qa[i], g_ka[i], g_qb[i], g_kb[i], w_o[i],
                   g_ffn[i], w_router[i], b_router[i], w_gate_up[i], b_gate_up[i], w_down[i],
                   b_down[i], g_ple[i], w_ple_gate[i], w_ple_proj[i])
    return h
```

```python
import dataclasses
import functools

import jax
import jax.numpy as jnp
from jax import lax
from jax.experimental import pallas as pl
from jax.experimental.pallas import tpu as pltpu
from jax.experimental.pallas import tpu_sc as plsc

HEAD_DIM = 64
N_HEADS_A = 8
N_HEADS_B = 8
WIDTH_A = N_HEADS_A * HEAD_DIM
WIDTH_B = N_HEADS_B * HEAD_DIM
DILATED_PATTERNS = ((128, 1), (512, 4), (2048, 16))
BLOCK = 128
N_EXPERTS = 32
TOP_K = 4
SWIGLU_LIMIT = 7.0
SWIGLU_ALPHA = 1.702
NORM_EPS = 1e-6

LANES = 128
PAIR = LANES // HEAD_DIM
ROW_TILE = 512
GMM_TILE = 512
FOX_TILE = 512
UNITS_PER_STEP = 8
SC_CHUNK = 32
SC_DEPTH = 4
COMBINE_PARTS = 4
SC_SCAN_CHUNK = 16384
SC_SCAN_UNROLL = 8
VMEM_LIMIT = 56 * 1024 * 1024

F32 = jnp.float32
BF16 = jnp.bfloat16
NEG_INF = float("-inf")
NT_DIMS = (((1,), (1,)), ((), ()))
LOG2E = 1.4426950408889634


def _cparams(*sem):
    return pltpu.CompilerParams(dimension_semantics=sem, vmem_limit_bytes=VMEM_LIMIT)


def _rms(x, g):
    return x * lax.rsqrt(jnp.mean(x * x, axis=-1, keepdims=True) + NORM_EPS) * g


def _lane_iota():
    return lax.broadcasted_iota(jnp.int32, (1, LANES), 1)


def _head_lane_mask(h):
    lane = _lane_iota()
    return (lane >= h * HEAD_DIM) & (lane < (h + 1) * HEAD_DIM)


def _merge_heads(acc0, acc1):
    first = _head_lane_mask(0)
    num = jnp.where(first, acc0, acc1)
    den = pltpu.roll(jnp.where(first, acc1, acc0), HEAD_DIM, axis=1)
    return num, den


def _in_proj_kernel(x_ref, g_ref, w_ref, wf_ref, bf_ref, gain_ref, bd_ref, z_ref, lf_ref):
    u = _rms(x_ref[...], g_ref[...]).astype(BF16)
    chunk = WIDTH_A
    normed = {0: 0, 1: 1, 3: 2, 4: 3}
    for c in range(6):
        acc = jnp.dot(u, w_ref[:, c * chunk:(c + 1) * chunk], preferred_element_type=F32)
        if c in normed:
            sq = (acc * acc).astype(BF16)
            half = chunk // 2
            ss = jnp.concatenate(
                [jnp.dot(sq[:, j * half:(j + 1) * half], bd_ref[...], preferred_element_type=F32)
                 for j in range(2)], axis=1)
            r = normed[c]
            acc = acc * lax.rsqrt(ss * (1.0 / HEAD_DIM) + NORM_EPS) * gain_ref[r:r + 1, :]
        z_ref[:, c * chunk:(c + 1) * chunk] = acc.astype(BF16)
    zf = jnp.dot(u, wf_ref[...], preferred_element_type=F32) + bf_ref[...]
    lf_ref[...] = jax.nn.log_sigmoid(zf)


def _in_proj(x2, g_mix, w_qkv, w_f, b_f, gains, bd):
    n, d = x2.shape
    cols = w_qkv.shape[1]
    tm = ROW_TILE
    const = lambda i: (0, 0)
    return pl.pallas_call(
        _in_proj_kernel,
        grid=(n // tm,),
        in_specs=[
            pl.BlockSpec((tm, d), lambda i: (i, 0)),
            pl.BlockSpec((1, d), const),
            pl.BlockSpec((d, cols), const),
            pl.BlockSpec((d, LANES), const),
            pl.BlockSpec((1, LANES), const),
            pl.BlockSpec(gains.shape, const),
            pl.BlockSpec(bd.shape, const),
        ],
        out_specs=[
            pl.BlockSpec((tm, cols), lambda i: (i, 0)),
            pl.BlockSpec((tm, LANES), lambda i: (i, 0)),
        ],
        out_shape=[
            jax.ShapeDtypeStruct((n, cols), BF16),
            jax.ShapeDtypeStruct((n, LANES), F32),
        ],
        compiler_params=_cparams("parallel"),
        name="in_proj",
    )(x2, g_mix, w_qkv, w_f, b_f, gains, bd)


def _cumsum_kernel(lf_ref, tri_ref, cpk_ref):
    s = lf_ref.shape[0]
    lane = _lane_iota()
    carry = jnp.zeros((1, LANES), F32)
    for blk in range(s // BLOCK):
        rows = slice(blk * BLOCK, (blk + 1) * BLOCK)
        part = jnp.dot(tri_ref[...], lf_ref[rows, :], precision=lax.Precision.HIGHEST,
                       preferred_element_type=F32) + carry
        carry = part[BLOCK - 1:BLOCK, :]
        c = part * LOG2E
        hi = c.astype(BF16).astype(F32)
        r1 = c - hi
        mid = r1.astype(BF16).astype(F32)
        lo = r1 - mid
        packed = jnp.where(lane < N_HEADS_B, hi,
                 jnp.where(lane < 2 * N_HEADS_B, pltpu.roll(mid, N_HEADS_B, axis=1),
                 jnp.where(lane < 3 * N_HEADS_B, pltpu.roll(lo, 2 * N_HEADS_B, axis=1),
                 jnp.where(lane == 3 * N_HEADS_B, 1.0, 0.0))))
        cpk_ref[rows, :] = packed.astype(BF16)


def _cumsum(logf, batch, seq):
    tri = (lax.broadcasted_iota(jnp.int32, (BLOCK, BLOCK), 0)
           >= lax.broadcasted_iota(jnp.int32, (BLOCK, BLOCK), 1)).astype(F32)
    return pl.pallas_call(
        _cumsum_kernel,
        grid=(batch,),
        in_specs=[
            pl.BlockSpec((seq, LANES), lambda b: (b, 0)),
            pl.BlockSpec((BLOCK, BLOCK), lambda b: (0, 0)),
        ],
        out_specs=pl.BlockSpec((seq, LANES), lambda b: (b, 0)),
        out_shape=jax.ShapeDtypeStruct((batch * seq, LANES), BF16),
        compiler_params=_cparams("parallel"),
        name="cumsum",
    )(logf, tri)


def _fox_features(cpk, pair, key_side):
    assert PAIR == 2
    r = lax.broadcasted_iota(jnp.int32, (LANES, PAIR * LANES), 0)
    c = lax.broadcasted_iota(jnp.int32, (LANES, PAIR * LANES), 1)
    hh = jnp.where(c >= LANES, 1, 0)
    slot = c - hh * LANES - HEAD_DIM * (1 - hh)
    head = PAIR * pair + hh
    piece_slot = slot - 3 if key_side else slot
    ones_slot = slot if key_side else slot - 3
    piece = (piece_slot >= 0) & (piece_slot < 3) & (r == N_HEADS_B * piece_slot + head)
    ones = (ones_slot >= 0) & (ones_slot < 3) & (r == 3 * N_HEADS_B)
    place = jnp.where(piece, -1.0 if key_side else 1.0, jnp.where(ones, 1.0, 0.0)).astype(BF16)
    return jnp.dot(cpk, place, preferred_element_type=F32).astype(BF16)


def _fox_kernel(q_ref, k_ref, v_ref, c_ref, o_ref, kf, vf, s_scr, *, tile):
    pair = pl.program_id(1)
    i = pl.program_id(2)
    half = tile // 2
    in_head = [_head_lane_mask(h) for h in range(PAIR)]
    block = lambda feat, h: feat[:, h * LANES:(h + 1) * LANES]

    @pl.when(i == 0)
    def _():
        feat = _fox_features(c_ref[...], pair, True)
        for h in range(PAIR):
            kf[h] = jnp.where(in_head[h], k_ref[...], block(feat, h))
            vf[h] = jnp.where(in_head[h], v_ref[...], jnp.ones_like(v_ref[...]))

    def lane_groups_max(s):
        m = s[:, :LANES]
        for g in range(1, s.shape[1] // LANES):
            m = jnp.maximum(m, s[:, g * LANES:(g + 1) * LANES])
        return m

    def scores(qrows, off, width, h):
        return lax.dot_general(qrows, kf[h, off:off + width, :], NT_DIMS, preferred_element_type=F32)

    up_r = lax.broadcasted_iota(jnp.int32, (half, half), 0)
    up_c = lax.broadcasted_iota(jnp.int32, (half, half), 1)
    lo_r = lax.broadcasted_iota(jnp.int32, (half, tile), 0)
    lo_c = lax.broadcasted_iota(jnp.int32, (half, tile), 1)

    def query_tile(nq):
        row0 = nq * tile
        feat_q = _fox_features(c_ref[row0:row0 + tile, :], pair, False)
        q = q_ref[...]
        qf = [jnp.where(in_head[h], q, block(feat_q, h)) for h in range(PAIR)]

        row_max = []
        for h in range(PAIR):
            s_up = jnp.where(up_c <= up_r, scores(qf[h][:half], row0, half, h), NEG_INF)
            s_lo = jnp.where(lo_c <= lo_r + half, scores(qf[h][half:], row0, tile, h), NEG_INF)
            s_scr[h, :half, row0:row0 + half] = s_up
            s_scr[h, half:, row0:row0 + tile] = s_lo
            m_h = jnp.concatenate([lane_groups_max(s_up), lane_groups_max(s_lo)], axis=0)
            for j in range(nq):
                s = scores(qf[h], j * tile, tile, h)
                s_scr[h, :, j * tile:(j + 1) * tile] = s
                m_h = jnp.maximum(m_h, lane_groups_max(s))
            row_max.append(jnp.max(m_h, axis=-1, keepdims=True))

        accs = []
        for h in range(PAIR):
            m = row_max[h]
            p_up = jnp.exp2(s_scr[h, :half, row0:row0 + half] - m[:half]).astype(BF16)
            p_lo = jnp.exp2(s_scr[h, half:, row0:row0 + tile] - m[half:]).astype(BF16)
            acc = jnp.concatenate([
                jnp.dot(p_up, vf[h, row0:row0 + half, :], preferred_element_type=F32),
                jnp.dot(p_lo, vf[h, row0:row0 + tile, :], preferred_element_type=F32)], axis=0)
            for j in range(nq):
                p = jnp.exp2(s_scr[h, :, j * tile:(j + 1) * tile] - m).astype(BF16)
                acc = acc + jnp.dot(p, vf[h, j * tile:(j + 1) * tile, :], preferred_element_type=F32)
            accs.append(acc)
        num, den = _merge_heads(*accs)
        o_ref[...] = (num / den).astype(o_ref.dtype)

    for nq in range(s_scr.shape[2] // tile):
        pl.when(i == nq)(functools.partial(query_tile, nq))


def _fox(z, ccol, batch, seq):
    n = z.shape[0]
    tile = FOX_TILE
    nq = seq // tile
    npair = N_HEADS_B // PAIR
    base = 3 * WIDTH_A // LANES
    qcol, kcol, vcol = base, base + WIDTH_B // LANES, base + 2 * WIDTH_B // LANES
    return pl.pallas_call(
        functools.partial(_fox_kernel, tile=tile),
        grid=(batch, npair, nq),
        in_specs=[
            pl.BlockSpec((tile, LANES), lambda b, p, i: (b * nq + i, qcol + p)),
            pl.BlockSpec((seq, LANES), lambda b, p, i: (b, kcol + p)),
            pl.BlockSpec((seq, LANES), lambda b, p, i: (b, vcol + p)),
            pl.BlockSpec((seq, LANES), lambda b, p, i: (b, 0)),
        ],
        out_specs=pl.BlockSpec((tile, LANES), lambda b, p, i: (b * nq + i, p)),
        out_shape=jax.ShapeDtypeStruct((n, WIDTH_B), BF16),
        scratch_shapes=[
            pltpu.VMEM((PAIR, seq, LANES), BF16),
            pltpu.VMEM((PAIR, seq, LANES), BF16),
            pltpu.VMEM((PAIR, tile, seq), F32),
        ],
        compiler_params=_cparams("parallel", "parallel", "arbitrary"),
        name="fox",
    )(z, z, z, ccol)


def _dilated_kernel(slope_ref, q_ref, k_ref, v_ref, o_ref,
                    natf, p4f, p4b, p16b, qfeat, kfeat, vals, dens, maxs, *, seq):
    pair = pl.program_id(1)
    lane = _lane_iota()
    first = _head_lane_mask(0)
    quarter = seq // 4
    units = seq // BLOCK

    def deinterleave(src, t, span_start, span):
        return [src[t, pl.ds(span_start + r, span // 4, stride=4), :] for r in range(4)]

    for t, ref in enumerate((q_ref, k_ref, v_ref)):
        natf[t] = ref[...].astype(F32)
        for r, part in enumerate(deinterleave(natf, t, 0, seq)):
            p4f[t, pl.ds(r * quarter, quarter), :] = part
            p4b[t, pl.ds(r * quarter, quarter), :] = part.astype(BF16)
        for r4 in range(4):
            for r, part in enumerate(deinterleave(p4f, t, r4 * quarter, quarter)):
                p16b[t, pl.ds(r4 * quarter + r * (quarter // 4), quarter // 4), :] = part.astype(BF16)

    qi = lax.broadcasted_iota(jnp.int32, (BLOCK, LANES), 0).astype(F32)
    kj = lax.broadcasted_iota(jnp.int32, (2 * BLOCK, LANES), 0).astype(F32)
    for p, (_, dil) in enumerate(DILATED_PATTERNS):
        for h in range(PAIR):
            sd = slope_ref[PAIR * pair + h] * float(dil)
            base = HEAD_DIM * (1 - h)
            def pieces_and_ones(value, first_piece_lane, first_one_lane):
                hi = value.astype(BF16).astype(F32)
                mid = (value - hi).astype(BF16).astype(F32)
                lo = value - hi - mid
                return jnp.where(lane == first_piece_lane, hi,
                       jnp.where(lane == first_piece_lane + 1, mid,
                       jnp.where(lane == first_piece_lane + 2, lo,
                       jnp.where((lane >= first_one_lane) & (lane < first_one_lane + 3), 1.0, 0.0)))).astype(BF16)
            qfeat[p * PAIR + h] = pieces_and_ones(-(qi + float(BLOCK)) * (sd * LOG2E), base, base + 3)
            kfeat[p * PAIR + h] = pieces_and_ones(kj * (sd * LOG2E), base + 3, base)

    bq = lax.broadcasted_iota(jnp.int32, (BLOCK, 2 * BLOCK), 0)
    bk = lax.broadcasted_iota(jnp.int32, (BLOCK, 2 * BLOCK), 1)
    rel = bq + BLOCK - bk
    band = (rel >= 0) & (rel <= BLOCK)

    def unit(p, srcs, u, prev_valid):
        qs, ks, vs = srcs
        start = pl.multiple_of(u * BLOCK, BLOCK)
        prev = pl.multiple_of(jnp.maximum(start - BLOCK, 0), BLOCK)
        qb = qs[pl.ds(start, BLOCK), :]
        kk = jnp.concatenate([ks[pl.ds(prev, BLOCK), :], ks[pl.ds(start, BLOCK), :]], axis=0)
        vv = jnp.concatenate([vs[pl.ds(prev, BLOCK), :], vs[pl.ds(start, BLOCK), :]], axis=0)
        if prev_valid is True:
            ok = band
        elif prev_valid is False:
            ok = band & (bk >= BLOCK)
        else:
            ok = band & ((bk >= BLOCK) | prev_valid)
        accs, ms = [], []
        for h in range(PAIR):
            in_h = _head_lane_mask(h)
            qh = jnp.where(in_h, qb, qfeat[p * PAIR + h])
            kh = jnp.where(in_h, kk, kfeat[p * PAIR + h])
            vh = jnp.where(in_h, vv, jnp.ones_like(vv))
            s = lax.dot_general(qh, kh, NT_DIMS, preferred_element_type=F32)
            s = jnp.where(ok, s, NEG_INF)
            m = jnp.max(s, axis=-1, keepdims=True)
            pr = jnp.exp2(s - m).astype(BF16)
            accs.append(jnp.dot(pr, vh, preferred_element_type=F32))
            ms.append(m)
        num, den = _merge_heads(*accs)
        vals[p, pl.ds(start, BLOCK), :] = num
        dens[p, pl.ds(start, BLOCK), :] = den
        maxs[p, pl.ds(start, BLOCK), :] = jnp.where(first, ms[0], ms[1])

    group = UNITS_PER_STEP
    sources = ((q_ref, k_ref, v_ref), tuple(p4b.at[t] for t in range(3)), tuple(p16b.at[t] for t in range(3)))
    for p, (_, dil) in enumerate(DILATED_PATTERNS):
        per_class = units // dil

        def step(g, _, p=p, per_class=per_class):
            for e in range(group):
                u = g * group + e
                if per_class >= group:
                    prev_valid = (u % per_class != 0) if e == 0 else True
                else:
                    prev_valid = e % per_class != 0
                unit(p, sources[p], u, prev_valid)
            return 0
        lax.fori_loop(0, units // group, step, 0)

    for t, arr in enumerate((vals, dens, maxs)):
        for r4 in range(4):
            for r in range(4):
                p4f[t, pl.ds(r4 * quarter + r, quarter // 4, stride=4), :] = \
                    arr[2, pl.ds(r4 * quarter + r * (quarter // 4), quarter // 4), :]

    for r in range(4):
        grouped = pl.ds(r * quarter, quarter)
        natural = pl.ds(r, quarter, stride=4)
        ms = (maxs[0, natural, :], maxs[1, grouped, :], p4f[2, grouped, :])
        vs = (vals[0, natural, :], vals[1, grouped, :], p4f[0, grouped, :])
        ds = (dens[0, natural, :], dens[1, grouped, :], p4f[1, grouped, :])
        m_all = jnp.maximum(jnp.maximum(ms[0], ms[1]), ms[2])
        num = jnp.zeros((quarter, LANES), F32)
        den = jnp.zeros((quarter, LANES), F32)
        for p in range(3):
            e = jnp.exp2(ms[p] - m_all)
            num = num + e * vs[p]
            den = den + e * ds[p]
        natf[0, natural, :] = num / den
    o_ref[...] = natf[0].astype(o_ref.dtype)


def _dilated(z, slopes, batch, seq):
    n = z.shape[0]
    npair = N_HEADS_A // PAIR
    npat = len(DILATED_PATTERNS)
    qcol, kcol, vcol = 0, WIDTH_A // LANES, 2 * WIDTH_A // LANES
    blk = lambda c0: pl.BlockSpec((seq, LANES), lambda b, p: (b, c0 + p))
    return pl.pallas_call(
        functools.partial(_dilated_kernel, seq=seq),
        grid=(batch, npair),
        in_specs=[pl.BlockSpec(memory_space=pltpu.SMEM), blk(qcol), blk(kcol), blk(vcol)],
        out_specs=pl.BlockSpec((seq, LANES), lambda b, p: (b, p)),
        out_shape=jax.ShapeDtypeStruct((n, WIDTH_A), BF16),
        scratch_shapes=[
            pltpu.VMEM((3, seq, LANES), F32),
            pltpu.VMEM((3, seq, LANES), F32),
            pltpu.VMEM((3, seq, LANES), BF16),
            pltpu.VMEM((3, seq, LANES), BF16),
            pltpu.VMEM((npat * PAIR, BLOCK, LANES), BF16),
            pltpu.VMEM((npat * PAIR, 2 * BLOCK, LANES), BF16),
            pltpu.VMEM((npat, seq, LANES), F32),
            pltpu.VMEM((npat, seq, LANES), F32),
            pltpu.VMEM((npat, seq, LANES), F32),
        ],
        compiler_params=_cparams("parallel", "parallel"),
        name="dilated",
    )(slopes, z, z, z)


def _pack_bf16_pairs(a, b):
    hi = pltpu.bitcast(a.astype(BF16).astype(F32), jnp.int32)
    lo = pltpu.bitcast(b.astype(BF16).astype(F32), jnp.int32)
    return (hi & jnp.int32(-65536)) | lax.shift_right_logical(lo, jnp.int32(16))


def _unpack_bf16_pairs(w):
    a = pltpu.bitcast(w & jnp.int32(-65536), F32)
    b = pltpu.bitcast(lax.shift_left(w, jnp.int32(16)), F32)
    return a, b


def _post_attn_kernel(ma_ref, mb_ref, x_ref, wo_ref, g_ref, wr_ref, br_ref, tri_ref,
                      h_ref, up_ref, idx_ref, gate_ref, rank_ref, cnt_ref, carry):
    @pl.when(pl.program_id(0) == 0)
    def _():
        carry[...] = jnp.zeros_like(carry)

    y = jnp.dot(ma_ref[...], wo_ref[:WIDTH_A, :], preferred_element_type=F32)
    y = y + jnp.dot(mb_ref[...], wo_ref[WIDTH_A:, :], preferred_element_type=F32)
    h = x_ref[...] + y
    h_ref[...] = h
    u = _rms(h, g_ref[...])
    half = u.shape[1] // 2
    up_ref[...] = _pack_bf16_pairs(u[:, :half], u[:, half:])

    u_hi = u.astype(BF16)
    u_lo = (u - u_hi.astype(F32)).astype(BF16)
    hi_terms = jnp.dot(u_hi, wr_ref[...], preferred_element_type=F32)
    logits = (hi_terms[:, :LANES] + hi_terms[:, LANES:]
              + jnp.dot(u_lo, wr_ref[:, :LANES], preferred_element_type=F32)) + br_ref[...]
    lane = lax.broadcasted_iota(jnp.int32, logits.shape, 1).astype(F32)
    work = logits
    idxs, tops = [], []
    for _ in range(TOP_K):
        top = jnp.max(work, axis=-1, keepdims=True)
        idx = jnp.min(jnp.where(work == top, lane, float(LANES)), axis=-1, keepdims=True)
        work = jnp.where(lane == idx, NEG_INF, work)
        idxs.append(idx)
        tops.append(top)
    exps = [jnp.exp(t - tops[0]) for t in tops]
    total = exps[0] + exps[1] + exps[2] + exps[3]

    onehot = jnp.zeros(logits.shape, F32)
    for idx in idxs:
        onehot = onehot + (lane == idx).astype(F32)
    before = jnp.dot(tri_ref[...], onehot.astype(BF16), preferred_element_type=F32) + carry[...]
    carry[...] = carry[...] + jnp.sum(onehot, axis=0, keepdims=True)
    cnt_ref[...] = carry[...]

    idx_out = jnp.zeros(logits.shape, F32)
    gate_out = jnp.zeros(logits.shape, F32)
    rank_out = jnp.zeros(logits.shape, F32)
    for k in range(TOP_K):
        rank_k = jnp.sum(jnp.where(lane == idxs[k], before, 0.0), axis=-1, keepdims=True)
        idx_out = jnp.where(lane == float(k), idxs[k], idx_out)
        gate_out = jnp.where(lane == float(k), exps[k] / total, gate_out)
        rank_out = jnp.where(lane == float(k), rank_k, rank_out)
    idx_ref[...] = idx_out.T[:8, :].astype(jnp.int32)
    gate_ref[...] = gate_out
    rank_ref[...] = rank_out.T[:8, :].astype(jnp.int32)


def _post_attn(mix_a, mix_b, x2, w_o, g_ffn, w_r, b_r):
    n, d = x2.shape
    tm = ROW_TILE
    tri = (lax.broadcasted_iota(jnp.int32, (tm, tm), 0)
           > lax.broadcasted_iota(jnp.int32, (tm, tm), 1)).astype(BF16)
    const = lambda i: (0, 0)
    row = lambda w: pl.BlockSpec((tm, w), lambda i: (i, 0))
    lanes_t = pl.BlockSpec((8, tm), lambda i: (0, i))
    return pl.pallas_call(
        _post_attn_kernel,
        grid=(n // tm,),
        in_specs=[
            row(WIDTH_A), row(WIDTH_B), row(d),
            pl.BlockSpec(w_o.shape, const),
            pl.BlockSpec((1, d), const),
            pl.BlockSpec(w_r.shape, const),
            pl.BlockSpec((1, LANES), const),
            pl.BlockSpec((tm, tm), const),
        ],
        out_specs=[row(d), row(d // 2), lanes_t, row(LANES), lanes_t,
                   pl.BlockSpec((1, LANES), const)],
        out_shape=[
            jax.ShapeDtypeStruct((n, d), F32),
            jax.ShapeDtypeStruct((n, d // 2), jnp.int32),
            jax.ShapeDtypeStruct((8, n), jnp.int32),
            jax.ShapeDtypeStruct((n, LANES), F32),
            jax.ShapeDtypeStruct((8, n), jnp.int32),
            jax.ShapeDtypeStruct((1, LANES), F32),
        ],
        scratch_shapes=[pltpu.VMEM((1, LANES), F32)],
        compiler_params=_cparams("arbitrary"),
        name="post_attn",
    )(mix_a, mix_b, x2, w_o, g_ffn, w_r, b_r, tri)


def _wprep_kernel(w_ref, wg_ref, wl_ref, wt):
    d, cols = w_ref.shape[1:]
    de = cols // 2
    for j in range(d // LANES):
        lanes = slice(j * LANES, (j + 1) * LANES)
        wt[j] = w_ref[0, lanes, :].T
        wg_ref[0, :, lanes] = wt[j, pl.ds(0, de, stride=2), :].astype(BF16)
        wl_ref[0, :, lanes] = wt[j, pl.ds(1, de, stride=2), :].astype(BF16)


def _wprep(w_gate_up, first, count, so_far=None):
    ne, d, cols = w_gate_up.shape
    de = cols // 2
    out = pl.BlockSpec((1, de, d), lambda e: (e + first, 0, 0))
    in_specs = [pl.BlockSpec((1, d, cols), lambda e: (e + first, 0, 0))]
    args, aliases, kernel_fn = [w_gate_up], {}, _wprep_kernel
    if so_far is not None:
        in_specs += [pl.BlockSpec(memory_space=pl.ANY)] * 2
        args += list(so_far)
        aliases = {1: 0, 2: 1}
        kernel_fn = lambda w_ref, _g, _l, wg_ref, wl_ref, wt: _wprep_kernel(w_ref, wg_ref, wl_ref, wt)
    return pl.pallas_call(
        kernel_fn,
        grid=(count,),
        in_specs=in_specs,
        out_specs=[out, out],
        out_shape=[jax.ShapeDtypeStruct((ne, de, d), BF16)] * 2,
        input_output_aliases=aliases,
        scratch_shapes=[pltpu.VMEM((d // LANES, cols, LANES), F32)],
        compiler_params=_cparams("parallel"),
        name="wprep",
    )(*args)


def _sc_invert(keys, n_tokens, fill):
    info = plsc.get_sparse_core_info()
    workers = info.num_cores * info.num_subcores
    lanes = info.num_lanes
    total, = fill.shape
    count, = keys.shape
    chunk, unroll = SC_SCAN_CHUNK, SC_SCAN_UNROLL
    assert total % (workers * lanes) == 0 and count % (2 * chunk) == 0 and n_tokens % chunk == 0
    own = total // workers
    nchunks = count // chunk
    chunks_per_pass = n_tokens // chunk
    mesh = plsc.VectorSubcoreMesh(core_axis_name="c", subcore_axis_name="s")
    params = pltpu.CompilerParams()
    if "needs_layout_passes" in pltpu.CompilerParams.__dataclass_fields__:
        params = dataclasses.replace(params, needs_layout_passes=False)

    @functools.partial(
        pl.kernel, mesh=mesh, compiler_params=params,
        out_type=jax.ShapeDtypeStruct((total,), jnp.int32),
        scratch_types=[
            pltpu.VMEM((own,), jnp.int32),
            pltpu.VMEM((2, chunk), jnp.int32),
            pltpu.SemaphoreType.DMA((2,)),
        ],
    )
    def invert_kernel(keys_hbm, fill_hbm, out_hbm, own_v, key_v, sem):
        wid = lax.axis_index("s") * info.num_cores + lax.axis_index("c")
        lo = wid * own
        lane_ids = lax.broadcasted_iota(jnp.int32, (lanes,), 0)

        def fetch(c, b):
            off = pl.multiple_of(c * chunk, chunk)
            return pltpu.make_async_copy(keys_hbm.at[pl.ds(off, chunk)], key_v.at[b], sem.at[b])

        fetch(0, 0).start()
        pltpu.sync_copy(fill_hbm.at[pl.ds(lo, own)], own_v)

        @pl.loop(0, nchunks, step=2)
        def _(c0):
            for b in range(2):
                c = c0 + b

                @pl.when(c + 1 < nchunks)
                def _():
                    fetch(c + 1, 1 - b).start()

                fetch(c, b).wait()
                first_token = (c % chunks_per_pass) * chunk

                @pl.loop(0, chunk // lanes, step=unroll)
                def _(j0):
                    for u in range(unroll):
                        start = pl.multiple_of((j0 + u) * lanes, lanes)
                        row = key_v[b, pl.ds(start, lanes)] - lo
                        mine = (row >= 0) & (row < own)
                        plsc.store_scatter(own_v, [jnp.where(mine, row, 0)], first_token + start + lane_ids,
                                           mask=mine)

        pltpu.sync_copy(own_v, out_hbm.at[pl.ds(lo, own)])

    return invert_kernel(keys, fill)


def _sc_gather(table, idx):
    info = plsc.get_sparse_core_info()
    workers = info.num_cores * info.num_subcores
    rows, width = idx.shape[0], table.shape[1]
    chunk, depth = SC_CHUNK, SC_DEPTH
    assert rows % (workers * chunk * depth) == 0
    per_worker = rows // workers
    nchunks = per_worker // chunk
    mesh = plsc.VectorSubcoreMesh(core_axis_name="c", subcore_axis_name="s")

    @functools.partial(
        pl.kernel, mesh=mesh,
        out_type=jax.ShapeDtypeStruct((rows, width), table.dtype),
        scratch_types=[
            pltpu.VMEM((nchunks, chunk), jnp.int32),
            pltpu.VMEM((depth, chunk, width), table.dtype),
            pltpu.SemaphoreType.DMA((depth,)),
            pltpu.SemaphoreType.DMA((depth,)),
        ],
    )
    def gather_kernel(table_hbm, idx_hbm, out_hbm, idx_v, rows_v, gsem, wsem):
        wid = lax.axis_index("s") * info.num_cores + lax.axis_index("c")
        base = wid * per_worker
        pltpu.sync_copy(idx_hbm.at[wid], idx_v)

        def gather(c, b):
            return pltpu.make_async_copy(table_hbm.at[idx_v.at[c]], rows_v.at[b], gsem.at[b])

        def write(c, b):
            off = pl.multiple_of(base + c * chunk, chunk)
            return pltpu.make_async_copy(rows_v.at[b], out_hbm.at[pl.ds(off, chunk)], wsem.at[b])

        @pl.loop(0, nchunks, step=depth)
        def _(c0):
            for b in range(depth):
                gather(c0 + b, b).start()
            for b in range(depth):
                gather(c0 + b, b).wait()
                write(c0 + b, b).start()
            for b in range(depth):
                write(c0 + b, b).wait()

    return gather_kernel(table, idx.reshape(workers, nchunks, chunk))


def _gmm_kernel(te_ref, used_ref, xs_ref, wg_ref, wl_ref, bg_ref, bl_ref, wd_ref, bd_ref, ys_ref):
    i = pl.program_id(0)

    @pl.when(used_ref[i] > 0)
    def _():
        a, b = _unpack_bf16_pairs(xs_ref[...])
        x = jnp.concatenate([a, b], axis=1).astype(BF16)
        hg = lax.dot_general(x, wg_ref[0], NT_DIMS, preferred_element_type=F32) + bg_ref[0]
        hl = lax.dot_general(x, wl_ref[0], NT_DIMS, preferred_element_type=F32) + bl_ref[0]
        xg = jnp.minimum(hg, SWIGLU_LIMIT)
        xl = jnp.clip(hl, -SWIGLU_LIMIT, SWIGLU_LIMIT)
        act = xg * jax.nn.sigmoid(SWIGLU_ALPHA * xg) * (xl + 1.0)
        out = jnp.dot(act.astype(BF16), wd_ref[0].astype(BF16), preferred_element_type=F32) + bd_ref[0]
        half = out.shape[1] // 2
        ys_ref[...] = _pack_bf16_pairs(out[:, :half], out[:, half:])

    @pl.when(used_ref[i] == 0)
    def _():
        ys_ref[...] = jnp.zeros_like(ys_ref)


def _gmm(tile_expert, tile_used, xs, wg_t, wl_t, b_glu, b_lin, w_down, b_down):
    rows, half = xs.shape
    d = 2 * half
    de = wg_t.shape[1]
    tm = GMM_TILE
    wspec = lambda shape: pl.BlockSpec((1,) + shape, lambda i, te, used: (te[i], 0, 0))
    grid_spec = pltpu.PrefetchScalarGridSpec(
        num_scalar_prefetch=2,
        grid=(rows // tm,),
        in_specs=[
            pl.BlockSpec((tm, half), lambda i, te, used: (i, 0)),
            wspec((de, d)), wspec((de, d)), wspec((1, de)), wspec((1, de)),
            wspec((de, d)), wspec((1, d)),
        ],
        out_specs=pl.BlockSpec((tm, half), lambda i, te, used: (i, 0)),
    )
    return pl.pallas_call(
        _gmm_kernel,
        grid_spec=grid_spec,
        out_shape=jax.ShapeDtypeStruct((rows, half), jnp.int32),
        compiler_params=_cparams("arbitrary"),
        name="gmm",
    )(tile_expert, tile_used, xs, wg_t, wl_t, b_glu, b_lin, w_down, b_down)


def _final_kernel(h_ref, yk_ref, gate_ref, p_ref, g_ref, wg_ref, wp_ref, o_ref):
    gates = gate_ref[...]
    h = h_ref[...]
    for k in range(TOP_K):
        h = h + gates[:, k:k + 1] * jnp.concatenate(_unpack_bf16_pairs(yk_ref[k]), axis=1)
    u = _rms(h, g_ref[...]).astype(BF16)
    gate = jax.nn.sigmoid(jnp.dot(u, wg_ref[...], preferred_element_type=F32))
    proj = jnp.dot(p_ref[...].astype(BF16), wp_ref[...], preferred_element_type=F32)
    o_ref[...] = h + gate * proj


def _final(h1, yk, gates, p2, g_ple, w_gate, w_proj, first_tile, out_so_far=None):
    n, d = h1.shape
    tm = ROW_TILE
    const = lambda i: (0, 0)
    rows = lambda width: pl.BlockSpec((tm, width), lambda i: (i + first_tile, 0))
    in_specs = [
        rows(d),
        pl.BlockSpec((TOP_K, tm, d // 2), lambda i: (0, i, 0)),
        rows(LANES),
        rows(p2.shape[1]),
        pl.BlockSpec((1, d), const),
        pl.BlockSpec(w_gate.shape, const),
        pl.BlockSpec(w_proj.shape, const),
    ]
    args = [h1, yk, gates, p2, g_ple, w_gate, w_proj]
    kernel_fn, aliases = _final_kernel, {}
    if out_so_far is not None:
        in_specs.append(pl.BlockSpec(memory_space=pl.ANY))
        args.append(out_so_far)
        aliases = {len(args) - 1: 0}
        kernel_fn = lambda *refs: _final_kernel(*refs[:7], refs[8])
    return pl.pallas_call(
        kernel_fn,
        grid=(yk.shape[1] // tm,),
        in_specs=in_specs,
        out_specs=rows(d),
        out_shape=jax.ShapeDtypeStruct((n, d), F32),
        input_output_aliases=aliases,
        compiler_params=_cparams("parallel"),
        name="final",
    )(*args)


def _layer(h, p, g_mix, w_in, b_f, g_qa, g_ka, g_qb, g_kb, w_o, g_ffn, w_router, b_router,
           w_gate_up, b_gate_up, w_down, b_down, g_ple, w_ple_gate, w_ple_proj):
    batch, seq, d = h.shape
    n = batch * seq
    assert tuple(dil for _, dil in DILATED_PATTERNS) == (1, 4, 16)
    for window, dil in DILATED_PATTERNS:
        per_class = seq // BLOCK // dil
        assert window // dil == BLOCK and seq % (dil * BLOCK) == 0
        assert per_class % UNITS_PER_STEP == 0 or UNITS_PER_STEP % per_class == 0
    assert n % ROW_TILE == 0 and d % (2 * LANES) == 0 and seq % FOX_TILE == 0
    x2 = h.reshape(n, d)

    qkv_cols = 3 * WIDTH_A + 3 * WIDTH_B
    w_qkv = w_in[:, :qkv_cols].astype(BF16)
    w_f = jnp.pad(w_in[:, qkv_cols:], ((0, 0), (0, LANES - N_HEADS_B))).astype(BF16)
    b_fp = jnp.pad(b_f.astype(F32), (0, LANES - N_HEADS_B)).reshape(1, LANES)
    scale = HEAD_DIM ** -0.5
    gains = jnp.stack([jnp.tile(g_qa, N_HEADS_A) * (scale * LOG2E), jnp.tile(g_ka, N_HEADS_A),
                       jnp.tile(g_qb, N_HEADS_B) * (scale * LOG2E), jnp.tile(g_kb, N_HEADS_B)]).astype(F32)
    hid = jnp.arange(2 * LANES) // HEAD_DIM
    bd = (hid[:, None] == hid[None, :]).astype(BF16)

    z, logf = _in_proj(x2, g_mix.reshape(1, d), w_qkv, w_f, b_fp, gains, bd)
    ccol = _cumsum(logf, batch, seq)

    slopes = 2.0 ** (-8.0 * jnp.arange(1, N_HEADS_A + 1, dtype=F32) / N_HEADS_A)
    mix_a = _dilated(z, slopes, batch, seq)
    mix_b = _fox(z, ccol, batch, seq)

    w_r = jnp.pad(w_router.astype(F32), ((0, 0), (0, LANES - N_EXPERTS)))
    w_r_hi = w_r.astype(BF16)
    w_r = jnp.concatenate([w_r_hi, (w_r - w_r_hi.astype(F32)).astype(BF16)], axis=1)
    b_r = jnp.concatenate([b_router.astype(F32), jnp.full((LANES - N_EXPERTS,), NEG_INF, F32)]).reshape(1, LANES)
    h1, u_packed, top_idx, gates, rank, counts = _post_attn(
        mix_a, mix_b, x2, w_o.astype(BF16), g_ffn.reshape(1, d), w_r, b_r)

    counts = counts[0, :N_EXPERTS].astype(jnp.int32)
    tiles_per = (counts + GMM_TILE - 1) // GMM_TILE
    tile_end = jnp.cumsum(tiles_per)
    starts = (tile_end - tiles_per) * GMM_TILE
    n_tiles = n * TOP_K // GMM_TILE + N_EXPERTS
    tile_ids = jnp.arange(n_tiles, dtype=jnp.int32)
    tile_used = (tile_ids < tile_end[-1]).astype(jnp.int32)
    last_used = jnp.minimum(tile_ids, tile_end[-1] - 1)
    tile_expert = jnp.sum((last_used[:, None] >= tile_end[None, :]).astype(jnp.int32), axis=1)
    tile_expert = jnp.minimum(tile_expert, N_EXPERTS - 1)
    experts = jnp.arange(N_EXPERTS, dtype=jnp.int32)[:, None, None]
    pos_t = rank[:TOP_K] + jnp.sum(jnp.where(top_idx[None, :TOP_K] == experts, starts[:, None, None], 0), axis=0)
    padding_tokens = jnp.arange(n_tiles * GMM_TILE, dtype=jnp.int32) % n
    src = _sc_invert(pos_t.reshape(-1), n, padding_tokens)
    early = N_EXPERTS // 4
    prepped = _wprep(w_gate_up, 0, early)
    xs = _sc_gather(u_packed, src)
    wg_t, wl_t = _wprep(w_gate_up, early, N_EXPERTS - early, prepped)

    de = w_down.shape[1]
    ys = _gmm(tile_expert, tile_used, xs, wg_t, wl_t,
              b_gate_up[:, 0::2].reshape(N_EXPERTS, 1, de).astype(F32),
              b_gate_up[:, 1::2].reshape(N_EXPERTS, 1, de).astype(F32),
              w_down, b_down.reshape(N_EXPERTS, 1, d).astype(F32))
    assert n % (COMBINE_PARTS * ROW_TILE) == 0
    part_n = n // COMBINE_PARTS
    out = None
    for part in range(COMBINE_PARTS):
        part_pos = pos_t[:, part * part_n:(part + 1) * part_n].reshape(-1)
        yk = _sc_gather(ys, part_pos).reshape(TOP_K, part_n, d // 2)
        out = _final(h1, yk, gates, p.reshape(n, -1), g_ple.reshape(1, d), w_ple_gate.astype(BF16),
                     w_ple_proj.astype(BF16), part * (part_n // ROW_TILE), out)
    return out.reshape(batch, seq, d)


def kernel(x, p, g_mix, w_in, b_f, g_qa, g_ka, g_qb, g_kb, w_o, g_ffn, w_router, b_router,
           w_gate_up, b_gate_up, w_down, b_down, g_ple, w_ple_gate, w_ple_proj):
    h = x
    for i in range(g_mix.shape[0]):
        h = _layer(h, p[i], g_mix[i], w_in[i], b_f[i], g_qa[i], g_ka[i], g_qb[i], g_kb[i], w_o[i],
                   g_ffn[i], w_router[i], b_router[i], w_gate_up[i], b_gate_up[i], w_down[i],
                   b_down[i], g_ple[i], w_ple_gate[i], w_ple_proj[i])
    return h
```

```python
import dataclasses
import functools

import jax
import jax.numpy as jnp
from jax import lax
from jax.experimental import pallas as pl
from jax.experimental.pallas import tpu as pltpu
from jax.experimental.pallas import tpu_sc as plsc

HEAD_DIM = 64
N_HEADS_A = 8
N_HEADS_B = 8
WIDTH_A = N_HEADS_A * HEAD_DIM
WIDTH_B = N_HEADS_B * HEAD_DIM
DILATED_PATTERNS = ((128, 1), (512, 4), (2048, 16))
BLOCK = 128
N_EXPERTS = 32
TOP_K = 4
SWIGLU_LIMIT = 7.0
SWIGLU_ALPHA = 1.702
NORM_EPS = 1e-6

LANES = 128
PAIR = LANES // HEAD_DIM
ROW_TILE = 512
GMM_TILE = 512
FOX_TILE = 512
UNITS_PER_STEP = 8
SC_CHUNK = 32
SC_DEPTH = 4
SC_SCAN_CHUNK = 16384
SC_SCAN_UNROLL = 8
VMEM_LIMIT = 56 * 1024 * 1024

F32 = jnp.float32
BF16 = jnp.bfloat16
NEG_INF = float("-inf")
NT_DIMS = (((1,), (1,)), ((), ()))
LOG2E = 1.4426950408889634


def _cparams(*sem):
    return pltpu.CompilerParams(dimension_semantics=sem, vmem_limit_bytes=VMEM_LIMIT)


def _rms(x, g):
    return x * lax.rsqrt(jnp.mean(x * x, axis=-1, keepdims=True) + NORM_EPS) * g


def _lane_iota():
    return lax.broadcasted_iota(jnp.int32, (1, LANES), 1)


def _head_lane_mask(h):
    lane = _lane_iota()
    return (lane >= h * HEAD_DIM) & (lane < (h + 1) * HEAD_DIM)


def _merge_heads(acc0, acc1):
    first = _head_lane_mask(0)
    num = jnp.where(first, acc0, acc1)
    den = pltpu.roll(jnp.where(first, acc1, acc0), HEAD_DIM, axis=1)
    return num, den


def _in_proj_kernel(x_ref, g_ref, w_ref, wf_ref, bf_ref, gain_ref, bd_ref, z_ref, lf_ref):
    u = _rms(x_ref[...], g_ref[...]).astype(BF16)
    chunk = WIDTH_A
    normed = {0: 0, 1: 1, 3: 2, 4: 3}
    for c in range(6):
        acc = jnp.dot(u, w_ref[:, c * chunk:(c + 1) * chunk], preferred_element_type=F32)
        if c in normed:
            sq = (acc * acc).astype(BF16)
            half = chunk // 2
            ss = jnp.concatenate(
                [jnp.dot(sq[:, j * half:(j + 1) * half], bd_ref[...], preferred_element_type=F32)
                 for j in range(2)], axis=1)
            r = normed[c]
            acc = acc * lax.rsqrt(ss * (1.0 / HEAD_DIM) + NORM_EPS) * gain_ref[r:r + 1, :]
        z_ref[:, c * chunk:(c + 1) * chunk] = acc.astype(BF16)
    zf = jnp.dot(u, wf_ref[...], preferred_element_type=F32) + bf_ref[...]
    lf_ref[...] = jax.nn.log_sigmoid(zf)


def _in_proj(x2, g_mix, w_qkv, w_f, b_f, gains, bd):
    n, d = x2.shape
    cols = w_qkv.shape[1]
    tm = ROW_TILE
    const = lambda i: (0, 0)
    return pl.pallas_call(
        _in_proj_kernel,
        grid=(n // tm,),
        in_specs=[
            pl.BlockSpec((tm, d), lambda i: (i, 0)),
            pl.BlockSpec((1, d), const),
            pl.BlockSpec((d, cols), const),
            pl.BlockSpec((d, LANES), const),
            pl.BlockSpec((1, LANES), const),
            pl.BlockSpec(gains.shape, const),
            pl.BlockSpec(bd.shape, const),
        ],
        out_specs=[
            pl.BlockSpec((tm, cols), lambda i: (i, 0)),
            pl.BlockSpec((tm, LANES), lambda i: (i, 0)),
        ],
        out_shape=[
            jax.ShapeDtypeStruct((n, cols), BF16),
            jax.ShapeDtypeStruct((n, LANES), F32),
        ],
        compiler_params=_cparams("parallel"),
        name="in_proj",
    )(x2, g_mix, w_qkv, w_f, b_f, gains, bd)


def _cumsum_kernel(lf_ref, tri_ref, cpk_ref):
    s = lf_ref.shape[0]
    lane = _lane_iota()
    carry = jnp.zeros((1, LANES), F32)
    for blk in range(s // BLOCK):
        rows = slice(blk * BLOCK, (blk + 1) * BLOCK)
        part = jnp.dot(tri_ref[...], lf_ref[rows, :], precision=lax.Precision.HIGHEST,
                       preferred_element_type=F32) + carry
        carry = part[BLOCK - 1:BLOCK, :]
        c = part * LOG2E
        hi = c.astype(BF16).astype(F32)
        r1 = c - hi
        mid = r1.astype(BF16).astype(F32)
        lo = r1 - mid
        packed = jnp.where(lane < N_HEADS_B, hi,
                 jnp.where(lane < 2 * N_HEADS_B, pltpu.roll(mid, N_HEADS_B, axis=1),
                 jnp.where(lane < 3 * N_HEADS_B, pltpu.roll(lo, 2 * N_HEADS_B, axis=1),
                 jnp.where(lane == 3 * N_HEADS_B, 1.0, 0.0))))
        cpk_ref[rows, :] = packed.astype(BF16)


def _cumsum(logf, batch, seq):
    tri = (lax.broadcasted_iota(jnp.int32, (BLOCK, BLOCK), 0)
           >= lax.broadcasted_iota(jnp.int32, (BLOCK, BLOCK), 1)).astype(F32)
    return pl.pallas_call(
        _cumsum_kernel,
        grid=(batch,),
        in_specs=[
            pl.BlockSpec((seq, LANES), lambda b: (b, 0)),
            pl.BlockSpec((BLOCK, BLOCK), lambda b: (0, 0)),
        ],
        out_specs=pl.BlockSpec((seq, LANES), lambda b: (b, 0)),
        out_shape=jax.ShapeDtypeStruct((batch * seq, LANES), BF16),
        compiler_params=_cparams("parallel"),
        name="cumsum",
    )(logf, tri)


def _fox_features(cpk, pair, key_side):
    assert PAIR == 2
    r = lax.broadcasted_iota(jnp.int32, (LANES, PAIR * LANES), 0)
    c = lax.broadcasted_iota(jnp.int32, (LANES, PAIR * LANES), 1)
    hh = jnp.where(c >= LANES, 1, 0)
    slot = c - hh * LANES - HEAD_DIM * (1 - hh)
    head = PAIR * pair + hh
    piece_slot = slot - 3 if key_side else slot
    ones_slot = slot if key_side else slot - 3
    piece = (piece_slot >= 0) & (piece_slot < 3) & (r == N_HEADS_B * piece_slot + head)
    ones = (ones_slot >= 0) & (ones_slot < 3) & (r == 3 * N_HEADS_B)
    place = jnp.where(piece, -1.0 if key_side else 1.0, jnp.where(ones, 1.0, 0.0)).astype(BF16)
    return jnp.dot(cpk, place, preferred_element_type=F32).astype(BF16)


def _fox_kernel(q_ref, k_ref, v_ref, c_ref, o_ref, kf, vf, s_scr, *, tile):
    pair = pl.program_id(1)
    i = pl.program_id(2)
    half = tile // 2
    in_head = [_head_lane_mask(h) for h in range(PAIR)]
    block = lambda feat, h: feat[:, h * LANES:(h + 1) * LANES]

    @pl.when(i == 0)
    def _():
        feat = _fox_features(c_ref[...], pair, True)
        for h in range(PAIR):
            kf[h] = jnp.where(in_head[h], k_ref[...], block(feat, h))
            vf[h] = jnp.where(in_head[h], v_ref[...], jnp.ones_like(v_ref[...]))

    def lane_groups_max(s):
        m = s[:, :LANES]
        for g in range(1, s.shape[1] // LANES):
            m = jnp.maximum(m, s[:, g * LANES:(g + 1) * LANES])
        return m

    def scores(qrows, off, width, h):
        return lax.dot_general(qrows, kf[h, off:off + width, :], NT_DIMS, preferred_element_type=F32)

    up_r = lax.broadcasted_iota(jnp.int32, (half, half), 0)
    up_c = lax.broadcasted_iota(jnp.int32, (half, half), 1)
    lo_r = lax.broadcasted_iota(jnp.int32, (half, tile), 0)
    lo_c = lax.broadcasted_iota(jnp.int32, (half, tile), 1)

    def query_tile(nq):
        row0 = nq * tile
        feat_q = _fox_features(c_ref[row0:row0 + tile, :], pair, False)
        q = q_ref[...]
        qf = [jnp.where(in_head[h], q, block(feat_q, h)) for h in range(PAIR)]

        row_max = []
        for h in range(PAIR):
            s_up = jnp.where(up_c <= up_r, scores(qf[h][:half], row0, half, h), NEG_INF)
            s_lo = jnp.where(lo_c <= lo_r + half, scores(qf[h][half:], row0, tile, h), NEG_INF)
            s_scr[h, :half, row0:row0 + half] = s_up
            s_scr[h, half:, row0:row0 + tile] = s_lo
            m_h = jnp.concatenate([lane_groups_max(s_up), lane_groups_max(s_lo)], axis=0)
            for j in range(nq):
                s = scores(qf[h], j * tile, tile, h)
                s_scr[h, :, j * tile:(j + 1) * tile] = s
                m_h = jnp.maximum(m_h, lane_groups_max(s))
            row_max.append(jnp.max(m_h, axis=-1, keepdims=True))

        accs = []
        for h in range(PAIR):
            m = row_max[h]
            p_up = jnp.exp2(s_scr[h, :half, row0:row0 + half] - m[:half]).astype(BF16)
            p_lo = jnp.exp2(s_scr[h, half:, row0:row0 + tile] - m[half:]).astype(BF16)
            acc = jnp.concatenate([
                jnp.dot(p_up, vf[h, row0:row0 + half, :], preferred_element_type=F32),
                jnp.dot(p_lo, vf[h, row0:row0 + tile, :], preferred_element_type=F32)], axis=0)
            for j in range(nq):
                p = jnp.exp2(s_scr[h, :, j * tile:(j + 1) * tile] - m).astype(BF16)
                acc = acc + jnp.dot(p, vf[h, j * tile:(j + 1) * tile, :], preferred_element_type=F32)
            accs.append(acc)
        num, den = _merge_heads(*accs)
        o_ref[...] = (num / den).astype(o_ref.dtype)

    for nq in range(s_scr.shape[2] // tile):
        pl.when(i == nq)(functools.partial(query_tile, nq))


def _fox(z, ccol, batch, seq):
    n = z.shape[0]
    tile = FOX_TILE
    nq = seq // tile
    npair = N_HEADS_B // PAIR
    base = 3 * WIDTH_A // LANES
    qcol, kcol, vcol = base, base + WIDTH_B // LANES, base + 2 * WIDTH_B // LANES
    return pl.pallas_call(
        functools.partial(_fox_kernel, tile=tile),
        grid=(batch, npair, nq),
        in_specs=[
            pl.BlockSpec((tile, LANES), lambda b, p, i: (b * nq + i, qcol + p)),
            pl.BlockSpec((seq, LANES), lambda b, p, i: (b, kcol + p)),
            pl.BlockSpec((seq, LANES), lambda b, p, i: (b, vcol + p)),
            pl.BlockSpec((seq, LANES), lambda b, p, i: (b, 0)),
        ],
        out_specs=pl.BlockSpec((tile, LANES), lambda b, p, i: (b * nq + i, p)),
        out_shape=jax.ShapeDtypeStruct((n, WIDTH_B), BF16),
        scratch_shapes=[
            pltpu.VMEM((PAIR, seq, LANES), BF16),
            pltpu.VMEM((PAIR, seq, LANES), BF16),
            pltpu.VMEM((PAIR, tile, seq), F32),
        ],
        compiler_params=_cparams("parallel", "parallel", "arbitrary"),
        name="fox",
    )(z, z, z, ccol)


def _dilated_kernel(slope_ref, q_ref, k_ref, v_ref, o_ref,
                    natf, p4f, p4b, p16b, qfeat, kfeat, vals, dens, maxs, *, seq):
    pair = pl.program_id(1)
    lane = _lane_iota()
    first = _head_lane_mask(0)
    quarter = seq // 4
    units = seq // BLOCK

    def deinterleave(src, t, span_start, span):
        return [src[t, pl.ds(span_start + r, span // 4, stride=4), :] for r in range(4)]

    for t, ref in enumerate((q_ref, k_ref, v_ref)):
        natf[t] = ref[...].astype(F32)
        for r, part in enumerate(deinterleave(natf, t, 0, seq)):
            p4f[t, pl.ds(r * quarter, quarter), :] = part
            p4b[t, pl.ds(r * quarter, quarter), :] = part.astype(BF16)
        for r4 in range(4):
            for r, part in enumerate(deinterleave(p4f, t, r4 * quarter, quarter)):
                p16b[t, pl.ds(r4 * quarter + r * (quarter // 4), quarter // 4), :] = part.astype(BF16)

    qi = lax.broadcasted_iota(jnp.int32, (BLOCK, LANES), 0).astype(F32)
    kj = lax.broadcasted_iota(jnp.int32, (2 * BLOCK, LANES), 0).astype(F32)
    for p, (_, dil) in enumerate(DILATED_PATTERNS):
        for h in range(PAIR):
            sd = slope_ref[PAIR * pair + h] * float(dil)
            base = HEAD_DIM * (1 - h)
            def pieces_and_ones(value, first_piece_lane, first_one_lane):
                hi = value.astype(BF16).astype(F32)
                mid = (value - hi).astype(BF16).astype(F32)
                lo = value - hi - mid
                return jnp.where(lane == first_piece_lane, hi,
                       jnp.where(lane == first_piece_lane + 1, mid,
                       jnp.where(lane == first_piece_lane + 2, lo,
                       jnp.where((lane >= first_one_lane) & (lane < first_one_lane + 3), 1.0, 0.0)))).astype(BF16)
            qfeat[p * PAIR + h] = pieces_and_ones(-(qi + float(BLOCK)) * (sd * LOG2E), base, base + 3)
            kfeat[p * PAIR + h] = pieces_and_ones(kj * (sd * LOG2E), base + 3, base)

    bq = lax.broadcasted_iota(jnp.int32, (BLOCK, 2 * BLOCK), 0)
    bk = lax.broadcasted_iota(jnp.int32, (BLOCK, 2 * BLOCK), 1)
    rel = bq + BLOCK - bk
    band = (rel >= 0) & (rel <= BLOCK)

    def unit(p, srcs, u, prev_valid):
        qs, ks, vs = srcs
        start = pl.multiple_of(u * BLOCK, BLOCK)
        prev = pl.multiple_of(jnp.maximum(start - BLOCK, 0), BLOCK)
        qb = qs[pl.ds(start, BLOCK), :]
        kk = jnp.concatenate([ks[pl.ds(prev, BLOCK), :], ks[pl.ds(start, BLOCK), :]], axis=0)
        vv = jnp.concatenate([vs[pl.ds(prev, BLOCK), :], vs[pl.ds(start, BLOCK), :]], axis=0)
        if prev_valid is True:
            ok = band
        elif prev_valid is False:
            ok = band & (bk >= BLOCK)
        else:
            ok = band & ((bk >= BLOCK) | prev_valid)
        accs, ms = [], []
        for h in range(PAIR):
            in_h = _head_lane_mask(h)
            qh = jnp.where(in_h, qb, qfeat[p * PAIR + h])
            kh = jnp.where(in_h, kk, kfeat[p * PAIR + h])
            vh = jnp.where(in_h, vv, jnp.ones_like(vv))
            s = lax.dot_general(qh, kh, NT_DIMS, preferred_element_type=F32)
            s = jnp.where(ok, s, NEG_INF)
            m = jnp.max(s, axis=-1, keepdims=True)
            pr = jnp.exp2(s - m).astype(BF16)
            accs.append(jnp.dot(pr, vh, preferred_element_type=F32))
            ms.append(m)
        num, den = _merge_heads(*accs)
        vals[p, pl.ds(start, BLOCK), :] = num
        dens[p, pl.ds(start, BLOCK), :] = den
        maxs[p, pl.ds(start, BLOCK), :] = jnp.where(first, ms[0], ms[1])

    group = UNITS_PER_STEP
    sources = ((q_ref, k_ref, v_ref), tuple(p4b.at[t] for t in range(3)), tuple(p16b.at[t] for t in range(3)))
    for p, (_, dil) in enumerate(DILATED_PATTERNS):
        per_class = units // dil

        def step(g, _, p=p, per_class=per_class):
            for e in range(group):
                u = g * group + e
                if per_class >= group:
                    prev_valid = (u % per_class != 0) if e == 0 else True
                else:
                    prev_valid = e % per_class != 0
                unit(p, sources[p], u, prev_valid)
            return 0
        lax.fori_loop(0, units // group, step, 0)

    for t, arr in enumerate((vals, dens, maxs)):
        for r4 in range(4):
            for r in range(4):
                p4f[t, pl.ds(r4 * quarter + r, quarter // 4, stride=4), :] = \
                    arr[2, pl.ds(r4 * quarter + r * (quarter // 4), quarter // 4), :]

    for r in range(4):
        grouped = pl.ds(r * quarter, quarter)
        natural = pl.ds(r, quarter, stride=4)
        ms = (maxs[0, natural, :], maxs[1, grouped, :], p4f[2, grouped, :])
        vs = (vals[0, natural, :], vals[1, grouped, :], p4f[0, grouped, :])
        ds = (dens[0, natural, :], dens[1, grouped, :], p4f[1, grouped, :])
        m_all = jnp.maximum(jnp.maximum(ms[0], ms[1]), ms[2])
        num = jnp.zeros((quarter, LANES), F32)
        den = jnp.zeros((quarter, LANES), F32)
        for p in range(3):
            e = jnp.exp2(ms[p] - m_all)
            num = num + e * vs[p]
            den = den + e * ds[p]
        natf[0, natural, :] = num / den
    o_ref[...] = natf[0].astype(o_ref.dtype)


def _dilated(z, slopes, batch, seq):
    n = z.shape[0]
    npair = N_HEADS_A // PAIR
    npat = len(DILATED_PATTERNS)
    qcol, kcol, vcol = 0, WIDTH_A // LANES, 2 * WIDTH_A // LANES
    blk = lambda c0: pl.BlockSpec((seq, LANES), lambda b, p: (b, c0 + p))
    return pl.pallas_call(
        functools.partial(_dilated_kernel, seq=seq),
        grid=(batch, npair),
        in_specs=[pl.BlockSpec(memory_space=pltpu.SMEM), blk(qcol), blk(kcol), blk(vcol)],
        out_specs=pl.BlockSpec((seq, LANES), lambda b, p: (b, p)),
        out_shape=jax.ShapeDtypeStruct((n, WIDTH_A), BF16),
        scratch_shapes=[
            pltpu.VMEM((3, seq, LANES), F32),
            pltpu.VMEM((3, seq, LANES), F32),
            pltpu.VMEM((3, seq, LANES), BF16),
            pltpu.VMEM((3, seq, LANES), BF16),
            pltpu.VMEM((npat * PAIR, BLOCK, LANES), BF16),
            pltpu.VMEM((npat * PAIR, 2 * BLOCK, LANES), BF16),
            pltpu.VMEM((npat, seq, LANES), F32),
            pltpu.VMEM((npat, seq, LANES), F32),
            pltpu.VMEM((npat, seq, LANES), F32),
        ],
        compiler_params=_cparams("parallel", "parallel"),
        name="dilated",
    )(slopes, z, z, z)


def _pack_bf16_pairs(a, b):
    hi = pltpu.bitcast(a.astype(BF16).astype(F32), jnp.int32)
    lo = pltpu.bitcast(b.astype(BF16).astype(F32), jnp.int32)
    return (hi & jnp.int32(-65536)) | lax.shift_right_logical(lo, jnp.int32(16))


def _unpack_bf16_pairs(w):
    a = pltpu.bitcast(w & jnp.int32(-65536), F32)
    b = pltpu.bitcast(lax.shift_left(w, jnp.int32(16)), F32)
    return a, b


def _post_attn_kernel(ma_ref, mb_ref, x_ref, wo_ref, g_ref, wr_ref, br_ref, tri_ref,
                      h_ref, up_ref, idx_ref, gate_ref, rank_ref, cnt_ref, carry):
    @pl.when(pl.program_id(0) == 0)
    def _():
        carry[...] = jnp.zeros_like(carry)

    y = jnp.dot(ma_ref[...], wo_ref[:WIDTH_A, :], preferred_element_type=F32)
    y = y + jnp.dot(mb_ref[...], wo_ref[WIDTH_A:, :], preferred_element_type=F32)
    h = x_ref[...] + y
    h_ref[...] = h
    u = _rms(h, g_ref[...])
    half = u.shape[1] // 2
    up_ref[...] = _pack_bf16_pairs(u[:, :half], u[:, half:])

    u_hi = u.astype(BF16)
    u_lo = (u - u_hi.astype(F32)).astype(BF16)
    hi_terms = jnp.dot(u_hi, wr_ref[...], preferred_element_type=F32)
    logits = (hi_terms[:, :LANES] + hi_terms[:, LANES:]
              + jnp.dot(u_lo, wr_ref[:, :LANES], preferred_element_type=F32)) + br_ref[...]
    lane = lax.broadcasted_iota(jnp.int32, logits.shape, 1).astype(F32)
    work = logits
    idxs, tops = [], []
    for _ in range(TOP_K):
        top = jnp.max(work, axis=-1, keepdims=True)
        idx = jnp.min(jnp.where(work == top, lane, float(LANES)), axis=-1, keepdims=True)
        work = jnp.where(lane == idx, NEG_INF, work)
        idxs.append(idx)
        tops.append(top)
    exps = [jnp.exp(t - tops[0]) for t in tops]
    total = exps[0] + exps[1] + exps[2] + exps[3]

    onehot = jnp.zeros(logits.shape, F32)
    for idx in idxs:
        onehot = onehot + (lane == idx).astype(F32)
    before = jnp.dot(tri_ref[...], onehot.astype(BF16), preferred_element_type=F32) + carry[...]
    carry[...] = carry[...] + jnp.sum(onehot, axis=0, keepdims=True)
    cnt_ref[...] = carry[...]

    idx_out = jnp.zeros(logits.shape, F32)
    gate_out = jnp.zeros(logits.shape, F32)
    rank_out = jnp.zeros(logits.shape, F32)
    for k in range(TOP_K):
        rank_k = jnp.sum(jnp.where(lane == idxs[k], before, 0.0), axis=-1, keepdims=True)
        idx_out = jnp.where(lane == float(k), idxs[k], idx_out)
        gate_out = jnp.where(lane == float(k), exps[k] / total, gate_out)
        rank_out = jnp.where(lane == float(k), rank_k, rank_out)
    idx_ref[...] = idx_out.T[:8, :].astype(jnp.int32)
    gate_ref[...] = gate_out
    rank_ref[...] = rank_out.T[:8, :].astype(jnp.int32)


def _post_attn(mix_a, mix_b, x2, w_o, g_ffn, w_r, b_r):
    n, d = x2.shape
    tm = ROW_TILE
    tri = (lax.broadcasted_iota(jnp.int32, (tm, tm), 0)
           > lax.broadcasted_iota(jnp.int32, (tm, tm), 1)).astype(BF16)
    const = lambda i: (0, 0)
    row = lambda w: pl.BlockSpec((tm, w), lambda i: (i, 0))
    lanes_t = pl.BlockSpec((8, tm), lambda i: (0, i))
    return pl.pallas_call(
        _post_attn_kernel,
        grid=(n // tm,),
        in_specs=[
            row(WIDTH_A), row(WIDTH_B), row(d),
            pl.BlockSpec(w_o.shape, const),
            pl.BlockSpec((1, d), const),
            pl.BlockSpec(w_r.shape, const),
            pl.BlockSpec((1, LANES), const),
            pl.BlockSpec((tm, tm), const),
        ],
        out_specs=[row(d), row(d // 2), lanes_t, row(LANES), lanes_t,
                   pl.BlockSpec((1, LANES), const)],
        out_shape=[
            jax.ShapeDtypeStruct((n, d), F32),
            jax.ShapeDtypeStruct((n, d // 2), jnp.int32),
            jax.ShapeDtypeStruct((8, n), jnp.int32),
            jax.ShapeDtypeStruct((n, LANES), F32),
            jax.ShapeDtypeStruct((8, n), jnp.int32),
            jax.ShapeDtypeStruct((1, LANES), F32),
        ],
        scratch_shapes=[pltpu.VMEM((1, LANES), F32)],
        compiler_params=_cparams("arbitrary"),
        name="post_attn",
    )(mix_a, mix_b, x2, w_o, g_ffn, w_r, b_r, tri)


def _wprep_kernel(w_ref, wg_ref, wl_ref, wt):
    d, cols = w_ref.shape[1:]
    de = cols // 2
    for j in range(d // LANES):
        lanes = slice(j * LANES, (j + 1) * LANES)
        wt[j] = w_ref[0, lanes, :].T
        wg_ref[0, :, lanes] = wt[j, pl.ds(0, de, stride=2), :].astype(BF16)
        wl_ref[0, :, lanes] = wt[j, pl.ds(1, de, stride=2), :].astype(BF16)


def _wprep(w_gate_up):
    ne, d, cols = w_gate_up.shape
    de = cols // 2
    out = pl.BlockSpec((1, de, d), lambda e: (e, 0, 0))
    return pl.pallas_call(
        _wprep_kernel,
        grid=(ne,),
        in_specs=[pl.BlockSpec((1, d, cols), lambda e: (e, 0, 0))],
        out_specs=[out, out],
        out_shape=[jax.ShapeDtypeStruct((ne, de, d), BF16)] * 2,
        scratch_shapes=[pltpu.VMEM((d // LANES, cols, LANES), F32)],
        compiler_params=_cparams("parallel"),
        name="wprep",
    )(w_gate_up)


def _sc_invert(keys, n_tokens, fill):
    info = plsc.get_sparse_core_info()
    workers = info.num_cores * info.num_subcores
    lanes = info.num_lanes
    total, = fill.shape
    count, = keys.shape
    chunk, unroll = SC_SCAN_CHUNK, SC_SCAN_UNROLL
    assert total % (workers * lanes) == 0 and count % (2 * chunk) == 0 and n_tokens % chunk == 0
    own = total // workers
    nchunks = count // chunk
    chunks_per_pass = n_tokens // chunk
    mesh = plsc.VectorSubcoreMesh(core_axis_name="c", subcore_axis_name="s")
    params = pltpu.CompilerParams()
    if "needs_layout_passes" in pltpu.CompilerParams.__dataclass_fields__:
        params = dataclasses.replace(params, needs_layout_passes=False)

    @functools.partial(
        pl.kernel, mesh=mesh, compiler_params=params,
        out_type=jax.ShapeDtypeStruct((total,), jnp.int32),
        scratch_types=[
            pltpu.VMEM((own,), jnp.int32),
            pltpu.VMEM((2, chunk), jnp.int32),
            pltpu.SemaphoreType.DMA((2,)),
        ],
    )
    def invert_kernel(keys_hbm, fill_hbm, out_hbm, own_v, key_v, sem):
        wid = lax.axis_index("s") * info.num_cores + lax.axis_index("c")
        lo = wid * own
        lane_ids = lax.broadcasted_iota(jnp.int32, (lanes,), 0)

        def fetch(c, b):
            off = pl.multiple_of(c * chunk, chunk)
            return pltpu.make_async_copy(keys_hbm.at[pl.ds(off, chunk)], key_v.at[b], sem.at[b])

        fetch(0, 0).start()
        pltpu.sync_copy(fill_hbm.at[pl.ds(lo, own)], own_v)

        @pl.loop(0, nchunks, step=2)
        def _(c0):
            for b in range(2):
                c = c0 + b

                @pl.when(c + 1 < nchunks)
                def _():
                    fetch(c + 1, 1 - b).start()

                fetch(c, b).wait()
                first_token = (c % chunks_per_pass) * chunk

                @plsc.parallel_loop(0, chunk // lanes, unroll=unroll)
                def _(j):
                    start = pl.multiple_of(j * lanes, lanes)
                    row = key_v[b, pl.ds(start, lanes)] - lo
                    mine = (row >= 0) & (row < own)
                    plsc.store_scatter(own_v, [jnp.where(mine, row, 0)], first_token + start + lane_ids, mask=mine)

        pltpu.sync_copy(own_v, out_hbm.at[pl.ds(lo, own)])

    return invert_kernel(keys, fill)


def _sc_gather(table, idx):
    info = plsc.get_sparse_core_info()
    workers = info.num_cores * info.num_subcores
    rows, width = idx.shape[0], table.shape[1]
    chunk, depth = SC_CHUNK, SC_DEPTH
    assert rows % (workers * chunk * depth) == 0
    per_worker = rows // workers
    nchunks = per_worker // chunk
    mesh = plsc.VectorSubcoreMesh(core_axis_name="c", subcore_axis_name="s")

    @functools.partial(
        pl.kernel, mesh=mesh,
        out_type=jax.ShapeDtypeStruct((rows, width), table.dtype),
        scratch_types=[
            pltpu.VMEM((nchunks, chunk), jnp.int32),
            pltpu.VMEM((depth, chunk, width), table.dtype),
            pltpu.SemaphoreType.DMA((depth,)),
            pltpu.SemaphoreType.DMA((depth,)),
        ],
    )
    def gather_kernel(table_hbm, idx_hbm, out_hbm, idx_v, rows_v, gsem, wsem):
        wid = lax.axis_index("s") * info.num_cores + lax.axis_index("c")
        base = wid * per_worker
        pltpu.sync_copy(idx_hbm.at[wid], idx_v)

        def gather(c, b):
            return pltpu.make_async_copy(table_hbm.at[idx_v.at[c]], rows_v.at[b], gsem.at[b])

        def write(c, b):
            off = pl.multiple_of(base + c * chunk, chunk)
            return pltpu.make_async_copy(rows_v.at[b], out_hbm.at[pl.ds(off, chunk)], wsem.at[b])

        @pl.loop(0, nchunks, step=depth)
        def _(c0):
            for b in range(depth):
                gather(c0 + b, b).start()
            for b in range(depth):
                gather(c0 + b, b).wait()
                write(c0 + b, b).start()
            for b in range(depth):
                write(c0 + b, b).wait()

    return gather_kernel(table, idx.reshape(workers, nchunks, chunk))


def _gmm_kernel(te_ref, used_ref, xs_ref, wg_ref, wl_ref, bg_ref, bl_ref, wd_ref, bd_ref, ys_ref):
    i = pl.program_id(0)

    @pl.when(used_ref[i] > 0)
    def _():
        a, b = _unpack_bf16_pairs(xs_ref[...])
        x = jnp.concatenate([a, b], axis=1).astype(BF16)
        hg = lax.dot_general(x, wg_ref[0], NT_DIMS, preferred_element_type=F32) + bg_ref[0]
        hl = lax.dot_general(x, wl_ref[0], NT_DIMS, preferred_element_type=F32) + bl_ref[0]
        xg = jnp.minimum(hg, SWIGLU_LIMIT)
        xl = jnp.clip(hl, -SWIGLU_LIMIT, SWIGLU_LIMIT)
        act = xg * jax.nn.sigmoid(SWIGLU_ALPHA * xg) * (xl + 1.0)
        out = jnp.dot(act.astype(BF16), wd_ref[0].astype(BF16), preferred_element_type=F32) + bd_ref[0]
        half = out.shape[1] // 2
        ys_ref[...] = _pack_bf16_pairs(out[:, :half], out[:, half:])

    @pl.when(used_ref[i] == 0)
    def _():
        ys_ref[...] = jnp.zeros_like(ys_ref)


def _gmm(tile_expert, tile_used, xs, wg_t, wl_t, b_glu, b_lin, w_down, b_down):
    rows, half = xs.shape
    d = 2 * half
    de = wg_t.shape[1]
    tm = GMM_TILE
    wspec = lambda shape: pl.BlockSpec((1,) + shape, lambda i, te, used: (te[i], 0, 0))
    grid_spec = pltpu.PrefetchScalarGridSpec(
        num_scalar_prefetch=2,
        grid=(rows // tm,),
        in_specs=[
            pl.BlockSpec((tm, half), lambda i, te, used: (i, 0)),
            wspec((de, d)), wspec((de, d)), wspec((1, de)), wspec((1, de)),
            wspec((de, d)), wspec((1, d)),
        ],
        out_specs=pl.BlockSpec((tm, half), lambda i, te, used: (i, 0)),
    )
    return pl.pallas_call(
        _gmm_kernel,
        grid_spec=grid_spec,
        out_shape=jax.ShapeDtypeStruct((rows, half), jnp.int32),
        compiler_params=_cparams("arbitrary"),
        name="gmm",
    )(tile_expert, tile_used, xs, wg_t, wl_t, b_glu, b_lin, w_down, b_down)


def _final_kernel(h_ref, yk_ref, gate_ref, p_ref, g_ref, wg_ref, wp_ref, o_ref):
    gates = gate_ref[...]
    h = h_ref[...]
    for k in range(TOP_K):
        h = h + gates[:, k:k + 1] * jnp.concatenate(_unpack_bf16_pairs(yk_ref[k]), axis=1)
    u = _rms(h, g_ref[...]).astype(BF16)
    gate = jax.nn.sigmoid(jnp.dot(u, wg_ref[...], preferred_element_type=F32))
    proj = jnp.dot(p_ref[...].astype(BF16), wp_ref[...], preferred_element_type=F32)
    o_ref[...] = h + gate * proj


def _final(h1, yk, gates, p2, g_ple, w_gate, w_proj, first_tile, out_so_far=None):
    n, d = h1.shape
    tm = ROW_TILE
    const = lambda i: (0, 0)
    rows = lambda width: pl.BlockSpec((tm, width), lambda i: (i + first_tile, 0))
    in_specs = [
        rows(d),
        pl.BlockSpec((TOP_K, tm, d // 2), lambda i: (0, i, 0)),
        rows(LANES),
        rows(p2.shape[1]),
        pl.BlockSpec((1, d), const),
        pl.BlockSpec(w_gate.shape, const),
        pl.BlockSpec(w_proj.shape, const),
    ]
    args = [h1, yk, gates, p2, g_ple, w_gate, w_proj]
    kernel_fn, aliases = _final_kernel, {}
    if out_so_far is not None:
        in_specs.append(pl.BlockSpec(memory_space=pl.ANY))
        args.append(out_so_far)
        aliases = {len(args) - 1: 0}
        kernel_fn = lambda *refs: _final_kernel(*refs[:7], refs[8])
    return pl.pallas_call(
        kernel_fn,
        grid=(yk.shape[1] // tm,),
        in_specs=in_specs,
        out_specs=rows(d),
        out_shape=jax.ShapeDtypeStruct((n, d), F32),
        input_output_aliases=aliases,
        compiler_params=_cparams("parallel"),
        name="final",
    )(*args)


def _layer(h, p, g_mix, w_in, b_f, g_qa, g_ka, g_qb, g_kb, w_o, g_ffn, w_router, b_router,
           w_gate_up, b_gate_up, w_down, b_down, g_ple, w_ple_gate, w_ple_proj):
    batch, seq, d = h.shape
    n = batch * seq
    assert tuple(dil for _, dil in DILATED_PATTERNS) == (1, 4, 16)
    for window, dil in DILATED_PATTERNS:
        per_class = seq // BLOCK // dil
        assert window // dil == BLOCK and seq % (dil * BLOCK) == 0
        assert per_class % UNITS_PER_STEP == 0 or UNITS_PER_STEP % per_class == 0
    assert n % ROW_TILE == 0 and d % (2 * LANES) == 0 and seq % FOX_TILE == 0
    x2 = h.reshape(n, d)

    qkv_cols = 3 * WIDTH_A + 3 * WIDTH_B
    w_qkv = w_in[:, :qkv_cols].astype(BF16)
    w_f = jnp.pad(w_in[:, qkv_cols:], ((0, 0), (0, LANES - N_HEADS_B))).astype(BF16)
    b_fp = jnp.pad(b_f.astype(F32), (0, LANES - N_HEADS_B)).reshape(1, LANES)
    scale = HEAD_DIM ** -0.5
    gains = jnp.stack([jnp.tile(g_qa, N_HEADS_A) * (scale * LOG2E), jnp.tile(g_ka, N_HEADS_A),
                       jnp.tile(g_qb, N_HEADS_B) * (scale * LOG2E), jnp.tile(g_kb, N_HEADS_B)]).astype(F32)
    hid = jnp.arange(2 * LANES) // HEAD_DIM
    bd = (hid[:, None] == hid[None, :]).astype(BF16)

    z, logf = _in_proj(x2, g_mix.reshape(1, d), w_qkv, w_f, b_fp, gains, bd)
    ccol = _cumsum(logf, batch, seq)

    slopes = 2.0 ** (-8.0 * jnp.arange(1, N_HEADS_A + 1, dtype=F32) / N_HEADS_A)
    mix_a = _dilated(z, slopes, batch, seq)
    mix_b = _fox(z, ccol, batch, seq)

    w_r = jnp.pad(w_router.astype(F32), ((0, 0), (0, LANES - N_EXPERTS)))
    w_r_hi = w_r.astype(BF16)
    w_r = jnp.concatenate([w_r_hi, (w_r - w_r_hi.astype(F32)).astype(BF16)], axis=1)
    b_r = jnp.concatenate([b_router.astype(F32), jnp.full((LANES - N_EXPERTS,), NEG_INF, F32)]).reshape(1, LANES)
    h1, u_packed, top_idx, gates, rank, counts = _post_attn(
        mix_a, mix_b, x2, w_o.astype(BF16), g_ffn.reshape(1, d), w_r, b_r)

    counts = counts[0, :N_EXPERTS].astype(jnp.int32)
    tiles_per = (counts + GMM_TILE - 1) // GMM_TILE
    tile_end = jnp.cumsum(tiles_per)
    starts = (tile_end - tiles_per) * GMM_TILE
    n_tiles = n * TOP_K // GMM_TILE + N_EXPERTS
    tile_ids = jnp.arange(n_tiles, dtype=jnp.int32)
    tile_used = (tile_ids < tile_end[-1]).astype(jnp.int32)
    last_used = jnp.minimum(tile_ids, tile_end[-1] - 1)
    tile_expert = jnp.sum((last_used[:, None] >= tile_end[None, :]).astype(jnp.int32), axis=1)
    tile_expert = jnp.minimum(tile_expert, N_EXPERTS - 1)
    experts = jnp.arange(N_EXPERTS, dtype=jnp.int32)[:, None, None]
    pos_t = rank[:TOP_K] + jnp.sum(jnp.where(top_idx[None, :TOP_K] == experts, starts[:, None, None], 0), axis=0)
    padding_tokens = jnp.arange(n_tiles * GMM_TILE, dtype=jnp.int32) % n
    src = _sc_invert(pos_t.reshape(-1), n, padding_tokens)
    xs = _sc_gather(u_packed, src)

    de = w_down.shape[1]
    wg_t, wl_t = _wprep(w_gate_up)
    ys = _gmm(tile_expert, tile_used, xs, wg_t, wl_t,
              b_gate_up[:, 0::2].reshape(N_EXPERTS, 1, de).astype(F32),
              b_gate_up[:, 1::2].reshape(N_EXPERTS, 1, de).astype(F32),
              w_down, b_down.reshape(N_EXPERTS, 1, d).astype(F32))
    assert n % (2 * ROW_TILE) == 0
    half_n = n // 2
    out = None
    for part in range(2):
        part_pos = pos_t[:, part * half_n:(part + 1) * half_n].reshape(-1)
        yk = _sc_gather(ys, part_pos).reshape(TOP_K, half_n, d // 2)
        out = _final(h1, yk, gates, p.reshape(n, -1), g_ple.reshape(1, d), w_ple_gate.astype(BF16),
                     w_ple_proj.astype(BF16), part * (half_n // ROW_TILE), out)
    return out.reshape(batch, seq, d)


def kernel(x, p, g_mix, w_in, b_f, g_qa, g_ka, g_qb, g_kb, w_o, g_ffn, w_router, b_router,
           w_gate_up, b_gate_up, w_down, b_down, g_ple, w_ple_gate, w_ple_proj):
    h = x
    for i in range(g_mix.shape[0]):
        h = _layer(h, p[i], g_mix[i], w_in[i], b_f[i], g_qa[i], g_ka[i], g_qb[i], g_kb[i], w_o[i],
                   g_ffn[i], w_router[i], b_router[i], w_gate_up[i], b_gate_up[i], w_down[i],
                   b_down[i], g_ple[i], w_ple_gate[i], w_ple_proj[i])
    return h
```

```python
import dataclasses
import functools

import jax
import jax.numpy as jnp
from jax import lax
from jax.experimental import pallas as pl
from jax.experimental.pallas import tpu as pltpu
from jax.experimental.pallas import tpu_sc as plsc

HEAD_DIM = 64
N_HEADS_A = 8
N_HEADS_B = 8
WIDTH_A = N_HEADS_A * HEAD_DIM
WIDTH_B = N_HEADS_B * HEAD_DIM
DILATED_PATTERNS = ((128, 1), (512, 4), (2048, 16))
BLOCK = 128
N_EXPERTS = 32
TOP_K = 4
SWIGLU_LIMIT = 7.0
SWIGLU_ALPHA = 1.702
NORM_EPS = 1e-6

LANES = 128
PAIR = LANES // HEAD_DIM
ROW_TILE = 512
GMM_TILE = 512
FOX_TILE = 512
UNITS_PER_STEP = 8
SC_CHUNK = 32
SC_DEPTH = 4
SC_SCAN_CHUNK = 16384
SC_SCAN_UNROLL = 8
VMEM_LIMIT = 56 * 1024 * 1024

F32 = jnp.float32
BF16 = jnp.bfloat16
NEG_INF = float("-inf")
NT_DIMS = (((1,), (1,)), ((), ()))
LOG2E = 1.4426950408889634


def _cparams(*sem):
    return pltpu.CompilerParams(dimension_semantics=sem, vmem_limit_bytes=VMEM_LIMIT)


def _rms(x, g):
    return x * lax.rsqrt(jnp.mean(x * x, axis=-1, keepdims=True) + NORM_EPS) * g


def _lane_iota():
    return lax.broadcasted_iota(jnp.int32, (1, LANES), 1)


def _head_lane_mask(h):
    lane = _lane_iota()
    return (lane >= h * HEAD_DIM) & (lane < (h + 1) * HEAD_DIM)


def _merge_heads(acc0, acc1):
    first = _head_lane_mask(0)
    num = jnp.where(first, acc0, acc1)
    den = pltpu.roll(jnp.where(first, acc1, acc0), HEAD_DIM, axis=1)
    return num, den


def _in_proj_kernel(x_ref, g_ref, w_ref, wf_ref, bf_ref, gain_ref, bd_ref, z_ref, lf_ref):
    u = _rms(x_ref[...], g_ref[...]).astype(BF16)
    chunk = WIDTH_A
    normed = {0: 0, 1: 1, 3: 2, 4: 3}
    for c in range(6):
        acc = jnp.dot(u, w_ref[:, c * chunk:(c + 1) * chunk], preferred_element_type=F32)
        if c in normed:
            sq = (acc * acc).astype(BF16)
            half = chunk // 2
            ss = jnp.concatenate(
                [jnp.dot(sq[:, j * half:(j + 1) * half], bd_ref[...], preferred_element_type=F32)
                 for j in range(2)], axis=1)
            r = normed[c]
            acc = acc * lax.rsqrt(ss * (1.0 / HEAD_DIM) + NORM_EPS) * gain_ref[r:r + 1, :]
        z_ref[:, c * chunk:(c + 1) * chunk] = acc.astype(BF16)
    zf = jnp.dot(u, wf_ref[...], preferred_element_type=F32) + bf_ref[...]
    lf_ref[...] = jax.nn.log_sigmoid(zf)


def _in_proj(x2, g_mix, w_qkv, w_f, b_f, gains, bd):
    n, d = x2.shape
    cols = w_qkv.shape[1]
    tm = ROW_TILE
    const = lambda i: (0, 0)
    return pl.pallas_call(
        _in_proj_kernel,
        grid=(n // tm,),
        in_specs=[
            pl.BlockSpec((tm, d), lambda i: (i, 0)),
            pl.BlockSpec((1, d), const),
            pl.BlockSpec((d, cols), const),
            pl.BlockSpec((d, LANES), const),
            pl.BlockSpec((1, LANES), const),
            pl.BlockSpec(gains.shape, const),
            pl.BlockSpec(bd.shape, const),
        ],
        out_specs=[
            pl.BlockSpec((tm, cols), lambda i: (i, 0)),
            pl.BlockSpec((tm, LANES), lambda i: (i, 0)),
        ],
        out_shape=[
            jax.ShapeDtypeStruct((n, cols), BF16),
            jax.ShapeDtypeStruct((n, LANES), F32),
        ],
        compiler_params=_cparams("parallel"),
        name="in_proj",
    )(x2, g_mix, w_qkv, w_f, b_f, gains, bd)


def _cumsum_kernel(lf_ref, tri_ref, cpk_ref):
    s = lf_ref.shape[0]
    lane = _lane_iota()
    carry = jnp.zeros((1, LANES), F32)
    for blk in range(s // BLOCK):
        rows = slice(blk * BLOCK, (blk + 1) * BLOCK)
        part = jnp.dot(tri_ref[...], lf_ref[rows, :], precision=lax.Precision.HIGHEST,
                       preferred_element_type=F32) + carry
        carry = part[BLOCK - 1:BLOCK, :]
        c = part * LOG2E
        hi = c.astype(BF16).astype(F32)
        r1 = c - hi
        mid = r1.astype(BF16).astype(F32)
        lo = r1 - mid
        packed = jnp.where(lane < N_HEADS_B, hi,
                 jnp.where(lane < 2 * N_HEADS_B, pltpu.roll(mid, N_HEADS_B, axis=1),
                 jnp.where(lane < 3 * N_HEADS_B, pltpu.roll(lo, 2 * N_HEADS_B, axis=1),
                 jnp.where(lane == 3 * N_HEADS_B, 1.0, 0.0))))
        cpk_ref[rows, :] = packed.astype(BF16)


def _cumsum(logf, batch, seq):
    tri = (lax.broadcasted_iota(jnp.int32, (BLOCK, BLOCK), 0)
           >= lax.broadcasted_iota(jnp.int32, (BLOCK, BLOCK), 1)).astype(F32)
    return pl.pallas_call(
        _cumsum_kernel,
        grid=(batch,),
        in_specs=[
            pl.BlockSpec((seq, LANES), lambda b: (b, 0)),
            pl.BlockSpec((BLOCK, BLOCK), lambda b: (0, 0)),
        ],
        out_specs=pl.BlockSpec((seq, LANES), lambda b: (b, 0)),
        out_shape=jax.ShapeDtypeStruct((batch * seq, LANES), BF16),
        compiler_params=_cparams("parallel"),
        name="cumsum",
    )(logf, tri)


def _fox_features(cpk, pair, key_side):
    assert PAIR == 2
    r = lax.broadcasted_iota(jnp.int32, (LANES, PAIR * LANES), 0)
    c = lax.broadcasted_iota(jnp.int32, (LANES, PAIR * LANES), 1)
    hh = jnp.where(c >= LANES, 1, 0)
    slot = c - hh * LANES - HEAD_DIM * (1 - hh)
    head = PAIR * pair + hh
    piece_slot = slot - 3 if key_side else slot
    ones_slot = slot if key_side else slot - 3
    piece = (piece_slot >= 0) & (piece_slot < 3) & (r == N_HEADS_B * piece_slot + head)
    ones = (ones_slot >= 0) & (ones_slot < 3) & (r == 3 * N_HEADS_B)
    place = jnp.where(piece, -1.0 if key_side else 1.0, jnp.where(ones, 1.0, 0.0)).astype(BF16)
    return jnp.dot(cpk, place, preferred_element_type=F32).astype(BF16)


def _fox_kernel(q_ref, k_ref, v_ref, c_ref, o_ref, kf, vf, s_scr, *, tile):
    pair = pl.program_id(1)
    i = pl.program_id(2)
    half = tile // 2
    in_head = [_head_lane_mask(h) for h in range(PAIR)]
    block = lambda feat, h: feat[:, h * LANES:(h + 1) * LANES]

    @pl.when(i == 0)
    def _():
        feat = _fox_features(c_ref[...], pair, True)
        for h in range(PAIR):
            kf[h] = jnp.where(in_head[h], k_ref[...], block(feat, h))
            vf[h] = jnp.where(in_head[h], v_ref[...], jnp.ones_like(v_ref[...]))

    def lane_groups_max(s):
        m = s[:, :LANES]
        for g in range(1, s.shape[1] // LANES):
            m = jnp.maximum(m, s[:, g * LANES:(g + 1) * LANES])
        return m

    def scores(qrows, off, width, h):
        return lax.dot_general(qrows, kf[h, off:off + width, :], NT_DIMS, preferred_element_type=F32)

    up_r = lax.broadcasted_iota(jnp.int32, (half, half), 0)
    up_c = lax.broadcasted_iota(jnp.int32, (half, half), 1)
    lo_r = lax.broadcasted_iota(jnp.int32, (half, tile), 0)
    lo_c = lax.broadcasted_iota(jnp.int32, (half, tile), 1)

    def query_tile(nq):
        row0 = nq * tile
        feat_q = _fox_features(c_ref[row0:row0 + tile, :], pair, False)
        q = q_ref[...]
        qf = [jnp.where(in_head[h], q, block(feat_q, h)) for h in range(PAIR)]

        row_max = []
        for h in range(PAIR):
            s_up = jnp.where(up_c <= up_r, scores(qf[h][:half], row0, half, h), NEG_INF)
            s_lo = jnp.where(lo_c <= lo_r + half, scores(qf[h][half:], row0, tile, h), NEG_INF)
            s_scr[h, :half, row0:row0 + half] = s_up
            s_scr[h, half:, row0:row0 + tile] = s_lo
            m_h = jnp.concatenate([lane_groups_max(s_up), lane_groups_max(s_lo)], axis=0)
            for j in range(nq):
                s = scores(qf[h], j * tile, tile, h)
                s_scr[h, :, j * tile:(j + 1) * tile] = s
                m_h = jnp.maximum(m_h, lane_groups_max(s))
            row_max.append(jnp.max(m_h, axis=-1, keepdims=True))

        accs = []
        for h in range(PAIR):
            m = row_max[h]
            p_up = jnp.exp2(s_scr[h, :half, row0:row0 + half] - m[:half]).astype(BF16)
            p_lo = jnp.exp2(s_scr[h, half:, row0:row0 + tile] - m[half:]).astype(BF16)
            acc = jnp.concatenate([
                jnp.dot(p_up, vf[h, row0:row0 + half, :], preferred_element_type=F32),
                jnp.dot(p_lo, vf[h, row0:row0 + tile, :], preferred_element_type=F32)], axis=0)
            for j in range(nq):
                p = jnp.exp2(s_scr[h, :, j * tile:(j + 1) * tile] - m).astype(BF16)
                acc = acc + jnp.dot(p, vf[h, j * tile:(j + 1) * tile, :], preferred_element_type=F32)
            accs.append(acc)
        num, den = _merge_heads(*accs)
        o_ref[...] = (num / den).astype(o_ref.dtype)

    for nq in range(s_scr.shape[2] // tile):
        pl.when(i == nq)(functools.partial(query_tile, nq))


def _fox(z, ccol, batch, seq):
    n = z.shape[0]
    tile = FOX_TILE
    nq = seq // tile
    npair = N_HEADS_B // PAIR
    base = 3 * WIDTH_A // LANES
    qcol, kcol, vcol = base, base + WIDTH_B // LANES, base + 2 * WIDTH_B // LANES
    return pl.pallas_call(
        functools.partial(_fox_kernel, tile=tile),
        grid=(batch, npair, nq),
        in_specs=[
            pl.BlockSpec((tile, LANES), lambda b, p, i: (b * nq + i, qcol + p)),
            pl.BlockSpec((seq, LANES), lambda b, p, i: (b, kcol + p)),
            pl.BlockSpec((seq, LANES), lambda b, p, i: (b, vcol + p)),
            pl.BlockSpec((seq, LANES), lambda b, p, i: (b, 0)),
        ],
        out_specs=pl.BlockSpec((tile, LANES), lambda b, p, i: (b * nq + i, p)),
        out_shape=jax.ShapeDtypeStruct((n, WIDTH_B), BF16),
        scratch_shapes=[
            pltpu.VMEM((PAIR, seq, LANES), BF16),
            pltpu.VMEM((PAIR, seq, LANES), BF16),
            pltpu.VMEM((PAIR, tile, seq), F32),
        ],
        compiler_params=_cparams("parallel", "parallel", "arbitrary"),
        name="fox",
    )(z, z, z, ccol)


def _dilated_kernel(slope_ref, q_ref, k_ref, v_ref, o_ref,
                    natf, p4f, p4b, p16b, bias, bias_first, vals, dens, maxs, *, seq):
    pair = pl.program_id(1)
    lane = _lane_iota()
    first = _head_lane_mask(0)
    quarter = seq // 4
    units = seq // BLOCK

    def deinterleave(src, t, span_start, span):
        return [src[t, pl.ds(span_start + r, span // 4, stride=4), :] for r in range(4)]

    for t, ref in enumerate((q_ref, k_ref, v_ref)):
        natf[t] = ref[...].astype(F32)
        for r, part in enumerate(deinterleave(natf, t, 0, seq)):
            p4f[t, pl.ds(r * quarter, quarter), :] = part
            p4b[t, pl.ds(r * quarter, quarter), :] = part.astype(BF16)
        for r4 in range(4):
            for r, part in enumerate(deinterleave(p4f, t, r4 * quarter, quarter)):
                p16b[t, pl.ds(r4 * quarter + r * (quarter // 4), quarter // 4), :] = part.astype(BF16)

    bq = lax.broadcasted_iota(jnp.int32, (BLOCK, 2 * BLOCK), 0)
    bk = lax.broadcasted_iota(jnp.int32, (BLOCK, 2 * BLOCK), 1)
    rel = bq + BLOCK - bk
    band = (rel >= 0) & (rel <= BLOCK)
    relf = rel.astype(F32)
    for p, (_, dil) in enumerate(DILATED_PATTERNS):
        for h in range(PAIR):
            alibi = relf * (-(slope_ref[PAIR * pair + h] * float(dil)) * LOG2E)
            bias[p * PAIR + h] = jnp.where(band, alibi, NEG_INF)
            bias_first[p * PAIR + h] = jnp.where(band & (bk >= BLOCK), alibi, NEG_INF)

    in_head = [_head_lane_mask(h) for h in range(PAIR)]
    ones = jnp.ones((2 * BLOCK, LANES), BF16)

    def unit(p, srcs, u, prev_valid):
        qs, ks, vs = srcs
        start = pl.multiple_of(u * BLOCK, BLOCK)
        prev = pl.multiple_of(jnp.maximum(start - BLOCK, 0), BLOCK)
        qb = qs[pl.ds(start, BLOCK), :]
        kk = jnp.concatenate([ks[pl.ds(prev, BLOCK), :], ks[pl.ds(start, BLOCK), :]], axis=0)
        vv = jnp.concatenate([vs[pl.ds(prev, BLOCK), :], vs[pl.ds(start, BLOCK), :]], axis=0)
        q2 = jnp.concatenate([jnp.where(in_head[h], qb, jnp.zeros_like(qb)) for h in range(PAIR)], axis=0)
        s2 = lax.dot_general(q2, kk, NT_DIMS, preferred_element_type=F32)
        probs, ms = [], []
        for h in range(PAIR):
            if prev_valid is True:
                b = bias[p * PAIR + h]
            elif prev_valid is False:
                b = bias_first[p * PAIR + h]
            else:
                b = jnp.where(prev_valid, bias[p * PAIR + h], bias_first[p * PAIR + h])
            s = s2[h * BLOCK:(h + 1) * BLOCK] + b
            m = jnp.max(s, axis=-1, keepdims=True)
            probs.append(jnp.exp2(s - m).astype(BF16))
            ms.append(m)
        out = jnp.dot(jnp.concatenate(probs, axis=0), jnp.concatenate([vv, ones], axis=1),
                      preferred_element_type=F32)
        top, bottom = out[:BLOCK], out[BLOCK:]
        vals[p, pl.ds(start, BLOCK), :] = jnp.where(first, top[:, :LANES], bottom[:, :LANES])
        dens[p, pl.ds(start, BLOCK), :] = jnp.where(first, top[:, LANES:], bottom[:, LANES:])
        maxs[p, pl.ds(start, BLOCK), :] = jnp.where(first, ms[0], ms[1])

    group = UNITS_PER_STEP
    sources = ((q_ref, k_ref, v_ref), tuple(p4b.at[t] for t in range(3)), tuple(p16b.at[t] for t in range(3)))
    for p, (_, dil) in enumerate(DILATED_PATTERNS):
        per_class = units // dil

        def step(g, _, p=p, per_class=per_class):
            for e in range(group):
                u = g * group + e
                if per_class >= group:
                    prev_valid = (u % per_class != 0) if e == 0 else True
                else:
                    prev_valid = e % per_class != 0
                unit(p, sources[p], u, prev_valid)
            return 0
        lax.fori_loop(0, units // group, step, 0)

    for t, arr in enumerate((vals, dens, maxs)):
        for r4 in range(4):
            for r in range(4):
                p4f[t, pl.ds(r4 * quarter + r, quarter // 4, stride=4), :] = \
                    arr[2, pl.ds(r4 * quarter + r * (quarter // 4), quarter // 4), :]

    for r in range(4):
        grouped = pl.ds(r * quarter, quarter)
        natural = pl.ds(r, quarter, stride=4)
        ms = (maxs[0, natural, :], maxs[1, grouped, :], p4f[2, grouped, :])
        vs = (vals[0, natural, :], vals[1, grouped, :], p4f[0, grouped, :])
        ds = (dens[0, natural, :], dens[1, grouped, :], p4f[1, grouped, :])
        m_all = jnp.maximum(jnp.maximum(ms[0], ms[1]), ms[2])
        num = jnp.zeros((quarter, LANES), F32)
        den = jnp.zeros((quarter, LANES), F32)
        for p in range(3):
            e = jnp.exp2(ms[p] - m_all)
            num = num + e * vs[p]
            den = den + e * ds[p]
        natf[0, natural, :] = num / den
    o_ref[...] = natf[0].astype(o_ref.dtype)


def _dilated(z, slopes, batch, seq):
    n = z.shape[0]
    npair = N_HEADS_A // PAIR
    npat = len(DILATED_PATTERNS)
    qcol, kcol, vcol = 0, WIDTH_A // LANES, 2 * WIDTH_A // LANES
    blk = lambda c0: pl.BlockSpec((seq, LANES), lambda b, p: (b, c0 + p))
    return pl.pallas_call(
        functools.partial(_dilated_kernel, seq=seq),
        grid=(batch, npair),
        in_specs=[pl.BlockSpec(memory_space=pltpu.SMEM), blk(qcol), blk(kcol), blk(vcol)],
        out_specs=pl.BlockSpec((seq, LANES), lambda b, p: (b, p)),
        out_shape=jax.ShapeDtypeStruct((n, WIDTH_A), BF16),
        scratch_shapes=[
            pltpu.VMEM((3, seq, LANES), F32),
            pltpu.VMEM((3, seq, LANES), F32),
            pltpu.VMEM((3, seq, LANES), BF16),
            pltpu.VMEM((3, seq, LANES), BF16),
            pltpu.VMEM((npat * PAIR, BLOCK, 2 * BLOCK), F32),
            pltpu.VMEM((npat * PAIR, BLOCK, 2 * BLOCK), F32),
            pltpu.VMEM((npat, seq, LANES), F32),
            pltpu.VMEM((npat, seq, LANES), F32),
            pltpu.VMEM((npat, seq, LANES), F32),
        ],
        compiler_params=_cparams("parallel", "parallel"),
        name="dilated",
    )(slopes, z, z, z)


def _pack_bf16_pairs(a, b):
    hi = pltpu.bitcast(a.astype(BF16).astype(F32), jnp.int32)
    lo = pltpu.bitcast(b.astype(BF16).astype(F32), jnp.int32)
    return (hi & jnp.int32(-65536)) | lax.shift_right_logical(lo, jnp.int32(16))


def _unpack_bf16_pairs(w):
    a = pltpu.bitcast(w & jnp.int32(-65536), F32)
    b = pltpu.bitcast(lax.shift_left(w, jnp.int32(16)), F32)
    return a, b


def _post_attn_kernel(ma_ref, mb_ref, x_ref, wo_ref, g_ref, wr_ref, br_ref, tri_ref,
                      h_ref, up_ref, idx_ref, gate_ref, rank_ref, cnt_ref, carry):
    @pl.when(pl.program_id(0) == 0)
    def _():
        carry[...] = jnp.zeros_like(carry)

    y = jnp.dot(ma_ref[...], wo_ref[:WIDTH_A, :], preferred_element_type=F32)
    y = y + jnp.dot(mb_ref[...], wo_ref[WIDTH_A:, :], preferred_element_type=F32)
    h = x_ref[...] + y
    h_ref[...] = h
    u = _rms(h, g_ref[...])
    half = u.shape[1] // 2
    up_ref[...] = _pack_bf16_pairs(u[:, :half], u[:, half:])

    u_hi = u.astype(BF16)
    u_lo = (u - u_hi.astype(F32)).astype(BF16)
    hi_terms = jnp.dot(u_hi, wr_ref[...], preferred_element_type=F32)
    logits = (hi_terms[:, :LANES] + hi_terms[:, LANES:]
              + jnp.dot(u_lo, wr_ref[:, :LANES], preferred_element_type=F32)) + br_ref[...]
    lane = lax.broadcasted_iota(jnp.int32, logits.shape, 1).astype(F32)
    work = logits
    idxs, tops = [], []
    for _ in range(TOP_K):
        top = jnp.max(work, axis=-1, keepdims=True)
        idx = jnp.min(jnp.where(work == top, lane, float(LANES)), axis=-1, keepdims=True)
        work = jnp.where(lane == idx, NEG_INF, work)
        idxs.append(idx)
        tops.append(top)
    exps = [jnp.exp(t - tops[0]) for t in tops]
    total = exps[0] + exps[1] + exps[2] + exps[3]

    onehot = jnp.zeros(logits.shape, F32)
    for idx in idxs:
        onehot = onehot + (lane == idx).astype(F32)
    before = jnp.dot(tri_ref[...], onehot.astype(BF16), preferred_element_type=F32) + carry[...]
    carry[...] = carry[...] + jnp.sum(onehot, axis=0, keepdims=True)
    cnt_ref[...] = carry[...]

    idx_out = jnp.zeros(logits.shape, F32)
    gate_out = jnp.zeros(logits.shape, F32)
    rank_out = jnp.zeros(logits.shape, F32)
    for k in range(TOP_K):
        rank_k = jnp.sum(jnp.where(lane == idxs[k], before, 0.0), axis=-1, keepdims=True)
        idx_out = jnp.where(lane == float(k), idxs[k], idx_out)
        gate_out = jnp.where(lane == float(k), exps[k] / total, gate_out)
        rank_out = jnp.where(lane == float(k), rank_k, rank_out)
    idx_ref[...] = idx_out.T[:8, :].astype(jnp.int32)
    gate_ref[...] = gate_out
    rank_ref[...] = rank_out.T[:8, :].astype(jnp.int32)


def _post_attn(mix_a, mix_b, x2, w_o, g_ffn, w_r, b_r):
    n, d = x2.shape
    tm = ROW_TILE
    tri = (lax.broadcasted_iota(jnp.int32, (tm, tm), 0)
           > lax.broadcasted_iota(jnp.int32, (tm, tm), 1)).astype(BF16)
    const = lambda i: (0, 0)
    row = lambda w: pl.BlockSpec((tm, w), lambda i: (i, 0))
    lanes_t = pl.BlockSpec((8, tm), lambda i: (0, i))
    return pl.pallas_call(
        _post_attn_kernel,
        grid=(n // tm,),
        in_specs=[
            row(WIDTH_A), row(WIDTH_B), row(d),
            pl.BlockSpec(w_o.shape, const),
            pl.BlockSpec((1, d), const),
            pl.BlockSpec(w_r.shape, const),
            pl.BlockSpec((1, LANES), const),
            pl.BlockSpec((tm, tm), const),
        ],
        out_specs=[row(d), row(d // 2), lanes_t, row(LANES), lanes_t,
                   pl.BlockSpec((1, LANES), const)],
        out_shape=[
            jax.ShapeDtypeStruct((n, d), F32),
            jax.ShapeDtypeStruct((n, d // 2), jnp.int32),
            jax.ShapeDtypeStruct((8, n), jnp.int32),
            jax.ShapeDtypeStruct((n, LANES), F32),
            jax.ShapeDtypeStruct((8, n), jnp.int32),
            jax.ShapeDtypeStruct((1, LANES), F32),
        ],
        scratch_shapes=[pltpu.VMEM((1, LANES), F32)],
        compiler_params=_cparams("arbitrary"),
        name="post_attn",
    )(mix_a, mix_b, x2, w_o, g_ffn, w_r, b_r, tri)


def _wprep_kernel(w_ref, wg_ref, wl_ref, wt):
    d, cols = w_ref.shape[1:]
    de = cols // 2
    for j in range(d // LANES):
        lanes = slice(j * LANES, (j + 1) * LANES)
        wt[j] = w_ref[0, lanes, :].T
        wg_ref[0, :, lanes] = wt[j, pl.ds(0, de, stride=2), :].astype(BF16)
        wl_ref[0, :, lanes] = wt[j, pl.ds(1, de, stride=2), :].astype(BF16)


def _wprep(w_gate_up):
    ne, d, cols = w_gate_up.shape
    de = cols // 2
    out = pl.BlockSpec((1, de, d), lambda e: (e, 0, 0))
    return pl.pallas_call(
        _wprep_kernel,
        grid=(ne,),
        in_specs=[pl.BlockSpec((1, d, cols), lambda e: (e, 0, 0))],
        out_specs=[out, out],
        out_shape=[jax.ShapeDtypeStruct((ne, de, d), BF16)] * 2,
        scratch_shapes=[pltpu.VMEM((d // LANES, cols, LANES), F32)],
        compiler_params=_cparams("parallel"),
        name="wprep",
    )(w_gate_up)


def _sc_invert(keys, n_tokens, fill):
    info = plsc.get_sparse_core_info()
    workers = info.num_cores * info.num_subcores
    lanes = info.num_lanes
    total, = fill.shape
    count, = keys.shape
    chunk, unroll = SC_SCAN_CHUNK, SC_SCAN_UNROLL
    assert total % (workers * lanes) == 0 and count % (2 * chunk) == 0 and n_tokens % chunk == 0
    own = total // workers
    nchunks = count // chunk
    chunks_per_pass = n_tokens // chunk
    mesh = plsc.VectorSubcoreMesh(core_axis_name="c", subcore_axis_name="s")
    params = pltpu.CompilerParams()
    if "needs_layout_passes" in pltpu.CompilerParams.__dataclass_fields__:
        params = dataclasses.replace(params, needs_layout_passes=False)

    @functools.partial(
        pl.kernel, mesh=mesh, compiler_params=params,
        out_type=jax.ShapeDtypeStruct((total,), jnp.int32),
        scratch_types=[
            pltpu.VMEM((own,), jnp.int32),
            pltpu.VMEM((2, chunk), jnp.int32),
            pltpu.SemaphoreType.DMA((2,)),
        ],
    )
    def invert_kernel(keys_hbm, fill_hbm, out_hbm, own_v, key_v, sem):
        wid = lax.axis_index("s") * info.num_cores + lax.axis_index("c")
        lo = wid * own
        lane_ids = lax.broadcasted_iota(jnp.int32, (lanes,), 0)

        def fetch(c, b):
            off = pl.multiple_of(c * chunk, chunk)
            return pltpu.make_async_copy(keys_hbm.at[pl.ds(off, chunk)], key_v.at[b], sem.at[b])

        fetch(0, 0).start()
        pltpu.sync_copy(fill_hbm.at[pl.ds(lo, own)], own_v)

        @pl.loop(0, nchunks, step=2)
        def _(c0):
            for b in range(2):
                c = c0 + b

                @pl.when(c + 1 < nchunks)
                def _():
                    fetch(c + 1, 1 - b).start()

                fetch(c, b).wait()
                first_token = (c % chunks_per_pass) * chunk

                @plsc.parallel_loop(0, chunk // lanes, unroll=unroll)
                def _(j):
                    start = pl.multiple_of(j * lanes, lanes)
                    row = key_v[b, pl.ds(start, lanes)] - lo
                    mine = (row >= 0) & (row < own)
                    plsc.store_scatter(own_v, [jnp.where(mine, row, 0)], first_token + start + lane_ids, mask=mine)

        pltpu.sync_copy(own_v, out_hbm.at[pl.ds(lo, own)])

    return invert_kernel(keys, fill)


def _sc_gather(table, idx):
    info = plsc.get_sparse_core_info()
    workers = info.num_cores * info.num_subcores
    rows, width = idx.shape[0], table.shape[1]
    chunk, depth = SC_CHUNK, SC_DEPTH
    assert rows % (workers * chunk * depth) == 0
    per_worker = rows // workers
    nchunks = per_worker // chunk
    mesh = plsc.VectorSubcoreMesh(core_axis_name="c", subcore_axis_name="s")

    @functools.partial(
        pl.kernel, mesh=mesh,
        out_type=jax.ShapeDtypeStruct((rows, width), table.dtype),
        scratch_types=[
            pltpu.VMEM((nchunks, chunk), jnp.int32),
            pltpu.VMEM((depth, chunk, width), table.dtype),
            pltpu.SemaphoreType.DMA((depth,)),
            pltpu.SemaphoreType.DMA((depth,)),
        ],
    )
    def gather_kernel(table_hbm, idx_hbm, out_hbm, idx_v, rows_v, gsem, wsem):
        wid = lax.axis_index("s") * info.num_cores + lax.axis_index("c")
        base = wid * per_worker
        pltpu.sync_copy(idx_hbm.at[wid], idx_v)

        def gather(c, b):
            return pltpu.make_async_copy(table_hbm.at[idx_v.at[c]], rows_v.at[b], gsem.at[b])

        def write(c, b):
            off = pl.multiple_of(base + c * chunk, chunk)
            return pltpu.make_async_copy(rows_v.at[b], out_hbm.at[pl.ds(off, chunk)], wsem.at[b])

        @pl.loop(0, nchunks, step=depth)
        def _(c0):
            for b in range(depth):
                gather(c0 + b, b).start()
            for b in range(depth):
                gather(c0 + b, b).wait()
                write(c0 + b, b).start()
            for b in range(depth):
                write(c0 + b, b).wait()

    return gather_kernel(table, idx.reshape(workers, nchunks, chunk))


def _gmm_kernel(te_ref, used_ref, xs_ref, wg_ref, wl_ref, bg_ref, bl_ref, wd_ref, bd_ref, ys_ref):
    i = pl.program_id(0)

    @pl.when(used_ref[i] > 0)
    def _():
        a, b = _unpack_bf16_pairs(xs_ref[...])
        x = jnp.concatenate([a, b], axis=1).astype(BF16)
        hg = lax.dot_general(x, wg_ref[0], NT_DIMS, preferred_element_type=F32) + bg_ref[0]
        hl = lax.dot_general(x, wl_ref[0], NT_DIMS, preferred_element_type=F32) + bl_ref[0]
        xg = jnp.minimum(hg, SWIGLU_LIMIT)
        xl = jnp.clip(hl, -SWIGLU_LIMIT, SWIGLU_LIMIT)
        act = xg * jax.nn.sigmoid(SWIGLU_ALPHA * xg) * (xl + 1.0)
        out = jnp.dot(act.astype(BF16), wd_ref[0].astype(BF16), preferred_element_type=F32) + bd_ref[0]
        half = out.shape[1] // 2
        ys_ref[...] = _pack_bf16_pairs(out[:, :half], out[:, half:])

    @pl.when(used_ref[i] == 0)
    def _():
        ys_ref[...] = jnp.zeros_like(ys_ref)


def _gmm(tile_expert, tile_used, xs, wg_t, wl_t, b_glu, b_lin, w_down, b_down):
    rows, half = xs.shape
    d = 2 * half
    de = wg_t.shape[1]
    tm = GMM_TILE
    wspec = lambda shape: pl.BlockSpec((1,) + shape, lambda i, te, used: (te[i], 0, 0))
    grid_spec = pltpu.PrefetchScalarGridSpec(
        num_scalar_prefetch=2,
        grid=(rows // tm,),
        in_specs=[
            pl.BlockSpec((tm, half), lambda i, te, used: (i, 0)),
            wspec((de, d)), wspec((de, d)), wspec((1, de)), wspec((1, de)),
            wspec((de, d)), wspec((1, d)),
        ],
        out_specs=pl.BlockSpec((tm, half), lambda i, te, used: (i, 0)),
    )
    return pl.pallas_call(
        _gmm_kernel,
        grid_spec=grid_spec,
        out_shape=jax.ShapeDtypeStruct((rows, half), jnp.int32),
        compiler_params=_cparams("arbitrary"),
        name="gmm",
    )(tile_expert, tile_used, xs, wg_t, wl_t, b_glu, b_lin, w_down, b_down)


def _final_kernel(h_ref, yk_ref, gate_ref, p_ref, g_ref, wg_ref, wp_ref, o_ref):
    gates = gate_ref[...]
    h = h_ref[...]
    for k in range(TOP_K):
        h = h + gates[:, k:k + 1] * jnp.concatenate(_unpack_bf16_pairs(yk_ref[k]), axis=1)
    u = _rms(h, g_ref[...]).astype(BF16)
    gate = jax.nn.sigmoid(jnp.dot(u, wg_ref[...], preferred_element_type=F32))
    proj = jnp.dot(p_ref[...].astype(BF16), wp_ref[...], preferred_element_type=F32)
    o_ref[...] = h + gate * proj


def _final(h1, yk, gates, p2, g_ple, w_gate, w_proj, first_tile, out_so_far=None):
    n, d = h1.shape
    tm = ROW_TILE
    const = lambda i: (0, 0)
    rows = lambda width: pl.BlockSpec((tm, width), lambda i: (i + first_tile, 0))
    in_specs = [
        rows(d),
        pl.BlockSpec((TOP_K, tm, d // 2), lambda i: (0, i, 0)),
        rows(LANES),
        rows(p2.shape[1]),
        pl.BlockSpec((1, d), const),
        pl.BlockSpec(w_gate.shape, const),
        pl.BlockSpec(w_proj.shape, const),
    ]
    args = [h1, yk, gates, p2, g_ple, w_gate, w_proj]
    kernel_fn, aliases = _final_kernel, {}
    if out_so_far is not None:
        in_specs.append(pl.BlockSpec(memory_space=pl.ANY))
        args.append(out_so_far)
        aliases = {len(args) - 1: 0}
        kernel_fn = lambda *refs: _final_kernel(*refs[:7], refs[8])
    return pl.pallas_call(
        kernel_fn,
        grid=(yk.shape[1] // tm,),
        in_specs=in_specs,
        out_specs=rows(d),
        out_shape=jax.ShapeDtypeStruct((n, d), F32),
        input_output_aliases=aliases,
        compiler_params=_cparams("parallel"),
        name="final",
    )(*args)


def _layer(h, p, g_mix, w_in, b_f, g_qa, g_ka, g_qb, g_kb, w_o, g_ffn, w_router, b_router,
           w_gate_up, b_gate_up, w_down, b_down, g_ple, w_ple_gate, w_ple_proj):
    batch, seq, d = h.shape
    n = batch * seq
    assert tuple(dil for _, dil in DILATED_PATTERNS) == (1, 4, 16)
    for window, dil in DILATED_PATTERNS:
        per_class = seq // BLOCK // dil
        assert window // dil == BLOCK and seq % (dil * BLOCK) == 0
        assert per_class % UNITS_PER_STEP == 0 or UNITS_PER_STEP % per_class == 0
    assert n % ROW_TILE == 0 and d % (2 * LANES) == 0 and seq % FOX_TILE == 0
    x2 = h.reshape(n, d)

    qkv_cols = 3 * WIDTH_A + 3 * WIDTH_B
    w_qkv = w_in[:, :qkv_cols].astype(BF16)
    w_f = jnp.pad(w_in[:, qkv_cols:], ((0, 0), (0, LANES - N_HEADS_B))).astype(BF16)
    b_fp = jnp.pad(b_f.astype(F32), (0, LANES - N_HEADS_B)).reshape(1, LANES)
    scale = HEAD_DIM ** -0.5
    gains = jnp.stack([jnp.tile(g_qa, N_HEADS_A) * (scale * LOG2E), jnp.tile(g_ka, N_HEADS_A),
                       jnp.tile(g_qb, N_HEADS_B) * (scale * LOG2E), jnp.tile(g_kb, N_HEADS_B)]).astype(F32)
    hid = jnp.arange(2 * LANES) // HEAD_DIM
    bd = (hid[:, None] == hid[None, :]).astype(BF16)

    z, logf = _in_proj(x2, g_mix.reshape(1, d), w_qkv, w_f, b_fp, gains, bd)
    ccol = _cumsum(logf, batch, seq)

    slopes = 2.0 ** (-8.0 * jnp.arange(1, N_HEADS_A + 1, dtype=F32) / N_HEADS_A)
    mix_a = _dilated(z, slopes, batch, seq)
    mix_b = _fox(z, ccol, batch, seq)

    w_r = jnp.pad(w_router.astype(F32), ((0, 0), (0, LANES - N_EXPERTS)))
    w_r_hi = w_r.astype(BF16)
    w_r = jnp.concatenate([w_r_hi, (w_r - w_r_hi.astype(F32)).astype(BF16)], axis=1)
    b_r = jnp.concatenate([b_router.astype(F32), jnp.full((LANES - N_EXPERTS,), NEG_INF, F32)]).reshape(1, LANES)
    h1, u_packed, top_idx, gates, rank, counts = _post_attn(
        mix_a, mix_b, x2, w_o.astype(BF16), g_ffn.reshape(1, d), w_r, b_r)

    counts = counts[0, :N_EXPERTS].astype(jnp.int32)
    tiles_per = (counts + GMM_TILE - 1) // GMM_TILE
    tile_end = jnp.cumsum(tiles_per)
    starts = (tile_end - tiles_per) * GMM_TILE
    n_tiles = n * TOP_K // GMM_TILE + N_EXPERTS
    tile_ids = jnp.arange(n_tiles, dtype=jnp.int32)
    tile_used = (tile_ids < tile_end[-1]).astype(jnp.int32)
    last_used = jnp.minimum(tile_ids, tile_end[-1] - 1)
    tile_expert = jnp.sum((last_used[:, None] >= tile_end[None, :]).astype(jnp.int32), axis=1)
    tile_expert = jnp.minimum(tile_expert, N_EXPERTS - 1)
    experts = jnp.arange(N_EXPERTS, dtype=jnp.int32)[:, None, None]
    pos_t = rank[:TOP_K] + jnp.sum(jnp.where(top_idx[None, :TOP_K] == experts, starts[:, None, None], 0), axis=0)
    padding_tokens = jnp.arange(n_tiles * GMM_TILE, dtype=jnp.int32) % n
    src = _sc_invert(pos_t.reshape(-1), n, padding_tokens)
    xs = _sc_gather(u_packed, src)

    de = w_down.shape[1]
    wg_t, wl_t = _wprep(w_gate_up)
    ys = _gmm(tile_expert, tile_used, xs, wg_t, wl_t,
              b_gate_up[:, 0::2].reshape(N_EXPERTS, 1, de).astype(F32),
              b_gate_up[:, 1::2].reshape(N_EXPERTS, 1, de).astype(F32),
              w_down, b_down.reshape(N_EXPERTS, 1, d).astype(F32))
    assert n % (2 * ROW_TILE) == 0
    half_n = n // 2
    out = None
    for part in range(2):
        part_pos = pos_t[:, part * half_n:(part + 1) * half_n].reshape(-1)
        yk = _sc_gather(ys, part_pos).reshape(TOP_K, half_n, d // 2)
        out = _final(h1, yk, gates, p.reshape(n, -1), g_ple.reshape(1, d), w_ple_gate.astype(BF16),
                     w_ple_proj.astype(BF16), part * (half_n // ROW_TILE), out)
    return out.reshape(batch, seq, d)


def kernel(x, p, g_mix, w_in, b_f, g_qa, g_ka, g_qb, g_kb, w_o, g_ffn, w_router, b_router,
           w_gate_up, b_gate_up, w_down, b_down, g_ple, w_ple_gate, w_ple_proj):
    h = x
    for i in range(g_mix.shape[0]):
        h = _layer(h, p[i], g_mix[i], w_in[i], b_f[i], g_qa[i], g_ka[i], g_qb[i], g_kb[i], w_o[i],
                   g_ffn[i], w_router[i], b_router[i], w_gate_up[i], b_gate_up[i], w_down[i],
                   b_down[i], g_ple[i], w_ple_gate[i], w_ple_proj[i])
    return h
```

```python
import dataclasses
import functools

import jax
import jax.numpy as jnp
from jax import lax
from jax.experimental import pallas as pl
from jax.experimental.pallas import tpu as pltpu
from jax.experimental.pallas import tpu_sc as plsc

HEAD_DIM = 64
N_HEADS_A = 8
N_HEADS_B = 8
WIDTH_A = N_HEADS_A * HEAD_DIM
WIDTH_B = N_HEADS_B * HEAD_DIM
DILATED_PATTERNS = ((128, 1), (512, 4), (2048, 16))
BLOCK = 128
N_EXPERTS = 32
TOP_K = 4
SWIGLU_LIMIT = 7.0
SWIGLU_ALPHA = 1.702
NORM_EPS = 1e-6

LANES = 128
PAIR = LANES // HEAD_DIM
ROW_TILE = 512
GMM_TILE = 512
FOX_TILE = 512
UNITS_PER_STEP = 16
SC_CHUNK = 32
SC_DEPTH = 4
SC_SCAN_CHUNK = 16384
SC_SCAN_UNROLL = 8
VMEM_LIMIT = 56 * 1024 * 1024

F32 = jnp.float32
BF16 = jnp.bfloat16
NEG_INF = float("-inf")
NT_DIMS = (((1,), (1,)), ((), ()))
LOG2E = 1.4426950408889634


def _cparams(*sem):
    return pltpu.CompilerParams(dimension_semantics=sem, vmem_limit_bytes=VMEM_LIMIT)


def _rms(x, g):
    return x * lax.rsqrt(jnp.mean(x * x, axis=-1, keepdims=True) + NORM_EPS) * g


def _lane_iota():
    return lax.broadcasted_iota(jnp.int32, (1, LANES), 1)


def _head_lane_mask(h):
    lane = _lane_iota()
    return (lane >= h * HEAD_DIM) & (lane < (h + 1) * HEAD_DIM)


def _merge_heads(acc0, acc1):
    first = _head_lane_mask(0)
    num = jnp.where(first, acc0, acc1)
    den = pltpu.roll(jnp.where(first, acc1, acc0), HEAD_DIM, axis=1)
    return num, den


def _in_proj_kernel(x_ref, g_ref, w_ref, wf_ref, bf_ref, gain_ref, bd_ref, z_ref, lf_ref):
    u = _rms(x_ref[...], g_ref[...]).astype(BF16)
    chunk = WIDTH_A
    normed = {0: 0, 1: 1, 3: 2, 4: 3}
    for c in range(6):
        acc = jnp.dot(u, w_ref[:, c * chunk:(c + 1) * chunk], preferred_element_type=F32)
        if c in normed:
            sq = (acc * acc).astype(BF16)
            half = chunk // 2
            ss = jnp.concatenate(
                [jnp.dot(sq[:, j * half:(j + 1) * half], bd_ref[...], preferred_element_type=F32)
                 for j in range(2)], axis=1)
            r = normed[c]
            acc = acc * lax.rsqrt(ss * (1.0 / HEAD_DIM) + NORM_EPS) * gain_ref[r:r + 1, :]
        z_ref[:, c * chunk:(c + 1) * chunk] = acc.astype(BF16)
    zf = jnp.dot(u, wf_ref[...], preferred_element_type=F32) + bf_ref[...]
    lf_ref[...] = jax.nn.log_sigmoid(zf)


def _in_proj(x2, g_mix, w_qkv, w_f, b_f, gains, bd):
    n, d = x2.shape
    cols = w_qkv.shape[1]
    tm = ROW_TILE
    const = lambda i: (0, 0)
    return pl.pallas_call(
        _in_proj_kernel,
        grid=(n // tm,),
        in_specs=[
            pl.BlockSpec((tm, d), lambda i: (i, 0)),
            pl.BlockSpec((1, d), const),
            pl.BlockSpec((d, cols), const),
            pl.BlockSpec((d, LANES), const),
            pl.BlockSpec((1, LANES), const),
            pl.BlockSpec(gains.shape, const),
            pl.BlockSpec(bd.shape, const),
        ],
        out_specs=[
            pl.BlockSpec((tm, cols), lambda i: (i, 0)),
            pl.BlockSpec((tm, LANES), lambda i: (i, 0)),
        ],
        out_shape=[
            jax.ShapeDtypeStruct((n, cols), BF16),
            jax.ShapeDtypeStruct((n, LANES), F32),
        ],
        compiler_params=_cparams("parallel"),
        name="in_proj",
    )(x2, g_mix, w_qkv, w_f, b_f, gains, bd)


def _cumsum_kernel(lf_ref, tri_ref, cpk_ref):
    s = lf_ref.shape[0]
    lane = _lane_iota()
    carry = jnp.zeros((1, LANES), F32)
    for blk in range(s // BLOCK):
        rows = slice(blk * BLOCK, (blk + 1) * BLOCK)
        part = jnp.dot(tri_ref[...], lf_ref[rows, :], precision=lax.Precision.HIGHEST,
                       preferred_element_type=F32) + carry
        carry = part[BLOCK - 1:BLOCK, :]
        c = part * LOG2E
        hi = c.astype(BF16).astype(F32)
        r1 = c - hi
        mid = r1.astype(BF16).astype(F32)
        lo = r1 - mid
        packed = jnp.where(lane < N_HEADS_B, hi,
                 jnp.where(lane < 2 * N_HEADS_B, pltpu.roll(mid, N_HEADS_B, axis=1),
                 jnp.where(lane < 3 * N_HEADS_B, pltpu.roll(lo, 2 * N_HEADS_B, axis=1),
                 jnp.where(lane == 3 * N_HEADS_B, 1.0, 0.0))))
        cpk_ref[rows, :] = packed.astype(BF16)


def _cumsum(logf, batch, seq):
    tri = (lax.broadcasted_iota(jnp.int32, (BLOCK, BLOCK), 0)
           >= lax.broadcasted_iota(jnp.int32, (BLOCK, BLOCK), 1)).astype(F32)
    return pl.pallas_call(
        _cumsum_kernel,
        grid=(batch,),
        in_specs=[
            pl.BlockSpec((seq, LANES), lambda b: (b, 0)),
            pl.BlockSpec((BLOCK, BLOCK), lambda b: (0, 0)),
        ],
        out_specs=pl.BlockSpec((seq, LANES), lambda b: (b, 0)),
        out_shape=jax.ShapeDtypeStruct((batch * seq, LANES), BF16),
        compiler_params=_cparams("parallel"),
        name="cumsum",
    )(logf, tri)


def _fox_features(cpk, pair, key_side):
    assert PAIR == 2
    r = lax.broadcasted_iota(jnp.int32, (LANES, PAIR * LANES), 0)
    c = lax.broadcasted_iota(jnp.int32, (LANES, PAIR * LANES), 1)
    hh = jnp.where(c >= LANES, 1, 0)
    slot = c - hh * LANES - HEAD_DIM * (1 - hh)
    head = PAIR * pair + hh
    piece_slot = slot - 3 if key_side else slot
    ones_slot = slot if key_side else slot - 3
    piece = (piece_slot >= 0) & (piece_slot < 3) & (r == N_HEADS_B * piece_slot + head)
    ones = (ones_slot >= 0) & (ones_slot < 3) & (r == 3 * N_HEADS_B)
    place = jnp.where(piece, -1.0 if key_side else 1.0, jnp.where(ones, 1.0, 0.0)).astype(BF16)
    return jnp.dot(cpk, place, preferred_element_type=F32).astype(BF16)


def _fox_kernel(q_ref, k_ref, v_ref, c_ref, o_ref, kf, vf, s_scr, *, tile):
    pair = pl.program_id(1)
    i = pl.program_id(2)
    half = tile // 2
    in_head = [_head_lane_mask(h) for h in range(PAIR)]
    block = lambda feat, h: feat[:, h * LANES:(h + 1) * LANES]

    @pl.when(i == 0)
    def _():
        feat = _fox_features(c_ref[...], pair, True)
        for h in range(PAIR):
            kf[h] = jnp.where(in_head[h], k_ref[...], block(feat, h))
            vf[h] = jnp.where(in_head[h], v_ref[...], jnp.ones_like(v_ref[...]))

    def lane_groups_max(s):
        m = s[:, :LANES]
        for g in range(1, s.shape[1] // LANES):
            m = jnp.maximum(m, s[:, g * LANES:(g + 1) * LANES])
        return m

    def scores(qrows, off, width, h):
        return lax.dot_general(qrows, kf[h, off:off + width, :], NT_DIMS, preferred_element_type=F32)

    up_r = lax.broadcasted_iota(jnp.int32, (half, half), 0)
    up_c = lax.broadcasted_iota(jnp.int32, (half, half), 1)
    lo_r = lax.broadcasted_iota(jnp.int32, (half, tile), 0)
    lo_c = lax.broadcasted_iota(jnp.int32, (half, tile), 1)

    def query_tile(nq):
        row0 = nq * tile
        feat_q = _fox_features(c_ref[row0:row0 + tile, :], pair, False)
        q = q_ref[...]
        qf = [jnp.where(in_head[h], q, block(feat_q, h)) for h in range(PAIR)]

        row_max = []
        for h in range(PAIR):
            s_up = jnp.where(up_c <= up_r, scores(qf[h][:half], row0, half, h), NEG_INF)
            s_lo = jnp.where(lo_c <= lo_r + half, scores(qf[h][half:], row0, tile, h), NEG_INF)
            s_scr[h, :half, row0:row0 + half] = s_up
            s_scr[h, half:, row0:row0 + tile] = s_lo
            m_h = jnp.concatenate([lane_groups_max(s_up), lane_groups_max(s_lo)], axis=0)
            for j in range(nq):
                s = scores(qf[h], j * tile, tile, h)
                s_scr[h, :, j * tile:(j + 1) * tile] = s
                m_h = jnp.maximum(m_h, lane_groups_max(s))
            row_max.append(jnp.max(m_h, axis=-1, keepdims=True))

        accs = []
        for h in range(PAIR):
            m = row_max[h]
            p_up = jnp.exp2(s_scr[h, :half, row0:row0 + half] - m[:half]).astype(BF16)
            p_lo = jnp.exp2(s_scr[h, half:, row0:row0 + tile] - m[half:]).astype(BF16)
            acc = jnp.concatenate([
                jnp.dot(p_up, vf[h, row0:row0 + half, :], preferred_element_type=F32),
                jnp.dot(p_lo, vf[h, row0:row0 + tile, :], preferred_element_type=F32)], axis=0)
            for j in range(nq):
                p = jnp.exp2(s_scr[h, :, j * tile:(j + 1) * tile] - m).astype(BF16)
                acc = acc + jnp.dot(p, vf[h, j * tile:(j + 1) * tile, :], preferred_element_type=F32)
            accs.append(acc)
        num, den = _merge_heads(*accs)
        o_ref[...] = (num / den).astype(o_ref.dtype)

    for nq in range(s_scr.shape[2] // tile):
        pl.when(i == nq)(functools.partial(query_tile, nq))


def _fox(z, ccol, batch, seq):
    n = z.shape[0]
    tile = FOX_TILE
    nq = seq // tile
    npair = N_HEADS_B // PAIR
    base = 3 * WIDTH_A // LANES
    qcol, kcol, vcol = base, base + WIDTH_B // LANES, base + 2 * WIDTH_B // LANES
    return pl.pallas_call(
        functools.partial(_fox_kernel, tile=tile),
        grid=(batch, npair, nq),
        in_specs=[
            pl.BlockSpec((tile, LANES), lambda b, p, i: (b * nq + i, qcol + p)),
            pl.BlockSpec((seq, LANES), lambda b, p, i: (b, kcol + p)),
            pl.BlockSpec((seq, LANES), lambda b, p, i: (b, vcol + p)),
            pl.BlockSpec((seq, LANES), lambda b, p, i: (b, 0)),
        ],
        out_specs=pl.BlockSpec((tile, LANES), lambda b, p, i: (b * nq + i, p)),
        out_shape=jax.ShapeDtypeStruct((n, WIDTH_B), BF16),
        scratch_shapes=[
            pltpu.VMEM((PAIR, seq, LANES), BF16),
            pltpu.VMEM((PAIR, seq, LANES), BF16),
            pltpu.VMEM((PAIR, tile, seq), F32),
        ],
        compiler_params=_cparams("parallel", "parallel", "arbitrary"),
        name="fox",
    )(z, z, z, ccol)


def _dilated_kernel(slope_ref, q_ref, k_ref, v_ref, o_ref,
                    natf, p4f, p4b, p16b, bias, bias_first, vals, dens, maxs, *, seq):
    pair = pl.program_id(1)
    lane = _lane_iota()
    first = _head_lane_mask(0)
    quarter = seq // 4
    units = seq // BLOCK

    def deinterleave(src, t, span_start, span):
        return [src[t, pl.ds(span_start + r, span // 4, stride=4), :] for r in range(4)]

    for t, ref in enumerate((q_ref, k_ref, v_ref)):
        natf[t] = ref[...].astype(F32)
        for r, part in enumerate(deinterleave(natf, t, 0, seq)):
            p4f[t, pl.ds(r * quarter, quarter), :] = part
            p4b[t, pl.ds(r * quarter, quarter), :] = part.astype(BF16)
        for r4 in range(4):
            for r, part in enumerate(deinterleave(p4f, t, r4 * quarter, quarter)):
                p16b[t, pl.ds(r4 * quarter + r * (quarter // 4), quarter // 4), :] = part.astype(BF16)

    bq = lax.broadcasted_iota(jnp.int32, (BLOCK, 2 * BLOCK), 0)
    bk = lax.broadcasted_iota(jnp.int32, (BLOCK, 2 * BLOCK), 1)
    rel = bq + BLOCK - bk
    band = (rel >= 0) & (rel <= BLOCK)
    relf = rel.astype(F32)
    for p, (_, dil) in enumerate(DILATED_PATTERNS):
        for h in range(PAIR):
            alibi = relf * (-(slope_ref[PAIR * pair + h] * float(dil)) * LOG2E)
            bias[p * PAIR + h] = jnp.where(band, alibi, NEG_INF)
            bias_first[p * PAIR + h] = jnp.where(band & (bk >= BLOCK), alibi, NEG_INF)

    in_head = [_head_lane_mask(h) for h in range(PAIR)]
    ones = jnp.ones((2 * BLOCK, LANES), BF16)

    def unit(p, srcs, u, prev_valid):
        qs, ks, vs = srcs
        start = pl.multiple_of(u * BLOCK, BLOCK)
        prev = pl.multiple_of(jnp.maximum(start - BLOCK, 0), BLOCK)
        qb = qs[pl.ds(start, BLOCK), :]
        kk = jnp.concatenate([ks[pl.ds(prev, BLOCK), :], ks[pl.ds(start, BLOCK), :]], axis=0)
        vv = jnp.concatenate([vs[pl.ds(prev, BLOCK), :], vs[pl.ds(start, BLOCK), :]], axis=0)
        q2 = jnp.concatenate([jnp.where(in_head[h], qb, jnp.zeros_like(qb)) for h in range(PAIR)], axis=0)
        s2 = lax.dot_general(q2, kk, NT_DIMS, preferred_element_type=F32)
        probs, ms = [], []
        for h in range(PAIR):
            if prev_valid is True:
                b = bias[p * PAIR + h]
            elif prev_valid is False:
                b = bias_first[p * PAIR + h]
            else:
                b = jnp.where(prev_valid, bias[p * PAIR + h], bias_first[p * PAIR + h])
            s = s2[h * BLOCK:(h + 1) * BLOCK] + b
            m = jnp.max(s, axis=-1, keepdims=True)
            probs.append(jnp.exp2(s - m).astype(BF16))
            ms.append(m)
        out = jnp.dot(jnp.concatenate(probs, axis=0), jnp.concatenate([vv, ones], axis=1),
                      preferred_element_type=F32)
        top, bottom = out[:BLOCK], out[BLOCK:]
        vals[p, pl.ds(start, BLOCK), :] = jnp.where(first, top[:, :LANES], bottom[:, :LANES])
        dens[p, pl.ds(start, BLOCK), :] = jnp.where(first, top[:, LANES:], bottom[:, LANES:])
        maxs[p, pl.ds(start, BLOCK), :] = jnp.where(first, ms[0], ms[1])

    group = UNITS_PER_STEP
    sources = ((q_ref, k_ref, v_ref), tuple(p4b.at[t] for t in range(3)), tuple(p16b.at[t] for t in range(3)))
    for p, (_, dil) in enumerate(DILATED_PATTERNS):
        per_class = units // dil

        def step(g, _, p=p, per_class=per_class):
            for e in range(group):
                u = g * group + e
                if per_class >= group:
                    prev_valid = (u % per_class != 0) if e == 0 else True
                else:
                    prev_valid = e % per_class != 0
                unit(p, sources[p], u, prev_valid)
            return 0
        lax.fori_loop(0, units // group, step, 0)

    for t, arr in enumerate((vals, dens, maxs)):
        for r4 in range(4):
            for r in range(4):
                p4f[t, pl.ds(r4 * quarter + r, quarter // 4, stride=4), :] = \
                    arr[2, pl.ds(r4 * quarter + r * (quarter // 4), quarter // 4), :]

    for r in range(4):
        grouped = pl.ds(r * quarter, quarter)
        natural = pl.ds(r, quarter, stride=4)
        ms = (maxs[0, natural, :], maxs[1, grouped, :], p4f[2, grouped, :])
        vs = (vals[0, natural, :], vals[1, grouped, :], p4f[0, grouped, :])
        ds = (dens[0, natural, :], dens[1, grouped, :], p4f[1, grouped, :])
        m_all = jnp.maximum(jnp.maximum(ms[0], ms[1]), ms[2])
        num = jnp.zeros((quarter, LANES), F32)
        den = jnp.zeros((quarter, LANES), F32)
        for p in range(3):
            e = jnp.exp2(ms[p] - m_all)
            num = num + e * vs[p]
            den = den + e * ds[p]
        natf[0, natural, :] = num / den
    o_ref[...] = natf[0].astype(o_ref.dtype)


def _dilated(z, slopes, batch, seq):
    n = z.shape[0]
    npair = N_HEADS_A // PAIR
    npat = len(DILATED_PATTERNS)
    qcol, kcol, vcol = 0, WIDTH_A // LANES, 2 * WIDTH_A // LANES
    blk = lambda c0: pl.BlockSpec((seq, LANES), lambda b, p: (b, c0 + p))
    return pl.pallas_call(
        functools.partial(_dilated_kernel, seq=seq),
        grid=(batch, npair),
        in_specs=[pl.BlockSpec(memory_space=pltpu.SMEM), blk(qcol), blk(kcol), blk(vcol)],
        out_specs=pl.BlockSpec((seq, LANES), lambda b, p: (b, p)),
        out_shape=jax.ShapeDtypeStruct((n, WIDTH_A), BF16),
        scratch_shapes=[
            pltpu.VMEM((3, seq, LANES), F32),
            pltpu.VMEM((3, seq, LANES), F32),
            pltpu.VMEM((3, seq, LANES), BF16),
            pltpu.VMEM((3, seq, LANES), BF16),
            pltpu.VMEM((npat * PAIR, BLOCK, 2 * BLOCK), F32),
            pltpu.VMEM((npat * PAIR, BLOCK, 2 * BLOCK), F32),
            pltpu.VMEM((npat, seq, LANES), F32),
            pltpu.VMEM((npat, seq, LANES), F32),
            pltpu.VMEM((npat, seq, LANES), F32),
        ],
        compiler_params=_cparams("parallel", "parallel"),
        name="dilated",
    )(slopes, z, z, z)


def _pack_bf16_pairs(a, b):
    hi = pltpu.bitcast(a.astype(BF16).astype(F32), jnp.int32)
    lo = pltpu.bitcast(b.astype(BF16).astype(F32), jnp.int32)
    return (hi & jnp.int32(-65536)) | lax.shift_right_logical(lo, jnp.int32(16))


def _unpack_bf16_pairs(w):
    a = pltpu.bitcast(w & jnp.int32(-65536), F32)
    b = pltpu.bitcast(lax.shift_left(w, jnp.int32(16)), F32)
    return a, b


def _post_attn_kernel(ma_ref, mb_ref, x_ref, wo_ref, g_ref, wr_ref, br_ref, tri_ref,
                      h_ref, up_ref, idx_ref, gate_ref, rank_ref, cnt_ref, carry):
    @pl.when(pl.program_id(0) == 0)
    def _():
        carry[...] = jnp.zeros_like(carry)

    y = jnp.dot(ma_ref[...], wo_ref[:WIDTH_A, :], preferred_element_type=F32)
    y = y + jnp.dot(mb_ref[...], wo_ref[WIDTH_A:, :], preferred_element_type=F32)
    h = x_ref[...] + y
    h_ref[...] = h
    u = _rms(h, g_ref[...])
    half = u.shape[1] // 2
    up_ref[...] = _pack_bf16_pairs(u[:, :half], u[:, half:])

    u_hi = u.astype(BF16)
    u_lo = (u - u_hi.astype(F32)).astype(BF16)
    hi_terms = jnp.dot(u_hi, wr_ref[...], preferred_element_type=F32)
    logits = (hi_terms[:, :LANES] + hi_terms[:, LANES:]
              + jnp.dot(u_lo, wr_ref[:, :LANES], preferred_element_type=F32)) + br_ref[...]
    lane = lax.broadcasted_iota(jnp.int32, logits.shape, 1).astype(F32)
    work = logits
    idxs, tops = [], []
    for _ in range(TOP_K):
        top = jnp.max(work, axis=-1, keepdims=True)
        idx = jnp.min(jnp.where(work == top, lane, float(LANES)), axis=-1, keepdims=True)
        work = jnp.where(lane == idx, NEG_INF, work)
        idxs.append(idx)
        tops.append(top)
    exps = [jnp.exp(t - tops[0]) for t in tops]
    total = exps[0] + exps[1] + exps[2] + exps[3]

    onehot = jnp.zeros(logits.shape, F32)
    for idx in idxs:
        onehot = onehot + (lane == idx).astype(F32)
    before = jnp.dot(tri_ref[...], onehot.astype(BF16), preferred_element_type=F32) + carry[...]
    carry[...] = carry[...] + jnp.sum(onehot, axis=0, keepdims=True)
    cnt_ref[...] = carry[...]

    idx_out = jnp.zeros(logits.shape, F32)
    gate_out = jnp.zeros(logits.shape, F32)
    rank_out = jnp.zeros(logits.shape, F32)
    for k in range(TOP_K):
        rank_k = jnp.sum(jnp.where(lane == idxs[k], before, 0.0), axis=-1, keepdims=True)
        idx_out = jnp.where(lane == float(k), idxs[k], idx_out)
        gate_out = jnp.where(lane == float(k), exps[k] / total, gate_out)
        rank_out = jnp.where(lane == float(k), rank_k, rank_out)
    idx_ref[...] = idx_out.T[:8, :].astype(jnp.int32)
    gate_ref[...] = gate_out
    rank_ref[...] = rank_out.T[:8, :].astype(jnp.int32)


def _post_attn(mix_a, mix_b, x2, w_o, g_ffn, w_r, b_r):
    n, d = x2.shape
    tm = ROW_TILE
    tri = (lax.broadcasted_iota(jnp.int32, (tm, tm), 0)
           > lax.broadcasted_iota(jnp.int32, (tm, tm), 1)).astype(BF16)
    const = lambda i: (0, 0)
    row = lambda w: pl.BlockSpec((tm, w), lambda i: (i, 0))
    lanes_t = pl.BlockSpec((8, tm), lambda i: (0, i))
    return pl.pallas_call(
        _post_attn_kernel,
        grid=(n // tm,),
        in_specs=[
            row(WIDTH_A), row(WIDTH_B), row(d),
            pl.BlockSpec(w_o.shape, const),
            pl.BlockSpec((1, d), const),
            pl.BlockSpec(w_r.shape, const),
            pl.BlockSpec((1, LANES), const),
            pl.BlockSpec((tm, tm), const),
        ],
        out_specs=[row(d), row(d // 2), lanes_t, row(LANES), lanes_t,
                   pl.BlockSpec((1, LANES), const)],
        out_shape=[
            jax.ShapeDtypeStruct((n, d), F32),
            jax.ShapeDtypeStruct((n, d // 2), jnp.int32),
            jax.ShapeDtypeStruct((8, n), jnp.int32),
            jax.ShapeDtypeStruct((n, LANES), F32),
            jax.ShapeDtypeStruct((8, n), jnp.int32),
            jax.ShapeDtypeStruct((1, LANES), F32),
        ],
        scratch_shapes=[pltpu.VMEM((1, LANES), F32)],
        compiler_params=_cparams("arbitrary"),
        name="post_attn",
    )(mix_a, mix_b, x2, w_o, g_ffn, w_r, b_r, tri)


def _wprep_kernel(w_ref, wg_ref, wl_ref, wt):
    d, cols = w_ref.shape[1:]
    de = cols // 2
    for j in range(d // LANES):
        lanes = slice(j * LANES, (j + 1) * LANES)
        wt[j] = w_ref[0, lanes, :].T
        wg_ref[0, :, lanes] = wt[j, pl.ds(0, de, stride=2), :].astype(BF16)
        wl_ref[0, :, lanes] = wt[j, pl.ds(1, de, stride=2), :].astype(BF16)


def _wprep(w_gate_up):
    ne, d, cols = w_gate_up.shape
    de = cols // 2
    out = pl.BlockSpec((1, de, d), lambda e: (e, 0, 0))
    return pl.pallas_call(
        _wprep_kernel,
        grid=(ne,),
        in_specs=[pl.BlockSpec((1, d, cols), lambda e: (e, 0, 0))],
        out_specs=[out, out],
        out_shape=[jax.ShapeDtypeStruct((ne, de, d), BF16)] * 2,
        scratch_shapes=[pltpu.VMEM((d // LANES, cols, LANES), F32)],
        compiler_params=_cparams("parallel"),
        name="wprep",
    )(w_gate_up)


def _sc_invert(keys, n_tokens, fill):
    info = plsc.get_sparse_core_info()
    workers = info.num_cores * info.num_subcores
    lanes = info.num_lanes
    total, = fill.shape
    count, = keys.shape
    chunk, unroll = SC_SCAN_CHUNK, SC_SCAN_UNROLL
    assert total % (workers * lanes) == 0 and count % (2 * chunk) == 0 and n_tokens % chunk == 0
    own = total // workers
    nchunks = count // chunk
    chunks_per_pass = n_tokens // chunk
    mesh = plsc.VectorSubcoreMesh(core_axis_name="c", subcore_axis_name="s")
    params = pltpu.CompilerParams()
    if "needs_layout_passes" in pltpu.CompilerParams.__dataclass_fields__:
        params = dataclasses.replace(params, needs_layout_passes=False)

    @functools.partial(
        pl.kernel, mesh=mesh, compiler_params=params,
        out_type=jax.ShapeDtypeStruct((total,), jnp.int32),
        scratch_types=[
            pltpu.VMEM((own,), jnp.int32),
            pltpu.VMEM((2, chunk), jnp.int32),
            pltpu.SemaphoreType.DMA((2,)),
        ],
    )
    def invert_kernel(keys_hbm, fill_hbm, out_hbm, own_v, key_v, sem):
        wid = lax.axis_index("s") * info.num_cores + lax.axis_index("c")
        lo = wid * own
        lane_ids = lax.broadcasted_iota(jnp.int32, (lanes,), 0)

        def fetch(c, b):
            off = pl.multiple_of(c * chunk, chunk)
            return pltpu.make_async_copy(keys_hbm.at[pl.ds(off, chunk)], key_v.at[b], sem.at[b])

        fetch(0, 0).start()
        pltpu.sync_copy(fill_hbm.at[pl.ds(lo, own)], own_v)

        @pl.loop(0, nchunks, step=2)
        def _(c0):
            for b in range(2):
                c = c0 + b

                @pl.when(c + 1 < nchunks)
                def _():
                    fetch(c + 1, 1 - b).start()

                fetch(c, b).wait()
                first_token = (c % chunks_per_pass) * chunk

                @plsc.parallel_loop(0, chunk // lanes, unroll=unroll)
                def _(j):
                    start = pl.multiple_of(j * lanes, lanes)
                    row = key_v[b, pl.ds(start, lanes)] - lo
                    mine = (row >= 0) & (row < own)
                    plsc.store_scatter(own_v, [jnp.where(mine, row, 0)], first_token + start + lane_ids, mask=mine)

        pltpu.sync_copy(own_v, out_hbm.at[pl.ds(lo, own)])

    return invert_kernel(keys, fill)


def _sc_gather(table, idx):
    info = plsc.get_sparse_core_info()
    workers = info.num_cores * info.num_subcores
    rows, width = idx.shape[0], table.shape[1]
    chunk, depth = SC_CHUNK, SC_DEPTH
    assert rows % (workers * chunk * depth) == 0
    per_worker = rows // workers
    nchunks = per_worker // chunk
    mesh = plsc.VectorSubcoreMesh(core_axis_name="c", subcore_axis_name="s")

    @functools.partial(
        pl.kernel, mesh=mesh,
        out_type=jax.ShapeDtypeStruct((rows, width), table.dtype),
        scratch_types=[
            pltpu.VMEM((nchunks, chunk), jnp.int32),
            pltpu.VMEM((depth, chunk, width), table.dtype),
            pltpu.SemaphoreType.DMA((depth,)),
            pltpu.SemaphoreType.DMA((depth,)),
        ],
    )
    def gather_kernel(table_hbm, idx_hbm, out_hbm, idx_v, rows_v, gsem, wsem):
        wid = lax.axis_index("s") * info.num_cores + lax.axis_index("c")
        base = wid * per_worker
        pltpu.sync_copy(idx_hbm.at[wid], idx_v)

        def gather(c, b):
            return pltpu.make_async_copy(table_hbm.at[idx_v.at[c]], rows_v.at[b], gsem.at[b])

        def write(c, b):
            off = pl.multiple_of(base + c * chunk, chunk)
            return pltpu.make_async_copy(rows_v.at[b], out_hbm.at[pl.ds(off, chunk)], wsem.at[b])

        @pl.loop(0, nchunks, step=depth)
        def _(c0):
            for b in range(depth):
                gather(c0 + b, b).start()
            for b in range(depth):
                gather(c0 + b, b).wait()
                write(c0 + b, b).start()
            for b in range(depth):
                write(c0 + b, b).wait()

    return gather_kernel(table, idx.reshape(workers, nchunks, chunk))


def _gmm_kernel(te_ref, used_ref, xs_ref, wg_ref, wl_ref, bg_ref, bl_ref, wd_ref, bd_ref, ys_ref):
    i = pl.program_id(0)

    @pl.when(used_ref[i] > 0)
    def _():
        a, b = _unpack_bf16_pairs(xs_ref[...])
        x = jnp.concatenate([a, b], axis=1).astype(BF16)
        hg = lax.dot_general(x, wg_ref[0], NT_DIMS, preferred_element_type=F32) + bg_ref[0]
        hl = lax.dot_general(x, wl_ref[0], NT_DIMS, preferred_element_type=F32) + bl_ref[0]
        xg = jnp.minimum(hg, SWIGLU_LIMIT)
        xl = jnp.clip(hl, -SWIGLU_LIMIT, SWIGLU_LIMIT)
        act = xg * jax.nn.sigmoid(SWIGLU_ALPHA * xg) * (xl + 1.0)
        out = jnp.dot(act.astype(BF16), wd_ref[0].astype(BF16), preferred_element_type=F32) + bd_ref[0]
        half = out.shape[1] // 2
        ys_ref[...] = _pack_bf16_pairs(out[:, :half], out[:, half:])

    @pl.when(used_ref[i] == 0)
    def _():
        ys_ref[...] = jnp.zeros_like(ys_ref)


def _gmm(tile_expert, tile_used, xs, wg_t, wl_t, b_glu, b_lin, w_down, b_down):
    rows, half = xs.shape
    d = 2 * half
    de = wg_t.shape[1]
    tm = GMM_TILE
    wspec = lambda shape: pl.BlockSpec((1,) + shape, lambda i, te, used: (te[i], 0, 0))
    grid_spec = pltpu.PrefetchScalarGridSpec(
        num_scalar_prefetch=2,
        grid=(rows // tm,),
        in_specs=[
            pl.BlockSpec((tm, half), lambda i, te, used: (i, 0)),
            wspec((de, d)), wspec((de, d)), wspec((1, de)), wspec((1, de)),
            wspec((de, d)), wspec((1, d)),
        ],
        out_specs=pl.BlockSpec((tm, half), lambda i, te, used: (i, 0)),
    )
    return pl.pallas_call(
        _gmm_kernel,
        grid_spec=grid_spec,
        out_shape=jax.ShapeDtypeStruct((rows, half), jnp.int32),
        compiler_params=_cparams("arbitrary"),
        name="gmm",
    )(tile_expert, tile_used, xs, wg_t, wl_t, b_glu, b_lin, w_down, b_down)


def _final_kernel(h_ref, yk_ref, gate_ref, p_ref, g_ref, wg_ref, wp_ref, o_ref):
    gates = gate_ref[...]
    h = h_ref[...]
    for k in range(TOP_K):
        h = h + gates[:, k:k + 1] * jnp.concatenate(_unpack_bf16_pairs(yk_ref[k]), axis=1)
    u = _rms(h, g_ref[...]).astype(BF16)
    gate = jax.nn.sigmoid(jnp.dot(u, wg_ref[...], preferred_element_type=F32))
    proj = jnp.dot(p_ref[...].astype(BF16), wp_ref[...], preferred_element_type=F32)
    o_ref[...] = h + gate * proj


def _final(h1, yk, gates, p2, g_ple, w_gate, w_proj, first_tile, out_so_far=None):
    n, d = h1.shape
    tm = ROW_TILE
    const = lambda i: (0, 0)
    rows = lambda width: pl.BlockSpec((tm, width), lambda i: (i + first_tile, 0))
    in_specs = [
        rows(d),
        pl.BlockSpec((TOP_K, tm, d // 2), lambda i: (0, i, 0)),
        rows(LANES),
        rows(p2.shape[1]),
        pl.BlockSpec((1, d), const),
        pl.BlockSpec(w_gate.shape, const),
        pl.BlockSpec(w_proj.shape, const),
    ]
    args = [h1, yk, gates, p2, g_ple, w_gate, w_proj]
    kernel_fn, aliases = _final_kernel, {}
    if out_so_far is not None:
        in_specs.append(pl.BlockSpec(memory_space=pl.ANY))
        args.append(out_so_far)
        aliases = {len(args) - 1: 0}
        kernel_fn = lambda *refs: _final_kernel(*refs[:7], refs[8])
    return pl.pallas_call(
        kernel_fn,
        grid=(yk.shape[1] // tm,),
        in_specs=in_specs,
        out_specs=rows(d),
        out_shape=jax.ShapeDtypeStruct((n, d), F32),
        input_output_aliases=aliases,
        compiler_params=_cparams("parallel"),
        name="final",
    )(*args)


def _layer(h, p, g_mix, w_in, b_f, g_qa, g_ka, g_qb, g_kb, w_o, g_ffn, w_router, b_router,
           w_gate_up, b_gate_up, w_down, b_down, g_ple, w_ple_gate, w_ple_proj):
    batch, seq, d = h.shape
    n = batch * seq
    assert tuple(dil for _, dil in DILATED_PATTERNS) == (1, 4, 16)
    for window, dil in DILATED_PATTERNS:
        per_class = seq // BLOCK // dil
        assert window // dil == BLOCK and seq % (dil * BLOCK) == 0
        assert per_class % UNITS_PER_STEP == 0 or UNITS_PER_STEP % per_class == 0
    assert n % ROW_TILE == 0 and d % (2 * LANES) == 0 and seq % FOX_TILE == 0
    x2 = h.reshape(n, d)

    qkv_cols = 3 * WIDTH_A + 3 * WIDTH_B
    w_qkv = w_in[:, :qkv_cols].astype(BF16)
    w_f = jnp.pad(w_in[:, qkv_cols:], ((0, 0), (0, LANES - N_HEADS_B))).astype(BF16)
    b_fp = jnp.pad(b_f.astype(F32), (0, LANES - N_HEADS_B)).reshape(1, LANES)
    scale = HEAD_DIM ** -0.5
    gains = jnp.stack([jnp.tile(g_qa, N_HEADS_A) * (scale * LOG2E), jnp.tile(g_ka, N_HEADS_A),
                       jnp.tile(g_qb, N_HEADS_B) * (scale * LOG2E), jnp.tile(g_kb, N_HEADS_B)]).astype(F32)
    hid = jnp.arange(2 * LANES) // HEAD_DIM
    bd = (hid[:, None] == hid[None, :]).astype(BF16)

    z, logf = _in_proj(x2, g_mix.reshape(1, d), w_qkv, w_f, b_fp, gains, bd)
    ccol = _cumsum(logf, batch, seq)

    slopes = 2.0 ** (-8.0 * jnp.arange(1, N_HEADS_A + 1, dtype=F32) / N_HEADS_A)
    mix_a = _dilated(z, slopes, batch, seq)
    mix_b = _fox(z, ccol, batch, seq)

    w_r = jnp.pad(w_router.astype(F32), ((0, 0), (0, LANES - N_EXPERTS)))
    w_r_hi = w_r.astype(BF16)
    w_r = jnp.concatenate([w_r_hi, (w_r - w_r_hi.astype(F32)).astype(BF16)], axis=1)
    b_r = jnp.concatenate([b_router.astype(F32), jnp.full((LANES - N_EXPERTS,), NEG_INF, F32)]).reshape(1, LANES)
    h1, u_packed, top_idx, gates, rank, counts = _post_attn(
        mix_a, mix_b, x2, w_o.astype(BF16), g_ffn.reshape(1, d), w_r, b_r)

    counts = counts[0, :N_EXPERTS].astype(jnp.int32)
    tiles_per = (counts + GMM_TILE - 1) // GMM_TILE
    tile_end = jnp.cumsum(tiles_per)
    starts = (tile_end - tiles_per) * GMM_TILE
    n_tiles = n * TOP_K // GMM_TILE + N_EXPERTS
    tile_ids = jnp.arange(n_tiles, dtype=jnp.int32)
    tile_used = (tile_ids < tile_end[-1]).astype(jnp.int32)
    last_used = jnp.minimum(tile_ids, tile_end[-1] - 1)
    tile_expert = jnp.sum((last_used[:, None] >= tile_end[None, :]).astype(jnp.int32), axis=1)
    tile_expert = jnp.minimum(tile_expert, N_EXPERTS - 1)
    experts = jnp.arange(N_EXPERTS, dtype=jnp.int32)[:, None, None]
    pos_t = rank[:TOP_K] + jnp.sum(jnp.where(top_idx[None, :TOP_K] == experts, starts[:, None, None], 0), axis=0)
    padding_tokens = jnp.arange(n_tiles * GMM_TILE, dtype=jnp.int32) % n
    src = _sc_invert(pos_t.reshape(-1), n, padding_tokens)
    xs = _sc_gather(u_packed, src)

    de = w_down.shape[1]
    wg_t, wl_t = _wprep(w_gate_up)
    ys = _gmm(tile_expert, tile_used, xs, wg_t, wl_t,
              b_gate_up[:, 0::2].reshape(N_EXPERTS, 1, de).astype(F32),
              b_gate_up[:, 1::2].reshape(N_EXPERTS, 1, de).astype(F32),
              w_down, b_down.reshape(N_EXPERTS, 1, d).astype(F32))
    assert n % (2 * ROW_TILE) == 0
    half_n = n // 2
    out = None
    for part in range(2):
        part_pos = pos_t[:, part * half_n:(part + 1) * half_n].reshape(-1)
        yk = _sc_gather(ys, part_pos).reshape(TOP_K, half_n, d // 2)
        out = _final(h1, yk, gates, p.reshape(n, -1), g_ple.reshape(1, d), w_ple_gate.astype(BF16),
                     w_ple_proj.astype(BF16), part * (half_n // ROW_TILE), out)
    return out.reshape(batch, seq, d)


def kernel(x, p, g_mix, w_in, b_f, g_qa, g_ka, g_qb, g_kb, w_o, g_ffn, w_router, b_router,
           w_gate_up, b_gate_up, w_down, b_down, g_ple, w_ple_gate, w_ple_proj):
    h = x
    for i in range(g_mix.shape[0]):
        h = _layer(h, p[i], g_mix[i], w_in[i], b_f[i], g_qa[i], g_ka[i], g_qb[i], g_kb[i], w_o[i],
                   g_ffn[i], w_router[i], b_router[i], w_gate_up[i], b_gate_up[i], w_down[i],
                   b_down[i], g_ple[i], w_ple_gate[i], w_ple_proj[i])
    return h
```

```python
import dataclasses
import functools

import jax
import jax.numpy as jnp
from jax import lax
from jax.experimental import pallas as pl
from jax.experimental.pallas import tpu as pltpu
from jax.experimental.pallas import tpu_sc as plsc

HEAD_DIM = 64
N_HEADS_A = 8
N_HEADS_B = 8
WIDTH_A = N_HEADS_A * HEAD_DIM
WIDTH_B = N_HEADS_B * HEAD_DIM
DILATED_PATTERNS = ((128, 1), (512, 4), (2048, 16))
BLOCK = 128
N_EXPERTS = 32
TOP_K = 4
SWIGLU_LIMIT = 7.0
SWIGLU_ALPHA = 1.702
NORM_EPS = 1e-6

LANES = 128
PAIR = LANES // HEAD_DIM
ROW_TILE = 512
GMM_TILE = 512
FOX_TILE = 512
UNITS_PER_STEP = 16
SC_CHUNK = 32
SC_DEPTH = 4
SC_SCAN_CHUNK = 16384
SC_SCAN_UNROLL = 8
VMEM_LIMIT = 56 * 1024 * 1024

F32 = jnp.float32
BF16 = jnp.bfloat16
NEG_INF = float("-inf")
NT_DIMS = (((1,), (1,)), ((), ()))
LOG2E = 1.4426950408889634


def _cparams(*sem):
    return pltpu.CompilerParams(dimension_semantics=sem, vmem_limit_bytes=VMEM_LIMIT)


def _rms(x, g):
    return x * lax.rsqrt(jnp.mean(x * x, axis=-1, keepdims=True) + NORM_EPS) * g


def _lane_iota():
    return lax.broadcasted_iota(jnp.int32, (1, LANES), 1)


def _head_lane_mask(h):
    lane = _lane_iota()
    return (lane >= h * HEAD_DIM) & (lane < (h + 1) * HEAD_DIM)


def _merge_heads(acc0, acc1):
    first = _head_lane_mask(0)
    num = jnp.where(first, acc0, acc1)
    den = pltpu.roll(jnp.where(first, acc1, acc0), HEAD_DIM, axis=1)
    return num, den


def _in_proj_kernel(x_ref, g_ref, w_ref, wf_ref, bf_ref, gain_ref, bd_ref, z_ref, lf_ref):
    u = _rms(x_ref[...], g_ref[...]).astype(BF16)
    chunk = WIDTH_A
    normed = {0: 0, 1: 1, 3: 2, 4: 3}
    for c in range(6):
        acc = jnp.dot(u, w_ref[:, c * chunk:(c + 1) * chunk], preferred_element_type=F32)
        if c in normed:
            sq = (acc * acc).astype(BF16)
            half = chunk // 2
            ss = jnp.concatenate(
                [jnp.dot(sq[:, j * half:(j + 1) * half], bd_ref[...], preferred_element_type=F32)
                 for j in range(2)], axis=1)
            r = normed[c]
            acc = acc * lax.rsqrt(ss * (1.0 / HEAD_DIM) + NORM_EPS) * gain_ref[r:r + 1, :]
        z_ref[:, c * chunk:(c + 1) * chunk] = acc.astype(BF16)
    zf = jnp.dot(u, wf_ref[...], preferred_element_type=F32) + bf_ref[...]
    lf_ref[...] = jax.nn.log_sigmoid(zf)


def _in_proj(x2, g_mix, w_qkv, w_f, b_f, gains, bd):
    n, d = x2.shape
    cols = w_qkv.shape[1]
    tm = ROW_TILE
    const = lambda i: (0, 0)
    return pl.pallas_call(
        _in_proj_kernel,
        grid=(n // tm,),
        in_specs=[
            pl.BlockSpec((tm, d), lambda i: (i, 0)),
            pl.BlockSpec((1, d), const),
            pl.BlockSpec((d, cols), const),
            pl.BlockSpec((d, LANES), const),
            pl.BlockSpec((1, LANES), const),
            pl.BlockSpec(gains.shape, const),
            pl.BlockSpec(bd.shape, const),
        ],
        out_specs=[
            pl.BlockSpec((tm, cols), lambda i: (i, 0)),
            pl.BlockSpec((tm, LANES), lambda i: (i, 0)),
        ],
        out_shape=[
            jax.ShapeDtypeStruct((n, cols), BF16),
            jax.ShapeDtypeStruct((n, LANES), F32),
        ],
        compiler_params=_cparams("parallel"),
        name="in_proj",
    )(x2, g_mix, w_qkv, w_f, b_f, gains, bd)


def _cumsum_kernel(lf_ref, tri_ref, cpk_ref):
    s = lf_ref.shape[0]
    lane = _lane_iota()
    carry = jnp.zeros((1, LANES), F32)
    for blk in range(s // BLOCK):
        rows = slice(blk * BLOCK, (blk + 1) * BLOCK)
        part = jnp.dot(tri_ref[...], lf_ref[rows, :], precision=lax.Precision.HIGHEST,
                       preferred_element_type=F32) + carry
        carry = part[BLOCK - 1:BLOCK, :]
        c = part * LOG2E
        hi = c.astype(BF16).astype(F32)
        r1 = c - hi
        mid = r1.astype(BF16).astype(F32)
        lo = r1 - mid
        packed = jnp.where(lane < N_HEADS_B, hi,
                 jnp.where(lane < 2 * N_HEADS_B, pltpu.roll(mid, N_HEADS_B, axis=1),
                 jnp.where(lane < 3 * N_HEADS_B, pltpu.roll(lo, 2 * N_HEADS_B, axis=1),
                 jnp.where(lane == 3 * N_HEADS_B, 1.0, 0.0))))
        cpk_ref[rows, :] = packed.astype(BF16)


def _cumsum(logf, batch, seq):
    tri = (lax.broadcasted_iota(jnp.int32, (BLOCK, BLOCK), 0)
           >= lax.broadcasted_iota(jnp.int32, (BLOCK, BLOCK), 1)).astype(F32)
    return pl.pallas_call(
        _cumsum_kernel,
        grid=(batch,),
        in_specs=[
            pl.BlockSpec((seq, LANES), lambda b: (b, 0)),
            pl.BlockSpec((BLOCK, BLOCK), lambda b: (0, 0)),
        ],
        out_specs=pl.BlockSpec((seq, LANES), lambda b: (b, 0)),
        out_shape=jax.ShapeDtypeStruct((batch * seq, LANES), BF16),
        compiler_params=_cparams("parallel"),
        name="cumsum",
    )(logf, tri)


def _fox_features(cpk, pair, key_side):
    assert PAIR == 2
    r = lax.broadcasted_iota(jnp.int32, (LANES, PAIR * LANES), 0)
    c = lax.broadcasted_iota(jnp.int32, (LANES, PAIR * LANES), 1)
    hh = jnp.where(c >= LANES, 1, 0)
    slot = c - hh * LANES - HEAD_DIM * (1 - hh)
    head = PAIR * pair + hh
    piece_slot = slot - 3 if key_side else slot
    ones_slot = slot if key_side else slot - 3
    piece = (piece_slot >= 0) & (piece_slot < 3) & (r == N_HEADS_B * piece_slot + head)
    ones = (ones_slot >= 0) & (ones_slot < 3) & (r == 3 * N_HEADS_B)
    place = jnp.where(piece, -1.0 if key_side else 1.0, jnp.where(ones, 1.0, 0.0)).astype(BF16)
    return jnp.dot(cpk, place, preferred_element_type=F32).astype(BF16)


def _fox_kernel(q_ref, k_ref, v_ref, c_ref, o_ref, kf, vf, s_scr, *, tile):
    pair = pl.program_id(1)
    half = tile // 2
    in_head = [_head_lane_mask(h) for h in range(PAIR)]
    block = lambda feat, h: feat[:, h * LANES:(h + 1) * LANES]

    feat = _fox_features(c_ref[...], pair, True)
    for h in range(PAIR):
        kf[h] = jnp.where(in_head[h], k_ref[...], block(feat, h))
        vf[h] = jnp.where(in_head[h], v_ref[...], jnp.ones_like(v_ref[...]))

    def lane_groups_max(s):
        m = s[:, :LANES]
        for g in range(1, s.shape[1] // LANES):
            m = jnp.maximum(m, s[:, g * LANES:(g + 1) * LANES])
        return m

    def scores(qrows, off, width, h):
        return lax.dot_general(qrows, kf[h, off:off + width, :], NT_DIMS, preferred_element_type=F32)

    up_r = lax.broadcasted_iota(jnp.int32, (half, half), 0)
    up_c = lax.broadcasted_iota(jnp.int32, (half, half), 1)
    lo_r = lax.broadcasted_iota(jnp.int32, (half, tile), 0)
    lo_c = lax.broadcasted_iota(jnp.int32, (half, tile), 1)

    def query_tile(nq):
        row0 = nq * tile
        feat_q = _fox_features(c_ref[row0:row0 + tile, :], pair, False)
        q = q_ref[row0:row0 + tile, :]
        qf = [jnp.where(in_head[h], q, block(feat_q, h)) for h in range(PAIR)]

        row_max = []
        for h in range(PAIR):
            s_up = jnp.where(up_c <= up_r, scores(qf[h][:half], row0, half, h), NEG_INF)
            s_lo = jnp.where(lo_c <= lo_r + half, scores(qf[h][half:], row0, tile, h), NEG_INF)
            s_scr[h, :half, row0:row0 + half] = s_up
            s_scr[h, half:, row0:row0 + tile] = s_lo
            m_h = jnp.concatenate([lane_groups_max(s_up), lane_groups_max(s_lo)], axis=0)
            for j in range(nq):
                s = scores(qf[h], j * tile, tile, h)
                s_scr[h, :, j * tile:(j + 1) * tile] = s
                m_h = jnp.maximum(m_h, lane_groups_max(s))
            row_max.append(jnp.max(m_h, axis=-1, keepdims=True))

        accs = []
        for h in range(PAIR):
            m = row_max[h]
            p_up = jnp.exp2(s_scr[h, :half, row0:row0 + half] - m[:half]).astype(BF16)
            p_lo = jnp.exp2(s_scr[h, half:, row0:row0 + tile] - m[half:]).astype(BF16)
            acc = jnp.concatenate([
                jnp.dot(p_up, vf[h, row0:row0 + half, :], preferred_element_type=F32),
                jnp.dot(p_lo, vf[h, row0:row0 + tile, :], preferred_element_type=F32)], axis=0)
            for j in range(nq):
                p = jnp.exp2(s_scr[h, :, j * tile:(j + 1) * tile] - m).astype(BF16)
                acc = acc + jnp.dot(p, vf[h, j * tile:(j + 1) * tile, :], preferred_element_type=F32)
            accs.append(acc)
        num, den = _merge_heads(*accs)
        o_ref[row0:row0 + tile, :] = (num / den).astype(o_ref.dtype)

    for nq in range(s_scr.shape[2] // tile):
        query_tile(nq)


def _fox(z, ccol, batch, seq):
    n = z.shape[0]
    tile = FOX_TILE
    nq = seq // tile
    npair = N_HEADS_B // PAIR
    base = 3 * WIDTH_A // LANES
    qcol, kcol, vcol = base, base + WIDTH_B // LANES, base + 2 * WIDTH_B // LANES
    return pl.pallas_call(
        functools.partial(_fox_kernel, tile=tile),
        grid=(batch, npair),
        in_specs=[
            pl.BlockSpec((seq, LANES), lambda b, p: (b, qcol + p)),
            pl.BlockSpec((seq, LANES), lambda b, p: (b, kcol + p)),
            pl.BlockSpec((seq, LANES), lambda b, p: (b, vcol + p)),
            pl.BlockSpec((seq, LANES), lambda b, p: (b, 0)),
        ],
        out_specs=pl.BlockSpec((seq, LANES), lambda b, p: (b, p)),
        out_shape=jax.ShapeDtypeStruct((n, WIDTH_B), BF16),
        scratch_shapes=[
            pltpu.VMEM((PAIR, seq, LANES), BF16),
            pltpu.VMEM((PAIR, seq, LANES), BF16),
            pltpu.VMEM((PAIR, tile, seq), F32),
        ],
        compiler_params=_cparams("parallel", "parallel"),
        name="fox",
    )(z, z, z, ccol)


def _dilated_kernel(slope_ref, q_ref, k_ref, v_ref, o_ref,
                    natf, p4f, p4b, p16b, bias, bias_first, vals, dens, maxs, *, seq):
    pair = pl.program_id(1)
    lane = _lane_iota()
    first = _head_lane_mask(0)
    quarter = seq // 4
    units = seq // BLOCK

    def deinterleave(src, t, span_start, span):
        return [src[t, pl.ds(span_start + r, span // 4, stride=4), :] for r in range(4)]

    for t, ref in enumerate((q_ref, k_ref, v_ref)):
        natf[t] = ref[...].astype(F32)
        for r, part in enumerate(deinterleave(natf, t, 0, seq)):
            p4f[t, pl.ds(r * quarter, quarter), :] = part
            p4b[t, pl.ds(r * quarter, quarter), :] = part.astype(BF16)
        for r4 in range(4):
            for r, part in enumerate(deinterleave(p4f, t, r4 * quarter, quarter)):
                p16b[t, pl.ds(r4 * quarter + r * (quarter // 4), quarter // 4), :] = part.astype(BF16)

    bq = lax.broadcasted_iota(jnp.int32, (BLOCK, 2 * BLOCK), 0)
    bk = lax.broadcasted_iota(jnp.int32, (BLOCK, 2 * BLOCK), 1)
    rel = bq + BLOCK - bk
    band = (rel >= 0) & (rel <= BLOCK)
    relf = rel.astype(F32)
    for p, (_, dil) in enumerate(DILATED_PATTERNS):
        for h in range(PAIR):
            alibi = relf * (-(slope_ref[PAIR * pair + h] * float(dil)) * LOG2E)
            bias[p * PAIR + h] = jnp.where(band, alibi, NEG_INF)
            bias_first[p * PAIR + h] = jnp.where(band & (bk >= BLOCK), alibi, NEG_INF)

    in_head = [_head_lane_mask(h) for h in range(PAIR)]
    ones = jnp.ones((2 * BLOCK, LANES), BF16)

    def unit(p, srcs, u, prev_valid):
        qs, ks, vs = srcs
        start = pl.multiple_of(u * BLOCK, BLOCK)
        prev = pl.multiple_of(jnp.maximum(start - BLOCK, 0), BLOCK)
        qb = qs[pl.ds(start, BLOCK), :]
        kk = jnp.concatenate([ks[pl.ds(prev, BLOCK), :], ks[pl.ds(start, BLOCK), :]], axis=0)
        vv = jnp.concatenate([vs[pl.ds(prev, BLOCK), :], vs[pl.ds(start, BLOCK), :]], axis=0)
        q2 = jnp.concatenate([jnp.where(in_head[h], qb, jnp.zeros_like(qb)) for h in range(PAIR)], axis=0)
        s2 = lax.dot_general(q2, kk, NT_DIMS, preferred_element_type=F32)
        probs, ms = [], []
        for h in range(PAIR):
            if prev_valid is True:
                b = bias[p * PAIR + h]
            elif prev_valid is False:
                b = bias_first[p * PAIR + h]
            else:
                b = jnp.where(prev_valid, bias[p * PAIR + h], bias_first[p * PAIR + h])
            s = s2[h * BLOCK:(h + 1) * BLOCK] + b
            m = jnp.max(s, axis=-1, keepdims=True)
            probs.append(jnp.exp2(s - m).astype(BF16))
            ms.append(m)
        out = jnp.dot(jnp.concatenate(probs, axis=0), jnp.concatenate([vv, ones], axis=1),
                      preferred_element_type=F32)
        top, bottom = out[:BLOCK], out[BLOCK:]
        vals[p, pl.ds(start, BLOCK), :] = jnp.where(first, top[:, :LANES], bottom[:, :LANES])
        dens[p, pl.ds(start, BLOCK), :] = jnp.where(first, top[:, LANES:], bottom[:, LANES:])
        maxs[p, pl.ds(start, BLOCK), :] = jnp.where(first, ms[0], ms[1])

    group = UNITS_PER_STEP
    sources = ((q_ref, k_ref, v_ref), tuple(p4b.at[t] for t in range(3)), tuple(p16b.at[t] for t in range(3)))
    for p, (_, dil) in enumerate(DILATED_PATTERNS):
        per_class = units // dil

        def step(g, _, p=p, per_class=per_class):
            for e in range(group):
                u = g * group + e
                if per_class >= group:
                    prev_valid = (u % per_class != 0) if e == 0 else True
                else:
                    prev_valid = e % per_class != 0
                unit(p, sources[p], u, prev_valid)
            return 0
        lax.fori_loop(0, units // group, step, 0)

    for t, arr in enumerate((vals, dens, maxs)):
        for r4 in range(4):
            for r in range(4):
                p4f[t, pl.ds(r4 * quarter + r, quarter // 4, stride=4), :] = \
                    arr[2, pl.ds(r4 * quarter + r * (quarter // 4), quarter // 4), :]

    for r in range(4):
        grouped = pl.ds(r * quarter, quarter)
        natural = pl.ds(r, quarter, stride=4)
        ms = (maxs[0, natural, :], maxs[1, grouped, :], p4f[2, grouped, :])
        vs = (vals[0, natural, :], vals[1, grouped, :], p4f[0, grouped, :])
        ds = (dens[0, natural, :], dens[1, grouped, :], p4f[1, grouped, :])
        m_all = jnp.maximum(jnp.maximum(ms[0], ms[1]), ms[2])
        num = jnp.zeros((quarter, LANES), F32)
        den = jnp.zeros((quarter, LANES), F32)
        for p in range(3):
            e = jnp.exp2(ms[p] - m_all)
            num = num + e * vs[p]
            den = den + e * ds[p]
        natf[0, natural, :] = num / den
    o_ref[...] = natf[0].astype(o_ref.dtype)


def _dilated(z, slopes, batch, seq):
    n = z.shape[0]
    npair = N_HEADS_A // PAIR
    npat = len(DILATED_PATTERNS)
    qcol, kcol, vcol = 0, WIDTH_A // LANES, 2 * WIDTH_A // LANES
    blk = lambda c0: pl.BlockSpec((seq, LANES), lambda b, p: (b, c0 + p))
    return pl.pallas_call(
        functools.partial(_dilated_kernel, seq=seq),
        grid=(batch, npair),
        in_specs=[pl.BlockSpec(memory_space=pltpu.SMEM), blk(qcol), blk(kcol), blk(vcol)],
        out_specs=pl.BlockSpec((seq, LANES), lambda b, p: (b, p)),
        out_shape=jax.ShapeDtypeStruct((n, WIDTH_A), BF16),
        scratch_shapes=[
            pltpu.VMEM((3, seq, LANES), F32),
            pltpu.VMEM((3, seq, LANES), F32),
            pltpu.VMEM((3, seq, LANES), BF16),
            pltpu.VMEM((3, seq, LANES), BF16),
            pltpu.VMEM((npat * PAIR, BLOCK, 2 * BLOCK), F32),
            pltpu.VMEM((npat * PAIR, BLOCK, 2 * BLOCK), F32),
            pltpu.VMEM((npat, seq, LANES), F32),
            pltpu.VMEM((npat, seq, LANES), F32),
            pltpu.VMEM((npat, seq, LANES), F32),
        ],
        compiler_params=_cparams("parallel", "parallel"),
        name="dilated",
    )(slopes, z, z, z)


def _pack_bf16_pairs(a, b):
    hi = pltpu.bitcast(a.astype(BF16).astype(F32), jnp.int32)
    lo = pltpu.bitcast(b.astype(BF16).astype(F32), jnp.int32)
    return (hi & jnp.int32(-65536)) | lax.shift_right_logical(lo, jnp.int32(16))


def _unpack_bf16_pairs(w):
    a = pltpu.bitcast(w & jnp.int32(-65536), F32)
    b = pltpu.bitcast(lax.shift_left(w, jnp.int32(16)), F32)
    return a, b


def _post_attn_kernel(ma_ref, mb_ref, x_ref, wo_ref, g_ref, wr_ref, br_ref, tri_ref,
                      h_ref, up_ref, idx_ref, gate_ref, rank_ref, cnt_ref, carry):
    @pl.when(pl.program_id(0) == 0)
    def _():
        carry[...] = jnp.zeros_like(carry)

    y = jnp.dot(ma_ref[...], wo_ref[:WIDTH_A, :], preferred_element_type=F32)
    y = y + jnp.dot(mb_ref[...], wo_ref[WIDTH_A:, :], preferred_element_type=F32)
    h = x_ref[...] + y
    h_ref[...] = h
    u = _rms(h, g_ref[...])
    half = u.shape[1] // 2
    up_ref[...] = _pack_bf16_pairs(u[:, :half], u[:, half:])

    u_hi = u.astype(BF16)
    u_lo = (u - u_hi.astype(F32)).astype(BF16)
    hi_terms = jnp.dot(u_hi, wr_ref[...], preferred_element_type=F32)
    logits = (hi_terms[:, :LANES] + hi_terms[:, LANES:]
              + jnp.dot(u_lo, wr_ref[:, :LANES], preferred_element_type=F32)) + br_ref[...]
    lane = lax.broadcasted_iota(jnp.int32, logits.shape, 1).astype(F32)
    work = logits
    idxs, tops = [], []
    for _ in range(TOP_K):
        top = jnp.max(work, axis=-1, keepdims=True)
        idx = jnp.min(jnp.where(work == top, lane, float(LANES)), axis=-1, keepdims=True)
        work = jnp.where(lane == idx, NEG_INF, work)
        idxs.append(idx)
        tops.append(top)
    exps = [jnp.exp(t - tops[0]) for t in tops]
    total = exps[0] + exps[1] + exps[2] + exps[3]

    onehot = jnp.zeros(logits.shape, F32)
    for idx in idxs:
        onehot = onehot + (lane == idx).astype(F32)
    before = jnp.dot(tri_ref[...], onehot.astype(BF16), preferred_element_type=F32) + carry[...]
    carry[...] = carry[...] + jnp.sum(onehot, axis=0, keepdims=True)
    cnt_ref[...] = carry[...]

    idx_out = jnp.zeros(logits.shape, F32)
    gate_out = jnp.zeros(logits.shape, F32)
    rank_out = jnp.zeros(logits.shape, F32)
    for k in range(TOP_K):
        rank_k = jnp.sum(jnp.where(lane == idxs[k], before, 0.0), axis=-1, keepdims=True)
        idx_out = jnp.where(lane == float(k), idxs[k], idx_out)
        gate_out = jnp.where(lane == float(k), exps[k] / total, gate_out)
        rank_out = jnp.where(lane == float(k), rank_k, rank_out)
    idx_ref[...] = idx_out.T[:8, :].astype(jnp.int32)
    gate_ref[...] = gate_out
    rank_ref[...] = rank_out.T[:8, :].astype(jnp.int32)


def _post_attn(mix_a, mix_b, x2, w_o, g_ffn, w_r, b_r):
    n, d = x2.shape
    tm = ROW_TILE
    tri = (lax.broadcasted_iota(jnp.int32, (tm, tm), 0)
           > lax.broadcasted_iota(jnp.int32, (tm, tm), 1)).astype(BF16)
    const = lambda i: (0, 0)
    row = lambda w: pl.BlockSpec((tm, w), lambda i: (i, 0))
    lanes_t = pl.BlockSpec((8, tm), lambda i: (0, i))
    return pl.pallas_call(
        _post_attn_kernel,
        grid=(n // tm,),
        in_specs=[
            row(WIDTH_A), row(WIDTH_B), row(d),
            pl.BlockSpec(w_o.shape, const),
            pl.BlockSpec((1, d), const),
            pl.BlockSpec(w_r.shape, const),
            pl.BlockSpec((1, LANES), const),
            pl.BlockSpec((tm, tm), const),
        ],
        out_specs=[row(d), row(d // 2), lanes_t, row(LANES), lanes_t,
                   pl.BlockSpec((1, LANES), const)],
        out_shape=[
            jax.ShapeDtypeStruct((n, d), F32),
            jax.ShapeDtypeStruct((n, d // 2), jnp.int32),
            jax.ShapeDtypeStruct((8, n), jnp.int32),
            jax.ShapeDtypeStruct((n, LANES), F32),
            jax.ShapeDtypeStruct((8, n), jnp.int32),
            jax.ShapeDtypeStruct((1, LANES), F32),
        ],
        scratch_shapes=[pltpu.VMEM((1, LANES), F32)],
        compiler_params=_cparams("arbitrary"),
        name="post_attn",
    )(mix_a, mix_b, x2, w_o, g_ffn, w_r, b_r, tri)


def _wprep_kernel(w_ref, wg_ref, wl_ref, wt):
    d, cols = w_ref.shape[1:]
    de = cols // 2
    for j in range(d // LANES):
        lanes = slice(j * LANES, (j + 1) * LANES)
        wt[j] = w_ref[0, lanes, :].T
        wg_ref[0, :, lanes] = wt[j, pl.ds(0, de, stride=2), :].astype(BF16)
        wl_ref[0, :, lanes] = wt[j, pl.ds(1, de, stride=2), :].astype(BF16)


def _wprep(w_gate_up):
    ne, d, cols = w_gate_up.shape
    de = cols // 2
    out = pl.BlockSpec((1, de, d), lambda e: (e, 0, 0))
    return pl.pallas_call(
        _wprep_kernel,
        grid=(ne,),
        in_specs=[pl.BlockSpec((1, d, cols), lambda e: (e, 0, 0))],
        out_specs=[out, out],
        out_shape=[jax.ShapeDtypeStruct((ne, de, d), BF16)] * 2,
        scratch_shapes=[pltpu.VMEM((d // LANES, cols, LANES), F32)],
        compiler_params=_cparams("parallel"),
        name="wprep",
    )(w_gate_up)


def _sc_invert(keys, n_tokens, fill):
    info = plsc.get_sparse_core_info()
    workers = info.num_cores * info.num_subcores
    lanes = info.num_lanes
    total, = fill.shape
    count, = keys.shape
    chunk, unroll = SC_SCAN_CHUNK, SC_SCAN_UNROLL
    assert total % (workers * lanes) == 0 and count % (2 * chunk) == 0 and n_tokens % chunk == 0
    own = total // workers
    nchunks = count // chunk
    chunks_per_pass = n_tokens // chunk
    mesh = plsc.VectorSubcoreMesh(core_axis_name="c", subcore_axis_name="s")
    params = pltpu.CompilerParams()
    if "needs_layout_passes" in pltpu.CompilerParams.__dataclass_fields__:
        params = dataclasses.replace(params, needs_layout_passes=False)

    @functools.partial(
        pl.kernel, mesh=mesh, compiler_params=params,
        out_type=jax.ShapeDtypeStruct((total,), jnp.int32),
        scratch_types=[
            pltpu.VMEM((own,), jnp.int32),
            pltpu.VMEM((2, chunk), jnp.int32),
            pltpu.SemaphoreType.DMA((2,)),
        ],
    )
    def invert_kernel(keys_hbm, fill_hbm, out_hbm, own_v, key_v, sem):
        wid = lax.axis_index("s") * info.num_cores + lax.axis_index("c")
        lo = wid * own
        lane_ids = lax.broadcasted_iota(jnp.int32, (lanes,), 0)

        def fetch(c, b):
            off = pl.multiple_of(c * chunk, chunk)
            return pltpu.make_async_copy(keys_hbm.at[pl.ds(off, chunk)], key_v.at[b], sem.at[b])

        fetch(0, 0).start()
        pltpu.sync_copy(fill_hbm.at[pl.ds(lo, own)], own_v)

        @pl.loop(0, nchunks, step=2)
        def _(c0):
            for b in range(2):
                c = c0 + b

                @pl.when(c + 1 < nchunks)
                def _():
                    fetch(c + 1, 1 - b).start()

                fetch(c, b).wait()
                first_token = (c % chunks_per_pass) * chunk

                @plsc.parallel_loop(0, chunk // lanes, unroll=unroll)
                def _(j):
                    start = pl.multiple_of(j * lanes, lanes)
                    row = key_v[b, pl.ds(start, lanes)] - lo
                    mine = (row >= 0) & (row < own)
                    plsc.store_scatter(own_v, [jnp.where(mine, row, 0)], first_token + start + lane_ids, mask=mine)

        pltpu.sync_copy(own_v, out_hbm.at[pl.ds(lo, own)])

    return invert_kernel(keys, fill)


def _sc_gather(table, idx):
    info = plsc.get_sparse_core_info()
    workers = info.num_cores * info.num_subcores
    rows, width = idx.shape[0], table.shape[1]
    chunk, depth = SC_CHUNK, SC_DEPTH
    assert rows % (workers * chunk * depth) == 0
    per_worker = rows // workers
    nchunks = per_worker // chunk
    mesh = plsc.VectorSubcoreMesh(core_axis_name="c", subcore_axis_name="s")

    @functools.partial(
        pl.kernel, mesh=mesh,
        out_type=jax.ShapeDtypeStruct((rows, width), table.dtype),
        scratch_types=[
            pltpu.VMEM((nchunks, chunk), jnp.int32),
            pltpu.VMEM((depth, chunk, width), table.dtype),
            pltpu.SemaphoreType.DMA((depth,)),
            pltpu.SemaphoreType.DMA((depth,)),
        ],
    )
    def gather_kernel(table_hbm, idx_hbm, out_hbm, idx_v, rows_v, gsem, wsem):
        wid = lax.axis_index("s") * info.num_cores + lax.axis_index("c")
        base = wid * per_worker
        pltpu.sync_copy(idx_hbm.at[wid], idx_v)

        def gather(c, b):
            return pltpu.make_async_copy(table_hbm.at[idx_v.at[c]], rows_v.at[b], gsem.at[b])

        def write(c, b):
            off = pl.multiple_of(base + c * chunk, chunk)
            return pltpu.make_async_copy(rows_v.at[b], out_hbm.at[pl.ds(off, chunk)], wsem.at[b])

        @pl.loop(0, nchunks, step=depth)
        def _(c0):
            for b in range(depth):
                gather(c0 + b, b).start()
            for b in range(depth):
                gather(c0 + b, b).wait()
                write(c0 + b, b).start()
            for b in range(depth):
                write(c0 + b, b).wait()

    return gather_kernel(table, idx.reshape(workers, nchunks, chunk))


def _gmm_kernel(te_ref, used_ref, xs_ref, wg_ref, wl_ref, bg_ref, bl_ref, wd_ref, bd_ref, ys_ref):
    i = pl.program_id(0)

    @pl.when(used_ref[i] > 0)
    def _():
        a, b = _unpack_bf16_pairs(xs_ref[...])
        x = jnp.concatenate([a, b], axis=1).astype(BF16)
        hg = lax.dot_general(x, wg_ref[0], NT_DIMS, preferred_element_type=F32) + bg_ref[0]
        hl = lax.dot_general(x, wl_ref[0], NT_DIMS, preferred_element_type=F32) + bl_ref[0]
        xg = jnp.minimum(hg, SWIGLU_LIMIT)
        xl = jnp.clip(hl, -SWIGLU_LIMIT, SWIGLU_LIMIT)
        act = xg * jax.nn.sigmoid(SWIGLU_ALPHA * xg) * (xl + 1.0)
        out = jnp.dot(act.astype(BF16), wd_ref[0].astype(BF16), preferred_element_type=F32) + bd_ref[0]
        half = out.shape[1] // 2
        ys_ref[...] = _pack_bf16_pairs(out[:, :half], out[:, half:])

    @pl.when(used_ref[i] == 0)
    def _():
        ys_ref[...] = jnp.zeros_like(ys_ref)


def _gmm(tile_expert, tile_used, xs, wg_t, wl_t, b_glu, b_lin, w_down, b_down):
    rows, half = xs.shape
    d = 2 * half
    de = wg_t.shape[1]
    tm = GMM_TILE
    wspec = lambda shape: pl.BlockSpec((1,) + shape, lambda i, te, used: (te[i], 0, 0))
    grid_spec = pltpu.PrefetchScalarGridSpec(
        num_scalar_prefetch=2,
        grid=(rows // tm,),
        in_specs=[
            pl.BlockSpec((tm, half), lambda i, te, used: (i, 0)),
            wspec((de, d)), wspec((de, d)), wspec((1, de)), wspec((1, de)),
            wspec((de, d)), wspec((1, d)),
        ],
        out_specs=pl.BlockSpec((tm, half), lambda i, te, used: (i, 0)),
    )
    return pl.pallas_call(
        _gmm_kernel,
        grid_spec=grid_spec,
        out_shape=jax.ShapeDtypeStruct((rows, half), jnp.int32),
        compiler_params=_cparams("arbitrary"),
        name="gmm",
    )(tile_expert, tile_used, xs, wg_t, wl_t, b_glu, b_lin, w_down, b_down)


def _final_kernel(h_ref, yk_ref, gate_ref, p_ref, g_ref, wg_ref, wp_ref, o_ref):
    gates = gate_ref[...]
    h = h_ref[...]
    for k in range(TOP_K):
        h = h + gates[:, k:k + 1] * jnp.concatenate(_unpack_bf16_pairs(yk_ref[k]), axis=1)
    u = _rms(h, g_ref[...]).astype(BF16)
    gate = jax.nn.sigmoid(jnp.dot(u, wg_ref[...], preferred_element_type=F32))
    proj = jnp.dot(p_ref[...].astype(BF16), wp_ref[...], preferred_element_type=F32)
    o_ref[...] = h + gate * proj


def _final(h1, yk, gates, p2, g_ple, w_gate, w_proj, first_tile, out_so_far=None):
    n, d = h1.shape
    tm = ROW_TILE
    const = lambda i: (0, 0)
    rows = lambda width: pl.BlockSpec((tm, width), lambda i: (i + first_tile, 0))
    in_specs = [
        rows(d),
        pl.BlockSpec((TOP_K, tm, d // 2), lambda i: (0, i, 0)),
        rows(LANES),
        rows(p2.shape[1]),
        pl.BlockSpec((1, d), const),
        pl.BlockSpec(w_gate.shape, const),
        pl.BlockSpec(w_proj.shape, const),
    ]
    args = [h1, yk, gates, p2, g_ple, w_gate, w_proj]
    kernel_fn, aliases = _final_kernel, {}
    if out_so_far is not None:
        in_specs.append(pl.BlockSpec(memory_space=pl.ANY))
        args.append(out_so_far)
        aliases = {len(args) - 1: 0}
        kernel_fn = lambda *refs: _final_kernel(*refs[:7], refs[8])
    return pl.pallas_call(
        kernel_fn,
        grid=(yk.shape[1] // tm,),
        in_specs=in_specs,
        out_specs=rows(d),
        out_shape=jax.ShapeDtypeStruct((n, d), F32),
        input_output_aliases=aliases,
        compiler_params=_cparams("parallel"),
        name="final",
    )(*args)


def _layer(h, p, g_mix, w_in, b_f, g_qa, g_ka, g_qb, g_kb, w_o, g_ffn, w_router, b_router,
           w_gate_up, b_gate_up, w_down, b_down, g_ple, w_ple_gate, w_ple_proj):
    batch, seq, d = h.shape
    n = batch * seq
    assert tuple(dil for _, dil in DILATED_PATTERNS) == (1, 4, 16)
    for window, dil in DILATED_PATTERNS:
        per_class = seq // BLOCK // dil
        assert window // dil == BLOCK and seq % (dil * BLOCK) == 0
        assert per_class % UNITS_PER_STEP == 0 or UNITS_PER_STEP % per_class == 0
    assert n % ROW_TILE == 0 and d % (2 * LANES) == 0 and seq % FOX_TILE == 0
    x2 = h.reshape(n, d)

    qkv_cols = 3 * WIDTH_A + 3 * WIDTH_B
    w_qkv = w_in[:, :qkv_cols].astype(BF16)
    w_f = jnp.pad(w_in[:, qkv_cols:], ((0, 0), (0, LANES - N_HEADS_B))).astype(BF16)
    b_fp = jnp.pad(b_f.astype(F32), (0, LANES - N_HEADS_B)).reshape(1, LANES)
    scale = HEAD_DIM ** -0.5
    gains = jnp.stack([jnp.tile(g_qa, N_HEADS_A) * (scale * LOG2E), jnp.tile(g_ka, N_HEADS_A),
                       jnp.tile(g_qb, N_HEADS_B) * (scale * LOG2E), jnp.tile(g_kb, N_HEADS_B)]).astype(F32)
    hid = jnp.arange(2 * LANES) // HEAD_DIM
    bd = (hid[:, None] == hid[None, :]).astype(BF16)

    z, logf = _in_proj(x2, g_mix.reshape(1, d), w_qkv, w_f, b_fp, gains, bd)
    ccol = _cumsum(logf, batch, seq)

    slopes = 2.0 ** (-8.0 * jnp.arange(1, N_HEADS_A + 1, dtype=F32) / N_HEADS_A)
    mix_a = _dilated(z, slopes, batch, seq)
    mix_b = _fox(z, ccol, batch, seq)

    w_r = jnp.pad(w_router.astype(F32), ((0, 0), (0, LANES - N_EXPERTS)))
    w_r_hi = w_r.astype(BF16)
    w_r = jnp.concatenate([w_r_hi, (w_r - w_r_hi.astype(F32)).astype(BF16)], axis=1)
    b_r = jnp.concatenate([b_router.astype(F32), jnp.full((LANES - N_EXPERTS,), NEG_INF, F32)]).reshape(1, LANES)
    h1, u_packed, top_idx, gates, rank, counts = _post_attn(
        mix_a, mix_b, x2, w_o.astype(BF16), g_ffn.reshape(1, d), w_r, b_r)

    counts = counts[0, :N_EXPERTS].astype(jnp.int32)
    tiles_per = (counts + GMM_TILE - 1) // GMM_TILE
    tile_end = jnp.cumsum(tiles_per)
    starts = (tile_end - tiles_per) * GMM_TILE
    n_tiles = n * TOP_K // GMM_TILE + N_EXPERTS
    tile_ids = jnp.arange(n_tiles, dtype=jnp.int32)
    tile_used = (tile_ids < tile_end[-1]).astype(jnp.int32)
    last_used = jnp.minimum(tile_ids, tile_end[-1] - 1)
    tile_expert = jnp.sum((last_used[:, None] >= tile_end[None, :]).astype(jnp.int32), axis=1)
    tile_expert = jnp.minimum(tile_expert, N_EXPERTS - 1)
    experts = jnp.arange(N_EXPERTS, dtype=jnp.int32)[:, None, None]
    pos_t = rank[:TOP_K] + jnp.sum(jnp.where(top_idx[None, :TOP_K] == experts, starts[:, None, None], 0), axis=0)
    padding_tokens = jnp.arange(n_tiles * GMM_TILE, dtype=jnp.int32) % n
    src = _sc_invert(pos_t.reshape(-1), n, padding_tokens)
    xs = _sc_gather(u_packed, src)

    de = w_down.shape[1]
    wg_t, wl_t = _wprep(w_gate_up)
    ys = _gmm(tile_expert, tile_used, xs, wg_t, wl_t,
              b_gate_up[:, 0::2].reshape(N_EXPERTS, 1, de).astype(F32),
              b_gate_up[:, 1::2].reshape(N_EXPERTS, 1, de).astype(F32),
              w_down, b_down.reshape(N_EXPERTS, 1, d).astype(F32))
    assert n % (2 * ROW_TILE) == 0
    half_n = n // 2
    out = None
    for part in range(2):
        part_pos = pos_t[:, part * half_n:(part + 1) * half_n].reshape(-1)
        yk = _sc_gather(ys, part_pos).reshape(TOP_K, half_n, d // 2)
        out = _final(h1, yk, gates, p.reshape(n, -1), g_ple.reshape(1, d), w_ple_gate.astype(BF16),
                     w_ple_proj.astype(BF16), part * (half_n // ROW_TILE), out)
    return out.reshape(batch, seq, d)


def kernel(x, p, g_mix, w_in, b_f, g_qa, g_ka, g_qb, g_kb, w_o, g_ffn, w_router, b_router,
           w_gate_up, b_gate_up, w_down, b_down, g_ple, w_ple_gate, w_ple_proj):
    h = x
    for i in range(g_mix.shape[0]):
        h = _layer(h, p[i], g_mix[i], w_in[i], b_f[i], g_qa[i], g_ka[i], g_qb[i], g_kb[i], w_o[i],
                   g_ffn[i], w_router[i], b_router[i], w_gate_up[i], b_gate_up[i], w_down[i],
                   b_down[i], g_ple[i], w_ple_gate[i], w_ple_proj[i])
    return h
```

```python
import dataclasses
import functools

import jax
import jax.numpy as jnp
from jax import lax
from jax.experimental import pallas as pl
from jax.experimental.pallas import tpu as pltpu
from jax.experimental.pallas import tpu_sc as plsc

HEAD_DIM = 64
N_HEADS_A = 8
N_HEADS_B = 8
WIDTH_A = N_HEADS_A * HEAD_DIM
WIDTH_B = N_HEADS_B * HEAD_DIM
DILATED_PATTERNS = ((128, 1), (512, 4), (2048, 16))
BLOCK = 128
N_EXPERTS = 32
TOP_K = 4
SWIGLU_LIMIT = 7.0
SWIGLU_ALPHA = 1.702
NORM_EPS = 1e-6

LANES = 128
PAIR = LANES // HEAD_DIM
ROW_TILE = 512
GMM_TILE = 512
FOX_TILE = 512
WPREP_SPLIT = 4
UNITS_PER_STEP = 16
SC_CHUNK = 32
SC_DEPTH = 4
SC_SCAN_CHUNK = 16384
SC_SCAN_UNROLL = 8
VMEM_LIMIT = 56 * 1024 * 1024

F32 = jnp.float32
BF16 = jnp.bfloat16
NEG_INF = float("-inf")
NT_DIMS = (((1,), (1,)), ((), ()))
LOG2E = 1.4426950408889634


def _cparams(*sem):
    return pltpu.CompilerParams(dimension_semantics=sem, vmem_limit_bytes=VMEM_LIMIT)


def _rms(x, g):
    return x * lax.rsqrt(jnp.mean(x * x, axis=-1, keepdims=True) + NORM_EPS) * g


def _lane_iota():
    return lax.broadcasted_iota(jnp.int32, (1, LANES), 1)


def _head_lane_mask(h):
    lane = _lane_iota()
    return (lane >= h * HEAD_DIM) & (lane < (h + 1) * HEAD_DIM)


def _merge_heads(acc0, acc1):
    first = _head_lane_mask(0)
    num = jnp.where(first, acc0, acc1)
    den = pltpu.roll(jnp.where(first, acc1, acc0), HEAD_DIM, axis=1)
    return num, den


def _in_proj_kernel(x_ref, g_ref, w_ref, wf_ref, bf_ref, gain_ref, bd_ref, z_ref, lf_ref):
    u = _rms(x_ref[...], g_ref[...]).astype(BF16)
    chunk = WIDTH_A
    normed = {0: 0, 1: 1, 3: 2, 4: 3}
    for c in range(6):
        acc = jnp.dot(u, w_ref[:, c * chunk:(c + 1) * chunk], preferred_element_type=F32)
        if c in normed:
            sq = (acc * acc).astype(BF16)
            half = chunk // 2
            ss = jnp.concatenate(
                [jnp.dot(sq[:, j * half:(j + 1) * half], bd_ref[...], preferred_element_type=F32)
                 for j in range(2)], axis=1)
            r = normed[c]
            acc = acc * lax.rsqrt(ss * (1.0 / HEAD_DIM) + NORM_EPS) * gain_ref[r:r + 1, :]
        z_ref[:, c * chunk:(c + 1) * chunk] = acc.astype(BF16)
    zf = jnp.dot(u, wf_ref[...], preferred_element_type=F32) + bf_ref[...]
    lf_ref[...] = jax.nn.log_sigmoid(zf)


def _in_proj(x2, g_mix, w_qkv, w_f, b_f, gains, bd):
    n, d = x2.shape
    cols = w_qkv.shape[1]
    tm = ROW_TILE
    const = lambda i: (0, 0)
    return pl.pallas_call(
        _in_proj_kernel,
        grid=(n // tm,),
        in_specs=[
            pl.BlockSpec((tm, d), lambda i: (i, 0)),
            pl.BlockSpec((1, d), const),
            pl.BlockSpec((d, cols), const),
            pl.BlockSpec((d, LANES), const),
            pl.BlockSpec((1, LANES), const),
            pl.BlockSpec(gains.shape, const),
            pl.BlockSpec(bd.shape, const),
        ],
        out_specs=[
            pl.BlockSpec((tm, cols), lambda i: (i, 0)),
            pl.BlockSpec((tm, LANES), lambda i: (i, 0)),
        ],
        out_shape=[
            jax.ShapeDtypeStruct((n, cols), BF16),
            jax.ShapeDtypeStruct((n, LANES), F32),
        ],
        compiler_params=_cparams("parallel"),
        name="in_proj",
    )(x2, g_mix, w_qkv, w_f, b_f, gains, bd)


def _cumsum_kernel(lf_ref, tri_ref, cpk_ref):
    s = lf_ref.shape[0]
    lane = _lane_iota()
    carry = jnp.zeros((1, LANES), F32)
    for blk in range(s // BLOCK):
        rows = slice(blk * BLOCK, (blk + 1) * BLOCK)
        part = jnp.dot(tri_ref[...], lf_ref[rows, :], precision=lax.Precision.HIGHEST,
                       preferred_element_type=F32) + carry
        carry = part[BLOCK - 1:BLOCK, :]
        c = part * LOG2E
        hi = c.astype(BF16).astype(F32)
        r1 = c - hi
        mid = r1.astype(BF16).astype(F32)
        lo = r1 - mid
        packed = jnp.where(lane < N_HEADS_B, hi,
                 jnp.where(lane < 2 * N_HEADS_B, pltpu.roll(mid, N_HEADS_B, axis=1),
                 jnp.where(lane < 3 * N_HEADS_B, pltpu.roll(lo, 2 * N_HEADS_B, axis=1),
                 jnp.where(lane == 3 * N_HEADS_B, 1.0, 0.0))))
        cpk_ref[rows, :] = packed.astype(BF16)


def _cumsum(logf, batch, seq):
    tri = (lax.broadcasted_iota(jnp.int32, (BLOCK, BLOCK), 0)
           >= lax.broadcasted_iota(jnp.int32, (BLOCK, BLOCK), 1)).astype(F32)
    return pl.pallas_call(
        _cumsum_kernel,
        grid=(batch,),
        in_specs=[
            pl.BlockSpec((seq, LANES), lambda b: (b, 0)),
            pl.BlockSpec((BLOCK, BLOCK), lambda b: (0, 0)),
        ],
        out_specs=pl.BlockSpec((seq, LANES), lambda b: (b, 0)),
        out_shape=jax.ShapeDtypeStruct((batch * seq, LANES), BF16),
        compiler_params=_cparams("parallel"),
        name="cumsum",
    )(logf, tri)


def _fox_features(cpk, pair, key_side):
    assert PAIR == 2
    r = lax.broadcasted_iota(jnp.int32, (LANES, PAIR * LANES), 0)
    c = lax.broadcasted_iota(jnp.int32, (LANES, PAIR * LANES), 1)
    hh = jnp.where(c >= LANES, 1, 0)
    slot = c - hh * LANES - HEAD_DIM * (1 - hh)
    head = PAIR * pair + hh
    piece_slot = slot - 3 if key_side else slot
    ones_slot = slot if key_side else slot - 3
    piece = (piece_slot >= 0) & (piece_slot < 3) & (r == N_HEADS_B * piece_slot + head)
    ones = (ones_slot >= 0) & (ones_slot < 3) & (r == 3 * N_HEADS_B)
    place = jnp.where(piece, -1.0 if key_side else 1.0, jnp.where(ones, 1.0, 0.0)).astype(BF16)
    return jnp.dot(cpk, place, preferred_element_type=F32).astype(BF16)


def _fox_kernel(q_ref, k_ref, v_ref, c_ref, o_ref, kf, vf, s_scr, *, tile):
    pair = pl.program_id(1)
    half = tile // 2
    in_head = [_head_lane_mask(h) for h in range(PAIR)]
    block = lambda feat, h: feat[:, h * LANES:(h + 1) * LANES]

    feat = _fox_features(c_ref[...], pair, True)
    for h in range(PAIR):
        kf[h] = jnp.where(in_head[h], k_ref[...], block(feat, h))
        vf[h] = jnp.where(in_head[h], v_ref[...], jnp.ones_like(v_ref[...]))

    def lane_groups_max(s):
        m = s[:, :LANES]
        for g in range(1, s.shape[1] // LANES):
            m = jnp.maximum(m, s[:, g * LANES:(g + 1) * LANES])
        return m

    def scores(qrows, off, width, h):
        return lax.dot_general(qrows, kf[h, off:off + width, :], NT_DIMS, preferred_element_type=F32)

    up_r = lax.broadcasted_iota(jnp.int32, (half, half), 0)
    up_c = lax.broadcasted_iota(jnp.int32, (half, half), 1)
    lo_r = lax.broadcasted_iota(jnp.int32, (half, tile), 0)
    lo_c = lax.broadcasted_iota(jnp.int32, (half, tile), 1)

    def query_tile(nq):
        row0 = nq * tile
        feat_q = _fox_features(c_ref[row0:row0 + tile, :], pair, False)
        q = q_ref[row0:row0 + tile, :]
        qf = [jnp.where(in_head[h], q, block(feat_q, h)) for h in range(PAIR)]

        row_max = []
        for h in range(PAIR):
            s_up = jnp.where(up_c <= up_r, scores(qf[h][:half], row0, half, h), NEG_INF)
            s_lo = jnp.where(lo_c <= lo_r + half, scores(qf[h][half:], row0, tile, h), NEG_INF)
            s_scr[h, :half, row0:row0 + half] = s_up
            s_scr[h, half:, row0:row0 + tile] = s_lo
            m_h = jnp.concatenate([lane_groups_max(s_up), lane_groups_max(s_lo)], axis=0)
            for j in range(nq):
                s = scores(qf[h], j * tile, tile, h)
                s_scr[h, :, j * tile:(j + 1) * tile] = s
                m_h = jnp.maximum(m_h, lane_groups_max(s))
            row_max.append(jnp.max(m_h, axis=-1, keepdims=True))

        accs = []
        for h in range(PAIR):
            m = row_max[h]
            p_up = jnp.exp2(s_scr[h, :half, row0:row0 + half] - m[:half]).astype(BF16)
            p_lo = jnp.exp2(s_scr[h, half:, row0:row0 + tile] - m[half:]).astype(BF16)
            acc = jnp.concatenate([
                jnp.dot(p_up, vf[h, row0:row0 + half, :], preferred_element_type=F32),
                jnp.dot(p_lo, vf[h, row0:row0 + tile, :], preferred_element_type=F32)], axis=0)
            for j in range(nq):
                p = jnp.exp2(s_scr[h, :, j * tile:(j + 1) * tile] - m).astype(BF16)
                acc = acc + jnp.dot(p, vf[h, j * tile:(j + 1) * tile, :], preferred_element_type=F32)
            accs.append(acc)
        num, den = _merge_heads(*accs)
        o_ref[row0:row0 + tile, :] = (num / den).astype(o_ref.dtype)

    for nq in range(s_scr.shape[2] // tile):
        query_tile(nq)


def _fox(z, ccol, batch, seq):
    n = z.shape[0]
    tile = FOX_TILE
    nq = seq // tile
    npair = N_HEADS_B // PAIR
    base = 3 * WIDTH_A // LANES
    qcol, kcol, vcol = base, base + WIDTH_B // LANES, base + 2 * WIDTH_B // LANES
    return pl.pallas_call(
        functools.partial(_fox_kernel, tile=tile),
        grid=(batch, npair),
        in_specs=[
            pl.BlockSpec((seq, LANES), lambda b, p: (b, qcol + p)),
            pl.BlockSpec((seq, LANES), lambda b, p: (b, kcol + p)),
            pl.BlockSpec((seq, LANES), lambda b, p: (b, vcol + p)),
            pl.BlockSpec((seq, LANES), lambda b, p: (b, 0)),
        ],
        out_specs=pl.BlockSpec((seq, LANES), lambda b, p: (b, p)),
        out_shape=jax.ShapeDtypeStruct((n, WIDTH_B), BF16),
        scratch_shapes=[
            pltpu.VMEM((PAIR, seq, LANES), BF16),
            pltpu.VMEM((PAIR, seq, LANES), BF16),
            pltpu.VMEM((PAIR, tile, seq), F32),
        ],
        compiler_params=_cparams("parallel", "parallel"),
        name="fox",
    )(z, z, z, ccol)


def _dilated_kernel(slope_ref, q_ref, k_ref, v_ref, o_ref,
                    natf, p4f, p4b, p16b, bias, bias_first, vals, dens, maxs, *, seq):
    pair = pl.program_id(1)
    lane = _lane_iota()
    first = _head_lane_mask(0)
    quarter = seq // 4
    units = seq // BLOCK

    def deinterleave(src, t, span_start, span):
        return [src[t, pl.ds(span_start + r, span // 4, stride=4), :] for r in range(4)]

    for t, ref in enumerate((q_ref, k_ref, v_ref)):
        natf[t] = ref[...].astype(F32)
        for r, part in enumerate(deinterleave(natf, t, 0, seq)):
            p4f[t, pl.ds(r * quarter, quarter), :] = part
            p4b[t, pl.ds(r * quarter, quarter), :] = part.astype(BF16)
        for r4 in range(4):
            for r, part in enumerate(deinterleave(p4f, t, r4 * quarter, quarter)):
                p16b[t, pl.ds(r4 * quarter + r * (quarter // 4), quarter // 4), :] = part.astype(BF16)

    bq = lax.broadcasted_iota(jnp.int32, (BLOCK, 2 * BLOCK), 0)
    bk = lax.broadcasted_iota(jnp.int32, (BLOCK, 2 * BLOCK), 1)
    rel = bq + BLOCK - bk
    band = (rel >= 0) & (rel <= BLOCK)
    relf = rel.astype(F32)
    for p, (_, dil) in enumerate(DILATED_PATTERNS):
        for h in range(PAIR):
            alibi = relf * (-(slope_ref[PAIR * pair + h] * float(dil)) * LOG2E)
            bias[p * PAIR + h] = jnp.where(band, alibi, NEG_INF)
            bias_first[p * PAIR + h] = jnp.where(band & (bk >= BLOCK), alibi, NEG_INF)

    in_head = [_head_lane_mask(h) for h in range(PAIR)]
    ones = jnp.ones((2 * BLOCK, LANES), BF16)

    def unit(p, srcs, u, prev_valid):
        qs, ks, vs = srcs
        start = pl.multiple_of(u * BLOCK, BLOCK)
        prev = pl.multiple_of(jnp.maximum(start - BLOCK, 0), BLOCK)
        qb = qs[pl.ds(start, BLOCK), :]
        kk = jnp.concatenate([ks[pl.ds(prev, BLOCK), :], ks[pl.ds(start, BLOCK), :]], axis=0)
        vv = jnp.concatenate([vs[pl.ds(prev, BLOCK), :], vs[pl.ds(start, BLOCK), :]], axis=0)
        q2 = jnp.concatenate([jnp.where(in_head[h], qb, jnp.zeros_like(qb)) for h in range(PAIR)], axis=0)
        s2 = lax.dot_general(q2, kk, NT_DIMS, preferred_element_type=F32)
        probs, ms = [], []
        for h in range(PAIR):
            if prev_valid is True:
                b = bias[p * PAIR + h]
            elif prev_valid is False:
                b = bias_first[p * PAIR + h]
            else:
                b = jnp.where(prev_valid, bias[p * PAIR + h], bias_first[p * PAIR + h])
            s = s2[h * BLOCK:(h + 1) * BLOCK] + b
            m = jnp.max(s, axis=-1, keepdims=True)
            probs.append(jnp.exp2(s - m).astype(BF16))
            ms.append(m)
        out = jnp.dot(jnp.concatenate(probs, axis=0), jnp.concatenate([vv, ones], axis=1),
                      preferred_element_type=F32)
        top, bottom = out[:BLOCK], out[BLOCK:]
        vals[p, pl.ds(start, BLOCK), :] = jnp.where(first, top[:, :LANES], bottom[:, :LANES])
        dens[p, pl.ds(start, BLOCK), :] = jnp.where(first, top[:, LANES:], bottom[:, LANES:])
        maxs[p, pl.ds(start, BLOCK), :] = jnp.where(first, ms[0], ms[1])

    group = UNITS_PER_STEP
    sources = ((q_ref, k_ref, v_ref), tuple(p4b.at[t] for t in range(3)), tuple(p16b.at[t] for t in range(3)))
    for p, (_, dil) in enumerate(DILATED_PATTERNS):
        per_class = units // dil

        def step(g, _, p=p, per_class=per_class):
            for e in range(group):
                u = g * group + e
                if per_class >= group:
                    prev_valid = (u % per_class != 0) if e == 0 else True
                else:
                    prev_valid = e % per_class != 0
                unit(p, sources[p], u, prev_valid)
            return 0
        lax.fori_loop(0, units // group, step, 0)

    for t, arr in enumerate((vals, dens, maxs)):
        for r4 in range(4):
            for r in range(4):
                p4f[t, pl.ds(r4 * quarter + r, quarter // 4, stride=4), :] = \
                    arr[2, pl.ds(r4 * quarter + r * (quarter // 4), quarter // 4), :]

    for r in range(4):
        grouped = pl.ds(r * quarter, quarter)
        natural = pl.ds(r, quarter, stride=4)
        ms = (maxs[0, natural, :], maxs[1, grouped, :], p4f[2, grouped, :])
        vs = (vals[0, natural, :], vals[1, grouped, :], p4f[0, grouped, :])
        ds = (dens[0, natural, :], dens[1, grouped, :], p4f[1, grouped, :])
        m_all = jnp.maximum(jnp.maximum(ms[0], ms[1]), ms[2])
        num = jnp.zeros((quarter, LANES), F32)
        den = jnp.zeros((quarter, LANES), F32)
        for p in range(3):
            e = jnp.exp2(ms[p] - m_all)
            num = num + e * vs[p]
            den = den + e * ds[p]
        natf[0, natural, :] = num / den
    o_ref[...] = natf[0].astype(o_ref.dtype)


def _dilated(z, slopes, batch, seq):
    n = z.shape[0]
    npair = N_HEADS_A // PAIR
    npat = len(DILATED_PATTERNS)
    qcol, kcol, vcol = 0, WIDTH_A // LANES, 2 * WIDTH_A // LANES
    blk = lambda c0: pl.BlockSpec((seq, LANES), lambda b, p: (b, c0 + p))
    return pl.pallas_call(
        functools.partial(_dilated_kernel, seq=seq),
        grid=(batch, npair),
        in_specs=[pl.BlockSpec(memory_space=pltpu.SMEM), blk(qcol), blk(kcol), blk(vcol)],
        out_specs=pl.BlockSpec((seq, LANES), lambda b, p: (b, p)),
        out_shape=jax.ShapeDtypeStruct((n, WIDTH_A), BF16),
        scratch_shapes=[
            pltpu.VMEM((3, seq, LANES), F32),
            pltpu.VMEM((3, seq, LANES), F32),
            pltpu.VMEM((3, seq, LANES), BF16),
            pltpu.VMEM((3, seq, LANES), BF16),
            pltpu.VMEM((npat * PAIR, BLOCK, 2 * BLOCK), F32),
            pltpu.VMEM((npat * PAIR, BLOCK, 2 * BLOCK), F32),
            pltpu.VMEM((npat, seq, LANES), F32),
            pltpu.VMEM((npat, seq, LANES), F32),
            pltpu.VMEM((npat, seq, LANES), F32),
        ],
        compiler_params=_cparams("parallel", "parallel"),
        name="dilated",
    )(slopes, z, z, z)


def _pack_bf16_pairs(a, b):
    hi = pltpu.bitcast(a.astype(BF16).astype(F32), jnp.int32)
    lo = pltpu.bitcast(b.astype(BF16).astype(F32), jnp.int32)
    return (hi & jnp.int32(-65536)) | lax.shift_right_logical(lo, jnp.int32(16))


def _unpack_bf16_pairs(w):
    a = pltpu.bitcast(w & jnp.int32(-65536), F32)
    b = pltpu.bitcast(lax.shift_left(w, jnp.int32(16)), F32)
    return a, b


def _post_attn_kernel(ma_ref, mb_ref, x_ref, wo_ref, g_ref, wr_ref, br_ref, tri_ref,
                      h_ref, up_ref, idx_ref, gate_ref, rank_ref, cnt_ref, carry):
    @pl.when(pl.program_id(0) == 0)
    def _():
        carry[...] = jnp.zeros_like(carry)

    y = jnp.dot(ma_ref[...], wo_ref[:WIDTH_A, :], preferred_element_type=F32)
    y = y + jnp.dot(mb_ref[...], wo_ref[WIDTH_A:, :], preferred_element_type=F32)
    h = x_ref[...] + y
    h_ref[...] = h
    u = _rms(h, g_ref[...])
    half = u.shape[1] // 2
    up_ref[...] = _pack_bf16_pairs(u[:, :half], u[:, half:])

    u_hi = u.astype(BF16)
    u_lo = (u - u_hi.astype(F32)).astype(BF16)
    hi_terms = jnp.dot(u_hi, wr_ref[...], preferred_element_type=F32)
    logits = (hi_terms[:, :LANES] + hi_terms[:, LANES:]
              + jnp.dot(u_lo, wr_ref[:, :LANES], preferred_element_type=F32)) + br_ref[...]
    lane = lax.broadcasted_iota(jnp.int32, logits.shape, 1).astype(F32)
    work = logits
    idxs, tops = [], []
    for _ in range(TOP_K):
        top = jnp.max(work, axis=-1, keepdims=True)
        idx = jnp.min(jnp.where(work == top, lane, float(LANES)), axis=-1, keepdims=True)
        work = jnp.where(lane == idx, NEG_INF, work)
        idxs.append(idx)
        tops.append(top)
    exps = [jnp.exp(t - tops[0]) for t in tops]
    total = exps[0] + exps[1] + exps[2] + exps[3]

    onehot = jnp.zeros(logits.shape, F32)
    for idx in idxs:
        onehot = onehot + (lane == idx).astype(F32)
    before = jnp.dot(tri_ref[...], onehot.astype(BF16), preferred_element_type=F32) + carry[...]
    carry[...] = carry[...] + jnp.sum(onehot, axis=0, keepdims=True)
    cnt_ref[...] = carry[...]

    idx_out = jnp.zeros(logits.shape, F32)
    gate_out = jnp.zeros(logits.shape, F32)
    rank_out = jnp.zeros(logits.shape, F32)
    for k in range(TOP_K):
        rank_k = jnp.sum(jnp.where(lane == idxs[k], before, 0.0), axis=-1, keepdims=True)
        idx_out = jnp.where(lane == float(k), idxs[k], idx_out)
        gate_out = jnp.where(lane == float(k), exps[k] / total, gate_out)
        rank_out = jnp.where(lane == float(k), rank_k, rank_out)
    idx_ref[...] = idx_out.T[:8, :].astype(jnp.int32)
    gate_ref[...] = gate_out
    rank_ref[...] = rank_out.T[:8, :].astype(jnp.int32)


def _post_attn(mix_a, mix_b, x2, w_o, g_ffn, w_r, b_r):
    n, d = x2.shape
    tm = ROW_TILE
    tri = (lax.broadcasted_iota(jnp.int32, (tm, tm), 0)
           > lax.broadcasted_iota(jnp.int32, (tm, tm), 1)).astype(BF16)
    const = lambda i: (0, 0)
    row = lambda w: pl.BlockSpec((tm, w), lambda i: (i, 0))
    lanes_t = pl.BlockSpec((8, tm), lambda i: (0, i))
    return pl.pallas_call(
        _post_attn_kernel,
        grid=(n // tm,),
        in_specs=[
            row(WIDTH_A), row(WIDTH_B), row(d),
            pl.BlockSpec(w_o.shape, const),
            pl.BlockSpec((1, d), const),
            pl.BlockSpec(w_r.shape, const),
            pl.BlockSpec((1, LANES), const),
            pl.BlockSpec((tm, tm), const),
        ],
        out_specs=[row(d), row(d // 2), lanes_t, row(LANES), lanes_t,
                   pl.BlockSpec((1, LANES), const)],
        out_shape=[
            jax.ShapeDtypeStruct((n, d), F32),
            jax.ShapeDtypeStruct((n, d // 2), jnp.int32),
            jax.ShapeDtypeStruct((8, n), jnp.int32),
            jax.ShapeDtypeStruct((n, LANES), F32),
            jax.ShapeDtypeStruct((8, n), jnp.int32),
            jax.ShapeDtypeStruct((1, LANES), F32),
        ],
        scratch_shapes=[pltpu.VMEM((1, LANES), F32)],
        compiler_params=_cparams("arbitrary"),
        name="post_attn",
    )(mix_a, mix_b, x2, w_o, g_ffn, w_r, b_r, tri)


def _wprep_kernel(w_ref, wg_ref, wl_ref, wt):
    d, cols = w_ref.shape[1:]
    de = cols // 2
    for j in range(d // LANES):
        lanes = slice(j * LANES, (j + 1) * LANES)
        wt[j] = w_ref[0, lanes, :].T
        wg_ref[0, :, lanes] = wt[j, pl.ds(0, de, stride=2), :].astype(BF16)
        wl_ref[0, :, lanes] = wt[j, pl.ds(1, de, stride=2), :].astype(BF16)


def _wprep(w_gate_up):
    ne, d, cols = w_gate_up.shape
    de = cols // 2
    part = d // WPREP_SPLIT
    out = pl.BlockSpec((1, de, part), lambda e, r: (e, 0, r))
    return pl.pallas_call(
        _wprep_kernel,
        grid=(ne, WPREP_SPLIT),
        in_specs=[pl.BlockSpec((1, part, cols), lambda e, r: (e, r, 0))],
        out_specs=[out, out],
        out_shape=[jax.ShapeDtypeStruct((ne, de, d), BF16)] * 2,
        scratch_shapes=[pltpu.VMEM((part // LANES, cols, LANES), F32)],
        compiler_params=_cparams("parallel", "parallel"),
        name="wprep",
    )(w_gate_up)


def _sc_invert(keys, n_tokens, fill):
    info = plsc.get_sparse_core_info()
    workers = info.num_cores * info.num_subcores
    lanes = info.num_lanes
    total, = fill.shape
    count, = keys.shape
    chunk, unroll = SC_SCAN_CHUNK, SC_SCAN_UNROLL
    assert total % (workers * lanes) == 0 and count % (2 * chunk) == 0 and n_tokens % chunk == 0
    own = total // workers
    nchunks = count // chunk
    chunks_per_pass = n_tokens // chunk
    mesh = plsc.VectorSubcoreMesh(core_axis_name="c", subcore_axis_name="s")
    params = pltpu.CompilerParams()
    if "needs_layout_passes" in pltpu.CompilerParams.__dataclass_fields__:
        params = dataclasses.replace(params, needs_layout_passes=False)

    @functools.partial(
        pl.kernel, mesh=mesh, compiler_params=params,
        out_type=jax.ShapeDtypeStruct((total,), jnp.int32),
        scratch_types=[
            pltpu.VMEM((own,), jnp.int32),
            pltpu.VMEM((2, chunk), jnp.int32),
            pltpu.SemaphoreType.DMA((2,)),
        ],
    )
    def invert_kernel(keys_hbm, fill_hbm, out_hbm, own_v, key_v, sem):
        wid = lax.axis_index("s") * info.num_cores + lax.axis_index("c")
        lo = wid * own
        lane_ids = lax.broadcasted_iota(jnp.int32, (lanes,), 0)

        def fetch(c, b):
            off = pl.multiple_of(c * chunk, chunk)
            return pltpu.make_async_copy(keys_hbm.at[pl.ds(off, chunk)], key_v.at[b], sem.at[b])

        fetch(0, 0).start()
        pltpu.sync_copy(fill_hbm.at[pl.ds(lo, own)], own_v)

        @pl.loop(0, nchunks, step=2)
        def _(c0):
            for b in range(2):
                c = c0 + b

                @pl.when(c + 1 < nchunks)
                def _():
                    fetch(c + 1, 1 - b).start()

                fetch(c, b).wait()
                first_token = (c % chunks_per_pass) * chunk

                @plsc.parallel_loop(0, chunk // lanes, unroll=unroll)
                def _(j):
                    start = pl.multiple_of(j * lanes, lanes)
                    row = key_v[b, pl.ds(start, lanes)] - lo
                    mine = (row >= 0) & (row < own)
                    plsc.store_scatter(own_v, [jnp.where(mine, row, 0)], first_token + start + lane_ids, mask=mine)

        pltpu.sync_copy(own_v, out_hbm.at[pl.ds(lo, own)])

    return invert_kernel(keys, fill)


def _sc_gather(table, idx):
    info = plsc.get_sparse_core_info()
    workers = info.num_cores * info.num_subcores
    rows, width = idx.shape[0], table.shape[1]
    chunk, depth = SC_CHUNK, SC_DEPTH
    assert rows % (workers * chunk * depth) == 0
    per_worker = rows // workers
    nchunks = per_worker // chunk
    mesh = plsc.VectorSubcoreMesh(core_axis_name="c", subcore_axis_name="s")

    @functools.partial(
        pl.kernel, mesh=mesh,
        out_type=jax.ShapeDtypeStruct((rows, width), table.dtype),
        scratch_types=[
            pltpu.VMEM((nchunks, chunk), jnp.int32),
            pltpu.VMEM((depth, chunk, width), table.dtype),
            pltpu.SemaphoreType.DMA((depth,)),
            pltpu.SemaphoreType.DMA((depth,)),
        ],
    )
    def gather_kernel(table_hbm, idx_hbm, out_hbm, idx_v, rows_v, gsem, wsem):
        wid = lax.axis_index("s") * info.num_cores + lax.axis_index("c")
        base = wid * per_worker
        pltpu.sync_copy(idx_hbm.at[wid], idx_v)

        def gather(c, b):
            return pltpu.make_async_copy(table_hbm.at[idx_v.at[c]], rows_v.at[b], gsem.at[b])

        def write(c, b):
            off = pl.multiple_of(base + c * chunk, chunk)
            return pltpu.make_async_copy(rows_v.at[b], out_hbm.at[pl.ds(off, chunk)], wsem.at[b])

        @pl.loop(0, nchunks, step=depth)
        def _(c0):
            for b in range(depth):
                gather(c0 + b, b).start()
            for b in range(depth):
                gather(c0 + b, b).wait()
                write(c0 + b, b).start()
            for b in range(depth):
                write(c0 + b, b).wait()

    return gather_kernel(table, idx.reshape(workers, nchunks, chunk))


def _gmm_kernel(te_ref, used_ref, xs_ref, wg_ref, wl_ref, bg_ref, bl_ref, wd_ref, bd_ref, ys_ref):
    i = pl.program_id(0)

    @pl.when(used_ref[i] > 0)
    def _():
        a, b = _unpack_bf16_pairs(xs_ref[...])
        x = jnp.concatenate([a, b], axis=1).astype(BF16)
        hg = lax.dot_general(x, wg_ref[0], NT_DIMS, preferred_element_type=F32) + bg_ref[0]
        hl = lax.dot_general(x, wl_ref[0], NT_DIMS, preferred_element_type=F32) + bl_ref[0]
        xg = jnp.minimum(hg, SWIGLU_LIMIT)
        xl = jnp.clip(hl, -SWIGLU_LIMIT, SWIGLU_LIMIT)
        act = xg * jax.nn.sigmoid(SWIGLU_ALPHA * xg) * (xl + 1.0)
        out = jnp.dot(act.astype(BF16), wd_ref[0].astype(BF16), preferred_element_type=F32) + bd_ref[0]
        half = out.shape[1] // 2
        ys_ref[...] = _pack_bf16_pairs(out[:, :half], out[:, half:])

    @pl.when(used_ref[i] == 0)
    def _():
        ys_ref[...] = jnp.zeros_like(ys_ref)


def _gmm(tile_expert, tile_used, xs, wg_t, wl_t, b_glu, b_lin, w_down, b_down):
    rows, half = xs.shape
    d = 2 * half
    de = wg_t.shape[1]
    tm = GMM_TILE
    wspec = lambda shape: pl.BlockSpec((1,) + shape, lambda i, te, used: (te[i], 0, 0))
    grid_spec = pltpu.PrefetchScalarGridSpec(
        num_scalar_prefetch=2,
        grid=(rows // tm,),
        in_specs=[
            pl.BlockSpec((tm, half), lambda i, te, used: (i, 0)),
            wspec((de, d)), wspec((de, d)), wspec((1, de)), wspec((1, de)),
            wspec((de, d)), wspec((1, d)),
        ],
        out_specs=pl.BlockSpec((tm, half), lambda i, te, used: (i, 0)),
    )
    return pl.pallas_call(
        _gmm_kernel,
        grid_spec=grid_spec,
        out_shape=jax.ShapeDtypeStruct((rows, half), jnp.int32),
        compiler_params=_cparams("arbitrary"),
        name="gmm",
    )(tile_expert, tile_used, xs, wg_t, wl_t, b_glu, b_lin, w_down, b_down)


def _final_kernel(h_ref, yk_ref, gate_ref, p_ref, g_ref, wg_ref, wp_ref, o_ref):
    gates = gate_ref[...]
    h = h_ref[...]
    for k in range(TOP_K):
        h = h + gates[:, k:k + 1] * jnp.concatenate(_unpack_bf16_pairs(yk_ref[k]), axis=1)
    u = _rms(h, g_ref[...]).astype(BF16)
    gate = jax.nn.sigmoid(jnp.dot(u, wg_ref[...], preferred_element_type=F32))
    proj = jnp.dot(p_ref[...].astype(BF16), wp_ref[...], preferred_element_type=F32)
    o_ref[...] = h + gate * proj


def _final(h1, yk, gates, p2, g_ple, w_gate, w_proj, first_tile, out_so_far=None):
    n, d = h1.shape
    tm = ROW_TILE
    const = lambda i: (0, 0)
    rows = lambda width: pl.BlockSpec((tm, width), lambda i: (i + first_tile, 0))
    in_specs = [
        rows(d),
        pl.BlockSpec((TOP_K, tm, d // 2), lambda i: (0, i, 0)),
        rows(LANES),
        rows(p2.shape[1]),
        pl.BlockSpec((1, d), const),
        pl.BlockSpec(w_gate.shape, const),
        pl.BlockSpec(w_proj.shape, const),
    ]
    args = [h1, yk, gates, p2, g_ple, w_gate, w_proj]
    kernel_fn, aliases = _final_kernel, {}
    if out_so_far is not None:
        in_specs.append(pl.BlockSpec(memory_space=pl.ANY))
        args.append(out_so_far)
        aliases = {len(args) - 1: 0}
        kernel_fn = lambda *refs: _final_kernel(*refs[:7], refs[8])
    return pl.pallas_call(
        kernel_fn,
        grid=(yk.shape[1] // tm,),
        in_specs=in_specs,
        out_specs=rows(d),
        out_shape=jax.ShapeDtypeStruct((n, d), F32),
        input_output_aliases=aliases,
        compiler_params=_cparams("parallel"),
        name="final",
    )(*args)


def _layer(h, p, g_mix, w_in, b_f, g_qa, g_ka, g_qb, g_kb, w_o, g_ffn, w_router, b_router,
           w_gate_up, b_gate_up, w_down, b_down, g_ple, w_ple_gate, w_ple_proj):
    batch, seq, d = h.shape
    n = batch * seq
    assert tuple(dil for _, dil in DILATED_PATTERNS) == (1, 4, 16)
    for window, dil in DILATED_PATTERNS:
        per_class = seq // BLOCK // dil
        assert window // dil == BLOCK and seq % (dil * BLOCK) == 0
        assert per_class % UNITS_PER_STEP == 0 or UNITS_PER_STEP % per_class == 0
    assert n % ROW_TILE == 0 and d % (2 * LANES) == 0 and seq % FOX_TILE == 0
    x2 = h.reshape(n, d)

    qkv_cols = 3 * WIDTH_A + 3 * WIDTH_B
    w_qkv = w_in[:, :qkv_cols].astype(BF16)
    w_f = jnp.pad(w_in[:, qkv_cols:], ((0, 0), (0, LANES - N_HEADS_B))).astype(BF16)
    b_fp = jnp.pad(b_f.astype(F32), (0, LANES - N_HEADS_B)).reshape(1, LANES)
    scale = HEAD_DIM ** -0.5
    gains = jnp.stack([jnp.tile(g_qa, N_HEADS_A) * (scale * LOG2E), jnp.tile(g_ka, N_HEADS_A),
                       jnp.tile(g_qb, N_HEADS_B) * (scale * LOG2E), jnp.tile(g_kb, N_HEADS_B)]).astype(F32)
    hid = jnp.arange(2 * LANES) // HEAD_DIM
    bd = (hid[:, None] == hid[None, :]).astype(BF16)

    z, logf = _in_proj(x2, g_mix.reshape(1, d), w_qkv, w_f, b_fp, gains, bd)
    ccol = _cumsum(logf, batch, seq)

    slopes = 2.0 ** (-8.0 * jnp.arange(1, N_HEADS_A + 1, dtype=F32) / N_HEADS_A)
    mix_a = _dilated(z, slopes, batch, seq)
    mix_b = _fox(z, ccol, batch, seq)

    w_r = jnp.pad(w_router.astype(F32), ((0, 0), (0, LANES - N_EXPERTS)))
    w_r_hi = w_r.astype(BF16)
    w_r = jnp.concatenate([w_r_hi, (w_r - w_r_hi.astype(F32)).astype(BF16)], axis=1)
    b_r = jnp.concatenate([b_router.astype(F32), jnp.full((LANES - N_EXPERTS,), NEG_INF, F32)]).reshape(1, LANES)
    h1, u_packed, top_idx, gates, rank, counts = _post_attn(
        mix_a, mix_b, x2, w_o.astype(BF16), g_ffn.reshape(1, d), w_r, b_r)

    counts = counts[0, :N_EXPERTS].astype(jnp.int32)
    tiles_per = (counts + GMM_TILE - 1) // GMM_TILE
    tile_end = jnp.cumsum(tiles_per)
    starts = (tile_end - tiles_per) * GMM_TILE
    n_tiles = n * TOP_K // GMM_TILE + N_EXPERTS
    tile_ids = jnp.arange(n_tiles, dtype=jnp.int32)
    tile_used = (tile_ids < tile_end[-1]).astype(jnp.int32)
    last_used = jnp.minimum(tile_ids, tile_end[-1] - 1)
    tile_expert = jnp.sum((last_used[:, None] >= tile_end[None, :]).astype(jnp.int32), axis=1)
    tile_expert = jnp.minimum(tile_expert, N_EXPERTS - 1)
    experts = jnp.arange(N_EXPERTS, dtype=jnp.int32)[:, None, None]
    pos_t = rank[:TOP_K] + jnp.sum(jnp.where(top_idx[None, :TOP_K] == experts, starts[:, None, None], 0), axis=0)
    padding_tokens = jnp.arange(n_tiles * GMM_TILE, dtype=jnp.int32) % n
    src = _sc_invert(pos_t.reshape(-1), n, padding_tokens)
    xs = _sc_gather(u_packed, src)

    de = w_down.shape[1]
    wg_t, wl_t = _wprep(w_gate_up)
    ys = _gmm(tile_expert, tile_used, xs, wg_t, wl_t,
              b_gate_up[:, 0::2].reshape(N_EXPERTS, 1, de).astype(F32),
              b_gate_up[:, 1::2].reshape(N_EXPERTS, 1, de).astype(F32),
              w_down, b_down.reshape(N_EXPERTS, 1, d).astype(F32))
    assert n % (2 * ROW_TILE) == 0
    half_n = n // 2
    out = None
    for part in range(2):
        part_pos = pos_t[:, part * half_n:(part + 1) * half_n].reshape(-1)
        yk = _sc_gather(ys, part_pos).reshape(TOP_K, half_n, d // 2)
        out = _final(h1, yk, gates, p.reshape(n, -1), g_ple.reshape(1, d), w_ple_gate.astype(BF16),
                     w_ple_proj.astype(BF16), part * (half_n // ROW_TILE), out)
    return out.reshape(batch, seq, d)


def kernel(x, p, g_mix, w_in, b_f, g_qa, g_ka, g_qb, g_kb, w_o, g_ffn, w_router, b_router,
           w_gate_up, b_gate_up, w_down, b_down, g_ple, w_ple_gate, w_ple_proj):
    h = x
    for i in range(g_mix.shape[0]):
        h = _layer(h, p[i], g_mix[i], w_in[i], b_f[i], g_qa[i], g_ka[i], g_qb[i], g_kb[i], w_o[i],
                   g_ffn[i], w_router[i], b_router[i], w_gate_up[i], b_gate_up[i], w_down[i],
                   b_down[i], g_ple[i], w_ple_gate[i], w_ple_proj[i])
    return h
```
